```python
import math
import jax, jax.numpy as jnp
from jax import lax
import numpy as np


D_MODEL = 1024
BATCH = 8
SEQ = 16384
DEPTH = 4

N_A_LAYERS = DEPTH // 2
N_B_LAYERS = DEPTH - N_A_LAYERS
SSM_GROUP = 16
SSM_GROUPS = D_MODEL // SSM_GROUP
SSM_STATE = 64
SSM_CHUNK = 128
DT_MIN = 1e-3
DT_MAX = 1e-1
HEAD_DIM = 64
N_HEADS = D_MODEL // HEAD_DIM
N_KV_HEADS = N_HEADS // 4
GQA_GROUP = N_HEADS // N_KV_HEADS
WINDOW = 128
ATTN_BLOCK = 128
ROPE_THETA = 500000.0
ROT_DIM = HEAD_DIM // 4
D_FF = 4 * D_MODEL
PLE_DIM = 256
RMS_EPS = 1e-6
NEG_INF = -1e30

kernel_name = 'yoco_s5_swa_sink_hybrid'


def rmsnorm(x, g):
    xf = x.astype(jnp.float32)
    y = xf * lax.rsqrt(jnp.mean(xf * xf, axis=-1, keepdims=True) + RMS_EPS)
    return (y * g.astype(jnp.float32)).astype(x.dtype)


def partial_rope(x, pos):
    inv = ROPE_THETA ** (-jnp.arange(0, ROT_DIM, 2, dtype=jnp.float32) / ROT_DIM)
    ang = pos.astype(jnp.float32)[:, None] * inv[None, :]
    cos = jnp.cos(ang)[None, :, None, :]
    sin = jnp.sin(ang)[None, :, None, :]
    xr = x[..., :ROT_DIM].astype(jnp.float32)
    x1, x2 = xr[..., :ROT_DIM // 2], xr[..., ROT_DIM // 2:]
    rot = jnp.concatenate([x1 * cos - x2 * sin, x2 * cos + x1 * sin], axis=-1).astype(x.dtype)
    return jnp.concatenate([rot, x[..., ROT_DIM:]], axis=-1)


def s5_mixer(u, lam_re, lam_im, log_dt, b_re, b_im, c_re, c_im, d, w_glu):
    bsz, seqlen, dm = u.shape
    f32 = jnp.float32
    lam_re = lam_re.astype(f32); lam_im = lam_im.astype(f32)
    b_re = b_re.astype(f32); b_im = b_im.astype(f32)
    c_re = c_re.astype(f32); c_im = c_im.astype(f32)
    dt = jnp.exp(log_dt.astype(f32))[:, None]
    mag = jnp.exp(lam_re * dt)
    a_r = mag * jnp.cos(lam_im * dt)
    a_i = mag * jnp.sin(lam_im * dt)
    den = lam_re * lam_re + lam_im * lam_im
    nr = a_r - 1.0
    coef_r = (nr * lam_re + a_i * lam_im) / den
    coef_i = (a_i * lam_re - nr * lam_im) / den
    bb_r = coef_r[..., None] * b_re - coef_i[..., None] * b_im
    bb_i = coef_r[..., None] * b_im + coef_i[..., None] * b_re

    n_chunks = seqlen // SSM_CHUNK
    ug = u.astype(f32).reshape(bsz, n_chunks, SSM_CHUNK, SSM_GROUPS, SSM_GROUP)
    ug = jnp.transpose(ug, (1, 0, 2, 3, 4))

    def combine(e1, e2):
        a1r, a1i, s1r, s1i = e1
        a2r, a2i, s2r, s2i = e2
        return (a2r * a1r - a2i * a1i,
                a2r * a1i + a2i * a1r,
                a2r * s1r - a2i * s1i + s2r,
                a2r * s1i + a2i * s1r + s2i)

    def chunk_step(carry, uc):
        cr, ci = carry
        bur = jnp.einsum('gnh,btgh->btgn', bb_r, uc)
        bui = jnp.einsum('gnh,btgh->btgn', bb_i, uc)
        ar = jnp.broadcast_to(a_r, bur.shape)
        ai = jnp.broadcast_to(a_i, bui.shape)
        pr, pi, sr, si = lax.associative_scan(combine, (ar, ai, bur, bui), axis=1)
        xr = sr + pr * cr[:, None] - pi * ci[:, None]
        xi = si + pr * ci[:, None] + pi * cr[:, None]
        y = jnp.einsum('ghn,btgn->btgh', c_re, xr) - jnp.einsum('ghn,btgn->btgh', c_im, xi)
        return (xr[:, -1], xi[:, -1]), y

    init = (jnp.zeros((bsz, SSM_GROUPS, SSM_STATE), f32), jnp.zeros((bsz, SSM_GROUPS, SSM_STATE), f32))
    _, ys = lax.scan(chunk_step, init, ug)
    y = jnp.transpose(ys, (1, 0, 2, 3, 4)).reshape(bsz, seqlen, dm)
    y = y + d.astype(f32) * u.astype(f32)
    y = jax.nn.gelu(y).astype(u.dtype)
    ab = y @ w_glu
    a, b = ab[..., :dm], ab[..., dm:]
    return a * jax.nn.sigmoid(b)


def swa_sink_attention(q, k, v, sinks):
    bsz, seqlen = q.shape[0], q.shape[1]
    nb = seqlen // ATTN_BLOCK
    f32 = jnp.float32
    qb = q.astype(f32).reshape(bsz, nb, ATTN_BLOCK, N_KV_HEADS, GQA_GROUP, HEAD_DIM)
    kb = k.astype(f32).reshape(bsz, nb, ATTN_BLOCK, N_KV_HEADS, HEAD_DIM)
    vb = v.astype(f32).reshape(bsz, nb, ATTN_BLOCK, N_KV_HEADS, HEAD_DIM)
    k_prev = jnp.concatenate([jnp.zeros_like(kb[:, :1]), kb[:, :-1]], axis=1)
    v_prev = jnp.concatenate([jnp.zeros_like(vb[:, :1]), vb[:, :-1]], axis=1)
    kk = jnp.concatenate([k_prev, kb], axis=2)
    vv = jnp.concatenate([v_prev, vb], axis=2)
    s = jnp.einsum('bnqkgd,bnjkd->bnkgqj', qb, kk) * (HEAD_DIM ** -0.5)
    qi = jnp.arange(ATTN_BLOCK)[:, None] + ATTN_BLOCK
    kj = jnp.arange(2 * ATTN_BLOCK)[None, :]
    band = (kj <= qi) & (qi - kj < WINDOW)
    has_prev = jnp.arange(nb) > 0
    mask = band[None] & ((kj >= ATTN_BLOCK)[None] | has_prev[:, None, None])
    s = jnp.where(mask[None, :, None, None], s, NEG_INF)
    sink = sinks.astype(f32).reshape(1, 1, N_KV_HEADS, GQA_GROUP, 1, 1)
    m = jnp.maximum(jnp.max(s, axis=-1, keepdims=True), sink)
    pr = jnp.exp(s - m)
    w = pr / (jnp.sum(pr, axis=-1, keepdims=True) + jnp.exp(sink - m))
    o = jnp.einsum('bnkgqj,bnjkd->bnqkgd', w, vv)
    return o.reshape(bsz, seqlen, N_HEADS * HEAD_DIM).astype(q.dtype)


def _fwd_setup_inputs(seed: int = 0) -> dict:
    key = jax.random.key(seed)
    ks = jax.random.split(key, 32)
    f32 = jnp.float32
    nrm = lambda k, shape, scale: jax.random.normal(k, shape, f32) * scale
    x = nrm(ks[0], (BATCH, SEQ, D_MODEL), 1.0)
    p = nrm(ks[1], (DEPTH, BATCH, SEQ, PLE_DIM), 1.0)
    norm_mix = 1.0 + nrm(ks[2], (DEPTH, D_MODEL), 0.02)
    ssm_lambda_re = -0.5 + nrm(ks[3], (N_A_LAYERS, SSM_GROUPS, SSM_STATE), 0.01)
    ssm_lambda_im = (jnp.pi * jnp.arange(SSM_STATE, dtype=f32))[None, None, :] + nrm(ks[4], (N_A_LAYERS, SSM_GROUPS, SSM_STATE), 0.01)
    ssm_log_dt = math.log(DT_MIN) + jax.random.uniform(ks[5], (N_A_LAYERS, SSM_GROUPS), f32) * (math.log(DT_MAX) - math.log(DT_MIN))
    ssm_b_re = nrm(ks[6], (N_A_LAYERS, SSM_GROUPS, SSM_STATE, SSM_GROUP), (2 * SSM_GROUP) ** -0.5)
    ssm_b_im = nrm(ks[7], (N_A_LAYERS, SSM_GROUPS, SSM_STATE, SSM_GROUP), (2 * SSM_GROUP) ** -0.5)
    ssm_c_re = nrm(ks[8], (N_A_LAYERS, SSM_GROUPS, SSM_GROUP, SSM_STATE), (2 * SSM_STATE) ** -0.5)
    ssm_c_im = nrm(ks[9], (N_A_LAYERS, SSM_GROUPS, SSM_GROUP, SSM_STATE), (2 * SSM_STATE) ** -0.5)
    ssm_d = nrm(ks[10], (N_A_LAYERS, D_MODEL), 1.0)
    ssm_w_glu = nrm(ks[11], (N_A_LAYERS, D_MODEL, 2 * D_MODEL), D_MODEL ** -0.5)
    kv_norm = 1.0 + nrm(ks[12], (D_MODEL,), 0.02)
    w_k = nrm(ks[13], (D_MODEL, N_KV_HEADS * HEAD_DIM), D_MODEL ** -0.5)
    w_v = nrm(ks[14], (D_MODEL, N_KV_HEADS * HEAD_DIM), D_MODEL ** -0.5)
    w_q = nrm(ks[15], (N_B_LAYERS, D_MODEL, N_HEADS * HEAD_DIM), D_MODEL ** -0.5)
    attn_sinks = nrm(ks[16], (N_B_LAYERS, N_HEADS), 0.5)
    w_o = nrm(ks[17], (N_B_LAYERS, N_HEADS * HEAD_DIM, D_MODEL), (N_HEADS * HEAD_DIM) ** -0.5)
    norm_mlp = 1.0 + nrm(ks[18], (DEPTH, D_MODEL), 0.02)
    w_up = nrm(ks[19], (DEPTH, D_MODEL, D_FF), D_MODEL ** -0.5)
    w_down = nrm(ks[20], (DEPTH, D_FF, D_MODEL), D_FF ** -0.5)
    norm_ple = 1.0 + nrm(ks[21], (DEPTH, D_MODEL), 0.02)
    w_ple_gate = nrm(ks[22], (DEPTH, D_MODEL, D_MODEL), D_MODEL ** -0.5)
    w_ple_proj = nrm(ks[23], (DEPTH, PLE_DIM, D_MODEL), PLE_DIM ** -0.5)
    norm_final = 1.0 + nrm(ks[24], (D_MODEL,), 0.02)
    return {'x': x, 'p': p, 'norm_mix': norm_mix,
            'ssm_lambda_re': ssm_lambda_re, 'ssm_lambda_im': ssm_lambda_im, 'ssm_log_dt': ssm_log_dt,
            'ssm_b_re': ssm_b_re, 'ssm_b_im': ssm_b_im, 'ssm_c_re': ssm_c_re, 'ssm_c_im': ssm_c_im,
            'ssm_d': ssm_d, 'ssm_w_glu': ssm_w_glu,
            'kv_norm': kv_norm, 'w_k': w_k, 'w_v': w_v, 'w_q': w_q, 'attn_sinks': attn_sinks, 'w_o': w_o,
            'norm_mlp': norm_mlp, 'w_up': w_up, 'w_down': w_down,
            'norm_ple': norm_ple, 'w_ple_gate': w_ple_gate, 'w_ple_proj': w_ple_proj,
            'norm_final': norm_final}


def _fwd_reference(x, p, norm_mix, ssm_lambda_re, ssm_lambda_im, ssm_log_dt, ssm_b_re, ssm_b_im,
              ssm_c_re, ssm_c_im, ssm_d, ssm_w_glu, kv_norm, w_k, w_v, w_q, attn_sinks, w_o,
              norm_mlp, w_up, w_down, norm_ple, w_ple_gate, w_ple_proj, norm_final):
    bsz, seqlen, _ = x.shape
    pos = jnp.arange(seqlen, dtype=jnp.int32)
    h = x
    k_shared = None
    v_shared = None
    for i in range(DEPTH):
        hn = rmsnorm(h, norm_mix[i])
        if i < N_A_LAYERS:
            mix = s5_mixer(hn, ssm_lambda_re[i], ssm_lambda_im[i], ssm_log_dt[i], ssm_b_re[i], ssm_b_im[i],
                           ssm_c_re[i], ssm_c_im[i], ssm_d[i], ssm_w_glu[i])
        else:
            j = i - N_A_LAYERS
            q = (hn @ w_q[j]).reshape(bsz, seqlen, N_HEADS, HEAD_DIM)
            q = partial_rope(q, pos)
            mix = swa_sink_attention(q, k_shared, v_shared, attn_sinks[j]) @ w_o[j]
        h = h + mix
        hm = rmsnorm(h, norm_mlp[i])
        h = h + jnp.square(jax.nn.relu(hm @ w_up[i])) @ w_down[i]
        gate = jax.nn.sigmoid(rmsnorm(h, norm_ple[i]) @ w_ple_gate[i])
        h = h + gate * (p[i] @ w_ple_proj[i])
        if i == N_A_LAYERS - 1:
            hk = rmsnorm(h, kv_norm)
            k_shared = partial_rope((hk @ w_k).reshape(bsz, seqlen, N_KV_HEADS, HEAD_DIM), pos)
            v_shared = (hk @ w_v).reshape(bsz, seqlen, N_KV_HEADS, HEAD_DIM)
    return rmsnorm(h, norm_final)


import jax as _jax
import jax.numpy as _jnp

TWIN_FORMAT = 'train_step'
FWD_PARAMS = ['x', 'p', 'norm_mix', 'ssm_lambda_re', 'ssm_lambda_im', 'ssm_log_dt', 'ssm_b_re', 'ssm_b_im', 'ssm_c_re', 'ssm_c_im', 'ssm_d', 'ssm_w_glu', 'kv_norm', 'w_k', 'w_v', 'w_q', 'attn_sinks', 'w_o', 'norm_mlp', 'w_up', 'w_down', 'norm_ple', 'w_ple_gate', 'w_ple_proj', 'norm_final']
TWIN_WEIGHTS = ['norm_mix', 'ssm_lambda_re', 'ssm_lambda_im', 'ssm_log_dt', 'ssm_b_re', 'ssm_b_im', 'ssm_c_re', 'ssm_c_im', 'ssm_d', 'ssm_w_glu', 'kv_norm', 'w_k', 'w_v', 'w_q', 'attn_sinks', 'w_o', 'norm_mlp', 'w_up', 'w_down', 'norm_ple', 'w_ple_gate', 'w_ple_proj', 'norm_final']
TWIN_DIFF_INPUT = 'x'
TWIN_INPUTS = ['x', 'p', 'norm_mix', 'ssm_lambda_re', 'ssm_lambda_im', 'ssm_log_dt', 'ssm_b_re', 'ssm_b_im', 'ssm_c_re', 'ssm_c_im', 'ssm_d', 'ssm_w_glu', 'kv_norm', 'w_k', 'w_v', 'w_q', 'attn_sinks', 'w_o', 'norm_mlp', 'w_up', 'w_down', 'norm_ple', 'w_ple_gate', 'w_ple_proj', 'norm_final', 'loss_target', 'm_norm_mix', 'm_ssm_lambda_re', 'm_ssm_lambda_im', 'm_ssm_log_dt', 'm_ssm_b_re', 'm_ssm_b_im', 'm_ssm_c_re', 'm_ssm_c_im', 'm_ssm_d', 'm_ssm_w_glu', 'm_kv_norm', 'm_w_k', 'm_w_v', 'm_w_q', 'm_attn_sinks', 'm_w_o', 'm_norm_mlp', 'm_w_up', 'm_w_down', 'm_norm_ple', 'm_w_ple_gate', 'm_w_ple_proj', 'm_norm_final', 'v_norm_mix', 'v_ssm_lambda_re', 'v_ssm_lambda_im', 'v_ssm_log_dt', 'v_ssm_b_re', 'v_ssm_b_im', 'v_ssm_c_re', 'v_ssm_c_im', 'v_ssm_d', 'v_ssm_w_glu', 'v_kv_norm', 'v_w_k', 'v_w_v', 'v_w_q', 'v_attn_sinks', 'v_w_o', 'v_norm_mlp', 'v_w_up', 'v_w_down', 'v_norm_ple', 'v_w_ple_gate', 'v_w_ple_proj', 'v_norm_final']
TWIN_OUTPUTS = ['loss', 'grad_x', 'grad_norm_mix', 'grad_ssm_lambda_re', 'grad_ssm_lambda_im', 'grad_ssm_log_dt', 'grad_ssm_b_re', 'grad_ssm_b_im', 'grad_ssm_c_re', 'grad_ssm_c_im', 'grad_ssm_d', 'grad_ssm_w_glu', 'grad_kv_norm', 'grad_w_k', 'grad_w_v', 'grad_w_q', 'grad_attn_sinks', 'grad_w_o', 'grad_norm_mlp', 'grad_w_up', 'grad_w_down', 'grad_norm_ple', 'grad_w_ple_gate', 'grad_w_ple_proj', 'grad_norm_final', 'delta_norm_mix', 'delta_ssm_lambda_re', 'delta_ssm_lambda_im', 'delta_ssm_log_dt', 'delta_ssm_b_re', 'delta_ssm_b_im', 'delta_ssm_c_re', 'delta_ssm_c_im', 'delta_ssm_d', 'delta_ssm_w_glu', 'delta_kv_norm', 'delta_w_k', 'delta_w_v', 'delta_w_q', 'delta_attn_sinks', 'delta_w_o', 'delta_norm_mlp', 'delta_w_up', 'delta_w_down', 'delta_norm_ple', 'delta_w_ple_gate', 'delta_w_ple_proj', 'delta_norm_final', 'new_m_norm_mix', 'new_m_ssm_lambda_re', 'new_m_ssm_lambda_im', 'new_m_ssm_log_dt', 'new_m_ssm_b_re', 'new_m_ssm_b_im', 'new_m_ssm_c_re', 'new_m_ssm_c_im', 'new_m_ssm_d', 'new_m_ssm_w_glu', 'new_m_kv_norm', 'new_m_w_k', 'new_m_w_v', 'new_m_w_q', 'new_m_attn_sinks', 'new_m_w_o', 'new_m_norm_mlp', 'new_m_w_up', 'new_m_w_down', 'new_m_norm_ple', 'new_m_w_ple_gate', 'new_m_w_ple_proj', 'new_m_norm_final', 'new_v_norm_mix', 'new_v_ssm_lambda_re', 'new_v_ssm_lambda_im', 'new_v_ssm_log_dt', 'new_v_ssm_b_re', 'new_v_ssm_b_im', 'new_v_ssm_c_re', 'new_v_ssm_c_im', 'new_v_ssm_d', 'new_v_ssm_w_glu', 'new_v_kv_norm', 'new_v_w_k', 'new_v_w_v', 'new_v_w_q', 'new_v_attn_sinks', 'new_v_w_o', 'new_v_norm_mlp', 'new_v_w_up', 'new_v_w_down', 'new_v_norm_ple', 'new_v_w_ple_gate', 'new_v_w_ple_proj', 'new_v_norm_final']
TWIN_LEAF_KINDS = {'loss': 'loss', 'grad_x': 'grad_x', 'grad_norm_mix': 'grad_w', 'grad_ssm_lambda_re': 'grad_w', 'grad_ssm_lambda_im': 'grad_w', 'grad_ssm_log_dt': 'grad_w', 'grad_ssm_b_re': 'grad_w', 'grad_ssm_b_im': 'grad_w', 'grad_ssm_c_re': 'grad_w', 'grad_ssm_c_im': 'grad_w', 'grad_ssm_d': 'grad_w', 'grad_ssm_w_glu': 'grad_w', 'grad_kv_norm': 'grad_w', 'grad_w_k': 'grad_w', 'grad_w_v': 'grad_w', 'grad_w_q': 'grad_w', 'grad_attn_sinks': 'grad_w', 'grad_w_o': 'grad_w', 'grad_norm_mlp': 'grad_w', 'grad_w_up': 'grad_w', 'grad_w_down': 'grad_w', 'grad_norm_ple': 'grad_w', 'grad_w_ple_gate': 'grad_w', 'grad_w_ple_proj': 'grad_w', 'grad_norm_final': 'grad_w', 'delta_norm_mix': 'delta_w', 'delta_ssm_lambda_re': 'delta_w', 'delta_ssm_lambda_im': 'delta_w', 'delta_ssm_log_dt': 'delta_w', 'delta_ssm_b_re': 'delta_w', 'delta_ssm_b_im': 'delta_w', 'delta_ssm_c_re': 'delta_w', 'delta_ssm_c_im': 'delta_w', 'delta_ssm_d': 'delta_w', 'delta_ssm_w_glu': 'delta_w', 'delta_kv_norm': 'delta_w', 'delta_w_k': 'delta_w', 'delta_w_v': 'delta_w', 'delta_w_q': 'delta_w', 'delta_attn_sinks': 'delta_w', 'delta_w_o': 'delta_w', 'delta_norm_mlp': 'delta_w', 'delta_w_up': 'delta_w', 'delta_w_down': 'delta_w', 'delta_norm_ple': 'delta_w', 'delta_w_ple_gate': 'delta_w', 'delta_w_ple_proj': 'delta_w', 'delta_norm_final': 'delta_w', 'new_m_norm_mix': 'new_m', 'new_m_ssm_lambda_re': 'new_m', 'new_m_ssm_lambda_im': 'new_m', 'new_m_ssm_log_dt': 'new_m', 'new_m_ssm_b_re': 'new_m', 'new_m_ssm_b_im': 'new_m', 'new_m_ssm_c_re': 'new_m', 'new_m_ssm_c_im': 'new_m', 'new_m_ssm_d': 'new_m', 'new_m_ssm_w_glu': 'new_m', 'new_m_kv_norm': 'new_m', 'new_m_w_k': 'new_m', 'new_m_w_v': 'new_m', 'new_m_w_q': 'new_m', 'new_m_attn_sinks': 'new_m', 'new_m_w_o': 'new_m', 'new_m_norm_mlp': 'new_m', 'new_m_w_up': 'new_m', 'new_m_w_down': 'new_m', 'new_m_norm_ple': 'new_m', 'new_m_w_ple_gate': 'new_m', 'new_m_w_ple_proj': 'new_m', 'new_m_norm_final': 'new_m', 'new_v_norm_mix': 'new_v', 'new_v_ssm_lambda_re': 'new_v', 'new_v_ssm_lambda_im': 'new_v', 'new_v_ssm_log_dt': 'new_v', 'new_v_ssm_b_re': 'new_v', 'new_v_ssm_b_im': 'new_v', 'new_v_ssm_c_re': 'new_v', 'new_v_ssm_c_im': 'new_v', 'new_v_ssm_d': 'new_v', 'new_v_ssm_w_glu': 'new_v', 'new_v_kv_norm': 'new_v', 'new_v_w_k': 'new_v', 'new_v_w_v': 'new_v', 'new_v_w_q': 'new_v', 'new_v_attn_sinks': 'new_v', 'new_v_w_o': 'new_v', 'new_v_norm_mlp': 'new_v', 'new_v_w_up': 'new_v', 'new_v_w_down': 'new_v', 'new_v_norm_ple': 'new_v', 'new_v_w_ple_gate': 'new_v', 'new_v_w_ple_proj': 'new_v', 'new_v_norm_final': 'new_v'}


def _forward(args):
    return _fwd_reference(*[args[k] for k in FWD_PARAMS])


def _output_shape():
    def fwd():
        inp = _fwd_setup_inputs(0)
        return _fwd_reference(*[inp[k] for k in FWD_PARAMS])
    out = _jax.eval_shape(fwd)
    return out.shape, out.dtype

N_MICROBATCH = 1
ADAM_LR = 0.001
ADAM_B1 = 0.9
ADAM_B2 = 0.999
ADAM_EPS = 1e-08
ADAM_WD = 0.01
ADAM_STEP = 10
PER_EXAMPLE_BATCH_AXIS = {'x': 0, 'p': 1, 'loss_target': 0}
SHARED_INPUTS = []
_WEIGHT_DTYPES = {'norm_mix': _jnp.float32, 'ssm_lambda_re': _jnp.float32, 'ssm_lambda_im': _jnp.float32, 'ssm_log_dt': _jnp.float32, 'ssm_b_re': _jnp.float32, 'ssm_b_im': _jnp.float32, 'ssm_c_re': _jnp.float32, 'ssm_c_im': _jnp.float32, 'ssm_d': _jnp.float32, 'ssm_w_glu': _jnp.float32, 'kv_norm': _jnp.float32, 'w_k': _jnp.float32, 'w_v': _jnp.float32, 'w_q': _jnp.float32, 'attn_sinks': _jnp.float32, 'w_o': _jnp.float32, 'norm_mlp': _jnp.float32, 'w_up': _jnp.float32, 'w_down': _jnp.float32, 'norm_ple': _jnp.float32, 'w_ple_gate': _jnp.float32, 'w_ple_proj': _jnp.float32, 'norm_final': _jnp.float32}
MOMENT_SCALE = {'norm_mix': 1.148916e-01, 'ssm_lambda_re': 1.013492e-02, 'ssm_lambda_im': 9.013499e-03, 'ssm_log_dt': 4.628132e+00, 'ssm_b_re': 4.708972e-03, 'ssm_b_im': 4.849825e-03, 'ssm_c_re': 9.760452e-03, 'ssm_c_im': 9.434022e-03, 'ssm_d': 1.561742e-01, 'ssm_w_glu': 1.037968e-01, 'kv_norm': 2.697912e-01, 'w_k': 6.574257e-02, 'w_v': 5.101925e-01, 'w_q': 2.311401e-02, 'attn_sinks': 2.633563e-02, 'w_o': 1.803287e-01, 'norm_mlp': 2.639931e-01, 'w_up': 1.287708e-01, 'w_down': 3.261571e-01, 'norm_ple': 3.565526e-02, 'w_ple_gate': 3.603465e-02, 'w_ple_proj': 8.897925e-02, 'norm_final': 1.324633e+02}


def _to_microbatches(a, axis):
    t = _jnp.moveaxis(a, axis, 0)
    t = t.reshape((N_MICROBATCH, t.shape[0] // N_MICROBATCH) + t.shape[1:])
    return _jnp.moveaxis(t, 1, axis + 1)


def setup_inputs(seed: int = 0) -> dict:
    inp = _fwd_setup_inputs(seed)
    key = _jax.random.fold_in(_jax.random.key(seed), 7919)
    shape, _ = _output_shape()
    out = dict(inp)
    out["loss_target"] = _jax.random.normal(_jax.random.fold_in(key, 0), shape, _jnp.float32)
    for i, name in enumerate(TWIN_WEIGHTS):
        w = inp[name].astype(_jnp.float32)
        if MOMENT_SCALE is None:
            s = _jnp.sqrt(_jnp.mean(_jnp.square(w)) + 1e-30)
        else:
            s = MOMENT_SCALE[name]
        km, kv = _jax.random.split(_jax.random.fold_in(key, i + 1))
        out[name] = w
        out["m_" + name] = s * _jax.random.normal(km, w.shape, _jnp.float32)
        out["v_" + name] = (s * s) * _jax.random.uniform(kv, w.shape, _jnp.float32, 0.5, 1.5)
    if N_MICROBATCH > 1:
        for name, axis in PER_EXAMPLE_BATCH_AXIS.items():
            out[name] = _to_microbatches(out[name], axis)
    return {'x': out['x'], 'p': out['p'], 'norm_mix': out['norm_mix'], 'ssm_lambda_re': out['ssm_lambda_re'], 'ssm_lambda_im': out['ssm_lambda_im'], 'ssm_log_dt': out['ssm_log_dt'], 'ssm_b_re': out['ssm_b_re'], 'ssm_b_im': out['ssm_b_im'], 'ssm_c_re': out['ssm_c_re'], 'ssm_c_im': out['ssm_c_im'], 'ssm_d': out['ssm_d'], 'ssm_w_glu': out['ssm_w_glu'], 'kv_norm': out['kv_norm'], 'w_k': out['w_k'], 'w_v': out['w_v'], 'w_q': out['w_q'], 'attn_sinks': out['attn_sinks'], 'w_o': out['w_o'], 'norm_mlp': out['norm_mlp'], 'w_up': out['w_up'], 'w_down': out['w_down'], 'norm_ple': out['norm_ple'], 'w_ple_gate': out['w_ple_gate'], 'w_ple_proj': out['w_ple_proj'], 'norm_final': out['norm_final'], 'loss_target': out['loss_target'], 'm_norm_mix': out['m_norm_mix'], 'm_ssm_lambda_re': out['m_ssm_lambda_re'], 'm_ssm_lambda_im': out['m_ssm_lambda_im'], 'm_ssm_log_dt': out['m_ssm_log_dt'], 'm_ssm_b_re': out['m_ssm_b_re'], 'm_ssm_b_im': out['m_ssm_b_im'], 'm_ssm_c_re': out['m_ssm_c_re'], 'm_ssm_c_im': out['m_ssm_c_im'], 'm_ssm_d': out['m_ssm_d'], 'm_ssm_w_glu': out['m_ssm_w_glu'], 'm_kv_norm': out['m_kv_norm'], 'm_w_k': out['m_w_k'], 'm_w_v': out['m_w_v'], 'm_w_q': out['m_w_q'], 'm_attn_sinks': out['m_attn_sinks'], 'm_w_o': out['m_w_o'], 'm_norm_mlp': out['m_norm_mlp'], 'm_w_up': out['m_w_up'], 'm_w_down': out['m_w_down'], 'm_norm_ple': out['m_norm_ple'], 'm_w_ple_gate': out['m_w_ple_gate'], 'm_w_ple_proj': out['m_w_ple_proj'], 'm_norm_final': out['m_norm_final'], 'v_norm_mix': out['v_norm_mix'], 'v_ssm_lambda_re': out['v_ssm_lambda_re'], 'v_ssm_lambda_im': out['v_ssm_lambda_im'], 'v_ssm_log_dt': out['v_ssm_log_dt'], 'v_ssm_b_re': out['v_ssm_b_re'], 'v_ssm_b_im': out['v_ssm_b_im'], 'v_ssm_c_re': out['v_ssm_c_re'], 'v_ssm_c_im': out['v_ssm_c_im'], 'v_ssm_d': out['v_ssm_d'], 'v_ssm_w_glu': out['v_ssm_w_glu'], 'v_kv_norm': out['v_kv_norm'], 'v_w_k': out['v_w_k'], 'v_w_v': out['v_w_v'], 'v_w_q': out['v_w_q'], 'v_attn_sinks': out['v_attn_sinks'], 'v_w_o': out['v_w_o'], 'v_norm_mlp': out['v_norm_mlp'], 'v_w_up': out['v_w_up'], 'v_w_down': out['v_w_down'], 'v_norm_ple': out['v_norm_ple'], 'v_w_ple_gate': out['v_w_ple_gate'], 'v_w_ple_proj': out['v_w_ple_proj'], 'v_norm_final': out['v_norm_final']}


def _loss(weights, diff, rest, loss_target):
    with _jax.named_scope("forward"):
        args = {**rest, TWIN_DIFF_INPUT: diff, **{k: w.astype(_WEIGHT_DTYPES[k]) for k, w in weights.items()}}
        y = _forward(args)
    with _jax.named_scope("loss_head"):
        err = _jnp.square(y.astype(_jnp.float32) - loss_target)
        return 0.5 * _jnp.sum(_jnp.mean(err, axis=-1)) if err.ndim else 0.5 * err


def _adamw(w, g, m, v):
    m = ADAM_B1 * m + (1.0 - ADAM_B1) * g
    v = ADAM_B2 * v + (1.0 - ADAM_B2) * _jnp.square(g)
    m_hat = m / (1.0 - ADAM_B1 ** ADAM_STEP)
    v_hat = v / (1.0 - ADAM_B2 ** ADAM_STEP)
    delta = -ADAM_LR * (m_hat / (_jnp.sqrt(v_hat) + ADAM_EPS) + ADAM_WD * w)
    return delta, m, v


def reference(x, p, norm_mix, ssm_lambda_re, ssm_lambda_im, ssm_log_dt, ssm_b_re, ssm_b_im, ssm_c_re, ssm_c_im, ssm_d, ssm_w_glu, kv_norm, w_k, w_v, w_q, attn_sinks, w_o, norm_mlp, w_up, w_down, norm_ple, w_ple_gate, w_ple_proj, norm_final, loss_target, m_norm_mix, m_ssm_lambda_re, m_ssm_lambda_im, m_ssm_log_dt, m_ssm_b_re, m_ssm_b_im, m_ssm_c_re, m_ssm_c_im, m_ssm_d, m_ssm_w_glu, m_kv_norm, m_w_k, m_w_v, m_w_q, m_attn_sinks, m_w_o, m_norm_mlp, m_w_up, m_w_down, m_norm_ple, m_w_ple_gate, m_w_ple_proj, m_norm_final, v_norm_mix, v_ssm_lambda_re, v_ssm_lambda_im, v_ssm_log_dt, v_ssm_b_re, v_ssm_b_im, v_ssm_c_re, v_ssm_c_im, v_ssm_d, v_ssm_w_glu, v_kv_norm, v_w_k, v_w_v, v_w_q, v_attn_sinks, v_w_o, v_norm_mlp, v_w_up, v_w_down, v_norm_ple, v_w_ple_gate, v_w_ple_proj, v_norm_final):
    given = dict(x=x, p=p, norm_mix=norm_mix, ssm_lambda_re=ssm_lambda_re, ssm_lambda_im=ssm_lambda_im, ssm_log_dt=ssm_log_dt, ssm_b_re=ssm_b_re, ssm_b_im=ssm_b_im, ssm_c_re=ssm_c_re, ssm_c_im=ssm_c_im, ssm_d=ssm_d, ssm_w_glu=ssm_w_glu, kv_norm=kv_norm, w_k=w_k, w_v=w_v, w_q=w_q, attn_sinks=attn_sinks, w_o=w_o, norm_mlp=norm_mlp, w_up=w_up, w_down=w_down, norm_ple=norm_ple, w_ple_gate=w_ple_gate, w_ple_proj=w_ple_proj, norm_final=norm_final, loss_target=loss_target, m_norm_mix=m_norm_mix, m_ssm_lambda_re=m_ssm_lambda_re, m_ssm_lambda_im=m_ssm_lambda_im, m_ssm_log_dt=m_ssm_log_dt, m_ssm_b_re=m_ssm_b_re, m_ssm_b_im=m_ssm_b_im, m_ssm_c_re=m_ssm_c_re, m_ssm_c_im=m_ssm_c_im, m_ssm_d=m_ssm_d, m_ssm_w_glu=m_ssm_w_glu, m_kv_norm=m_kv_norm, m_w_k=m_w_k, m_w_v=m_w_v, m_w_q=m_w_q, m_attn_sinks=m_attn_sinks, m_w_o=m_w_o, m_norm_mlp=m_norm_mlp, m_w_up=m_w_up, m_w_down=m_w_down, m_norm_ple=m_norm_ple, m_w_ple_gate=m_w_ple_gate, m_w_ple_proj=m_w_ple_proj, m_norm_final=m_norm_final, v_norm_mix=v_norm_mix, v_ssm_lambda_re=v_ssm_lambda_re, v_ssm_lambda_im=v_ssm_lambda_im, v_ssm_log_dt=v_ssm_log_dt, v_ssm_b_re=v_ssm_b_re, v_ssm_b_im=v_ssm_b_im, v_ssm_c_re=v_ssm_c_re, v_ssm_c_im=v_ssm_c_im, v_ssm_d=v_ssm_d, v_ssm_w_glu=v_ssm_w_glu, v_kv_norm=v_kv_norm, v_w_k=v_w_k, v_w_v=v_w_v, v_w_q=v_w_q, v_attn_sinks=v_attn_sinks, v_w_o=v_w_o, v_norm_mlp=v_norm_mlp, v_w_up=v_w_up, v_w_down=v_w_down, v_norm_ple=v_norm_ple, v_w_ple_gate=v_w_ple_gate, v_w_ple_proj=v_w_ple_proj, v_norm_final=v_norm_final)
    weights = {n: given[n] for n in TWIN_WEIGHTS}
    shared = {n: given[n] for n in SHARED_INPUTS}
    per_example = {n: given[n] for n in ['x', 'p']}
    grad_fn = _jax.value_and_grad(_loss, argnums=(0, 1))

    def one_microbatch(ex, loss_target):
        ex = dict(ex)
        diff = ex.pop(TWIN_DIFF_INPUT)
        return grad_fn(weights, diff, {**shared, **ex}, loss_target)

    if N_MICROBATCH == 1:
        loss, (grad_w, grad_x) = one_microbatch(per_example, given["loss_target"])
    else:
        def body(carry, xs):
            loss_sum, grad_sum = carry
            l_k, (gw_k, gx_k) = one_microbatch(xs[0], xs[1])
            with _jax.named_scope("update"):
                return (loss_sum + l_k, _jax.tree.map(_jnp.add, grad_sum, gw_k)), gx_k

        init = (_jnp.zeros((), _jnp.float32), _jax.tree.map(_jnp.zeros_like, weights))
        (loss, grad_w), grad_x = _jax.lax.scan(body, init, (per_example, given["loss_target"]))
    with _jax.named_scope("update"):
        delta_w, new_m, new_v = {}, {}, {}
        for n in TWIN_WEIGHTS:
            delta_w[n], new_m[n], new_v[n] = _adamw(weights[n], grad_w[n], given["m_" + n], given["v_" + n])
    return (loss, grad_x, *[grad_w[n] for n in TWIN_WEIGHTS], *[delta_w[n] for n in TWIN_WEIGHTS],
            *[new_m[n] for n in TWIN_WEIGHTS], *[new_v[n] for n in TWIN_WEIGHTS])
```

```python
import functools
import math

import jax
import jax.numpy as jnp
from jax import lax
from jax.experimental import pallas as pl
from jax.experimental.pallas import tpu as pltpu

F32 = jnp.float32
BF16 = jnp.bfloat16
SDS = jax.ShapeDtypeStruct
MESH = pl.DeviceIdType.MESH
AXES = ("x", "y", "c")
NDEV = 8

RMS_EPS = 1e-6
SSM_GROUP = 16
SSM_STATE = 64
SSM_T = 8
SSM_W = SSM_T * SSM_GROUP
HEAD_DIM = 64
GQA_GROUP = 4
ATTN_BLOCK = 128
ROT_DIM = 16
ROPE_THETA = 500000.0
NEG_INF = -1e30
ADAM_LR, ADAM_B1, ADAM_B2, ADAM_EPS, ADAM_WD, ADAM_STEP = 0.001, 0.9, 0.999, 1e-08, 0.01, 10

VMEM_CAP = 56 * 1024 * 1024
HI = lax.Precision.HIGHEST

NN = ((1,), (0,))
NT = ((1,), (1,))
TN = ((0,), (0,))


def _dot(a, b, dims=NN, precision=None):
    return lax.dot_general(a, b, (dims, ((), ())), preferred_element_type=F32, precision=precision)


def _tile(n, pref):
    t = min(n, pref)
    while n % t:
        t //= 2
    return t


def _nbytes(shape, dtype):
    return math.prod(s for s in shape if s is not None) * jnp.dtype(dtype).itemsize


def _vmem_limit(blocks, extra=0):
    need = sum(_nbytes(s, d) * n for s, d, n in blocks) + extra + (4 << 20)
    return int(min(VMEM_CAP, max(need, 16 << 20)))


def _pcall(body, *, name, out_shape, grid, in_specs, out_specs, scratch=(), aliases=None, vmem=None):
    return pl.pallas_call(
        body, out_shape=out_shape, grid=grid, in_specs=in_specs, out_specs=out_specs,
        scratch_shapes=scratch, input_output_aliases=aliases or {}, name=name,
        compiler_params=pltpu.CompilerParams(
            dimension_semantics=("arbitrary",) * len(grid), vmem_limit_bytes=vmem),
        interpret=False)


def _rms(x, g):
    r = lax.rsqrt(jnp.mean(x * x, axis=-1, keepdims=True) + RMS_EPS)
    return x * r * g, r


def _rms_bwd(x, g, r, dy):
    xh = x * r
    dyg = dy * g
    dx = r * (dyg - xh * jnp.mean(dyg * xh, axis=-1, keepdims=True))
    return dx, jnp.sum(dy * xh, axis=0, keepdims=True)


_GELU_C = math.sqrt(2.0 / math.pi)


def _gelu_parts(x):
    t = jnp.tanh(_GELU_C * (x + 0.044715 * x * x * x))
    return 0.5 * x * (1.0 + t), t


def _gelu_grad(x, t):
    return 0.5 * (1.0 + t) + 0.5 * x * (1.0 - t * t) * _GELU_C * (1.0 + 3 * 0.044715 * x * x)


def _rope_tables(seqlen):
    half = ROT_DIM // 2
    inv = ROPE_THETA ** (-jnp.arange(0, ROT_DIM, 2, dtype=F32) / ROT_DIM)
    ang = jnp.arange(seqlen, dtype=jnp.int32).astype(F32)[:, None] * inv[None, :]
    cos, sin = jnp.cos(ang), jnp.sin(ang)
    zeros = jnp.zeros((seqlen, HEAD_DIM - ROT_DIM), F32)
    zh = jnp.zeros((seqlen, half), F32)
    c = jnp.concatenate([cos, cos, zeros + 1.0], axis=1)
    sa = jnp.concatenate([zh, sin, zeros], axis=1)
    sb = jnp.concatenate([-sin, zh, zeros], axis=1)
    return tuple(jnp.tile(t, (1, 128 // HEAD_DIM)) for t in (c, sa, sb))


def _rope(x, c, sa, sb):
    w = x.shape[1]
    reps = w // 128
    half = ROT_DIM // 2
    return (x * jnp.tile(c, (1, reps)) + pltpu.roll(x, half, 1) * jnp.tile(sa, (1, reps))
            + pltpu.roll(x, w - half, 1) * jnp.tile(sb, (1, reps)))


def _rope_bwd(dy, c, sa, sb):
    w = dy.shape[1]
    reps = w // 128
    half = ROT_DIM // 2
    return (dy * jnp.tile(c, (1, reps)) + pltpu.roll(dy * jnp.tile(sa, (1, reps)), w - half, 1)
            + pltpu.roll(dy * jnp.tile(sb, (1, reps)), half, 1))


def _rspec(tm, c):
    return pl.BlockSpec((tm, c), lambda i: (i, 0))


def _cspec(shape, idx=None):
    idx = tuple(idx) if idx is not None else (0,) * len(shape)
    return pl.BlockSpec(tuple(shape), lambda i: idx, pipeline_mode=pl.Buffered(1))


def _rowcall(body, name, seqlen, tm, rows_in, consts_in, rows_out, acc_out=(), extra_vmem=0):
    in_specs = [_rspec(tm, a.shape[1]) for a in rows_in] + [_cspec(bs, ix) for _, bs, ix in consts_in]
    out_shape = [SDS((seqlen, c), d) for c, d in rows_out] + [SDS(s, F32) for s in acc_out]
    out_specs = [_rspec(tm, c) for c, _ in rows_out] + [pl.BlockSpec(s, lambda i: (0, 0)) for s in acc_out]
    blocks = ([((tm, a.shape[1]), a.dtype, 2) for a in rows_in] + [(bs, a.dtype, 1) for a, bs, _ in consts_in]
              + [((tm, c), d, 2) for c, d in rows_out])
    temporaries = 12 * tm * rows_in[0].shape[1] * 4
    out = _pcall(body, name=name, out_shape=out_shape, grid=(seqlen // tm,), in_specs=in_specs,
                 out_specs=out_specs, vmem=_vmem_limit(blocks, extra_vmem + temporaries))(
                     *rows_in, *[a for a, _, _ in consts_in])
    return out


def _whole(a):
    return (a, a.shape, None)


def _norm_fwd(h, g, name):
    seqlen, d = h.shape
    tm = _tile(seqlen, 1024)

    def body(h_ref, g_ref, o_ref):
        o_ref[...] = _rms(h_ref[...], g_ref[...])[0]

    return _rowcall(body, name, seqlen, tm, [h], [_whole(g.reshape(1, d))], [(d, F32)])[0]


def _norm_bwd(h, g, dy1, dy2, dres, name):
    seqlen, d = h.shape
    tm = _tile(seqlen, 512)

    def body(h_ref, dy1_ref, dy2_ref, dres_ref, g_ref, dh_ref, dg_ref):
        @pl.when(pl.program_id(0) == 0)
        def _():
            dg_ref[...] = jnp.zeros_like(dg_ref)
        x = h_ref[...]
        gv = g_ref[...]
        _, r = _rms(x, gv)
        dx, dg = _rms_bwd(x, gv, r, dy1_ref[...] + dy2_ref[...])
        dh_ref[...] = dres_ref[...] + dx
        dg_ref[...] += dg

    return _rowcall(body, name, seqlen, tm, [h, dy1, dy2, dres], [_whole(g.reshape(1, d))], [(d, F32)], [(1, d)])


def _glu_fwd(y, hn, h, dskip, wglu, name):
    seqlen, d = h.shape
    tm = _tile(seqlen, 512)

    def body(y_ref, hn_ref, h_ref, d_ref, w_ref, o_ref):
        yy = y_ref[...] + d_ref[...] * hn_ref[...]
        ge, _ = _gelu_parts(yy)
        ab = _dot(ge.astype(BF16), w_ref[...])
        o_ref[...] = h_ref[...] + ab[:, :d] * jax.nn.sigmoid(ab[:, d:])

    return _rowcall(body, name, seqlen, tm, [y, hn, h], [_whole(dskip.reshape(1, d)), _whole(wglu)], [(d, F32)],
                    extra_vmem=tm * d * 4 * 6)[0]


def _glu_bwd(y, hn, dmix, dskip, wglu, name):
    seqlen, d = hn.shape
    tm = _tile(seqlen, 512)

    def body(y_ref, hn_ref, dm_ref, d_ref, w_ref, dyy_ref, dhn_ref, ge_ref, dab_ref, dd_ref):
        @pl.when(pl.program_id(0) == 0)
        def _():
            dd_ref[...] = jnp.zeros_like(dd_ref)
        hn_v = hn_ref[...]
        dsk = d_ref[...]
        yy = y_ref[...] + dsk * hn_v
        ge, t = _gelu_parts(yy)
        geb = ge.astype(BF16)
        ab = _dot(geb, w_ref[...])
        a = ab[:, :d]
        sg = jax.nn.sigmoid(ab[:, d:])
        dm = dm_ref[...]
        dab_ref[:, :d] = (dm * sg).astype(BF16)
        dab_ref[:, d:] = (dm * a * sg * (1.0 - sg)).astype(BF16)
        dge = _dot(dab_ref[...], w_ref[...], NT)
        dyy = dge * _gelu_grad(yy, t)
        dyy_ref[...] = dyy
        dhn_ref[...] = dyy * dsk
        ge_ref[...] = geb
        dd_ref[...] += jnp.sum(dyy * hn_v, axis=0, keepdims=True)

    return _rowcall(body, name, seqlen, tm, [y, hn, dmix], [_whole(dskip.reshape(1, d)), _whole(wglu)],
                    [(d, F32), (d, F32), (d, BF16), (2 * d, BF16)], [(1, d)], extra_vmem=tm * d * 4 * 8)


def _q_fwd(h, g, wq, tabs, name):
    seqlen, d = h.shape
    tm = _tile(seqlen, 512)

    def body(h_ref, c_ref, sa_ref, sb_ref, g_ref, w_ref, q_ref):
        hn, _ = _rms(h_ref[...], g_ref[...])
        qp = _dot(hn.astype(BF16), w_ref[...])
        q_ref[...] = _rope(qp, c_ref[...], sa_ref[...], sb_ref[...]).astype(BF16)

    return _rowcall(body, name, seqlen, tm, [h, *tabs], [_whole(g.reshape(1, d)), _whole(wq)], [(d, BF16)],
                    extra_vmem=tm * d * 4 * 6)[0]


def _q_bwd(dq, h, g, dres, wq, tabs, name):
    seqlen, d = h.shape
    tm = _tile(seqlen, 512)

    def body(dq_ref, h_ref, dres_ref, c_ref, sa_ref, sb_ref, g_ref, w_ref, dh_ref, dqp_ref, hn_ref, dg_ref):
        @pl.when(pl.program_id(0) == 0)
        def _():
            dg_ref[...] = jnp.zeros_like(dg_ref)
        dqp = _rope_bwd(dq_ref[...], c_ref[...], sa_ref[...], sb_ref[...]).astype(BF16)
        x = h_ref[...]
        gv = g_ref[...]
        hn, r = _rms(x, gv)
        dhn = _dot(dqp, w_ref[...], NT)
        dx, dg = _rms_bwd(x, gv, r, dhn)
        dh_ref[...] = dres_ref[...] + dx
        dqp_ref[...] = dqp
        hn_ref[...] = hn.astype(BF16)
        dg_ref[...] += dg

    return _rowcall(body, name, seqlen, tm, [dq, h, dres, *tabs], [_whole(g.reshape(1, d)), _whole(wq)],
                    [(d, F32), (d, BF16), (d, BF16)], [(1, d)], extra_vmem=tm * d * 4 * 6)


def _kv_fwd(h, g, wk, wv, tabs, name):
    seqlen, d = h.shape
    dk = wk.shape[1]
    tm = _tile(seqlen, 512)

    def body(h_ref, c_ref, sa_ref, sb_ref, g_ref, wk_ref, wv_ref, k_ref, v_ref):
        hk = _rms(h_ref[...], g_ref[...])[0].astype(BF16)
        k_ref[...] = _rope(_dot(hk, wk_ref[...]), c_ref[...], sa_ref[...], sb_ref[...]).astype(BF16)
        v_ref[...] = _dot(hk, wv_ref[...]).astype(BF16)

    return _rowcall(body, name, seqlen, tm, [h, *tabs], [_whole(g.reshape(1, d)), _whole(wk), _whole(wv)],
                    [(dk, BF16), (dk, BF16)], extra_vmem=tm * d * 4 * 4)


def _kv_bwd(dks, dvs, h, g, dres, wk, wv, tabs, name):
    seqlen, d = h.shape
    dkw = wk.shape[1]
    tm = _tile(seqlen, 512)

    def body(dk0_ref, dk1_ref, dv0_ref, dv1_ref, h_ref, dres_ref, c_ref, sa_ref, sb_ref, g_ref, wk_ref, wv_ref,
             dh_ref, dkp_ref, dvb_ref, hk_ref, dg_ref):
        @pl.when(pl.program_id(0) == 0)
        def _():
            dg_ref[...] = jnp.zeros_like(dg_ref)
        dkp = _rope_bwd(dk0_ref[...] + dk1_ref[...], c_ref[...], sa_ref[...], sb_ref[...]).astype(BF16)
        dvb = (dv0_ref[...] + dv1_ref[...]).astype(BF16)
        x = h_ref[...]
        gv = g_ref[...]
        hk, r = _rms(x, gv)
        dhk = _dot(dkp, wk_ref[...], NT) + _dot(dvb, wv_ref[...], NT)
        dx, dg = _rms_bwd(x, gv, r, dhk)
        dh_ref[...] = dres_ref[...] + dx
        dkp_ref[...] = dkp
        dvb_ref[...] = dvb
        hk_ref[...] = hk.astype(BF16)
        dg_ref[...] += dg

    return _rowcall(body, name, seqlen, tm, [dks[0], dks[1], dvs[0], dvs[1], h, dres, *tabs],
                    [_whole(g.reshape(1, d)), _whole(wk), _whole(wv)],
                    [(d, F32), (dkw, BF16), (dkw, BF16), (d, BF16)], [(1, d)], extra_vmem=tm * d * 4 * 6)


def _lin_res(h, xb, w, name):
    seqlen, d = h.shape
    tm = _tile(seqlen, 512)

    def body(h_ref, x_ref, w_ref, o_ref):
        o_ref[...] = h_ref[...] + _dot(x_ref[...], w_ref[...])

    return _rowcall(body, name, seqlen, tm, [h, xb], [_whole(w)], [(d, F32)], extra_vmem=tm * d * 4 * 2)[0]


def _lin_nt(dy, w, name):
    seqlen, d = dy.shape
    tm = _tile(seqlen, 512)

    def body(dy_ref, w_ref, o_ref):
        o_ref[...] = _dot(dy_ref[...].astype(BF16), w_ref[...], NT).astype(BF16)

    return _rowcall(body, name, seqlen, tm, [dy], [_whole(w)], [(w.shape[0], BF16)], extra_vmem=tm * d * 4 * 2)[0]


def _mlp_fwd(h, g, wup_g, wdn_g, layer, name):
    seqlen, d = h.shape
    nb, _, _, fb = wup_g.shape
    tm = _tile(seqlen, 512)

    def body(h_ref, g_ref, wup_ref, wdn_ref, o_ref):
        x = h_ref[...]
        hm = _rms(x, g_ref[...])[0].astype(BF16)
        acc = x
        for j in range(nb):
            r = jnp.maximum(_dot(hm, wup_ref[j]), 0.0)
            acc = acc + _dot((r * r).astype(BF16), wdn_ref[j])
        o_ref[...] = acc

    consts = [_whole(g.reshape(1, d)), (wup_g, (nb, None, d, fb), (0, layer, 0, 0)),
              (wdn_g, (nb, None, fb, d), (0, layer, 0, 0))]
    return _rowcall(body, name, seqlen, tm, [h], consts, [(d, F32)], extra_vmem=tm * (d + fb) * 4 * 4)[0]


def _mlp_bwd(h, dh, g, wup_g, wdn_g, layer, name):
    seqlen, d = h.shape
    nb, _, _, fb = wup_g.shape
    tm = _tile(seqlen, 256)

    def body(h_ref, dh_ref, g_ref, wup_ref, wdn_ref, dhin_ref, hm_ref, da_ref, act_ref, dg_ref):
        @pl.when(pl.program_id(0) == 0)
        def _():
            dg_ref[...] = jnp.zeros_like(dg_ref)
        x = h_ref[...]
        gv = g_ref[...]
        dy = dh_ref[...]
        hm, r = _rms(x, gv)
        hmb = hm.astype(BF16)
        dyb = dy.astype(BF16)
        dhm = jnp.zeros_like(x)
        for j in range(nb):
            rl = jnp.maximum(_dot(hmb, wup_ref[j]), 0.0)
            act_ref[:, j * fb:(j + 1) * fb] = (rl * rl).astype(BF16)
            da = (_dot(dyb, wdn_ref[j], NT) * (2.0 * rl)).astype(BF16)
            da_ref[:, j * fb:(j + 1) * fb] = da
            dhm = dhm + _dot(da, wup_ref[j], NT)
        dx, dg = _rms_bwd(x, gv, r, dhm)
        dhin_ref[...] = dy + dx
        hm_ref[...] = hmb
        dg_ref[...] += dg

    consts = [_whole(g.reshape(1, d)), (wup_g, (nb, None, d, fb), (0, layer, 0, 0)),
              (wdn_g, (nb, None, fb, d), (0, layer, 0, 0))]
    return _rowcall(body, name, seqlen, tm, [h, dh], consts,
                    [(d, F32), (d, BF16), (nb * fb, BF16), (nb * fb, BF16)], [(1, d)],
                    extra_vmem=tm * (d + fb) * 4 * 6)


def _ple_fwd(h, p, g, wg, wpp, name):
    seqlen, d = h.shape
    tm = _tile(seqlen, 512)

    def body(h_ref, p_ref, g_ref, wg_ref, wpp_ref, o_ref):
        x = h_ref[...]
        n = _rms(x, g_ref[...])[0].astype(BF16)
        gate = jax.nn.sigmoid(_dot(n, wg_ref[...]))
        o_ref[...] = x + gate * _dot(p_ref[...].astype(BF16), wpp_ref[...])

    return _rowcall(body, name, seqlen, tm, [h, p], [_whole(g.reshape(1, d)), _whole(wg), _whole(wpp)], [(d, F32)],
                    extra_vmem=tm * d * 4 * 5)[0]


def _ple_bwd(h, p, dh, g, wg, wpp, name):
    seqlen, d = h.shape
    tm = _tile(seqlen, 512)

    def body(h_ref, p_ref, dh_ref, g_ref, wg_ref, wpp_ref, dhin_ref, dz_ref, n_ref, dpp_ref, dg_ref):
        @pl.when(pl.program_id(0) == 0)
        def _():
            dg_ref[...] = jnp.zeros_like(dg_ref)
        x = h_ref[...]
        gv = g_ref[...]
        dy = dh_ref[...]
        n, r = _rms(x, gv)
        nb16 = n.astype(BF16)
        gate = jax.nn.sigmoid(_dot(nb16, wg_ref[...]))
        pp = _dot(p_ref[...].astype(BF16), wpp_ref[...])
        dz = (dy * pp * gate * (1.0 - gate)).astype(BF16)
        dn = _dot(dz, wg_ref[...], NT)
        dx, dg = _rms_bwd(x, gv, r, dn)
        dhin_ref[...] = dy + dx
        dz_ref[...] = dz
        n_ref[...] = nb16
        dpp_ref[...] = (dy * gate).astype(BF16)
        dg_ref[...] += dg

    return _rowcall(body, name, seqlen, tm, [h, p, dh], [_whole(g.reshape(1, d)), _whole(wg), _whole(wpp)],
                    [(d, F32), (d, BF16), (d, BF16), (d, BF16)], [(1, d)], extra_vmem=tm * d * 4 * 8)


def _loss_bwd(h, g, tgt, name):
    seqlen, d = h.shape
    tm = _tile(seqlen, 512)

    def body(h_ref, t_ref, g_ref, dh_ref, loss_ref, dg_ref):
        @pl.when(pl.program_id(0) == 0)
        def _():
            dg_ref[...] = jnp.zeros_like(dg_ref)
            loss_ref[...] = jnp.zeros_like(loss_ref)
        x = h_ref[...]
        gv = g_ref[...]
        y, r = _rms(x, gv)
        diff = y - t_ref[...]
        loss_ref[...] += (0.5 / d) * jnp.sum(jnp.sum(diff * diff, axis=1, keepdims=True), axis=0, keepdims=True)
        dx, dg = _rms_bwd(x, gv, r, diff * (1.0 / d))
        dh_ref[...] = dx
        dg_ref[...] += dg

    return _rowcall(body, name, seqlen, tm, [h, tgt], [_whole(g.reshape(1, d))], [(d, F32)], [(1, 128), (1, d)])


def _atb(a, b, stack, layer, col_blocked, name):
    seqlen, k1 = a.shape
    k2 = b.shape[1]
    nl = stack.shape[1]
    if col_blocked:
        cs = stack.shape[3]
        t1 = _tile(k1, 512)
        nblk = _tile(NDEV, max(1, 2048 // cs))
        t2 = nblk * cs
        oblock = (nblk, None, t1, cs)
        omap = lambda i, j, l: (j, layer, i, 0)
    else:
        rs = stack.shape[2]
        t2 = _tile(k2, 2048)
        nblk = _tile(NDEV, max(1, 512 // rs))
        t1 = nblk * rs
        oblock = (nblk, None, rs, t2)
        omap = lambda i, j, l: (i, layer, 0, j)
    tl = _tile(seqlen, 1024 if b.dtype == BF16 else 512)

    def body(a_ref, b_ref, _, o_ref):
        @pl.when(pl.program_id(2) == 0)
        def _():
            o_ref[...] = jnp.zeros_like(o_ref)
        res = _dot(a_ref[...].astype(BF16), b_ref[...].astype(BF16), TN)
        for n in range(nblk):
            if col_blocked:
                o_ref[n] += res[:, n * cs:(n + 1) * cs]
            else:
                o_ref[n] += res[n * rs:(n + 1) * rs, :]

    blocks = [((tl, t1), a.dtype, 2), ((tl, t2), b.dtype, 2), ((t1, t2), F32, 2)]
    return _pcall(
        body, name=name, out_shape=SDS(stack.shape, F32), grid=(k1 // t1, k2 // t2, seqlen // tl),
        in_specs=[pl.BlockSpec((tl, t1), lambda i, j, l: (l, i)), pl.BlockSpec((tl, t2), lambda i, j, l: (l, j)),
                  pl.BlockSpec(memory_space=pl.ANY)],
        out_specs=pl.BlockSpec(oblock, omap), aliases={2: 0},
        vmem=_vmem_limit(blocks, extra=t1 * t2 * 4 + tl * (t1 + t2) * 2))(a, b, stack)


def _attn_probs(q4, kk, sink_col, has_prev):
    rows = q4.shape[0]
    s = _dot(q4, kk, NT) * (HEAD_DIM ** -0.5)
    qi = lax.broadcasted_iota(jnp.int32, (rows, 2 * ATTN_BLOCK), 0) % ATTN_BLOCK + ATTN_BLOCK
    kj = lax.broadcasted_iota(jnp.int32, (rows, 2 * ATTN_BLOCK), 1)
    mask = (kj <= qi) & (qi - kj < ATTN_BLOCK) & ((kj >= ATTN_BLOCK) | has_prev)
    s = jnp.where(mask, s, NEG_INF)
    m = jnp.maximum(jnp.max(s, axis=1, keepdims=True), sink_col)
    pr = jnp.exp(s - m)
    es = jnp.exp(sink_col - m)
    inv = 1.0 / (jnp.sum(pr, axis=1, keepdims=True) + es)
    return pr * inv, es * inv


def _sink_col(sink_ref, kh):
    return jnp.concatenate(
        [jnp.full((ATTN_BLOCK, 1), sink_ref[kh * GQA_GROUP + g], F32) for g in range(GQA_GROUP)], axis=0)


def _stack_heads(ref, kh):
    return jnp.concatenate(
        [ref[:, (kh * GQA_GROUP + g) * HEAD_DIM:(kh * GQA_GROUP + g + 1) * HEAD_DIM] for g in range(GQA_GROUP)], axis=0)


def _attn_fwd(q, k, v, sinks, name):
    seqlen, d = q.shape
    dkv = k.shape[1]
    nkv = dkv // HEAD_DIM
    nb = seqlen // ATTN_BLOCK
    blk = ATTN_BLOCK

    def body(sink_ref, q_ref, kc_ref, kp_ref, vc_ref, vp_ref, o_ref):
        has_prev = pl.program_id(0) > 0
        for kh in range(nkv):
            sl = slice(kh * HEAD_DIM, (kh + 1) * HEAD_DIM)
            kk = jnp.concatenate([kp_ref[:, sl], kc_ref[:, sl]], axis=0)
            vv = jnp.concatenate([vp_ref[:, sl], vc_ref[:, sl]], axis=0)
            w, _ = _attn_probs(_stack_heads(q_ref, kh), kk, _sink_col(sink_ref, kh), has_prev)
            o4 = _dot(w.astype(BF16), vv)
            for g in range(GQA_GROUP):
                hq = kh * GQA_GROUP + g
                o_ref[:, hq * HEAD_DIM:(hq + 1) * HEAD_DIM] = o4[g * blk:(g + 1) * blk, :].astype(BF16)

    cur = lambda n: (n, 0)
    prev = lambda n: (jnp.maximum(n - 1, 0), 0)
    return _pcall(
        body, name=name, out_shape=SDS((seqlen, d), BF16), grid=(nb,),
        in_specs=[pl.BlockSpec(memory_space=pltpu.SMEM), pl.BlockSpec((blk, d), cur),
                  pl.BlockSpec((blk, dkv), cur), pl.BlockSpec((blk, dkv), prev),
                  pl.BlockSpec((blk, dkv), cur), pl.BlockSpec((blk, dkv), prev)],
        out_specs=pl.BlockSpec((blk, d), cur), vmem=32 << 20)(sinks, q, k, k, v, v)


def _attn_bwd(q, k, v, do, sinks, name):
    seqlen, d = q.shape
    dkv = k.shape[1]
    nkv = dkv // HEAD_DIM
    nh = d // HEAD_DIM
    nb = seqlen // ATTN_BLOCK
    blk = ATTN_BLOCK
    scale = HEAD_DIM ** -0.5

    def body(sink_ref, q_ref, kc_ref, kp_ref, vc_ref, vp_ref, do_ref, dq_ref, dk_ref, dv_ref, ds_ref,
             ck_ref, cv_ref):
        n = pl.program_id(0)

        @pl.when(n == 0)
        def _():
            ck_ref[...] = jnp.zeros_like(ck_ref)
            cv_ref[...] = jnp.zeros_like(cv_ref)
            ds_ref[...] = jnp.zeros_like(ds_ref)

        @pl.when(n == nb)
        def _():
            dk_ref[...] = ck_ref[...]
            dv_ref[...] = cv_ref[...]

        @pl.when(n < nb)
        def _():
            has_prev = n > 0
            for kh in range(nkv):
                sl = slice(kh * HEAD_DIM, (kh + 1) * HEAD_DIM)
                kk = jnp.concatenate([kp_ref[:, sl], kc_ref[:, sl]], axis=0)
                vv = jnp.concatenate([vp_ref[:, sl], vc_ref[:, sl]], axis=0)
                q4 = _stack_heads(q_ref, kh)
                do4 = _stack_heads(do_ref, kh)
                w, wsink = _attn_probs(q4, kk, _sink_col(sink_ref, kh), has_prev)
                dw = _dot(do4, vv, NT)
                dsum = jnp.sum(w * dw, axis=1, keepdims=True)
                ds = (w * (dw - dsum) * scale).astype(BF16)
                dq4 = _dot(ds, kk)
                dkk = _dot(ds, q4, TN)
                dvv = _dot(w.astype(BF16), do4, TN)
                dsk = -wsink * dsum
                for g in range(GQA_GROUP):
                    hq = kh * GQA_GROUP + g
                    dq_ref[:, hq * HEAD_DIM:(hq + 1) * HEAD_DIM] = dq4[g * blk:(g + 1) * blk, :]
                    ds_ref[hq:hq + 1, :] += jnp.sum(dsk[g * blk:(g + 1) * blk, :], axis=0, keepdims=True)
                dk_ref[:, sl] = ck_ref[:, sl] + dkk[:blk, :]
                dv_ref[:, sl] = cv_ref[:, sl] + dvv[:blk, :]
                ck_ref[:, sl] = dkk[blk:, :]
                cv_ref[:, sl] = dvv[blk:, :]

    cur = lambda n: (jnp.minimum(n, nb - 1), 0)
    prev = lambda n: (jnp.clip(n - 1, 0, nb - 1), 0)
    lag = lambda n: (jnp.maximum(n - 1, 0), 0)
    return _pcall(
        body, name=name,
        out_shape=[SDS((seqlen, d), F32), SDS((seqlen, dkv), F32), SDS((seqlen, dkv), F32), SDS((nh, 128), F32)],
        grid=(nb + 1,),
        in_specs=[pl.BlockSpec(memory_space=pltpu.SMEM), pl.BlockSpec((blk, d), cur),
                  pl.BlockSpec((blk, dkv), cur), pl.BlockSpec((blk, dkv), prev),
                  pl.BlockSpec((blk, dkv), cur), pl.BlockSpec((blk, dkv), prev), pl.BlockSpec((blk, d), cur)],
        out_specs=[pl.BlockSpec((blk, d), cur), pl.BlockSpec((blk, dkv), lag), pl.BlockSpec((blk, dkv), lag),
                   pl.BlockSpec((nh, 128), lambda n: (0, 0))],
        scratch=[pltpu.VMEM((blk, dkv), F32)] * 2, vmem=32 << 20)(sinks, q, k, k, v, v, do)


def _ssm_mats(lre, lim, ldt, btr, bti, cr, ci):
    dt = jnp.exp(ldt)
    mag = jnp.exp(lre * dt)
    ar = mag * jnp.cos(lim * dt)
    ai = mag * jnp.sin(lim * dt)
    den = lre * lre + lim * lim
    nr = ar - 1.0
    cfr = (nr * lre + ai * lim) / den
    cfi = (ai * lre - nr * lim) / den
    bbr = cfr * btr - cfi * bti
    bbi = cfr * bti + cfi * btr
    pr = [jnp.ones_like(ar)]
    pi = [jnp.zeros_like(ai)]
    for _ in range(SSM_T):
        pr.append(pr[-1] * ar - pi[-1] * ai)
        pi.append(pr[-2] * ai + pi[-1] * ar)
    last = SSM_T - 1
    p_re = jnp.concatenate([pr[last - s] * bbr - pi[last - s] * bbi for s in range(SSM_T)], axis=0)
    p_im = jnp.concatenate([pr[last - s] * bbi + pi[last - s] * bbr for s in range(SSM_T)], axis=0)
    qt_re = jnp.concatenate([pr[t + 1] * cr - pi[t + 1] * ci for t in range(SSM_T)], axis=0)
    qt_im = jnp.concatenate([-(pr[t + 1] * ci + pi[t + 1] * cr) for t in range(SSM_T)], axis=0)
    ctr = jnp.concatenate([cr] * SSM_T, axis=0)
    cti = jnp.concatenate([ci] * SSM_T, axis=0)
    lag = (lax.broadcasted_iota(jnp.int32, (SSM_W, SSM_W), 1) // SSM_GROUP
           - lax.broadcasted_iota(jnp.int32, (SSM_W, SSM_W), 0) // SSM_GROUP)
    m = jnp.zeros((SSM_W, SSM_W), F32)
    for l in range(SSM_T):
        zr = jnp.concatenate([pr[l] * bbr - pi[l] * bbi] * SSM_T, axis=0)
        zi = jnp.concatenate([pr[l] * bbi + pi[l] * bbr] * SSM_T, axis=0)
        kl = _dot(zr, ctr, NT, HI) - _dot(zi, cti, NT, HI)
        m = m + jnp.where(lag == l, kl, 0.0)
    return m, p_re, p_im, qt_re, qt_im, pr[SSM_T], pi[SSM_T]


_SSM_GB = 8


def _ssm_param_specs(ng):
    n, hh = SSM_STATE, SSM_GROUP
    gb = _tile(ng, _SSM_GB)
    row = pl.BlockSpec((gb, 1, n), lambda i: (i, 0, 0))
    one = pl.BlockSpec((gb, 1, 1), lambda i: (i, 0, 0))
    mat = pl.BlockSpec((gb, hh, n), lambda i: (i, 0, 0))
    big = pl.BlockSpec((gb, SSM_W, SSM_W), lambda i: (i, 0, 0))
    half = pl.BlockSpec((gb, SSM_W, n), lambda i: (i, 0, 0))
    return gb, row, one, mat, big, half


def _ssm_prep(params, name):
    ng = params[0].shape[0]
    n = SSM_STATE
    gb, row, one, mat, big, half = _ssm_param_specs(ng)

    def body(lre, lim, ldt, btr, bti, cr, ci, m_ref, pre_ref, pim_ref, qre_ref, qim_ref, atr_ref, ati_ref):
        for gi in range(gb):
            outs = _ssm_mats(lre[gi], lim[gi], ldt[gi], btr[gi], bti[gi], cr[gi], ci[gi])
            for ref, val in zip((m_ref, pre_ref, pim_ref, qre_ref, qim_ref, atr_ref, ati_ref), outs):
                ref[gi] = val

    return _pcall(
        body, name=name,
        out_shape=[SDS((ng, SSM_W, SSM_W), F32)] + [SDS((ng, SSM_W, n), F32)] * 4 + [SDS((ng, 1, n), F32)] * 2,
        grid=(ng // gb,), in_specs=[row, row, one, mat, mat, mat, mat],
        out_specs=[big, half, half, half, half, row, row], vmem=40 << 20)(*params)


def _ssm_prep_vjp(params, cots, name):
    ng = params[0].shape[0]
    n, hh = SSM_STATE, SSM_GROUP
    gb, row, one, mat, big, half = _ssm_param_specs(ng)

    def body(lre, lim, ldt, btr, bti, cr, ci, dm, dpre, dpim, dqre, dqim, datr, dati,
             o_lre, o_lim, o_ldt, o_btr, o_bti, o_cr, o_ci):
        for gi in range(gb):
            prm = (lre[gi], lim[gi], ldt[gi], btr[gi], bti[gi], cr[gi], ci[gi])
            _, pull = jax.vjp(_ssm_mats, *prm)
            grads = pull((dm[gi], dpre[gi], dpim[gi], dqre[gi], dqim[gi], datr[gi], dati[gi]))
            for ref, val in zip((o_lre, o_lim, o_ldt, o_btr, o_bti, o_cr, o_ci), grads):
                ref[gi] = val

    return _pcall(
        body, name=name,
        out_shape=[SDS((ng, 1, n), F32)] * 2 + [SDS((ng, 1, 1), F32)] + [SDS((ng, hh, n), F32)] * 4,
        grid=(ng // gb,), in_specs=[row, row, one, mat, mat, mat, mat, big, half, half, half, half, row, row],
        out_specs=[row, row, one, mat, mat, mat, mat], vmem=48 << 20)(*params, *cots)


def _gspec(nc):
    return pl.BlockSpec((None, nc, SSM_W), lambda g: (g, 0, 0))


def _mspec():
    return pl.BlockSpec((None, SSM_W, SSM_W), lambda g: (g, 0, 0))


def _lspec(nc):
    return pl.BlockSpec((nc, SSM_W), lambda g: (0, g))


def _ssm_state_in(u, pmat, name):
    ng, nc, _ = u.shape

    def body(u_ref, p_ref, s_ref):
        s_ref[...] = _dot(u_ref[...], p_ref[...], NN, HI)

    return _pcall(body, name=name, out_shape=SDS((nc, ng * SSM_W), F32), grid=(ng,),
                  in_specs=[_gspec(nc), _mspec()], out_specs=_lspec(nc), vmem=32 << 20)(u, pmat)


def _ssm_out(u, xp, mmat, qt, name):
    ng, nc, _ = u.shape

    def body(u_ref, x_ref, m_ref, q_ref, y_ref):
        y_ref[...] = _dot(u_ref[...], m_ref[...], NN, HI) + _dot(x_ref[...], q_ref[...], NT, HI)

    return _pcall(body, name=name, out_shape=SDS((ng, nc, SSM_W), F32), grid=(ng,),
                  in_specs=[_gspec(nc), _lspec(nc), _mspec(), _mspec()], out_specs=_gspec(nc),
                  vmem=32 << 20)(u, xp, mmat, qt)


def _ssm_dstate(dy, qt, name):
    ng, nc, _ = dy.shape

    def body(dy_ref, q_ref, o_ref):
        o_ref[...] = _dot(dy_ref[...], q_ref[...], NN, HI)

    return _pcall(body, name=name, out_shape=SDS((nc, ng * SSM_W), F32), grid=(ng,),
                  in_specs=[_gspec(nc), _mspec()], out_specs=_lspec(nc), vmem=32 << 20)(dy, qt)


def _ssm_bwd(u, dy, xp, gs, mmat, pmat, name):
    ng, nc, _ = u.shape

    def body(u_ref, dy_ref, x_ref, g_ref, m_ref, p_ref, du_ref, dm_ref, dp_ref, dq_ref, da_ref):
        uu, dyv, xv, gv = u_ref[...], dy_ref[...], x_ref[...], g_ref[...]
        du_ref[...] = _dot(dyv, m_ref[...], NT, HI) + _dot(gv, p_ref[...], NT, HI)
        dm_ref[...] = _dot(uu, dyv, TN, HI)
        dp_ref[...] = _dot(uu, gv, TN, HI)
        dq_ref[...] = _dot(dyv, xv, TN, HI)
        da_ref[0:1, :] = jnp.sum(xv * gv, axis=0, keepdims=True)
        da_ref[1:2, :] = jnp.sum(xv * pltpu.roll(gv, SSM_STATE, 1), axis=0, keepdims=True)

    return _pcall(
        body, name=name,
        out_shape=[SDS((ng, nc, SSM_W), F32)] + [SDS((ng, SSM_W, SSM_W), F32)] * 3 + [SDS((ng, 2, SSM_W), F32)],
        grid=(ng,), in_specs=[_gspec(nc), _gspec(nc), _lspec(nc), _lspec(nc), _mspec(), _mspec()],
        out_specs=[_gspec(nc), _mspec(), _mspec(), _mspec(), pl.BlockSpec((None, 2, SSM_W), lambda g: (g, 0, 0))],
        vmem=40 << 20)(u, dy, xp, gs, mmat, pmat)


def _ssm_carry(s, a1, a2, reverse, name):
    nc, w = s.shape
    tc = _tile(nc, 256)
    nblk = nc // tc
    sub = 8

    def body(s_ref, a1_ref, a2_ref, o_ref, st_ref, sw_ref):
        @pl.when(pl.program_id(0) == 0)
        def _():
            st_ref[...] = jnp.zeros_like(st_ref)
            sw_ref[...] = jnp.zeros_like(sw_ref)
        a1v = jnp.broadcast_to(a1_ref[...], (sub, w))
        a2v = jnp.broadcast_to(a2_ref[...], (sub, w))
        first = lax.broadcasted_iota(jnp.int32, (sub, w), 1) % SSM_W < SSM_STATE
        row = lax.broadcasted_iota(jnp.int32, (sub, w), 0)

        def step(t, carry):
            x, xs = carry
            tt = (tc // sub - 1 - t) if reverse else t
            base = pl.multiple_of(tt * sub, sub)
            blk = s_ref[pl.ds(base, sub), :]
            blks = jnp.where(first, pltpu.roll(blk, w - SSM_STATE, 1), pltpu.roll(blk, SSM_STATE, 1))
            out = jnp.zeros((sub, w), F32)
            for r in (range(sub - 1, -1, -1) if reverse else range(sub)):
                out = jnp.where(row == r, x, out)
                sr = jnp.broadcast_to(blk[r:r + 1, :], (sub, w))
                ssr = jnp.broadcast_to(blks[r:r + 1, :], (sub, w))
                x, xs = a1v * x + a2v * xs + sr, a1v * xs - a2v * x + ssr
            o_ref[pl.ds(base, sub), :] = out
            return x, xs

        x, xs = lax.fori_loop(0, tc // sub, step, (st_ref[...], sw_ref[...]))
        st_ref[...] = x
        sw_ref[...] = xs

    imap = (lambda i: (nblk - 1 - i, 0)) if reverse else (lambda i: (i, 0))
    cst = pl.BlockSpec((1, w), lambda i: (0, 0))
    return _pcall(body, name=name, out_shape=SDS((nc, w), F32), grid=(nblk,),
                  in_specs=[pl.BlockSpec((tc, w), imap), cst, cst], out_specs=pl.BlockSpec((tc, w), imap),
                  scratch=[pltpu.VMEM((sub, w), F32)] * 2,
                  vmem=_vmem_limit([((tc, w), F32, 4)], extra=8 << 20))(s, a1, a2)


def _to_groups(a):
    seqlen, d = a.shape
    ng = d // SSM_GROUP
    return a.reshape(seqlen // SSM_T, SSM_T, ng, SSM_GROUP).transpose(2, 0, 1, 3).reshape(ng, seqlen // SSM_T, SSM_W)


def _from_groups(a):
    ng, nc, _ = a.shape
    return a.reshape(ng, nc, SSM_T, SSM_GROUP).transpose(1, 2, 0, 3).reshape(nc * SSM_T, ng * SSM_GROUP)


def _ssm_rows(atr, ati, conj):
    ng = atr.shape[0]
    ai = -ati if conj else ati
    a1 = jnp.concatenate([atr, atr], axis=2).reshape(1, ng * SSM_W)
    a2 = jnp.concatenate([-ai, ai], axis=2).reshape(1, ng * SSM_W)
    return a1, a2


def _peers():
    x, y, c = (lax.axis_index(a) for a in AXES)
    me = 4 * x + 2 * y + c
    peers = []
    for dx, dy, dc in [(0, 0, 1), (0, 1, 0), (0, 1, 1), (1, 0, 0), (1, 0, 1), (1, 1, 0), (1, 1, 1)]:
        px, py, pc = (1 - x) if dx else x, (1 - y) if dy else y, (1 - c) if dc else c
        peers.append(((px, py, pc), 4 * px + 2 * py + pc))
    return me, peers


def _exchange(arrs, scatter, name):
    n = len(arrs)
    npeer = NDEV - 1
    out_shape = [SDS(a.shape if scatter else (NDEV,) + a.shape, a.dtype) for a in arrs]

    def body(*refs):
        ins, outs = refs[:n], refs[n:2 * n]
        ssem, rsem, lsem = refs[2 * n:]
        me, peers = _peers()
        local = []
        for a in range(n):
            cp = pltpu.make_async_copy(ins[a].at[me] if scatter else ins[a], outs[a].at[me], lsem.at[a])
            cp.start()
            local.append(cp)
        for k, (pid, pidx) in enumerate(peers):
            for a in range(n):
                pltpu.make_async_remote_copy(
                    src_ref=ins[a].at[pidx] if scatter else ins[a], dst_ref=outs[a].at[me],
                    send_sem=ssem.at[a * npeer + k], recv_sem=rsem.at[a * npeer + k],
                    device_id=pid, device_id_type=MESH).start()
        for cp in local:
            cp.wait()
        for k, (pid, pidx) in enumerate(peers):
            for a in range(n):
                cp = pltpu.make_async_remote_copy(
                    src_ref=ins[a].at[pidx] if scatter else ins[a], dst_ref=outs[a].at[pidx],
                    send_sem=ssem.at[a * npeer + k], recv_sem=rsem.at[a * npeer + k],
                    device_id=pid, device_id_type=MESH)
                cp.wait_send()
                cp.wait_recv()

    hbm = pl.BlockSpec(memory_space=pl.ANY)
    return pl.pallas_call(
        body, out_shape=out_shape, in_specs=[hbm] * n, out_specs=[hbm] * n,
        scratch_shapes=[pltpu.SemaphoreType.DMA((n * npeer,)), pltpu.SemaphoreType.DMA((n * npeer,)),
                        pltpu.SemaphoreType.DMA((n,))],
        name=name, interpret=False)(*arrs)


def _adamw(parts, w, m, v, name):
    rows, cols = w.shape
    tr = _tile(rows, max(8, (1 << 17) // cols))
    c1 = 1.0 - ADAM_B1 ** ADAM_STEP
    c2 = 1.0 - ADAM_B2 ** ADAM_STEP

    def body(p_ref, w_ref, m_ref, v_ref, g_ref, d_ref, nm_ref, nv_ref):
        g = p_ref[0]
        for j in range(1, NDEV):
            g = g + p_ref[j]
        mm = ADAM_B1 * m_ref[...] + (1.0 - ADAM_B1) * g
        vv = ADAM_B2 * v_ref[...] + (1.0 - ADAM_B2) * (g * g)
        g_ref[...] = g
        nm_ref[...] = mm
        nv_ref[...] = vv
        d_ref[...] = -ADAM_LR * ((mm / c1) / (jnp.sqrt(vv / c2) + ADAM_EPS) + ADAM_WD * w_ref[...])

    spec = pl.BlockSpec((tr, cols), lambda i: (i, 0))
    return _pcall(
        body, name=name, out_shape=[SDS((rows, cols), F32)] * 4, grid=(rows // tr,),
        in_specs=[pl.BlockSpec((NDEV, tr, cols), lambda i: (0, i, 0)), spec, spec, spec], out_specs=[spec] * 4,
        vmem=_vmem_limit([((NDEV + 7, tr, cols), F32, 2)]))(parts, w, m, v)


def kernel(x, p, norm_mix, ssm_lambda_re, ssm_lambda_im, ssm_log_dt, ssm_b_re, ssm_b_im, ssm_c_re, ssm_c_im, ssm_d, ssm_w_glu, kv_norm, w_k, w_v, w_q, attn_sinks, w_o, norm_mlp, w_up, w_down, norm_ple, w_ple_gate, w_ple_proj, norm_final, loss_target, m_norm_mix, m_ssm_lambda_re, m_ssm_lambda_im, m_ssm_log_dt, m_ssm_b_re, m_ssm_b_im, m_ssm_c_re, m_ssm_c_im, m_ssm_d, m_ssm_w_glu, m_kv_norm, m_w_k, m_w_v, m_w_q, m_attn_sinks, m_w_o, m_norm_mlp, m_w_up, m_w_down, m_norm_ple, m_w_ple_gate, m_w_ple_proj, m_norm_final, v_norm_mix, v_ssm_lambda_re, v_ssm_lambda_im, v_ssm_log_dt, v_ssm_b_re, v_ssm_b_im, v_ssm_c_re, v_ssm_c_im, v_ssm_d, v_ssm_w_glu, v_kv_norm, v_w_k, v_w_v, v_w_q, v_attn_sinks, v_w_o, v_norm_mlp, v_w_up, v_w_down, v_norm_ple, v_w_ple_gate, v_w_ple_proj, v_norm_final):
    names = ['norm_mix', 'ssm_lambda_re', 'ssm_lambda_im', 'ssm_log_dt', 'ssm_b_re', 'ssm_b_im', 'ssm_c_re',
             'ssm_c_im', 'ssm_d', 'ssm_w_glu', 'kv_norm', 'w_k', 'w_v', 'w_q', 'attn_sinks', 'w_o', 'norm_mlp',
             'w_up', 'w_down', 'norm_ple', 'w_ple_gate', 'w_ple_proj', 'norm_final']
    weights = dict(zip(names, (norm_mix, ssm_lambda_re, ssm_lambda_im, ssm_log_dt, ssm_b_re, ssm_b_im, ssm_c_re,
                               ssm_c_im, ssm_d, ssm_w_glu, kv_norm, w_k, w_v, w_q, attn_sinks, w_o, norm_mlp,
                               w_up, w_down, norm_ple, w_ple_gate, w_ple_proj, norm_final)))
    mom1 = dict(zip(names, (m_norm_mix, m_ssm_lambda_re, m_ssm_lambda_im, m_ssm_log_dt, m_ssm_b_re, m_ssm_b_im,
                            m_ssm_c_re, m_ssm_c_im, m_ssm_d, m_ssm_w_glu, m_kv_norm, m_w_k, m_w_v, m_w_q,
                            m_attn_sinks, m_w_o, m_norm_mlp, m_w_up, m_w_down, m_norm_ple, m_w_ple_gate,
                            m_w_ple_proj, m_norm_final)))
    mom2 = dict(zip(names, (v_norm_mix, v_ssm_lambda_re, v_ssm_lambda_im, v_ssm_log_dt, v_ssm_b_re, v_ssm_b_im,
                            v_ssm_c_re, v_ssm_c_im, v_ssm_d, v_ssm_w_glu, v_kv_norm, v_w_k, v_w_v, v_w_q,
                            v_attn_sinks, v_w_o, v_norm_mlp, v_w_up, v_w_down, v_norm_ple, v_w_ple_gate,
                            v_w_ple_proj, v_norm_final)))

    seqlen, d = x.shape[1], x.shape[2]
    depth = w_up.shape[0]
    n_ssm = ssm_w_glu.shape[0]
    n_att = w_q.shape[0]
    ng = d // SSM_GROUP
    nh = d // HEAD_DIM
    h0 = x[0]
    tgt = loss_target[0]
    tabs = _rope_tables(seqlen)

    sharded = ['w_up', 'w_down', 'w_ple_gate', 'w_ple_proj', 'ssm_w_glu', 'w_q', 'w_o', 'w_k', 'w_v']
    gath = _exchange([weights[k].astype(BF16) for k in sharded] + [ssm_d], False, "gather_weights")
    wup_g, wdn_g, wg_g, wpp_g, wglu_g, wq_g, wo_g, wk_g, wv_g, dsk_g = gath
    dkv = wk_g.shape[2]
    wg = [wg_g[:, i].reshape(d, d) for i in range(depth)]
    wpp = [wpp_g[:, i].transpose(1, 0, 2).reshape(wpp_g.shape[2], d) for i in range(depth)]
    wglu = [wglu_g[:, i].transpose(1, 0, 2).reshape(d, 2 * d) for i in range(n_ssm)]
    wq = [wq_g[:, j].reshape(d, d) for j in range(n_att)]
    wo = [wo_g[:, j].reshape(d, d) for j in range(n_att)]
    wk = wk_g.reshape(d, dkv)
    wv = wv_g.reshape(d, dkv)
    dskip = dsk_g.transpose(1, 0, 2).reshape(n_ssm, d)

    def ssm_params(i):
        n = SSM_STATE
        return (ssm_lambda_re[i].reshape(ng, 1, n), ssm_lambda_im[i].reshape(ng, 1, n),
                ssm_log_dt[i].reshape(ng, 1, 1), jnp.swapaxes(ssm_b_re[i], 1, 2), jnp.swapaxes(ssm_b_im[i], 1, 2),
                ssm_c_re[i], ssm_c_im[i])

    h = h0
    h_in, h_a, h_b = [], [], []
    ssm_saved, att_saved = {}, {}
    k_sh = v_sh = None
    for i in range(depth):
        h_in.append(h)
        if i < n_ssm:
            hn = _norm_fwd(h, norm_mix[i], f"norm_mix_fwd{i}")
            u = _to_groups(hn)
            mats = _ssm_prep(ssm_params(i), f"ssm_prep{i}")
            mmat, atr, ati = mats[0], mats[5], mats[6]
            pmat = jnp.concatenate([mats[1], mats[2]], axis=2)
            qt = jnp.concatenate([mats[3], mats[4]], axis=2)
            s_in = _ssm_state_in(u, pmat, f"ssm_state_in{i}")
            xp = _ssm_carry(s_in, *_ssm_rows(atr, ati, False), False, f"ssm_carry_fwd{i}")
            y = _from_groups(_ssm_out(u, xp, mmat, qt, f"ssm_out{i}"))
            ha = _glu_fwd(y, hn, h, dskip[i], wglu[i], f"glu_fwd{i}")
            ssm_saved[i] = (hn, u, mmat, pmat, qt, atr, ati, xp, y)
        else:
            j = i - n_ssm
            q = _q_fwd(h, norm_mix[i], wq[j], tabs, f"q_fwd{j}")
            o = _attn_fwd(q, k_sh, v_sh, attn_sinks[j], f"attn_fwd{j}")
            ha = _lin_res(h, o, wo[j], f"attn_out{j}")
            att_saved[j] = (q, o)
        h_a.append(ha)
        hb = _mlp_fwd(ha, norm_mlp[i], wup_g, wdn_g, i, f"mlp_fwd{i}")
        h_b.append(hb)
        h = _ple_fwd(hb, p[i, 0], norm_ple[i], wg[i], wpp[i], f"ple_fwd{i}")
        if i == n_ssm - 1:
            k_sh, v_sh = _kv_fwd(h, kv_norm, wk, wv, tabs, "kv_fwd")
    h_kv = h_in[n_ssm] if n_ssm < depth else h
    dh, loss_row, g_norm_final = _loss_bwd(h, norm_final, tgt, "loss_bwd")
    loss = lax.psum(loss_row[0, 0], AXES)

    zeros = lambda *s: jnp.zeros(s, F32)
    fb = wup_g.shape[3]
    rs = d // NDEV
    st = {'w_up': zeros(NDEV, depth, d, fb), 'w_down': zeros(NDEV, depth, fb, d),
          'w_ple_gate': zeros(NDEV, depth, rs, d), 'w_ple_proj': zeros(NDEV, depth, wpp_g.shape[2], rs),
          'ssm_w_glu': zeros(NDEV, n_ssm, d, 2 * d // NDEV), 'w_q': zeros(NDEV, n_att, rs, d),
          'w_o': zeros(NDEV, n_att, rs, d), 'w_k': zeros(NDEV, 1, rs, dkv), 'w_v': zeros(NDEV, 1, rs, dkv)}
    g_norm_mix, g_norm_mlp, g_norm_ple = [None] * depth, [None] * depth, [None] * depth
    g_dskip, g_ssm, g_sinks = [None] * n_ssm, [None] * n_ssm, [None] * n_att
    g_kv_norm = None
    dks, dvs = [], []
    for i in range(depth - 1, -1, -1):
        if i == n_ssm - 1:
            dh, dkp, dvb, hkb, g_kv_norm = _kv_bwd(dks, dvs, h_kv, kv_norm, dh, wk, wv, tabs, "kv_bwd")
            st['w_k'] = _atb(hkb, dkp, st['w_k'], 0, False, "grad_w_k")
            st['w_v'] = _atb(hkb, dvb, st['w_v'], 0, False, "grad_w_v")
        dhb, dz, nb16, dpp, g_norm_ple[i] = _ple_bwd(h_b[i], p[i, 0], dh, norm_ple[i], wg[i], wpp[i], f"ple_bwd{i}")
        st['w_ple_gate'] = _atb(nb16, dz, st['w_ple_gate'], i, False, f"grad_w_ple_gate{i}")
        st['w_ple_proj'] = _atb(p[i, 0], dpp, st['w_ple_proj'], i, True, f"grad_w_ple_proj{i}")
        dha, hmb, da, act, g_norm_mlp[i] = _mlp_bwd(h_a[i], dhb, norm_mlp[i], wup_g, wdn_g, i, f"mlp_bwd{i}")
        st['w_up'] = _atb(hmb, da, st['w_up'], i, True, f"grad_w_up{i}")
        st['w_down'] = _atb(act, dhb, st['w_down'], i, False, f"grad_w_down{i}")
        if i >= n_ssm:
            j = i - n_ssm
            q, o = att_saved[j]
            do = _lin_nt(dha, wo[j], f"attn_out_bwd{j}")
            st['w_o'] = _atb(o, dha, st['w_o'], j, False, f"grad_w_o{j}")
            dq, dk_j, dv_j, dsink = _attn_bwd(q, k_sh, v_sh, do, attn_sinks[j], f"attn_bwd{j}")
            dks.append(dk_j)
            dvs.append(dv_j)
            g_sinks[j] = dsink[:, 0]
            dh, dqp, hnb, g_norm_mix[i] = _q_bwd(dq, h_in[i], norm_mix[i], dha, wq[j], tabs, f"q_bwd{j}")
            st['w_q'] = _atb(hnb, dqp, st['w_q'], j, False, f"grad_w_q{j}")
        else:
            hn, u, mmat, pmat, qt, atr, ati, xp, y = ssm_saved[i]
            dyy, dhn_d, geb, dab, g_dskip[i] = _glu_bwd(y, hn, dha, dskip[i], wglu[i], f"glu_bwd{i}")
            st['ssm_w_glu'] = _atb(geb, dab, st['ssm_w_glu'], i, True, f"grad_ssm_w_glu{i}")
            dyg = _to_groups(dyy)
            dxp = _ssm_dstate(dyg, qt, f"ssm_dstate{i}")
            gs = _ssm_carry(dxp, *_ssm_rows(atr, ati, True), True, f"ssm_carry_bwd{i}")
            du, dm, dp, dqt, da_raw = _ssm_bwd(u, dyg, xp, gs, mmat, pmat, f"ssm_bwd{i}")
            n = SSM_STATE
            cots = (dm, dp[:, :, :n], dp[:, :, n:], dqt[:, :, :n], dqt[:, :, n:],
                    (da_raw[:, 0:1, :n] + da_raw[:, 0:1, n:]), (da_raw[:, 1:2, :n] - da_raw[:, 1:2, n:]))
            g_ssm[i] = _ssm_prep_vjp(ssm_params(i), cots, f"ssm_prep_vjp{i}")
            dh, g_norm_mix[i] = _norm_bwd(h_in[i], norm_mix[i], dhn_d, _from_groups(du), dha, f"norm_mix_bwd{i}")
    grad_x = dh[None]

    big = ['ssm_d', 'ssm_w_glu', 'w_k', 'w_v', 'w_q', 'w_o', 'w_up', 'w_down', 'w_ple_gate', 'w_ple_proj']
    st['ssm_d'] = jnp.concatenate(g_dskip, axis=0).reshape(n_ssm, NDEV, rs).transpose(1, 0, 2)
    recv = dict(zip(big, _exchange([st[k] for k in big], True, "scatter_grads")))

    out_g, out_d, out_m, out_v = {}, {}, {}, {}
    for k in big:
        shp = weights[k].shape
        r2 = (math.prod(shp[:-1]), shp[-1])
        res = _adamw(recv[k].reshape((NDEV,) + r2), weights[k].reshape(r2), mom1[k].reshape(r2),
                     mom2[k].reshape(r2), f"adamw_{k}")
        out_g[k], out_d[k], out_m[k], out_v[k] = (t.reshape(shp) for t in res)

    def ssm_grad(idx, unswap=False):
        g = jnp.stack([g_ssm[i][idx] for i in range(n_ssm)])
        return jnp.swapaxes(g, 2, 3) if unswap else g

    small = {'norm_mix': jnp.concatenate(g_norm_mix, axis=0),
             'ssm_lambda_re': ssm_grad(0), 'ssm_lambda_im': ssm_grad(1), 'ssm_log_dt': ssm_grad(2),
             'ssm_b_re': ssm_grad(3, True), 'ssm_b_im': ssm_grad(4, True),
             'ssm_c_re': ssm_grad(5), 'ssm_c_im': ssm_grad(6),
             'kv_norm': g_kv_norm, 'attn_sinks': jnp.stack(g_sinks),
             'norm_mlp': jnp.concatenate(g_norm_mlp, axis=0), 'norm_ple': jnp.concatenate(g_norm_ple, axis=0),
             'norm_final': g_norm_final}
    snames = list(small)
    sizes = [weights[k].size for k in snames]
    total = sum(sizes)
    lanes = 128
    padded = -(-total // (512 * lanes)) * (512 * lanes)

    def flat(parts):
        v = jnp.concatenate([t.reshape(-1) for t in parts] + [jnp.zeros((padded - total,), F32)])
        return v.reshape(padded // lanes, lanes)

    parts = _exchange([flat([small[k] for k in snames])], False, "gather_small_grads")[0]
    res = _adamw(parts, flat([weights[k] for k in snames]), flat([mom1[k] for k in snames]),
                 flat([mom2[k] for k in snames]), "adamw_small")
    off = 0
    for k, sz in zip(snames, sizes):
        for dst, t in zip((out_g, out_d, out_m, out_v), res):
            dst[k] = t.reshape(-1)[off:off + sz].reshape(weights[k].shape)
        off += sz

    return (loss, grad_x, *[out_g[k] for k in names], *[out_d[k] for k in names],
            *[out_m[k] for k in names], *[out_v[k] for k in names])
```

```python
import functools
import math

import jax
import jax.numpy as jnp
from jax import lax
from jax.experimental import pallas as pl
from jax.experimental.pallas import tpu as pltpu

F32 = jnp.float32
BF16 = jnp.bfloat16
SDS = jax.ShapeDtypeStruct
MESH = pl.DeviceIdType.MESH
AXES = ("x", "y", "c")
NDEV = 8

RMS_EPS = 1e-6
SSM_GROUP = 16
SSM_STATE = 64
SSM_T = 8
SSM_W = SSM_T * SSM_GROUP
HEAD_DIM = 64
GQA_GROUP = 4
ATTN_BLOCK = 128
ROT_DIM = 16
ROPE_THETA = 500000.0
NEG_INF = -1e30
ADAM_LR, ADAM_B1, ADAM_B2, ADAM_EPS, ADAM_WD, ADAM_STEP = 0.001, 0.9, 0.999, 1e-08, 0.01, 10

VMEM_CAP = 56 * 1024 * 1024
HI = lax.Precision.HIGHEST

NN = ((1,), (0,))
NT = ((1,), (1,))
TN = ((0,), (0,))


def _dot(a, b, dims=NN, precision=None):
    return lax.dot_general(a, b, (dims, ((), ())), preferred_element_type=F32, precision=precision)


def _tile(n, pref):
    t = min(n, pref)
    while n % t:
        t //= 2
    return t


def _nbytes(shape, dtype):
    return math.prod(s for s in shape if s is not None) * jnp.dtype(dtype).itemsize


def _vmem_limit(blocks, extra=0):
    need = sum(_nbytes(s, d) * n for s, d, n in blocks) + extra + (4 << 20)
    return int(min(VMEM_CAP, max(need, 16 << 20)))


def _pcall(body, *, name, out_shape, grid, in_specs, out_specs, scratch=(), aliases=None, vmem=None):
    return pl.pallas_call(
        body, out_shape=out_shape, grid=grid, in_specs=in_specs, out_specs=out_specs,
        scratch_shapes=scratch, input_output_aliases=aliases or {}, name=name,
        compiler_params=pltpu.CompilerParams(
            dimension_semantics=("arbitrary",) * len(grid), vmem_limit_bytes=vmem),
        interpret=False)


def _rms(x, g):
    r = lax.rsqrt(jnp.mean(x * x, axis=-1, keepdims=True) + RMS_EPS)
    return x * r * g, r


def _rms_bwd(x, g, r, dy):
    xh = x * r
    dyg = dy * g
    dx = r * (dyg - xh * jnp.mean(dyg * xh, axis=-1, keepdims=True))
    return dx, jnp.sum(dy * xh, axis=0, keepdims=True)


_GELU_C = math.sqrt(2.0 / math.pi)


def _gelu_parts(x):
    t = jnp.tanh(_GELU_C * (x + 0.044715 * x * x * x))
    return 0.5 * x * (1.0 + t), t


def _gelu_grad(x, t):
    return 0.5 * (1.0 + t) + 0.5 * x * (1.0 - t * t) * _GELU_C * (1.0 + 3 * 0.044715 * x * x)


def _rope_tables(seqlen):
    half = ROT_DIM // 2
    inv = ROPE_THETA ** (-jnp.arange(0, ROT_DIM, 2, dtype=F32) / ROT_DIM)
    ang = jnp.arange(seqlen, dtype=jnp.int32).astype(F32)[:, None] * inv[None, :]
    cos, sin = jnp.cos(ang), jnp.sin(ang)
    zeros = jnp.zeros((seqlen, HEAD_DIM - ROT_DIM), F32)
    zh = jnp.zeros((seqlen, half), F32)
    c = jnp.concatenate([cos, cos, zeros + 1.0], axis=1)
    sa = jnp.concatenate([zh, sin, zeros], axis=1)
    sb = jnp.concatenate([-sin, zh, zeros], axis=1)
    return tuple(jnp.tile(t, (1, 128 // HEAD_DIM)) for t in (c, sa, sb))


def _rope(x, c, sa, sb):
    w = x.shape[1]
    reps = w // 128
    half = ROT_DIM // 2
    return (x * jnp.tile(c, (1, reps)) + pltpu.roll(x, half, 1) * jnp.tile(sa, (1, reps))
            + pltpu.roll(x, w - half, 1) * jnp.tile(sb, (1, reps)))


def _rope_bwd(dy, c, sa, sb):
    w = dy.shape[1]
    reps = w // 128
    half = ROT_DIM // 2
    return (dy * jnp.tile(c, (1, reps)) + pltpu.roll(dy * jnp.tile(sa, (1, reps)), w - half, 1)
            + pltpu.roll(dy * jnp.tile(sb, (1, reps)), half, 1))


def _rspec(tm, c):
    return pl.BlockSpec((tm, c), lambda i: (i, 0))


def _cspec(shape, idx=None):
    idx = tuple(idx) if idx is not None else (0,) * len(shape)
    return pl.BlockSpec(tuple(shape), lambda i: idx, pipeline_mode=pl.Buffered(1))


def _rowcall(body, name, seqlen, tm, rows_in, consts_in, rows_out, acc_out=(), extra_vmem=0):
    in_specs = [_rspec(tm, a.shape[1]) for a in rows_in] + [_cspec(bs, ix) for _, bs, ix in consts_in]
    out_shape = [SDS((seqlen, c), d) for c, d in rows_out] + [SDS(s, F32) for s in acc_out]
    out_specs = [_rspec(tm, c) for c, _ in rows_out] + [pl.BlockSpec(s, lambda i: (0, 0)) for s in acc_out]
    blocks = ([((tm, a.shape[1]), a.dtype, 2) for a in rows_in] + [(bs, a.dtype, 1) for a, bs, _ in consts_in]
              + [((tm, c), d, 2) for c, d in rows_out])
    temporaries = 12 * tm * rows_in[0].shape[1] * 4
    out = _pcall(body, name=name, out_shape=out_shape, grid=(seqlen // tm,), in_specs=in_specs,
                 out_specs=out_specs, vmem=_vmem_limit(blocks, extra_vmem + temporaries))(
                     *rows_in, *[a for a, _, _ in consts_in])
    return out


def _whole(a):
    return (a, a.shape, None)


def _norm_fwd(h, g, name):
    seqlen, d = h.shape
    tm = _tile(seqlen, 1024)

    def body(h_ref, g_ref, o_ref):
        o_ref[...] = _rms(h_ref[...], g_ref[...])[0]

    return _rowcall(body, name, seqlen, tm, [h], [_whole(g.reshape(1, d))], [(d, F32)])[0]


def _norm_bwd(h, g, dy1, dy2, dres, name):
    seqlen, d = h.shape
    tm = _tile(seqlen, 512)

    def body(h_ref, dy1_ref, dy2_ref, dres_ref, g_ref, dh_ref, dg_ref):
        @pl.when(pl.program_id(0) == 0)
        def _():
            dg_ref[...] = jnp.zeros_like(dg_ref)
        x = h_ref[...]
        gv = g_ref[...]
        _, r = _rms(x, gv)
        dx, dg = _rms_bwd(x, gv, r, dy1_ref[...] + dy2_ref[...])
        dh_ref[...] = dres_ref[...] + dx
        dg_ref[...] += dg

    return _rowcall(body, name, seqlen, tm, [h, dy1, dy2, dres], [_whole(g.reshape(1, d))], [(d, F32)], [(1, d)])


def _glu_fwd(y, hn, h, dskip, wglu, name):
    seqlen, d = h.shape
    tm = _tile(seqlen, 512)

    def body(y_ref, hn_ref, h_ref, d_ref, w_ref, o_ref):
        yy = y_ref[...] + d_ref[...] * hn_ref[...]
        ge, _ = _gelu_parts(yy)
        ab = _dot(ge.astype(BF16), w_ref[...])
        o_ref[...] = h_ref[...] + ab[:, :d] * jax.nn.sigmoid(ab[:, d:])

    return _rowcall(body, name, seqlen, tm, [y, hn, h], [_whole(dskip.reshape(1, d)), _whole(wglu)], [(d, F32)],
                    extra_vmem=tm * d * 4 * 6)[0]


def _glu_bwd(y, hn, dmix, dskip, wglu, name):
    seqlen, d = hn.shape
    tm = _tile(seqlen, 512)

    def body(y_ref, hn_ref, dm_ref, d_ref, w_ref, dyy_ref, dhn_ref, ge_ref, dab_ref, dd_ref):
        @pl.when(pl.program_id(0) == 0)
        def _():
            dd_ref[...] = jnp.zeros_like(dd_ref)
        hn_v = hn_ref[...]
        dsk = d_ref[...]
        yy = y_ref[...] + dsk * hn_v
        ge, t = _gelu_parts(yy)
        geb = ge.astype(BF16)
        ab = _dot(geb, w_ref[...])
        a = ab[:, :d]
        sg = jax.nn.sigmoid(ab[:, d:])
        dm = dm_ref[...]
        dab_ref[:, :d] = (dm * sg).astype(BF16)
        dab_ref[:, d:] = (dm * a * sg * (1.0 - sg)).astype(BF16)
        dge = _dot(dab_ref[...], w_ref[...], NT)
        dyy = dge * _gelu_grad(yy, t)
        dyy_ref[...] = dyy
        dhn_ref[...] = dyy * dsk
        ge_ref[...] = geb
        dd_ref[...] += jnp.sum(dyy * hn_v, axis=0, keepdims=True)

    return _rowcall(body, name, seqlen, tm, [y, hn, dmix], [_whole(dskip.reshape(1, d)), _whole(wglu)],
                    [(d, F32), (d, F32), (d, BF16), (2 * d, BF16)], [(1, d)], extra_vmem=tm * d * 4 * 8)


def _q_fwd(h, g, wq, tabs, name):
    seqlen, d = h.shape
    tm = _tile(seqlen, 512)

    def body(h_ref, c_ref, sa_ref, sb_ref, g_ref, w_ref, q_ref):
        hn, _ = _rms(h_ref[...], g_ref[...])
        qp = _dot(hn.astype(BF16), w_ref[...])
        q_ref[...] = _rope(qp, c_ref[...], sa_ref[...], sb_ref[...]).astype(BF16)

    return _rowcall(body, name, seqlen, tm, [h, *tabs], [_whole(g.reshape(1, d)), _whole(wq)], [(d, BF16)],
                    extra_vmem=tm * d * 4 * 6)[0]


def _q_bwd(dq, h, g, dres, wq, tabs, name):
    seqlen, d = h.shape
    tm = _tile(seqlen, 512)

    def body(dq_ref, h_ref, dres_ref, c_ref, sa_ref, sb_ref, g_ref, w_ref, dh_ref, dqp_ref, hn_ref, dg_ref):
        @pl.when(pl.program_id(0) == 0)
        def _():
            dg_ref[...] = jnp.zeros_like(dg_ref)
        dqp = _rope_bwd(dq_ref[...], c_ref[...], sa_ref[...], sb_ref[...]).astype(BF16)
        x = h_ref[...]
        gv = g_ref[...]
        hn, r = _rms(x, gv)
        dhn = _dot(dqp, w_ref[...], NT)
        dx, dg = _rms_bwd(x, gv, r, dhn)
        dh_ref[...] = dres_ref[...] + dx
        dqp_ref[...] = dqp
        hn_ref[...] = hn.astype(BF16)
        dg_ref[...] += dg

    return _rowcall(body, name, seqlen, tm, [dq, h, dres, *tabs], [_whole(g.reshape(1, d)), _whole(wq)],
                    [(d, F32), (d, BF16), (d, BF16)], [(1, d)], extra_vmem=tm * d * 4 * 6)


def _kv_fwd(h, g, wk, wv, tabs, name):
    seqlen, d = h.shape
    dk = wk.shape[1]
    tm = _tile(seqlen, 512)

    def body(h_ref, c_ref, sa_ref, sb_ref, g_ref, wk_ref, wv_ref, k_ref, v_ref):
        hk = _rms(h_ref[...], g_ref[...])[0].astype(BF16)
        k_ref[...] = _rope(_dot(hk, wk_ref[...]), c_ref[...], sa_ref[...], sb_ref[...]).astype(BF16)
        v_ref[...] = _dot(hk, wv_ref[...]).astype(BF16)

    return _rowcall(body, name, seqlen, tm, [h, *tabs], [_whole(g.reshape(1, d)), _whole(wk), _whole(wv)],
                    [(dk, BF16), (dk, BF16)], extra_vmem=tm * d * 4 * 4)


def _kv_bwd(dks, dvs, h, g, dres, wk, wv, tabs, name):
    seqlen, d = h.shape
    dkw = wk.shape[1]
    tm = _tile(seqlen, 512)

    def body(dk0_ref, dk1_ref, dv0_ref, dv1_ref, h_ref, dres_ref, c_ref, sa_ref, sb_ref, g_ref, wk_ref, wv_ref,
             dh_ref, dkp_ref, dvb_ref, hk_ref, dg_ref):
        @pl.when(pl.program_id(0) == 0)
        def _():
            dg_ref[...] = jnp.zeros_like(dg_ref)
        dkp = _rope_bwd(dk0_ref[...] + dk1_ref[...], c_ref[...], sa_ref[...], sb_ref[...]).astype(BF16)
        dvb = (dv0_ref[...] + dv1_ref[...]).astype(BF16)
        x = h_ref[...]
        gv = g_ref[...]
        hk, r = _rms(x, gv)
        dhk = _dot(dkp, wk_ref[...], NT) + _dot(dvb, wv_ref[...], NT)
        dx, dg = _rms_bwd(x, gv, r, dhk)
        dh_ref[...] = dres_ref[...] + dx
        dkp_ref[...] = dkp
        dvb_ref[...] = dvb
        hk_ref[...] = hk.astype(BF16)
        dg_ref[...] += dg

    return _rowcall(body, name, seqlen, tm, [dks[0], dks[1], dvs[0], dvs[1], h, dres, *tabs],
                    [_whole(g.reshape(1, d)), _whole(wk), _whole(wv)],
                    [(d, F32), (dkw, BF16), (dkw, BF16), (d, BF16)], [(1, d)], extra_vmem=tm * d * 4 * 6)


def _lin_res(h, xb, w, name):
    seqlen, d = h.shape
    tm = _tile(seqlen, 512)

    def body(h_ref, x_ref, w_ref, o_ref):
        o_ref[...] = h_ref[...] + _dot(x_ref[...], w_ref[...])

    return _rowcall(body, name, seqlen, tm, [h, xb], [_whole(w)], [(d, F32)], extra_vmem=tm * d * 4 * 2)[0]


def _lin_nt(dy, w, name):
    seqlen, d = dy.shape
    tm = _tile(seqlen, 512)

    def body(dy_ref, w_ref, o_ref):
        o_ref[...] = _dot(dy_ref[...].astype(BF16), w_ref[...], NT).astype(BF16)

    return _rowcall(body, name, seqlen, tm, [dy], [_whole(w)], [(w.shape[0], BF16)], extra_vmem=tm * d * 4 * 2)[0]


def _mlp_fwd(h, g, wup_g, wdn_g, layer, name):
    seqlen, d = h.shape
    nb, _, _, fb = wup_g.shape
    tm = _tile(seqlen, 512)

    def body(h_ref, g_ref, wup_ref, wdn_ref, o_ref):
        x = h_ref[...]
        hm = _rms(x, g_ref[...])[0].astype(BF16)
        acc = x
        for j in range(nb):
            r = jnp.maximum(_dot(hm, wup_ref[j]), 0.0)
            acc = acc + _dot((r * r).astype(BF16), wdn_ref[j])
        o_ref[...] = acc

    consts = [_whole(g.reshape(1, d)), (wup_g, (nb, None, d, fb), (0, layer, 0, 0)),
              (wdn_g, (nb, None, fb, d), (0, layer, 0, 0))]
    return _rowcall(body, name, seqlen, tm, [h], consts, [(d, F32)], extra_vmem=tm * (d + fb) * 4 * 4)[0]


def _mlp_bwd(h, dh, g, wup_g, wdn_g, layer, name):
    seqlen, d = h.shape
    nb, _, _, fb = wup_g.shape
    tm = _tile(seqlen, 256)

    def body(h_ref, dh_ref, g_ref, wup_ref, wdn_ref, dhin_ref, hm_ref, da_ref, act_ref, dg_ref):
        @pl.when(pl.program_id(0) == 0)
        def _():
            dg_ref[...] = jnp.zeros_like(dg_ref)
        x = h_ref[...]
        gv = g_ref[...]
        dy = dh_ref[...]
        hm, r = _rms(x, gv)
        hmb = hm.astype(BF16)
        dyb = dy.astype(BF16)
        dhm = jnp.zeros_like(x)
        for j in range(nb):
            rl = jnp.maximum(_dot(hmb, wup_ref[j]), 0.0)
            act_ref[:, j * fb:(j + 1) * fb] = (rl * rl).astype(BF16)
            da = (_dot(dyb, wdn_ref[j], NT) * (2.0 * rl)).astype(BF16)
            da_ref[:, j * fb:(j + 1) * fb] = da
            dhm = dhm + _dot(da, wup_ref[j], NT)
        dx, dg = _rms_bwd(x, gv, r, dhm)
        dhin_ref[...] = dy + dx
        hm_ref[...] = hmb
        dg_ref[...] += dg

    consts = [_whole(g.reshape(1, d)), (wup_g, (nb, None, d, fb), (0, layer, 0, 0)),
              (wdn_g, (nb, None, fb, d), (0, layer, 0, 0))]
    return _rowcall(body, name, seqlen, tm, [h, dh], consts,
                    [(d, F32), (d, BF16), (nb * fb, BF16), (nb * fb, BF16)], [(1, d)],
                    extra_vmem=tm * (d + fb) * 4 * 6)


def _ple_fwd(h, p, g, wg, wpp, name):
    seqlen, d = h.shape
    tm = _tile(seqlen, 512)

    def body(h_ref, p_ref, g_ref, wg_ref, wpp_ref, o_ref):
        x = h_ref[...]
        n = _rms(x, g_ref[...])[0].astype(BF16)
        gate = jax.nn.sigmoid(_dot(n, wg_ref[...]))
        o_ref[...] = x + gate * _dot(p_ref[...].astype(BF16), wpp_ref[...])

    return _rowcall(body, name, seqlen, tm, [h, p], [_whole(g.reshape(1, d)), _whole(wg), _whole(wpp)], [(d, F32)],
                    extra_vmem=tm * d * 4 * 5)[0]


def _ple_bwd(h, p, dh, g, wg, wpp, name):
    seqlen, d = h.shape
    tm = _tile(seqlen, 512)

    def body(h_ref, p_ref, dh_ref, g_ref, wg_ref, wpp_ref, dhin_ref, dz_ref, n_ref, dpp_ref, dg_ref):
        @pl.when(pl.program_id(0) == 0)
        def _():
            dg_ref[...] = jnp.zeros_like(dg_ref)
        x = h_ref[...]
        gv = g_ref[...]
        dy = dh_ref[...]
        n, r = _rms(x, gv)
        nb16 = n.astype(BF16)
        gate = jax.nn.sigmoid(_dot(nb16, wg_ref[...]))
        pp = _dot(p_ref[...].astype(BF16), wpp_ref[...])
        dz = (dy * pp * gate * (1.0 - gate)).astype(BF16)
        dn = _dot(dz, wg_ref[...], NT)
        dx, dg = _rms_bwd(x, gv, r, dn)
        dhin_ref[...] = dy + dx
        dz_ref[...] = dz
        n_ref[...] = nb16
        dpp_ref[...] = (dy * gate).astype(BF16)
        dg_ref[...] += dg

    return _rowcall(body, name, seqlen, tm, [h, p, dh], [_whole(g.reshape(1, d)), _whole(wg), _whole(wpp)],
                    [(d, F32), (d, BF16), (d, BF16), (d, BF16)], [(1, d)], extra_vmem=tm * d * 4 * 8)


def _loss_bwd(h, g, tgt, name):
    seqlen, d = h.shape
    tm = _tile(seqlen, 512)

    def body(h_ref, t_ref, g_ref, dh_ref, loss_ref, dg_ref):
        @pl.when(pl.program_id(0) == 0)
        def _():
            dg_ref[...] = jnp.zeros_like(dg_ref)
            loss_ref[...] = jnp.zeros_like(loss_ref)
        x = h_ref[...]
        gv = g_ref[...]
        y, r = _rms(x, gv)
        diff = y - t_ref[...]
        loss_ref[...] += (0.5 / d) * jnp.sum(jnp.sum(diff * diff, axis=1, keepdims=True), axis=0, keepdims=True)
        dx, dg = _rms_bwd(x, gv, r, diff * (1.0 / d))
        dh_ref[...] = dx
        dg_ref[...] += dg

    return _rowcall(body, name, seqlen, tm, [h, tgt], [_whole(g.reshape(1, d))], [(d, F32)], [(1, 128), (1, d)])


def _atb(a, b, stack, layer, col_blocked, name):
    seqlen, k1 = a.shape
    k2 = b.shape[1]
    nl = stack.shape[1]
    if col_blocked:
        cs = stack.shape[3]
        t1 = _tile(k1, 512)
        nblk = _tile(NDEV, max(1, 2048 // cs))
        t2 = nblk * cs
        oblock = (nblk, None, t1, cs)
        omap = lambda i, j, l: (j, layer, i, 0)
    else:
        rs = stack.shape[2]
        t2 = _tile(k2, 2048)
        nblk = _tile(NDEV, max(1, 1024 // rs))
        t1 = nblk * rs
        oblock = (nblk, None, rs, t2)
        omap = lambda i, j, l: (i, layer, 0, j)
    tl = _tile(seqlen, 2048 if b.dtype == BF16 else 1024)

    def body(a_ref, b_ref, _, o_ref):
        @pl.when(pl.program_id(2) == 0)
        def _():
            o_ref[...] = jnp.zeros_like(o_ref)
        res = _dot(a_ref[...].astype(BF16), b_ref[...].astype(BF16), TN)
        for n in range(nblk):
            if col_blocked:
                o_ref[n] += res[:, n * cs:(n + 1) * cs]
            else:
                o_ref[n] += res[n * rs:(n + 1) * rs, :]

    blocks = [((tl, t1), a.dtype, 2), ((tl, t2), b.dtype, 2), ((t1, t2), F32, 2)]
    return _pcall(
        body, name=name, out_shape=SDS(stack.shape, F32), grid=(k1 // t1, k2 // t2, seqlen // tl),
        in_specs=[pl.BlockSpec((tl, t1), lambda i, j, l: (l, i)), pl.BlockSpec((tl, t2), lambda i, j, l: (l, j)),
                  pl.BlockSpec(memory_space=pl.ANY)],
        out_specs=pl.BlockSpec(oblock, omap), aliases={2: 0},
        vmem=_vmem_limit(blocks, extra=t1 * t2 * 4 + tl * (t1 + t2) * 2))(a, b, stack)


def _attn_probs(q4s, kks, sink_col, has_prev):
    s = jnp.concatenate([_dot(q4, kk, NT) for q4, kk in zip(q4s, kks)], axis=0) * (HEAD_DIM ** -0.5)
    rows = s.shape[0]
    qi = lax.broadcasted_iota(jnp.int32, (rows, 2 * ATTN_BLOCK), 0) % ATTN_BLOCK + ATTN_BLOCK
    kj = lax.broadcasted_iota(jnp.int32, (rows, 2 * ATTN_BLOCK), 1)
    mask = (kj <= qi) & (qi - kj < ATTN_BLOCK) & ((kj >= ATTN_BLOCK) | has_prev)
    s = jnp.where(mask, s, NEG_INF)
    m = jnp.maximum(jnp.max(s, axis=1, keepdims=True), sink_col)
    pr = jnp.exp(s - m)
    es = jnp.exp(sink_col - m)
    inv = 1.0 / (jnp.sum(pr, axis=1, keepdims=True) + es)
    return pr * inv, es * inv


def _sink_col(sink_ref, nheads):
    return jnp.concatenate([jnp.full((ATTN_BLOCK, 1), sink_ref[hq], F32) for hq in range(nheads)], axis=0)


def _kv_pair(p_ref, c_ref, kh):
    sl = slice(kh * HEAD_DIM, (kh + 1) * HEAD_DIM)
    return jnp.concatenate([p_ref[:, sl], c_ref[:, sl]], axis=0)


def _stack_heads(ref, kh):
    return jnp.concatenate(
        [ref[:, (kh * GQA_GROUP + g) * HEAD_DIM:(kh * GQA_GROUP + g + 1) * HEAD_DIM] for g in range(GQA_GROUP)], axis=0)


def _attn_fwd(q, k, v, sinks, name):
    seqlen, d = q.shape
    dkv = k.shape[1]
    nkv = dkv // HEAD_DIM
    nb = seqlen // ATTN_BLOCK
    blk = ATTN_BLOCK

    def body(sink_ref, q_ref, kc_ref, kp_ref, vc_ref, vp_ref, o_ref):
        has_prev = pl.program_id(0) > 0
        q4s = [_stack_heads(q_ref, kh) for kh in range(nkv)]
        kks = [_kv_pair(kp_ref, kc_ref, kh) for kh in range(nkv)]
        w, _ = _attn_probs(q4s, kks, _sink_col(sink_ref, nkv * GQA_GROUP), has_prev)
        wb = w.astype(BF16)
        for kh in range(nkv):
            o4 = _dot(wb[kh * GQA_GROUP * blk:(kh + 1) * GQA_GROUP * blk, :], _kv_pair(vp_ref, vc_ref, kh))
            for g in range(GQA_GROUP):
                hq = kh * GQA_GROUP + g
                o_ref[:, hq * HEAD_DIM:(hq + 1) * HEAD_DIM] = o4[g * blk:(g + 1) * blk, :].astype(BF16)

    cur = lambda n: (n, 0)
    prev = lambda n: (jnp.maximum(n - 1, 0), 0)
    return _pcall(
        body, name=name, out_shape=SDS((seqlen, d), BF16), grid=(nb,),
        in_specs=[pl.BlockSpec(memory_space=pltpu.SMEM), pl.BlockSpec((blk, d), cur),
                  pl.BlockSpec((blk, dkv), cur), pl.BlockSpec((blk, dkv), prev),
                  pl.BlockSpec((blk, dkv), cur), pl.BlockSpec((blk, dkv), prev)],
        out_specs=pl.BlockSpec((blk, d), cur), vmem=32 << 20)(sinks, q, k, k, v, v)


def _attn_bwd(q, k, v, do, sinks, name):
    seqlen, d = q.shape
    dkv = k.shape[1]
    nkv = dkv // HEAD_DIM
    nh = d // HEAD_DIM
    nb = seqlen // ATTN_BLOCK
    blk = ATTN_BLOCK
    scale = HEAD_DIM ** -0.5

    def body(sink_ref, q_ref, kc_ref, kp_ref, vc_ref, vp_ref, do_ref, dq_ref, dk_ref, dv_ref, ds_ref,
             ck_ref, cv_ref):
        n = pl.program_id(0)

        @pl.when(n == 0)
        def _():
            ck_ref[...] = jnp.zeros_like(ck_ref)
            cv_ref[...] = jnp.zeros_like(cv_ref)
            ds_ref[...] = jnp.zeros_like(ds_ref)

        @pl.when(n == nb)
        def _():
            dk_ref[...] = ck_ref[...]
            dv_ref[...] = cv_ref[...]

        @pl.when(n < nb)
        def _():
            has_prev = n > 0
            q4s = [_stack_heads(q_ref, kh) for kh in range(nkv)]
            do4s = [_stack_heads(do_ref, kh) for kh in range(nkv)]
            kks = [_kv_pair(kp_ref, kc_ref, kh) for kh in range(nkv)]
            w, wsink = _attn_probs(q4s, kks, _sink_col(sink_ref, nh), has_prev)
            dw = jnp.concatenate([_dot(do4s[kh], _kv_pair(vp_ref, vc_ref, kh), NT) for kh in range(nkv)], axis=0)
            dsum = jnp.sum(w * dw, axis=1, keepdims=True)
            ds_all = (w * (dw - dsum) * scale).astype(BF16)
            wb = w.astype(BF16)
            dsk = -wsink * dsum
            for kh in range(nkv):
                sl = slice(kh * HEAD_DIM, (kh + 1) * HEAD_DIM)
                rows = slice(kh * GQA_GROUP * blk, (kh + 1) * GQA_GROUP * blk)
                ds = ds_all[rows, :]
                dq4 = _dot(ds, kks[kh])
                dkk = _dot(ds, q4s[kh], TN)
                dvv = _dot(wb[rows, :], do4s[kh], TN)
                for g in range(GQA_GROUP):
                    hq = kh * GQA_GROUP + g
                    dq_ref[:, hq * HEAD_DIM:(hq + 1) * HEAD_DIM] = dq4[g * blk:(g + 1) * blk, :]
                    ds_ref[hq:hq + 1, :] += jnp.sum(dsk[hq * blk:(hq + 1) * blk, :], axis=0, keepdims=True)
                dk_ref[:, sl] = ck_ref[:, sl] + dkk[:blk, :]
                dv_ref[:, sl] = cv_ref[:, sl] + dvv[:blk, :]
                ck_ref[:, sl] = dkk[blk:, :]
                cv_ref[:, sl] = dvv[blk:, :]

    cur = lambda n: (jnp.minimum(n, nb - 1), 0)
    prev = lambda n: (jnp.clip(n - 1, 0, nb - 1), 0)
    lag = lambda n: (jnp.maximum(n - 1, 0), 0)
    return _pcall(
        body, name=name,
        out_shape=[SDS((seqlen, d), F32), SDS((seqlen, dkv), F32), SDS((seqlen, dkv), F32), SDS((nh, 128), F32)],
        grid=(nb + 1,),
        in_specs=[pl.BlockSpec(memory_space=pltpu.SMEM), pl.BlockSpec((blk, d), cur),
                  pl.BlockSpec((blk, dkv), cur), pl.BlockSpec((blk, dkv), prev),
                  pl.BlockSpec((blk, dkv), cur), pl.BlockSpec((blk, dkv), prev), pl.BlockSpec((blk, d), cur)],
        out_specs=[pl.BlockSpec((blk, d), cur), pl.BlockSpec((blk, dkv), lag), pl.BlockSpec((blk, dkv), lag),
                   pl.BlockSpec((nh, 128), lambda n: (0, 0))],
        scratch=[pltpu.VMEM((blk, dkv), F32)] * 2, vmem=32 << 20)(sinks, q, k, k, v, v, do)


def _ssm_mats(lre, lim, ldt, btr, bti, cr, ci):
    dt = jnp.exp(ldt)
    mag = jnp.exp(lre * dt)
    ar = mag * jnp.cos(lim * dt)
    ai = mag * jnp.sin(lim * dt)
    den = lre * lre + lim * lim
    nr = ar - 1.0
    cfr = (nr * lre + ai * lim) / den
    cfi = (ai * lre - nr * lim) / den
    bbr = cfr * btr - cfi * bti
    bbi = cfr * bti + cfi * btr
    pr = [jnp.ones_like(ar)]
    pi = [jnp.zeros_like(ai)]
    for _ in range(SSM_T):
        pr.append(pr[-1] * ar - pi[-1] * ai)
        pi.append(pr[-2] * ai + pi[-1] * ar)
    last = SSM_T - 1
    p_re = jnp.concatenate([pr[last - s] * bbr - pi[last - s] * bbi for s in range(SSM_T)], axis=0)
    p_im = jnp.concatenate([pr[last - s] * bbi + pi[last - s] * bbr for s in range(SSM_T)], axis=0)
    qt_re = jnp.concatenate([pr[t + 1] * cr - pi[t + 1] * ci for t in range(SSM_T)], axis=0)
    qt_im = jnp.concatenate([-(pr[t + 1] * ci + pi[t + 1] * cr) for t in range(SSM_T)], axis=0)
    ctr = jnp.concatenate([cr] * SSM_T, axis=0)
    cti = jnp.concatenate([ci] * SSM_T, axis=0)
    lag = (lax.broadcasted_iota(jnp.int32, (SSM_W, SSM_W), 1) // SSM_GROUP
           - lax.broadcasted_iota(jnp.int32, (SSM_W, SSM_W), 0) // SSM_GROUP)
    m = jnp.zeros((SSM_W, SSM_W), F32)
    for l in range(SSM_T):
        zr = jnp.concatenate([pr[l] * bbr - pi[l] * bbi] * SSM_T, axis=0)
        zi = jnp.concatenate([pr[l] * bbi + pi[l] * bbr] * SSM_T, axis=0)
        kl = _dot(zr, ctr, NT, HI) - _dot(zi, cti, NT, HI)
        m = m + jnp.where(lag == l, kl, 0.0)
    return m, p_re, p_im, qt_re, qt_im, pr[SSM_T], pi[SSM_T]


_SSM_GB = 8


def _ssm_param_specs(ng):
    n, hh = SSM_STATE, SSM_GROUP
    gb = _tile(ng, _SSM_GB)
    row = pl.BlockSpec((gb, 1, n), lambda i: (i, 0, 0))
    one = pl.BlockSpec((gb, 1, 1), lambda i: (i, 0, 0))
    mat = pl.BlockSpec((gb, hh, n), lambda i: (i, 0, 0))
    big = pl.BlockSpec((gb, SSM_W, SSM_W), lambda i: (i, 0, 0))
    half = pl.BlockSpec((gb, SSM_W, n), lambda i: (i, 0, 0))
    return gb, row, one, mat, big, half


def _ssm_prep(params, name):
    ng = params[0].shape[0]
    n = SSM_STATE
    gb, row, one, mat, big, half = _ssm_param_specs(ng)

    def body(lre, lim, ldt, btr, bti, cr, ci, m_ref, pre_ref, pim_ref, qre_ref, qim_ref, atr_ref, ati_ref):
        for gi in range(gb):
            outs = _ssm_mats(lre[gi], lim[gi], ldt[gi], btr[gi], bti[gi], cr[gi], ci[gi])
            for ref, val in zip((m_ref, pre_ref, pim_ref, qre_ref, qim_ref, atr_ref, ati_ref), outs):
                ref[gi] = val

    return _pcall(
        body, name=name,
        out_shape=[SDS((ng, SSM_W, SSM_W), F32)] + [SDS((ng, SSM_W, n), F32)] * 4 + [SDS((ng, 1, n), F32)] * 2,
        grid=(ng // gb,), in_specs=[row, row, one, mat, mat, mat, mat],
        out_specs=[big, half, half, half, half, row, row], vmem=40 << 20)(*params)


def _ssm_prep_vjp(params, cots, name):
    ng = params[0].shape[0]
    n, hh = SSM_STATE, SSM_GROUP
    gb, row, one, mat, big, half = _ssm_param_specs(ng)

    def body(lre, lim, ldt, btr, bti, cr, ci, dm, dpre, dpim, dqre, dqim, datr, dati,
             o_lre, o_lim, o_ldt, o_btr, o_bti, o_cr, o_ci):
        for gi in range(gb):
            prm = (lre[gi], lim[gi], ldt[gi], btr[gi], bti[gi], cr[gi], ci[gi])
            _, pull = jax.vjp(_ssm_mats, *prm)
            grads = pull((dm[gi], dpre[gi], dpim[gi], dqre[gi], dqim[gi], datr[gi], dati[gi]))
            for ref, val in zip((o_lre, o_lim, o_ldt, o_btr, o_bti, o_cr, o_ci), grads):
                ref[gi] = val

    return _pcall(
        body, name=name,
        out_shape=[SDS((ng, 1, n), F32)] * 2 + [SDS((ng, 1, 1), F32)] + [SDS((ng, hh, n), F32)] * 4,
        grid=(ng // gb,), in_specs=[row, row, one, mat, mat, mat, mat, big, half, half, half, half, row, row],
        out_specs=[row, row, one, mat, mat, mat, mat], vmem=48 << 20)(*params, *cots)


_SSM_GT = SSM_W // SSM_GROUP


def _tile_groups(x_ref, ncb):
    lane_blk = lax.broadcasted_iota(jnp.int32, (ncb, SSM_W), 1) // SSM_GROUP
    xs = [x_ref[pl.ds(t, ncb, stride=SSM_T), :] for t in range(SSM_T)]
    out = []
    for gl in range(_SSM_GT):
        u = jnp.zeros((ncb, SSM_W), F32)
        for t in range(SSM_T):
            sh = (SSM_GROUP * (t - gl)) % SSM_W
            u = jnp.where(lane_blk == t, pltpu.roll(xs[t], sh, 1) if sh else xs[t], u)
        out.append(u)
    return out


def _groups_tile(ys, o_ref, ncb):
    lane_blk = lax.broadcasted_iota(jnp.int32, (ncb, SSM_W), 1) // SSM_GROUP
    for t in range(SSM_T):
        y = jnp.zeros((ncb, SSM_W), F32)
        for gl in range(_SSM_GT):
            sh = (SSM_GROUP * (gl - t)) % SSM_W
            y = jnp.where(lane_blk == gl, pltpu.roll(ys[gl], sh, 1) if sh else ys[gl], y)
        o_ref[pl.ds(t, ncb, stride=SSM_T), :] = y


def _ssm_specs(seqlen, d):
    ncb = _tile(seqlen // SSM_T, 512)
    grid = (d // SSM_W, seqlen // (SSM_T * ncb))
    act = pl.BlockSpec((SSM_T * ncb, SSM_W), lambda j, r: (r, j))
    state = pl.BlockSpec((ncb, _SSM_GT * SSM_W), lambda j, r: (r, j))
    mats = pl.BlockSpec((_SSM_GT, SSM_W, SSM_W), lambda j, r: (j, 0, 0))
    return ncb, grid, act, state, mats


def _gsl(gl):
    return slice(gl * SSM_W, (gl + 1) * SSM_W)


def _ssm_state_in(hn, pmat, name):
    seqlen, d = hn.shape
    ncb, grid, act, state, mats = _ssm_specs(seqlen, d)

    def body(x_ref, p_ref, s_ref):
        us = _tile_groups(x_ref, ncb)
        for gl in range(_SSM_GT):
            s_ref[:, _gsl(gl)] = _dot(us[gl], p_ref[gl], NN, HI)

    return _pcall(body, name=name, out_shape=SDS((seqlen // SSM_T, d * SSM_T), F32), grid=grid,
                  in_specs=[act, mats], out_specs=state, vmem=40 << 20)(hn, pmat)


def _ssm_out(hn, xp, mmat, qt, name):
    seqlen, d = hn.shape
    ncb, grid, act, state, mats = _ssm_specs(seqlen, d)

    def body(x_ref, xp_ref, m_ref, q_ref, y_ref):
        us = _tile_groups(x_ref, ncb)
        ys = [_dot(us[gl], m_ref[gl], NN, HI) + _dot(xp_ref[:, _gsl(gl)], q_ref[gl], NT, HI)
              for gl in range(_SSM_GT)]
        _groups_tile(ys, y_ref, ncb)

    return _pcall(body, name=name, out_shape=SDS((seqlen, d), F32), grid=grid,
                  in_specs=[act, state, mats, mats], out_specs=act, vmem=40 << 20)(hn, xp, mmat, qt)


def _ssm_dstate(dy, qt, name):
    seqlen, d = dy.shape
    ncb, grid, act, state, mats = _ssm_specs(seqlen, d)

    def body(dy_ref, q_ref, o_ref):
        dys = _tile_groups(dy_ref, ncb)
        for gl in range(_SSM_GT):
            o_ref[:, _gsl(gl)] = _dot(dys[gl], q_ref[gl], NN, HI)

    return _pcall(body, name=name, out_shape=SDS((seqlen // SSM_T, d * SSM_T), F32), grid=grid,
                  in_specs=[act, mats], out_specs=state, vmem=40 << 20)(dy, qt)


def _ssm_bwd(hn, dy, xp, gs, mmat, pmat, name):
    seqlen, d = hn.shape
    ng = d // SSM_GROUP
    ncb, grid, act, state, mats = _ssm_specs(seqlen, d)

    def body(x_ref, dy_ref, xp_ref, g_ref, m_ref, p_ref, du_ref, dm_ref, dp_ref, dq_ref, da_ref):
        @pl.when(pl.program_id(1) == 0)
        def _():
            for ref in (dm_ref, dp_ref, dq_ref, da_ref):
                ref[...] = jnp.zeros_like(ref)
        us = _tile_groups(x_ref, ncb)
        dys = _tile_groups(dy_ref, ncb)
        dus = []
        for gl in range(_SSM_GT):
            xv, gv = xp_ref[:, _gsl(gl)], g_ref[:, _gsl(gl)]
            dus.append(_dot(dys[gl], m_ref[gl], NT, HI) + _dot(gv, p_ref[gl], NT, HI))
            dm_ref[gl] += _dot(us[gl], dys[gl], TN, HI)
            dp_ref[gl] += _dot(us[gl], gv, TN, HI)
            dq_ref[gl] += _dot(dys[gl], xv, TN, HI)
            da_ref[gl, 0:1, :] += jnp.sum(xv * gv, axis=0, keepdims=True)
            da_ref[gl, 1:2, :] += jnp.sum(xv * pltpu.roll(gv, SSM_STATE, 1), axis=0, keepdims=True)
        _groups_tile(dus, du_ref, ncb)

    return _pcall(
        body, name=name,
        out_shape=[SDS((seqlen, d), F32)] + [SDS((ng, SSM_W, SSM_W), F32)] * 3 + [SDS((ng, 2, SSM_W), F32)],
        grid=grid, in_specs=[act, act, state, state, mats, mats],
        out_specs=[act, mats, mats, mats, pl.BlockSpec((_SSM_GT, 2, SSM_W), lambda j, r: (j, 0, 0))],
        vmem=48 << 20)(hn, dy, xp, gs, mmat, pmat)


def _ssm_carry(s, a1, a2, reverse, name):
    nc, w = s.shape
    tc = _tile(nc, 256)
    nblk = nc // tc
    sub = 8

    def body(s_ref, a1_ref, a2_ref, o_ref, st_ref, sw_ref):
        @pl.when(pl.program_id(0) == 0)
        def _():
            st_ref[...] = jnp.zeros_like(st_ref)
            sw_ref[...] = jnp.zeros_like(sw_ref)
        a1v = jnp.broadcast_to(a1_ref[...], (sub, w))
        a2v = jnp.broadcast_to(a2_ref[...], (sub, w))
        first = lax.broadcasted_iota(jnp.int32, (sub, w), 1) % SSM_W < SSM_STATE
        row = lax.broadcasted_iota(jnp.int32, (sub, w), 0)

        def step(t, carry):
            x, xs = carry
            tt = (tc // sub - 1 - t) if reverse else t
            base = pl.multiple_of(tt * sub, sub)
            blk = s_ref[pl.ds(base, sub), :]
            blks = jnp.where(first, pltpu.roll(blk, w - SSM_STATE, 1), pltpu.roll(blk, SSM_STATE, 1))
            out = jnp.zeros((sub, w), F32)
            for r in (range(sub - 1, -1, -1) if reverse else range(sub)):
                out = jnp.where(row == r, x, out)
                sr = jnp.broadcast_to(blk[r:r + 1, :], (sub, w))
                ssr = jnp.broadcast_to(blks[r:r + 1, :], (sub, w))
                x, xs = a1v * x + a2v * xs + sr, a1v * xs - a2v * x + ssr
            o_ref[pl.ds(base, sub), :] = out
            return x, xs

        x, xs = lax.fori_loop(0, tc // sub, step, (st_ref[...], sw_ref[...]))
        st_ref[...] = x
        sw_ref[...] = xs

    imap = (lambda i: (nblk - 1 - i, 0)) if reverse else (lambda i: (i, 0))
    cst = pl.BlockSpec((1, w), lambda i: (0, 0))
    return _pcall(body, name=name, out_shape=SDS((nc, w), F32), grid=(nblk,),
                  in_specs=[pl.BlockSpec((tc, w), imap), cst, cst], out_specs=pl.BlockSpec((tc, w), imap),
                  scratch=[pltpu.VMEM((sub, w), F32)] * 2,
                  vmem=_vmem_limit([((tc, w), F32, 4)], extra=8 << 20))(s, a1, a2)


def _ssm_rows(atr, ati, conj):
    ng = atr.shape[0]
    ai = -ati if conj else ati
    a1 = jnp.concatenate([atr, atr], axis=2).reshape(1, ng * SSM_W)
    a2 = jnp.concatenate([-ai, ai], axis=2).reshape(1, ng * SSM_W)
    return a1, a2


def _peers():
    x, y, c = (lax.axis_index(a) for a in AXES)
    me = 4 * x + 2 * y + c
    peers = []
    for dx, dy, dc in [(0, 0, 1), (0, 1, 0), (0, 1, 1), (1, 0, 0), (1, 0, 1), (1, 1, 0), (1, 1, 1)]:
        px, py, pc = (1 - x) if dx else x, (1 - y) if dy else y, (1 - c) if dc else c
        peers.append(((px, py, pc), 4 * px + 2 * py + pc))
    return me, peers


def _exchange(arrs, scatter, name):
    n = len(arrs)
    npeer = NDEV - 1
    out_shape = [SDS(a.shape if scatter else (NDEV,) + a.shape, a.dtype) for a in arrs]

    def body(*refs):
        ins, outs = refs[:n], refs[n:2 * n]
        ssem, rsem, lsem = refs[2 * n:]
        me, peers = _peers()
        local = []
        for a in range(n):
            cp = pltpu.make_async_copy(ins[a].at[me] if scatter else ins[a], outs[a].at[me], lsem.at[a])
            cp.start()
            local.append(cp)
        for k, (pid, pidx) in enumerate(peers):
            for a in range(n):
                pltpu.make_async_remote_copy(
                    src_ref=ins[a].at[pidx] if scatter else ins[a], dst_ref=outs[a].at[me],
                    send_sem=ssem.at[a * npeer + k], recv_sem=rsem.at[a * npeer + k],
                    device_id=pid, device_id_type=MESH).start()
        for cp in local:
            cp.wait()
        for k, (pid, pidx) in enumerate(peers):
            for a in range(n):
                cp = pltpu.make_async_remote_copy(
                    src_ref=ins[a].at[pidx] if scatter else ins[a], dst_ref=outs[a].at[pidx],
                    send_sem=ssem.at[a * npeer + k], recv_sem=rsem.at[a * npeer + k],
                    device_id=pid, device_id_type=MESH)
                cp.wait_send()
                cp.wait_recv()

    hbm = pl.BlockSpec(memory_space=pl.ANY)
    return pl.pallas_call(
        body, out_shape=out_shape, in_specs=[hbm] * n, out_specs=[hbm] * n,
        scratch_shapes=[pltpu.SemaphoreType.DMA((n * npeer,)), pltpu.SemaphoreType.DMA((n * npeer,)),
                        pltpu.SemaphoreType.DMA((n,))],
        name=name, interpret=False)(*arrs)


def _adamw(parts, w, m, v, name):
    rows, cols = w.shape
    tr = _tile(rows, max(8, (1 << 17) // cols))
    c1 = 1.0 - ADAM_B1 ** ADAM_STEP
    c2 = 1.0 - ADAM_B2 ** ADAM_STEP

    def body(p_ref, w_ref, m_ref, v_ref, g_ref, d_ref, nm_ref, nv_ref):
        g = p_ref[0]
        for j in range(1, NDEV):
            g = g + p_ref[j]
        mm = ADAM_B1 * m_ref[...] + (1.0 - ADAM_B1) * g
        vv = ADAM_B2 * v_ref[...] + (1.0 - ADAM_B2) * (g * g)
        g_ref[...] = g
        nm_ref[...] = mm
        nv_ref[...] = vv
        d_ref[...] = -ADAM_LR * ((mm / c1) / (jnp.sqrt(vv / c2) + ADAM_EPS) + ADAM_WD * w_ref[...])

    spec = pl.BlockSpec((tr, cols), lambda i: (i, 0))
    return _pcall(
        body, name=name, out_shape=[SDS((rows, cols), F32)] * 4, grid=(rows // tr,),
        in_specs=[pl.BlockSpec((NDEV, tr, cols), lambda i: (0, i, 0)), spec, spec, spec], out_specs=[spec] * 4,
        vmem=_vmem_limit([((NDEV + 7, tr, cols), F32, 2)]))(parts, w, m, v)


def kernel(x, p, norm_mix, ssm_lambda_re, ssm_lambda_im, ssm_log_dt, ssm_b_re, ssm_b_im, ssm_c_re, ssm_c_im, ssm_d, ssm_w_glu, kv_norm, w_k, w_v, w_q, attn_sinks, w_o, norm_mlp, w_up, w_down, norm_ple, w_ple_gate, w_ple_proj, norm_final, loss_target, m_norm_mix, m_ssm_lambda_re, m_ssm_lambda_im, m_ssm_log_dt, m_ssm_b_re, m_ssm_b_im, m_ssm_c_re, m_ssm_c_im, m_ssm_d, m_ssm_w_glu, m_kv_norm, m_w_k, m_w_v, m_w_q, m_attn_sinks, m_w_o, m_norm_mlp, m_w_up, m_w_down, m_norm_ple, m_w_ple_gate, m_w_ple_proj, m_norm_final, v_norm_mix, v_ssm_lambda_re, v_ssm_lambda_im, v_ssm_log_dt, v_ssm_b_re, v_ssm_b_im, v_ssm_c_re, v_ssm_c_im, v_ssm_d, v_ssm_w_glu, v_kv_norm, v_w_k, v_w_v, v_w_q, v_attn_sinks, v_w_o, v_norm_mlp, v_w_up, v_w_down, v_norm_ple, v_w_ple_gate, v_w_ple_proj, v_norm_final):
    names = ['norm_mix', 'ssm_lambda_re', 'ssm_lambda_im', 'ssm_log_dt', 'ssm_b_re', 'ssm_b_im', 'ssm_c_re',
             'ssm_c_im', 'ssm_d', 'ssm_w_glu', 'kv_norm', 'w_k', 'w_v', 'w_q', 'attn_sinks', 'w_o', 'norm_mlp',
             'w_up', 'w_down', 'norm_ple', 'w_ple_gate', 'w_ple_proj', 'norm_final']
    weights = dict(zip(names, (norm_mix, ssm_lambda_re, ssm_lambda_im, ssm_log_dt, ssm_b_re, ssm_b_im, ssm_c_re,
                               ssm_c_im, ssm_d, ssm_w_glu, kv_norm, w_k, w_v, w_q, attn_sinks, w_o, norm_mlp,
                               w_up, w_down, norm_ple, w_ple_gate, w_ple_proj, norm_final)))
    mom1 = dict(zip(names, (m_norm_mix, m_ssm_lambda_re, m_ssm_lambda_im, m_ssm_log_dt, m_ssm_b_re, m_ssm_b_im,
                            m_ssm_c_re, m_ssm_c_im, m_ssm_d, m_ssm_w_glu, m_kv_norm, m_w_k, m_w_v, m_w_q,
                            m_attn_sinks, m_w_o, m_norm_mlp, m_w_up, m_w_down, m_norm_ple, m_w_ple_gate,
                            m_w_ple_proj, m_norm_final)))
    mom2 = dict(zip(names, (v_norm_mix, v_ssm_lambda_re, v_ssm_lambda_im, v_ssm_log_dt, v_ssm_b_re, v_ssm_b_im,
                            v_ssm_c_re, v_ssm_c_im, v_ssm_d, v_ssm_w_glu, v_kv_norm, v_w_k, v_w_v, v_w_q,
                            v_attn_sinks, v_w_o, v_norm_mlp, v_w_up, v_w_down, v_norm_ple, v_w_ple_gate,
                            v_w_ple_proj, v_norm_final)))

    seqlen, d = x.shape[1], x.shape[2]
    depth = w_up.shape[0]
    n_ssm = ssm_w_glu.shape[0]
    n_att = w_q.shape[0]
    ng = d // SSM_GROUP
    nh = d // HEAD_DIM
    h0 = x[0]
    tgt = loss_target[0]
    tabs = _rope_tables(seqlen)

    sharded = ['w_up', 'w_down', 'w_ple_gate', 'w_ple_proj', 'ssm_w_glu', 'w_q', 'w_o', 'w_k', 'w_v']
    gath = _exchange([weights[k].astype(BF16) for k in sharded] + [ssm_d], False, "gather_weights")
    wup_g, wdn_g, wg_g, wpp_g, wglu_g, wq_g, wo_g, wk_g, wv_g, dsk_g = gath
    dkv = wk_g.shape[2]
    wg = [wg_g[:, i].reshape(d, d) for i in range(depth)]
    wpp = [wpp_g[:, i].transpose(1, 0, 2).reshape(wpp_g.shape[2], d) for i in range(depth)]
    wglu = [wglu_g[:, i].transpose(1, 0, 2).reshape(d, 2 * d) for i in range(n_ssm)]
    wq = [wq_g[:, j].reshape(d, d) for j in range(n_att)]
    wo = [wo_g[:, j].reshape(d, d) for j in range(n_att)]
    wk = wk_g.reshape(d, dkv)
    wv = wv_g.reshape(d, dkv)
    dskip = dsk_g.transpose(1, 0, 2).reshape(n_ssm, d)

    def ssm_params(i):
        n = SSM_STATE
        return (ssm_lambda_re[i].reshape(ng, 1, n), ssm_lambda_im[i].reshape(ng, 1, n),
                ssm_log_dt[i].reshape(ng, 1, 1), jnp.swapaxes(ssm_b_re[i], 1, 2), jnp.swapaxes(ssm_b_im[i], 1, 2),
                ssm_c_re[i], ssm_c_im[i])

    h = h0
    h_in, h_a, h_b = [], [], []
    ssm_saved, att_saved = {}, {}
    k_sh = v_sh = None
    for i in range(depth):
        h_in.append(h)
        if i < n_ssm:
            hn = _norm_fwd(h, norm_mix[i], f"norm_mix_fwd{i}")
            mats = _ssm_prep(ssm_params(i), f"ssm_prep{i}")
            mmat, atr, ati = mats[0], mats[5], mats[6]
            pmat = jnp.concatenate([mats[1], mats[2]], axis=2)
            qt = jnp.concatenate([mats[3], mats[4]], axis=2)
            s_in = _ssm_state_in(hn, pmat, f"ssm_state_in{i}")
            xp = _ssm_carry(s_in, *_ssm_rows(atr, ati, False), False, f"ssm_carry_fwd{i}")
            y = _ssm_out(hn, xp, mmat, qt, f"ssm_out{i}")
            ha = _glu_fwd(y, hn, h, dskip[i], wglu[i], f"glu_fwd{i}")
            ssm_saved[i] = (hn, mmat, pmat, qt, atr, ati, xp, y)
        else:
            j = i - n_ssm
            q = _q_fwd(h, norm_mix[i], wq[j], tabs, f"q_fwd{j}")
            o = _attn_fwd(q, k_sh, v_sh, attn_sinks[j], f"attn_fwd{j}")
            ha = _lin_res(h, o, wo[j], f"attn_out{j}")
            att_saved[j] = (q, o)
        h_a.append(ha)
        hb = _mlp_fwd(ha, norm_mlp[i], wup_g, wdn_g, i, f"mlp_fwd{i}")
        h_b.append(hb)
        h = _ple_fwd(hb, p[i, 0], norm_ple[i], wg[i], wpp[i], f"ple_fwd{i}")
        if i == n_ssm - 1:
            k_sh, v_sh = _kv_fwd(h, kv_norm, wk, wv, tabs, "kv_fwd")
    h_kv = h_in[n_ssm] if n_ssm < depth else h
    dh, loss_row, g_norm_final = _loss_bwd(h, norm_final, tgt, "loss_bwd")
    loss = lax.psum(loss_row[0, 0], AXES)

    zeros = lambda *s: jnp.zeros(s, F32)
    fb = wup_g.shape[3]
    rs = d // NDEV
    st = {'w_up': zeros(NDEV, depth, d, fb), 'w_down': zeros(NDEV, depth, fb, d),
          'w_ple_gate': zeros(NDEV, depth, rs, d), 'w_ple_proj': zeros(NDEV, depth, wpp_g.shape[2], rs),
          'ssm_w_glu': zeros(NDEV, n_ssm, d, 2 * d // NDEV), 'w_q': zeros(NDEV, n_att, rs, d),
          'w_o': zeros(NDEV, n_att, rs, d), 'w_k': zeros(NDEV, 1, rs, dkv), 'w_v': zeros(NDEV, 1, rs, dkv)}
    g_norm_mix, g_norm_mlp, g_norm_ple = [None] * depth, [None] * depth, [None] * depth
    g_dskip, g_ssm, g_sinks = [None] * n_ssm, [None] * n_ssm, [None] * n_att
    g_kv_norm = None
    dks, dvs = [], []
    for i in range(depth - 1, -1, -1):
        if i == n_ssm - 1:
            dh, dkp, dvb, hkb, g_kv_norm = _kv_bwd(dks, dvs, h_kv, kv_norm, dh, wk, wv, tabs, "kv_bwd")
            st['w_k'] = _atb(hkb, dkp, st['w_k'], 0, False, "grad_w_k")
            st['w_v'] = _atb(hkb, dvb, st['w_v'], 0, False, "grad_w_v")
        dhb, dz, nb16, dpp, g_norm_ple[i] = _ple_bwd(h_b[i], p[i, 0], dh, norm_ple[i], wg[i], wpp[i], f"ple_bwd{i}")
        st['w_ple_gate'] = _atb(nb16, dz, st['w_ple_gate'], i, False, f"grad_w_ple_gate{i}")
        st['w_ple_proj'] = _atb(p[i, 0], dpp, st['w_ple_proj'], i, True, f"grad_w_ple_proj{i}")
        dha, hmb, da, act, g_norm_mlp[i] = _mlp_bwd(h_a[i], dhb, norm_mlp[i], wup_g, wdn_g, i, f"mlp_bwd{i}")
        st['w_up'] = _atb(hmb, da, st['w_up'], i, True, f"grad_w_up{i}")
        st['w_down'] = _atb(act, dhb, st['w_down'], i, False, f"grad_w_down{i}")
        if i >= n_ssm:
            j = i - n_ssm
            q, o = att_saved[j]
            do = _lin_nt(dha, wo[j], f"attn_out_bwd{j}")
            st['w_o'] = _atb(o, dha, st['w_o'], j, False, f"grad_w_o{j}")
            dq, dk_j, dv_j, dsink = _attn_bwd(q, k_sh, v_sh, do, attn_sinks[j], f"attn_bwd{j}")
            dks.append(dk_j)
            dvs.append(dv_j)
            g_sinks[j] = dsink[:, 0]
            dh, dqp, hnb, g_norm_mix[i] = _q_bwd(dq, h_in[i], norm_mix[i], dha, wq[j], tabs, f"q_bwd{j}")
            st['w_q'] = _atb(hnb, dqp, st['w_q'], j, False, f"grad_w_q{j}")
        else:
            hn, mmat, pmat, qt, atr, ati, xp, y = ssm_saved[i]
            dyy, dhn_d, geb, dab, g_dskip[i] = _glu_bwd(y, hn, dha, dskip[i], wglu[i], f"glu_bwd{i}")
            st['ssm_w_glu'] = _atb(geb, dab, st['ssm_w_glu'], i, True, f"grad_ssm_w_glu{i}")
            dxp = _ssm_dstate(dyy, qt, f"ssm_dstate{i}")
            gs = _ssm_carry(dxp, *_ssm_rows(atr, ati, True), True, f"ssm_carry_bwd{i}")
            du, dm, dp, dqt, da_raw = _ssm_bwd(hn, dyy, xp, gs, mmat, pmat, f"ssm_bwd{i}")
            n = SSM_STATE
            cots = (dm, dp[:, :, :n], dp[:, :, n:], dqt[:, :, :n], dqt[:, :, n:],
                    (da_raw[:, 0:1, :n] + da_raw[:, 0:1, n:]), (da_raw[:, 1:2, :n] - da_raw[:, 1:2, n:]))
            g_ssm[i] = _ssm_prep_vjp(ssm_params(i), cots, f"ssm_prep_vjp{i}")
            dh, g_norm_mix[i] = _norm_bwd(h_in[i], norm_mix[i], dhn_d, du, dha, f"norm_mix_bwd{i}")
    grad_x = dh[None]

    big = ['ssm_d', 'ssm_w_glu', 'w_k', 'w_v', 'w_q', 'w_o', 'w_up', 'w_down', 'w_ple_gate', 'w_ple_proj']
    st['ssm_d'] = jnp.concatenate(g_dskip, axis=0).reshape(n_ssm, NDEV, rs).transpose(1, 0, 2)
    recv = dict(zip(big, _exchange([st[k] for k in big], True, "scatter_grads")))

    out_g, out_d, out_m, out_v = {}, {}, {}, {}
    for k in big:
        shp = weights[k].shape
        r2 = (math.prod(shp[:-1]), shp[-1])
        res = _adamw(recv[k].reshape((NDEV,) + r2), weights[k].reshape(r2), mom1[k].reshape(r2),
                     mom2[k].reshape(r2), f"adamw_{k}")
        out_g[k], out_d[k], out_m[k], out_v[k] = (t.reshape(shp) for t in res)

    def ssm_grad(idx, unswap=False):
        g = jnp.stack([g_ssm[i][idx] for i in range(n_ssm)])
        return jnp.swapaxes(g, 2, 3) if unswap else g

    small = {'norm_mix': jnp.concatenate(g_norm_mix, axis=0),
             'ssm_lambda_re': ssm_grad(0), 'ssm_lambda_im': ssm_grad(1), 'ssm_log_dt': ssm_grad(2),
             'ssm_b_re': ssm_grad(3, True), 'ssm_b_im': ssm_grad(4, True),
             'ssm_c_re': ssm_grad(5), 'ssm_c_im': ssm_grad(6),
             'kv_norm': g_kv_norm, 'attn_sinks': jnp.stack(g_sinks),
             'norm_mlp': jnp.concatenate(g_norm_mlp, axis=0), 'norm_ple': jnp.concatenate(g_norm_ple, axis=0),
             'norm_final': g_norm_final}
    snames = list(small)
    sizes = [weights[k].size for k in snames]
    total = sum(sizes)
    lanes = 128
    padded = -(-total // (512 * lanes)) * (512 * lanes)

    def flat(parts):
        v = jnp.concatenate([t.reshape(-1) for t in parts] + [jnp.zeros((padded - total,), F32)])
        return v.reshape(padded // lanes, lanes)

    parts = _exchange([flat([small[k] for k in snames])], False, "gather_small_grads")[0]
    res = _adamw(parts, flat([weights[k] for k in snames]), flat([mom1[k] for k in snames]),
                 flat([mom2[k] for k in snames]), "adamw_small")
    off = 0
    for k, sz in zip(snames, sizes):
        for dst, t in zip((out_g, out_d, out_m, out_v), res):
            dst[k] = t.reshape(-1)[off:off + sz].reshape(weights[k].shape)
        off += sz

    return (loss, grad_x, *[out_g[k] for k in names], *[out_d[k] for k in names],
            *[out_m[k] for k in names], *[out_v[k] for k in names])
```

```python
import functools
import math

import jax
import jax.numpy as jnp
from jax import lax
from jax.experimental import pallas as pl
from jax.experimental.pallas import tpu as pltpu

F32 = jnp.float32
BF16 = jnp.bfloat16
SDS = jax.ShapeDtypeStruct
MESH = pl.DeviceIdType.MESH
AXES = ("x", "y", "c")
NDEV = 8

RMS_EPS = 1e-6
SSM_GROUP = 16
SSM_STATE = 64
SSM_T = 8
SSM_W = SSM_T * SSM_GROUP
HEAD_DIM = 64
GQA_GROUP = 4
ATTN_BLOCK = 128
ROT_DIM = 16
ROPE_THETA = 500000.0
NEG_INF = -1e30
ADAM_LR, ADAM_B1, ADAM_B2, ADAM_EPS, ADAM_WD, ADAM_STEP = 0.001, 0.9, 0.999, 1e-08, 0.01, 10

VMEM_CAP = 56 * 1024 * 1024
HI = lax.Precision.HIGHEST

NN = ((1,), (0,))
NT = ((1,), (1,))
TN = ((0,), (0,))


def _dot(a, b, dims=NN, precision=None):
    return lax.dot_general(a, b, (dims, ((), ())), preferred_element_type=F32, precision=precision)


def _tile(n, pref):
    t = min(n, pref)
    while n % t:
        t //= 2
    return t


def _nbytes(shape, dtype):
    return math.prod(s for s in shape if s is not None) * jnp.dtype(dtype).itemsize


def _vmem_limit(blocks, extra=0):
    need = sum(_nbytes(s, d) * n for s, d, n in blocks) + extra + (4 << 20)
    return int(min(VMEM_CAP, max(need, 16 << 20)))


def _pcall(body, *, name, out_shape, grid, in_specs, out_specs, scratch=(), aliases=None, vmem=None):
    return pl.pallas_call(
        body, out_shape=out_shape, grid=grid, in_specs=in_specs, out_specs=out_specs,
        scratch_shapes=scratch, input_output_aliases=aliases or {}, name=name,
        compiler_params=pltpu.CompilerParams(
            dimension_semantics=("arbitrary",) * len(grid), vmem_limit_bytes=vmem),
        interpret=False)


def _rms(x, g):
    r = lax.rsqrt(jnp.mean(x * x, axis=-1, keepdims=True) + RMS_EPS)
    return x * r * g, r


def _rms_bwd(x, g, r, dy):
    xh = x * r
    dyg = dy * g
    dx = r * (dyg - xh * jnp.mean(dyg * xh, axis=-1, keepdims=True))
    return dx, jnp.sum(dy * xh, axis=0, keepdims=True)


_GELU_C = math.sqrt(2.0 / math.pi)


def _gelu_parts(x):
    t = jnp.tanh(_GELU_C * (x + 0.044715 * x * x * x))
    return 0.5 * x * (1.0 + t), t


def _gelu_grad(x, t):
    return 0.5 * (1.0 + t) + 0.5 * x * (1.0 - t * t) * _GELU_C * (1.0 + 3 * 0.044715 * x * x)


def _rope_tables(seqlen):
    half = ROT_DIM // 2
    inv = ROPE_THETA ** (-jnp.arange(0, ROT_DIM, 2, dtype=F32) / ROT_DIM)
    ang = jnp.arange(seqlen, dtype=jnp.int32).astype(F32)[:, None] * inv[None, :]
    cos, sin = jnp.cos(ang), jnp.sin(ang)
    zeros = jnp.zeros((seqlen, HEAD_DIM - ROT_DIM), F32)
    zh = jnp.zeros((seqlen, half), F32)
    c = jnp.concatenate([cos, cos, zeros + 1.0], axis=1)
    sa = jnp.concatenate([zh, sin, zeros], axis=1)
    sb = jnp.concatenate([-sin, zh, zeros], axis=1)
    return tuple(jnp.tile(t, (1, 128 // HEAD_DIM)) for t in (c, sa, sb))


def _rope(x, c, sa, sb):
    w = x.shape[1]
    reps = w // 128
    half = ROT_DIM // 2
    return (x * jnp.tile(c, (1, reps)) + pltpu.roll(x, half, 1) * jnp.tile(sa, (1, reps))
            + pltpu.roll(x, w - half, 1) * jnp.tile(sb, (1, reps)))


def _rope_bwd(dy, c, sa, sb):
    w = dy.shape[1]
    reps = w // 128
    half = ROT_DIM // 2
    return (dy * jnp.tile(c, (1, reps)) + pltpu.roll(dy * jnp.tile(sa, (1, reps)), w - half, 1)
            + pltpu.roll(dy * jnp.tile(sb, (1, reps)), half, 1))


def _rspec(tm, c):
    return pl.BlockSpec((tm, c), lambda i: (i, 0))


def _cspec(shape, idx=None):
    idx = tuple(idx) if idx is not None else (0,) * len(shape)
    return pl.BlockSpec(tuple(shape), lambda i: idx, pipeline_mode=pl.Buffered(1))


def _rowcall(body, name, seqlen, tm, rows_in, consts_in, rows_out, acc_out=(), extra_vmem=0, comm=None):
    in_specs = [_rspec(tm, a.shape[1]) for a in rows_in] + [_cspec(bs, ix) for _, bs, ix in consts_in]
    out_shape = [SDS((seqlen, c), d) for c, d in rows_out] + [SDS(s, F32) for s in acc_out]
    out_specs = [_rspec(tm, c) for c, _ in rows_out] + [pl.BlockSpec(s, lambda i: (0, 0)) for s in acc_out]
    blocks = ([((tm, a.shape[1]), a.dtype, 2) for a in rows_in] + [(bs, a.dtype, 1) for a, bs, _ in consts_in]
              + [((tm, c), d, 2) for c, d in rows_out])
    temporaries = 12 * tm * rows_in[0].shape[1] * 4
    args = [*rows_in, *[a for a, _, _ in consts_in]]
    steps = seqlen // tm
    scratch = ()
    if comm is not None:
        n_in, n_out, nx = len(in_specs), len(out_specs), len(comm.arrs)
        hbm = pl.BlockSpec(memory_space=pl.ANY)
        in_specs = in_specs + [hbm] * nx
        out_specs = out_specs + [hbm] * nx
        out_shape = out_shape + comm.out_shape()
        args = args + list(comm.arrs)
        scratch = comm.semaphores()
        inner = body

        def body(*refs):
            ins, xin = refs[:n_in], refs[n_in:n_in + nx]
            outs, xout = refs[n_in + nx:n_in + nx + n_out], refs[n_in + nx + n_out:n_in + 2 * nx + n_out]
            sems = refs[n_in + 2 * nx + n_out:]

            @pl.when(pl.program_id(0) == 0)
            def _():
                comm.start(xin, xout, sems)
            inner(*ins, *outs)

            @pl.when(pl.program_id(0) == steps - 1)
            def _():
                comm.wait(xin, xout, sems)

    return _pcall(body, name=name, out_shape=out_shape, grid=(steps,), in_specs=in_specs, out_specs=out_specs,
                  scratch=scratch, vmem=_vmem_limit(blocks, extra_vmem + temporaries))(*args)


def _whole(a):
    return (a, a.shape, None)


def _norm_fwd(h, g, name):
    seqlen, d = h.shape
    tm = _tile(seqlen, 1024)

    def body(h_ref, g_ref, o_ref):
        o_ref[...] = _rms(h_ref[...], g_ref[...])[0]

    return _rowcall(body, name, seqlen, tm, [h], [_whole(g.reshape(1, d))], [(d, F32)])[0]


def _norm_bwd(h, g, dy1, dy2, dres, name):
    seqlen, d = h.shape
    tm = _tile(seqlen, 512)

    def body(h_ref, dy1_ref, dy2_ref, dres_ref, g_ref, dh_ref, dg_ref):
        @pl.when(pl.program_id(0) == 0)
        def _():
            dg_ref[...] = jnp.zeros_like(dg_ref)
        x = h_ref[...]
        gv = g_ref[...]
        _, r = _rms(x, gv)
        dx, dg = _rms_bwd(x, gv, r, dy1_ref[...] + dy2_ref[...])
        dh_ref[...] = dres_ref[...] + dx
        dg_ref[...] += dg

    return _rowcall(body, name, seqlen, tm, [h, dy1, dy2, dres], [_whole(g.reshape(1, d))], [(d, F32)], [(1, d)])


def _glu_fwd(y, hn, h, dskip, wglu, name):
    seqlen, d = h.shape
    tm = _tile(seqlen, 512)

    def body(y_ref, hn_ref, h_ref, d_ref, w_ref, o_ref):
        yy = y_ref[...] + d_ref[...] * hn_ref[...]
        ge, _ = _gelu_parts(yy)
        ab = _dot(ge.astype(BF16), w_ref[...])
        o_ref[...] = h_ref[...] + ab[:, :d] * jax.nn.sigmoid(ab[:, d:])

    return _rowcall(body, name, seqlen, tm, [y, hn, h], [_whole(dskip.reshape(1, d)), _whole(wglu)], [(d, F32)],
                    extra_vmem=tm * d * 4 * 6)[0]


def _glu_bwd(y, hn, dmix, dskip, wglu, name):
    seqlen, d = hn.shape
    tm = _tile(seqlen, 512)

    def body(y_ref, hn_ref, dm_ref, d_ref, w_ref, dyy_ref, dhn_ref, ge_ref, dab_ref, dd_ref):
        @pl.when(pl.program_id(0) == 0)
        def _():
            dd_ref[...] = jnp.zeros_like(dd_ref)
        hn_v = hn_ref[...]
        dsk = d_ref[...]
        yy = y_ref[...] + dsk * hn_v
        ge, t = _gelu_parts(yy)
        geb = ge.astype(BF16)
        ab = _dot(geb, w_ref[...])
        a = ab[:, :d]
        sg = jax.nn.sigmoid(ab[:, d:])
        dm = dm_ref[...]
        dab_ref[:, :d] = (dm * sg).astype(BF16)
        dab_ref[:, d:] = (dm * a * sg * (1.0 - sg)).astype(BF16)
        dge = _dot(dab_ref[...], w_ref[...], NT)
        dyy = dge * _gelu_grad(yy, t)
        dyy_ref[...] = dyy
        dhn_ref[...] = dyy * dsk
        ge_ref[...] = geb
        dd_ref[...] += jnp.sum(dyy * hn_v, axis=0, keepdims=True)

    return _rowcall(body, name, seqlen, tm, [y, hn, dmix], [_whole(dskip.reshape(1, d)), _whole(wglu)],
                    [(d, F32), (d, F32), (d, BF16), (2 * d, BF16)], [(1, d)], extra_vmem=tm * d * 4 * 8)


def _q_fwd(h, g, wq, tabs, name):
    seqlen, d = h.shape
    tm = _tile(seqlen, 512)

    def body(h_ref, c_ref, sa_ref, sb_ref, g_ref, w_ref, q_ref):
        hn, _ = _rms(h_ref[...], g_ref[...])
        qp = _dot(hn.astype(BF16), w_ref[...])
        q_ref[...] = _rope(qp, c_ref[...], sa_ref[...], sb_ref[...]).astype(BF16)

    return _rowcall(body, name, seqlen, tm, [h, *tabs], [_whole(g.reshape(1, d)), _whole(wq)], [(d, BF16)],
                    extra_vmem=tm * d * 4 * 6)[0]


def _q_bwd(dq, h, g, dres, wq, tabs, name):
    seqlen, d = h.shape
    tm = _tile(seqlen, 512)

    def body(dq_ref, h_ref, dres_ref, c_ref, sa_ref, sb_ref, g_ref, w_ref, dh_ref, dqp_ref, hn_ref, dg_ref):
        @pl.when(pl.program_id(0) == 0)
        def _():
            dg_ref[...] = jnp.zeros_like(dg_ref)
        dqp = _rope_bwd(dq_ref[...], c_ref[...], sa_ref[...], sb_ref[...]).astype(BF16)
        x = h_ref[...]
        gv = g_ref[...]
        hn, r = _rms(x, gv)
        dhn = _dot(dqp, w_ref[...], NT)
        dx, dg = _rms_bwd(x, gv, r, dhn)
        dh_ref[...] = dres_ref[...] + dx
        dqp_ref[...] = dqp
        hn_ref[...] = hn.astype(BF16)
        dg_ref[...] += dg

    return _rowcall(body, name, seqlen, tm, [dq, h, dres, *tabs], [_whole(g.reshape(1, d)), _whole(wq)],
                    [(d, F32), (d, BF16), (d, BF16)], [(1, d)], extra_vmem=tm * d * 4 * 6)


def _kv_fwd(h, g, wk, wv, tabs, name):
    seqlen, d = h.shape
    dk = wk.shape[1]
    tm = _tile(seqlen, 512)

    def body(h_ref, c_ref, sa_ref, sb_ref, g_ref, wk_ref, wv_ref, k_ref, v_ref):
        hk = _rms(h_ref[...], g_ref[...])[0].astype(BF16)
        k_ref[...] = _rope(_dot(hk, wk_ref[...]), c_ref[...], sa_ref[...], sb_ref[...]).astype(BF16)
        v_ref[...] = _dot(hk, wv_ref[...]).astype(BF16)

    return _rowcall(body, name, seqlen, tm, [h, *tabs], [_whole(g.reshape(1, d)), _whole(wk), _whole(wv)],
                    [(dk, BF16), (dk, BF16)], extra_vmem=tm * d * 4 * 4)


def _kv_bwd(dks, dvs, h, g, dres, wk, wv, tabs, name):
    seqlen, d = h.shape
    dkw = wk.shape[1]
    tm = _tile(seqlen, 512)

    def body(dk0_ref, dk1_ref, dv0_ref, dv1_ref, h_ref, dres_ref, c_ref, sa_ref, sb_ref, g_ref, wk_ref, wv_ref,
             dh_ref, dkp_ref, dvb_ref, hk_ref, dg_ref):
        @pl.when(pl.program_id(0) == 0)
        def _():
            dg_ref[...] = jnp.zeros_like(dg_ref)
        dkp = _rope_bwd(dk0_ref[...] + dk1_ref[...], c_ref[...], sa_ref[...], sb_ref[...]).astype(BF16)
        dvb = (dv0_ref[...] + dv1_ref[...]).astype(BF16)
        x = h_ref[...]
        gv = g_ref[...]
        hk, r = _rms(x, gv)
        dhk = _dot(dkp, wk_ref[...], NT) + _dot(dvb, wv_ref[...], NT)
        dx, dg = _rms_bwd(x, gv, r, dhk)
        dh_ref[...] = dres_ref[...] + dx
        dkp_ref[...] = dkp
        dvb_ref[...] = dvb
        hk_ref[...] = hk.astype(BF16)
        dg_ref[...] += dg

    return _rowcall(body, name, seqlen, tm, [dks[0], dks[1], dvs[0], dvs[1], h, dres, *tabs],
                    [_whole(g.reshape(1, d)), _whole(wk), _whole(wv)],
                    [(d, F32), (dkw, BF16), (dkw, BF16), (d, BF16)], [(1, d)], extra_vmem=tm * d * 4 * 6)


def _lin_res(h, xb, w, name):
    seqlen, d = h.shape
    tm = _tile(seqlen, 512)

    def body(h_ref, x_ref, w_ref, o_ref):
        o_ref[...] = h_ref[...] + _dot(x_ref[...], w_ref[...])

    return _rowcall(body, name, seqlen, tm, [h, xb], [_whole(w)], [(d, F32)], extra_vmem=tm * d * 4 * 2)[0]


def _lin_nt(dy, w, name):
    seqlen, d = dy.shape
    tm = _tile(seqlen, 512)

    def body(dy_ref, w_ref, o_ref):
        o_ref[...] = _dot(dy_ref[...].astype(BF16), w_ref[...], NT).astype(BF16)

    return _rowcall(body, name, seqlen, tm, [dy], [_whole(w)], [(w.shape[0], BF16)], extra_vmem=tm * d * 4 * 2)[0]


def _mlp_fwd(h, g, wup_g, wdn_g, name, comm=None):
    seqlen, d = h.shape
    nb, _, fb = wup_g.shape
    tm = _tile(seqlen, 512)

    def body(h_ref, g_ref, wup_ref, wdn_ref, o_ref):
        x = h_ref[...]
        hm = _rms(x, g_ref[...])[0].astype(BF16)
        acc = x
        for j in range(nb):
            r = jnp.maximum(_dot(hm, wup_ref[j]), 0.0)
            acc = acc + _dot((r * r).astype(BF16), wdn_ref[j])
        o_ref[...] = acc

    consts = [_whole(g.reshape(1, d)), _whole(wup_g), _whole(wdn_g)]
    return _rowcall(body, name, seqlen, tm, [h], consts, [(d, F32)], extra_vmem=tm * (d + fb) * 4 * 4, comm=comm)


def _mlp_bwd(h, dh, g, wup_g, wdn_g, name, comm=None):
    seqlen, d = h.shape
    nb, _, fb = wup_g.shape
    tm = _tile(seqlen, 256)

    def body(h_ref, dh_ref, g_ref, wup_ref, wdn_ref, dhin_ref, hm_ref, da_ref, act_ref, dg_ref):
        @pl.when(pl.program_id(0) == 0)
        def _():
            dg_ref[...] = jnp.zeros_like(dg_ref)
        x = h_ref[...]
        gv = g_ref[...]
        dy = dh_ref[...]
        hm, r = _rms(x, gv)
        hmb = hm.astype(BF16)
        dyb = dy.astype(BF16)
        dhm = jnp.zeros_like(x)
        for j in range(nb):
            rl = jnp.maximum(_dot(hmb, wup_ref[j]), 0.0)
            act_ref[:, j * fb:(j + 1) * fb] = (rl * rl).astype(BF16)
            da = (_dot(dyb, wdn_ref[j], NT) * (2.0 * rl)).astype(BF16)
            da_ref[:, j * fb:(j + 1) * fb] = da
            dhm = dhm + _dot(da, wup_ref[j], NT)
        dx, dg = _rms_bwd(x, gv, r, dhm)
        dhin_ref[...] = dy + dx
        hm_ref[...] = hmb
        dg_ref[...] += dg

    consts = [_whole(g.reshape(1, d)), _whole(wup_g), _whole(wdn_g)]
    return _rowcall(body, name, seqlen, tm, [h, dh], consts,
                    [(d, F32), (d, BF16), (nb * fb, BF16), (nb * fb, BF16)], [(1, d)],
                    extra_vmem=tm * (d + fb) * 4 * 6, comm=comm)


def _ple_fwd(h, p, g, wg, wpp, name):
    seqlen, d = h.shape
    tm = _tile(seqlen, 512)

    def body(h_ref, p_ref, g_ref, wg_ref, wpp_ref, o_ref):
        x = h_ref[...]
        n = _rms(x, g_ref[...])[0].astype(BF16)
        gate = jax.nn.sigmoid(_dot(n, wg_ref[...]))
        o_ref[...] = x + gate * _dot(p_ref[...].astype(BF16), wpp_ref[...])

    return _rowcall(body, name, seqlen, tm, [h, p], [_whole(g.reshape(1, d)), _whole(wg), _whole(wpp)], [(d, F32)],
                    extra_vmem=tm * d * 4 * 5)[0]


def _ple_bwd(h, p, dh, g, wg, wpp, name):
    seqlen, d = h.shape
    tm = _tile(seqlen, 512)

    def body(h_ref, p_ref, dh_ref, g_ref, wg_ref, wpp_ref, dhin_ref, dz_ref, n_ref, dpp_ref, dg_ref):
        @pl.when(pl.program_id(0) == 0)
        def _():
            dg_ref[...] = jnp.zeros_like(dg_ref)
        x = h_ref[...]
        gv = g_ref[...]
        dy = dh_ref[...]
        n, r = _rms(x, gv)
        nb16 = n.astype(BF16)
        gate = jax.nn.sigmoid(_dot(nb16, wg_ref[...]))
        pp = _dot(p_ref[...].astype(BF16), wpp_ref[...])
        dz = (dy * pp * gate * (1.0 - gate)).astype(BF16)
        dn = _dot(dz, wg_ref[...], NT)
        dx, dg = _rms_bwd(x, gv, r, dn)
        dhin_ref[...] = dy + dx
        dz_ref[...] = dz
        n_ref[...] = nb16
        dpp_ref[...] = (dy * gate).astype(BF16)
        dg_ref[...] += dg

    return _rowcall(body, name, seqlen, tm, [h, p, dh], [_whole(g.reshape(1, d)), _whole(wg), _whole(wpp)],
                    [(d, F32), (d, BF16), (d, BF16), (d, BF16)], [(1, d)], extra_vmem=tm * d * 4 * 8)


def _loss_bwd(h, g, tgt, name):
    seqlen, d = h.shape
    tm = _tile(seqlen, 512)

    def body(h_ref, t_ref, g_ref, dh_ref, loss_ref, dg_ref):
        @pl.when(pl.program_id(0) == 0)
        def _():
            dg_ref[...] = jnp.zeros_like(dg_ref)
            loss_ref[...] = jnp.zeros_like(loss_ref)
        x = h_ref[...]
        gv = g_ref[...]
        y, r = _rms(x, gv)
        diff = y - t_ref[...]
        loss_ref[...] += (0.5 / d) * jnp.sum(jnp.sum(diff * diff, axis=1, keepdims=True), axis=0, keepdims=True)
        dx, dg = _rms_bwd(x, gv, r, diff * (1.0 / d))
        dh_ref[...] = dx
        dg_ref[...] += dg

    return _rowcall(body, name, seqlen, tm, [h, tgt], [_whole(g.reshape(1, d))], [(d, F32)], [(1, 128), (1, d)])


def _atb(a, b, col_blocked, name):
    seqlen, k1 = a.shape
    k2 = b.shape[1]
    if col_blocked:
        cs = k2 // NDEV
        t1 = _tile(k1, 512)
        nblk = _tile(NDEV, max(1, 2048 // cs))
        t2 = nblk * cs
        oshape = (NDEV, k1, cs)
        oblock = (nblk, t1, cs)
        omap = lambda i, j, l: (j, i, 0)
    else:
        rs = k1 // NDEV
        t2 = _tile(k2, 2048)
        nblk = _tile(NDEV, max(1, 1024 // rs))
        t1 = nblk * rs
        oshape = (NDEV, rs, k2)
        oblock = (nblk, rs, t2)
        omap = lambda i, j, l: (i, 0, j)
    tl = _tile(seqlen, 2048 if b.dtype == BF16 else 1024)

    def body(a_ref, b_ref, o_ref):
        @pl.when(pl.program_id(2) == 0)
        def _():
            o_ref[...] = jnp.zeros_like(o_ref)
        res = _dot(a_ref[...].astype(BF16), b_ref[...].astype(BF16), TN)
        for n in range(nblk):
            if col_blocked:
                o_ref[n] += res[:, n * cs:(n + 1) * cs]
            else:
                o_ref[n] += res[n * rs:(n + 1) * rs, :]

    blocks = [((tl, t1), a.dtype, 2), ((tl, t2), b.dtype, 2), ((t1, t2), F32, 2)]
    return _pcall(
        body, name=name, out_shape=SDS(oshape, F32), grid=(k1 // t1, k2 // t2, seqlen // tl),
        in_specs=[pl.BlockSpec((tl, t1), lambda i, j, l: (l, i)), pl.BlockSpec((tl, t2), lambda i, j, l: (l, j))],
        out_specs=pl.BlockSpec(oblock, omap),
        vmem=_vmem_limit(blocks, extra=t1 * t2 * 4 + tl * (t1 + t2) * 2))(a, b)


def _attn_probs(q4s, kks, sink_col, has_prev):
    s = jnp.concatenate([_dot(q4, kk, NT) for q4, kk in zip(q4s, kks)], axis=0) * (HEAD_DIM ** -0.5)
    rows = s.shape[0]
    qi = lax.broadcasted_iota(jnp.int32, (rows, 2 * ATTN_BLOCK), 0) % ATTN_BLOCK + ATTN_BLOCK
    kj = lax.broadcasted_iota(jnp.int32, (rows, 2 * ATTN_BLOCK), 1)
    mask = (kj <= qi) & (qi - kj < ATTN_BLOCK) & ((kj >= ATTN_BLOCK) | has_prev)
    s = jnp.where(mask, s, NEG_INF)
    m = jnp.maximum(jnp.max(s, axis=1, keepdims=True), sink_col)
    pr = jnp.exp(s - m)
    es = jnp.exp(sink_col - m)
    inv = 1.0 / (jnp.sum(pr, axis=1, keepdims=True) + es)
    return pr * inv, es * inv


def _sink_col(sink_ref, nheads):
    return jnp.concatenate([jnp.full((ATTN_BLOCK, 1), sink_ref[hq], F32) for hq in range(nheads)], axis=0)


def _kv_pair(p_ref, c_ref, kh):
    sl = slice(kh * HEAD_DIM, (kh + 1) * HEAD_DIM)
    return jnp.concatenate([p_ref[:, sl], c_ref[:, sl]], axis=0)


def _stack_heads(ref, kh):
    return jnp.concatenate(
        [ref[:, (kh * GQA_GROUP + g) * HEAD_DIM:(kh * GQA_GROUP + g + 1) * HEAD_DIM] for g in range(GQA_GROUP)], axis=0)


def _attn_fwd(q, k, v, sinks, name):
    seqlen, d = q.shape
    dkv = k.shape[1]
    nkv = dkv // HEAD_DIM
    nb = seqlen // ATTN_BLOCK
    blk = ATTN_BLOCK

    def body(sink_ref, q_ref, kc_ref, kp_ref, vc_ref, vp_ref, o_ref):
        has_prev = pl.program_id(0) > 0
        q4s = [_stack_heads(q_ref, kh) for kh in range(nkv)]
        kks = [_kv_pair(kp_ref, kc_ref, kh) for kh in range(nkv)]
        w, _ = _attn_probs(q4s, kks, _sink_col(sink_ref, nkv * GQA_GROUP), has_prev)
        wb = w.astype(BF16)
        for kh in range(nkv):
            o4 = _dot(wb[kh * GQA_GROUP * blk:(kh + 1) * GQA_GROUP * blk, :], _kv_pair(vp_ref, vc_ref, kh))
            for g in range(GQA_GROUP):
                hq = kh * GQA_GROUP + g
                o_ref[:, hq * HEAD_DIM:(hq + 1) * HEAD_DIM] = o4[g * blk:(g + 1) * blk, :].astype(BF16)

    cur = lambda n: (n, 0)
    prev = lambda n: (jnp.maximum(n - 1, 0), 0)
    return _pcall(
        body, name=name, out_shape=SDS((seqlen, d), BF16), grid=(nb,),
        in_specs=[pl.BlockSpec(memory_space=pltpu.SMEM), pl.BlockSpec((blk, d), cur),
                  pl.BlockSpec((blk, dkv), cur), pl.BlockSpec((blk, dkv), prev),
                  pl.BlockSpec((blk, dkv), cur), pl.BlockSpec((blk, dkv), prev)],
        out_specs=pl.BlockSpec((blk, d), cur), vmem=32 << 20)(sinks, q, k, k, v, v)


def _attn_bwd(q, k, v, do, sinks, name):
    seqlen, d = q.shape
    dkv = k.shape[1]
    nkv = dkv // HEAD_DIM
    nh = d // HEAD_DIM
    nb = seqlen // ATTN_BLOCK
    blk = ATTN_BLOCK
    scale = HEAD_DIM ** -0.5

    def body(sink_ref, q_ref, kc_ref, kp_ref, vc_ref, vp_ref, do_ref, dq_ref, dk_ref, dv_ref, ds_ref,
             ck_ref, cv_ref):
        n = pl.program_id(0)

        @pl.when(n == 0)
        def _():
            ck_ref[...] = jnp.zeros_like(ck_ref)
            cv_ref[...] = jnp.zeros_like(cv_ref)
            ds_ref[...] = jnp.zeros_like(ds_ref)

        @pl.when(n == nb)
        def _():
            dk_ref[...] = ck_ref[...]
            dv_ref[...] = cv_ref[...]

        @pl.when(n < nb)
        def _():
            has_prev = n > 0
            q4s = [_stack_heads(q_ref, kh) for kh in range(nkv)]
            do4s = [_stack_heads(do_ref, kh) for kh in range(nkv)]
            kks = [_kv_pair(kp_ref, kc_ref, kh) for kh in range(nkv)]
            w, wsink = _attn_probs(q4s, kks, _sink_col(sink_ref, nh), has_prev)
            dw = jnp.concatenate([_dot(do4s[kh], _kv_pair(vp_ref, vc_ref, kh), NT) for kh in range(nkv)], axis=0)
            dsum = jnp.sum(w * dw, axis=1, keepdims=True)
            ds_all = (w * (dw - dsum) * scale).astype(BF16)
            wb = w.astype(BF16)
            dsk = -wsink * dsum
            for kh in range(nkv):
                sl = slice(kh * HEAD_DIM, (kh + 1) * HEAD_DIM)
                rows = slice(kh * GQA_GROUP * blk, (kh + 1) * GQA_GROUP * blk)
                ds = ds_all[rows, :]
                dq4 = _dot(ds, kks[kh])
                dkk = _dot(ds, q4s[kh], TN)
                dvv = _dot(wb[rows, :], do4s[kh], TN)
                for g in range(GQA_GROUP):
                    hq = kh * GQA_GROUP + g
                    dq_ref[:, hq * HEAD_DIM:(hq + 1) * HEAD_DIM] = dq4[g * blk:(g + 1) * blk, :]
                    ds_ref[hq:hq + 1, :] += jnp.sum(dsk[hq * blk:(hq + 1) * blk, :], axis=0, keepdims=True)
                dk_ref[:, sl] = ck_ref[:, sl] + dkk[:blk, :]
                dv_ref[:, sl] = cv_ref[:, sl] + dvv[:blk, :]
                ck_ref[:, sl] = dkk[blk:, :]
                cv_ref[:, sl] = dvv[blk:, :]

    cur = lambda n: (jnp.minimum(n, nb - 1), 0)
    prev = lambda n: (jnp.clip(n - 1, 0, nb - 1), 0)
    lag = lambda n: (jnp.maximum(n - 1, 0), 0)
    return _pcall(
        body, name=name,
        out_shape=[SDS((seqlen, d), F32), SDS((seqlen, dkv), F32), SDS((seqlen, dkv), F32), SDS((nh, 128), F32)],
        grid=(nb + 1,),
        in_specs=[pl.BlockSpec(memory_space=pltpu.SMEM), pl.BlockSpec((blk, d), cur),
                  pl.BlockSpec((blk, dkv), cur), pl.BlockSpec((blk, dkv), prev),
                  pl.BlockSpec((blk, dkv), cur), pl.BlockSpec((blk, dkv), prev), pl.BlockSpec((blk, d), cur)],
        out_specs=[pl.BlockSpec((blk, d), cur), pl.BlockSpec((blk, dkv), lag), pl.BlockSpec((blk, dkv), lag),
                   pl.BlockSpec((nh, 128), lambda n: (0, 0))],
        scratch=[pltpu.VMEM((blk, dkv), F32)] * 2, vmem=32 << 20)(sinks, q, k, k, v, v, do)


def _ssm_mats(lre, lim, ldt, btr, bti, cr, ci):
    dt = jnp.exp(ldt)
    mag = jnp.exp(lre * dt)
    ar = mag * jnp.cos(lim * dt)
    ai = mag * jnp.sin(lim * dt)
    den = lre * lre + lim * lim
    nr = ar - 1.0
    cfr = (nr * lre + ai * lim) / den
    cfi = (ai * lre - nr * lim) / den
    bbr = cfr * btr - cfi * bti
    bbi = cfr * bti + cfi * btr
    pr = [jnp.ones_like(ar)]
    pi = [jnp.zeros_like(ai)]
    for _ in range(SSM_T):
        pr.append(pr[-1] * ar - pi[-1] * ai)
        pi.append(pr[-2] * ai + pi[-1] * ar)
    last = SSM_T - 1
    p_re = jnp.concatenate([pr[last - s] * bbr - pi[last - s] * bbi for s in range(SSM_T)], axis=0)
    p_im = jnp.concatenate([pr[last - s] * bbi + pi[last - s] * bbr for s in range(SSM_T)], axis=0)
    qt_re = jnp.concatenate([pr[t + 1] * cr - pi[t + 1] * ci for t in range(SSM_T)], axis=0)
    qt_im = jnp.concatenate([-(pr[t + 1] * ci + pi[t + 1] * cr) for t in range(SSM_T)], axis=0)
    ctr = jnp.concatenate([cr] * SSM_T, axis=0)
    cti = jnp.concatenate([ci] * SSM_T, axis=0)
    lag = (lax.broadcasted_iota(jnp.int32, (SSM_W, SSM_W), 1) // SSM_GROUP
           - lax.broadcasted_iota(jnp.int32, (SSM_W, SSM_W), 0) // SSM_GROUP)
    m = jnp.zeros((SSM_W, SSM_W), F32)
    for l in range(SSM_T):
        zr = jnp.concatenate([pr[l] * bbr - pi[l] * bbi] * SSM_T, axis=0)
        zi = jnp.concatenate([pr[l] * bbi + pi[l] * bbr] * SSM_T, axis=0)
        kl = _dot(zr, ctr, NT, HI) - _dot(zi, cti, NT, HI)
        m = m + jnp.where(lag == l, kl, 0.0)
    return m, p_re, p_im, qt_re, qt_im, pr[SSM_T], pi[SSM_T]


_SSM_GB = 8


def _ssm_param_specs(ng):
    n, hh = SSM_STATE, SSM_GROUP
    gb = _tile(ng, _SSM_GB)
    row = pl.BlockSpec((gb, 1, n), lambda i: (i, 0, 0))
    one = pl.BlockSpec((gb, 1, 1), lambda i: (i, 0, 0))
    mat = pl.BlockSpec((gb, hh, n), lambda i: (i, 0, 0))
    big = pl.BlockSpec((gb, SSM_W, SSM_W), lambda i: (i, 0, 0))
    half = pl.BlockSpec((gb, SSM_W, n), lambda i: (i, 0, 0))
    return gb, row, one, mat, big, half


def _ssm_prep(params, name):
    ng = params[0].shape[0]
    n = SSM_STATE
    gb, row, one, mat, big, half = _ssm_param_specs(ng)

    def body(lre, lim, ldt, btr, bti, cr, ci, m_ref, pre_ref, pim_ref, qre_ref, qim_ref, atr_ref, ati_ref):
        for gi in range(gb):
            outs = _ssm_mats(lre[gi], lim[gi], ldt[gi], btr[gi], bti[gi], cr[gi], ci[gi])
            for ref, val in zip((m_ref, pre_ref, pim_ref, qre_ref, qim_ref, atr_ref, ati_ref), outs):
                ref[gi] = val

    return _pcall(
        body, name=name,
        out_shape=[SDS((ng, SSM_W, SSM_W), F32)] + [SDS((ng, SSM_W, n), F32)] * 4 + [SDS((ng, 1, n), F32)] * 2,
        grid=(ng // gb,), in_specs=[row, row, one, mat, mat, mat, mat],
        out_specs=[big, half, half, half, half, row, row], vmem=40 << 20)(*params)


def _ssm_prep_vjp(params, cots, name):
    ng = params[0].shape[0]
    n, hh = SSM_STATE, SSM_GROUP
    gb, row, one, mat, big, half = _ssm_param_specs(ng)

    def body(lre, lim, ldt, btr, bti, cr, ci, dm, dpre, dpim, dqre, dqim, datr, dati,
             o_lre, o_lim, o_ldt, o_btr, o_bti, o_cr, o_ci):
        for gi in range(gb):
            prm = (lre[gi], lim[gi], ldt[gi], btr[gi], bti[gi], cr[gi], ci[gi])
            _, pull = jax.vjp(_ssm_mats, *prm)
            grads = pull((dm[gi], dpre[gi], dpim[gi], dqre[gi], dqim[gi], datr[gi], dati[gi]))
            for ref, val in zip((o_lre, o_lim, o_ldt, o_btr, o_bti, o_cr, o_ci), grads):
                ref[gi] = val

    return _pcall(
        body, name=name,
        out_shape=[SDS((ng, 1, n), F32)] * 2 + [SDS((ng, 1, 1), F32)] + [SDS((ng, hh, n), F32)] * 4,
        grid=(ng // gb,), in_specs=[row, row, one, mat, mat, mat, mat, big, half, half, half, half, row, row],
        out_specs=[row, row, one, mat, mat, mat, mat], vmem=48 << 20)(*params, *cots)


_SSM_GT = SSM_W // SSM_GROUP


def _tile_groups(x_ref, ncb):
    lane_blk = lax.broadcasted_iota(jnp.int32, (ncb, SSM_W), 1) // SSM_GROUP
    xs = [x_ref[pl.ds(t, ncb, stride=SSM_T), :] for t in range(SSM_T)]
    out = []
    for gl in range(_SSM_GT):
        u = jnp.zeros((ncb, SSM_W), F32)
        for t in range(SSM_T):
            sh = (SSM_GROUP * (t - gl)) % SSM_W
            u = jnp.where(lane_blk == t, pltpu.roll(xs[t], sh, 1) if sh else xs[t], u)
        out.append(u)
    return out


def _groups_tile(ys, o_ref, ncb):
    lane_blk = lax.broadcasted_iota(jnp.int32, (ncb, SSM_W), 1) // SSM_GROUP
    for t in range(SSM_T):
        y = jnp.zeros((ncb, SSM_W), F32)
        for gl in range(_SSM_GT):
            sh = (SSM_GROUP * (gl - t)) % SSM_W
            y = jnp.where(lane_blk == gl, pltpu.roll(ys[gl], sh, 1) if sh else ys[gl], y)
        o_ref[pl.ds(t, ncb, stride=SSM_T), :] = y


def _ssm_specs(seqlen, d):
    ncb = _tile(seqlen // SSM_T, 512)
    grid = (d // SSM_W, seqlen // (SSM_T * ncb))
    act = pl.BlockSpec((SSM_T * ncb, SSM_W), lambda j, r: (r, j))
    state = pl.BlockSpec((ncb, _SSM_GT * SSM_W), lambda j, r: (r, j))
    mats = pl.BlockSpec((_SSM_GT, SSM_W, SSM_W), lambda j, r: (j, 0, 0))
    return ncb, grid, act, state, mats


def _gsl(gl):
    return slice(gl * SSM_W, (gl + 1) * SSM_W)


def _ssm_state_in(hn, pmat, name):
    seqlen, d = hn.shape
    ncb, grid, act, state, mats = _ssm_specs(seqlen, d)

    def body(x_ref, p_ref, s_ref):
        us = _tile_groups(x_ref, ncb)
        for gl in range(_SSM_GT):
            s_ref[:, _gsl(gl)] = _dot(us[gl], p_ref[gl], NN, HI)

    return _pcall(body, name=name, out_shape=SDS((seqlen // SSM_T, d * SSM_T), F32), grid=grid,
                  in_specs=[act, mats], out_specs=state, vmem=40 << 20)(hn, pmat)


def _ssm_out(hn, xp, mmat, qt, name):
    seqlen, d = hn.shape
    ncb, grid, act, state, mats = _ssm_specs(seqlen, d)

    def body(x_ref, xp_ref, m_ref, q_ref, y_ref):
        us = _tile_groups(x_ref, ncb)
        ys = [_dot(us[gl], m_ref[gl], NN, HI) + _dot(xp_ref[:, _gsl(gl)], q_ref[gl], NT, HI)
              for gl in range(_SSM_GT)]
        _groups_tile(ys, y_ref, ncb)

    return _pcall(body, name=name, out_shape=SDS((seqlen, d), F32), grid=grid,
                  in_specs=[act, state, mats, mats], out_specs=act, vmem=40 << 20)(hn, xp, mmat, qt)


def _ssm_dstate(dy, qt, name):
    seqlen, d = dy.shape
    ncb, grid, act, state, mats = _ssm_specs(seqlen, d)

    def body(dy_ref, q_ref, o_ref):
        dys = _tile_groups(dy_ref, ncb)
        for gl in range(_SSM_GT):
            o_ref[:, _gsl(gl)] = _dot(dys[gl], q_ref[gl], NN, HI)

    return _pcall(body, name=name, out_shape=SDS((seqlen // SSM_T, d * SSM_T), F32), grid=grid,
                  in_specs=[act, mats], out_specs=state, vmem=40 << 20)(dy, qt)


def _ssm_bwd(hn, dy, xp, gs, mmat, pmat, name):
    seqlen, d = hn.shape
    ng = d // SSM_GROUP
    ncb, grid, act, state, mats = _ssm_specs(seqlen, d)

    def body(x_ref, dy_ref, xp_ref, g_ref, m_ref, p_ref, du_ref, dm_ref, dp_ref, dq_ref, da_ref):
        @pl.when(pl.program_id(1) == 0)
        def _():
            for ref in (dm_ref, dp_ref, dq_ref, da_ref):
                ref[...] = jnp.zeros_like(ref)
        us = _tile_groups(x_ref, ncb)
        dys = _tile_groups(dy_ref, ncb)
        dus = []
        for gl in range(_SSM_GT):
            xv, gv = xp_ref[:, _gsl(gl)], g_ref[:, _gsl(gl)]
            dus.append(_dot(dys[gl], m_ref[gl], NT, HI) + _dot(gv, p_ref[gl], NT, HI))
            dm_ref[gl] += _dot(us[gl], dys[gl], TN, HI)
            dp_ref[gl] += _dot(us[gl], gv, TN, HI)
            dq_ref[gl] += _dot(dys[gl], xv, TN, HI)
            da_ref[gl, 0:1, :] += jnp.sum(xv * gv, axis=0, keepdims=True)
            da_ref[gl, 1:2, :] += jnp.sum(xv * pltpu.roll(gv, SSM_STATE, 1), axis=0, keepdims=True)
        _groups_tile(dus, du_ref, ncb)

    return _pcall(
        body, name=name,
        out_shape=[SDS((seqlen, d), F32)] + [SDS((ng, SSM_W, SSM_W), F32)] * 3 + [SDS((ng, 2, SSM_W), F32)],
        grid=grid, in_specs=[act, act, state, state, mats, mats],
        out_specs=[act, mats, mats, mats, pl.BlockSpec((_SSM_GT, 2, SSM_W), lambda j, r: (j, 0, 0))],
        vmem=48 << 20)(hn, dy, xp, gs, mmat, pmat)


def _ssm_carry(s, a1, a2, reverse, name):
    nc, w = s.shape
    tc = _tile(nc, 256)
    nblk = nc // tc
    sub = 8

    def body(s_ref, a1_ref, a2_ref, o_ref, st_ref, sw_ref):
        @pl.when(pl.program_id(0) == 0)
        def _():
            st_ref[...] = jnp.zeros_like(st_ref)
            sw_ref[...] = jnp.zeros_like(sw_ref)
        a1v = jnp.broadcast_to(a1_ref[...], (sub, w))
        a2v = jnp.broadcast_to(a2_ref[...], (sub, w))
        first = lax.broadcasted_iota(jnp.int32, (sub, w), 1) % SSM_W < SSM_STATE
        row = lax.broadcasted_iota(jnp.int32, (sub, w), 0)

        def step(t, carry):
            x, xs = carry
            tt = (tc // sub - 1 - t) if reverse else t
            base = pl.multiple_of(tt * sub, sub)
            blk = s_ref[pl.ds(base, sub), :]
            blks = jnp.where(first, pltpu.roll(blk, w - SSM_STATE, 1), pltpu.roll(blk, SSM_STATE, 1))
            out = jnp.zeros((sub, w), F32)
            for r in (range(sub - 1, -1, -1) if reverse else range(sub)):
                out = jnp.where(row == r, x, out)
                sr = jnp.broadcast_to(blk[r:r + 1, :], (sub, w))
                ssr = jnp.broadcast_to(blks[r:r + 1, :], (sub, w))
                x, xs = a1v * x + a2v * xs + sr, a1v * xs - a2v * x + ssr
            o_ref[pl.ds(base, sub), :] = out
            return x, xs

        x, xs = lax.fori_loop(0, tc // sub, step, (st_ref[...], sw_ref[...]))
        st_ref[...] = x
        sw_ref[...] = xs

    imap = (lambda i: (nblk - 1 - i, 0)) if reverse else (lambda i: (i, 0))
    cst = pl.BlockSpec((1, w), lambda i: (0, 0))
    return _pcall(body, name=name, out_shape=SDS((nc, w), F32), grid=(nblk,),
                  in_specs=[pl.BlockSpec((tc, w), imap), cst, cst], out_specs=pl.BlockSpec((tc, w), imap),
                  scratch=[pltpu.VMEM((sub, w), F32)] * 2,
                  vmem=_vmem_limit([((tc, w), F32, 4)], extra=8 << 20))(s, a1, a2)


def _ssm_rows(atr, ati, conj):
    ng = atr.shape[0]
    ai = -ati if conj else ati
    a1 = jnp.concatenate([atr, atr], axis=2).reshape(1, ng * SSM_W)
    a2 = jnp.concatenate([-ai, ai], axis=2).reshape(1, ng * SSM_W)
    return a1, a2


def _peers():
    x, y, c = (lax.axis_index(a) for a in AXES)
    me = 4 * x + 2 * y + c
    peers = []
    for dx, dy, dc in [(0, 0, 1), (0, 1, 0), (0, 1, 1), (1, 0, 0), (1, 0, 1), (1, 1, 0), (1, 1, 1)]:
        px, py, pc = (1 - x) if dx else x, (1 - y) if dy else y, (1 - c) if dc else c
        peers.append(((px, py, pc), 4 * px + 2 * py + pc))
    return me, peers


class _Exchange:
    def __init__(self, arrs, scatter, layers=None):
        self.arrs = list(arrs)
        self.scatter = scatter
        self.layers = list(layers) if layers is not None else [None] * len(self.arrs)

    def out_shape(self):
        shapes = []
        for arr, layer in zip(self.arrs, self.layers):
            block = arr.shape[1:] if (self.scatter or layer is not None) else arr.shape
            shapes.append(SDS((NDEV,) + tuple(block), arr.dtype))
        return shapes

    def semaphores(self):
        n = len(self.arrs)
        return [pltpu.SemaphoreType.DMA((n * (NDEV - 1),)), pltpu.SemaphoreType.DMA((n * (NDEV - 1),)),
                pltpu.SemaphoreType.DMA((n,))]

    def _src(self, ref, a, block):
        if self.scatter:
            return ref.at[block]
        return ref if self.layers[a] is None else ref.at[self.layers[a]]

    def _remote(self, xin, xout, sems, a, k, peer, landing):
        pid, pidx = peer
        slot = a * (NDEV - 1) + k
        return pltpu.make_async_remote_copy(
            src_ref=self._src(xin[a], a, pidx), dst_ref=xout[a].at[landing],
            send_sem=sems[0].at[slot], recv_sem=sems[1].at[slot], device_id=pid, device_id_type=MESH)

    def _local(self, xin, xout, sems, a, me):
        return pltpu.make_async_copy(self._src(xin[a], a, me), xout[a].at[me], sems[2].at[a])

    def start(self, xin, xout, sems):
        me, peers = _peers()
        for a in range(len(self.arrs)):
            self._local(xin, xout, sems, a, me).start()
        for k, peer in enumerate(peers):
            for a in range(len(self.arrs)):
                self._remote(xin, xout, sems, a, k, peer, me).start()

    def wait(self, xin, xout, sems):
        me, peers = _peers()
        for a in range(len(self.arrs)):
            self._local(xin, xout, sems, a, me).wait()
        for k, peer in enumerate(peers):
            for a in range(len(self.arrs)):
                cp = self._remote(xin, xout, sems, a, k, peer, peer[1])
                cp.wait_send()
                cp.wait_recv()


def _exchange(arrs, scatter, name, layers=None):
    comm = _Exchange(arrs, scatter, layers)
    n = len(comm.arrs)

    def body(*refs):
        xin, xout, sems = refs[:n], refs[n:2 * n], refs[2 * n:]
        comm.start(xin, xout, sems)
        comm.wait(xin, xout, sems)

    hbm = pl.BlockSpec(memory_space=pl.ANY)
    return pl.pallas_call(
        body, out_shape=comm.out_shape(), in_specs=[hbm] * n, out_specs=[hbm] * n,
        scratch_shapes=comm.semaphores(), name=name, interpret=False)(*comm.arrs)


def _adamw(parts, w, m, v, name):
    rows, cols = w.shape
    tr = _tile(rows, max(8, (1 << 17) // cols))
    c1 = 1.0 - ADAM_B1 ** ADAM_STEP
    c2 = 1.0 - ADAM_B2 ** ADAM_STEP

    def body(p_ref, w_ref, m_ref, v_ref, g_ref, d_ref, nm_ref, nv_ref):
        g = p_ref[0]
        for j in range(1, NDEV):
            g = g + p_ref[j]
        mm = ADAM_B1 * m_ref[...] + (1.0 - ADAM_B1) * g
        vv = ADAM_B2 * v_ref[...] + (1.0 - ADAM_B2) * (g * g)
        g_ref[...] = g
        nm_ref[...] = mm
        nv_ref[...] = vv
        d_ref[...] = -ADAM_LR * ((mm / c1) / (jnp.sqrt(vv / c2) + ADAM_EPS) + ADAM_WD * w_ref[...])

    spec = pl.BlockSpec((tr, cols), lambda i: (i, 0))
    return _pcall(
        body, name=name, out_shape=[SDS((rows, cols), F32)] * 4, grid=(rows // tr,),
        in_specs=[pl.BlockSpec((NDEV, tr, cols), lambda i: (0, i, 0)), spec, spec, spec], out_specs=[spec] * 4,
        vmem=_vmem_limit([((NDEV + 7, tr, cols), F32, 2)]))(parts, w, m, v)


def kernel(x, p, norm_mix, ssm_lambda_re, ssm_lambda_im, ssm_log_dt, ssm_b_re, ssm_b_im, ssm_c_re, ssm_c_im, ssm_d, ssm_w_glu, kv_norm, w_k, w_v, w_q, attn_sinks, w_o, norm_mlp, w_up, w_down, norm_ple, w_ple_gate, w_ple_proj, norm_final, loss_target, m_norm_mix, m_ssm_lambda_re, m_ssm_lambda_im, m_ssm_log_dt, m_ssm_b_re, m_ssm_b_im, m_ssm_c_re, m_ssm_c_im, m_ssm_d, m_ssm_w_glu, m_kv_norm, m_w_k, m_w_v, m_w_q, m_attn_sinks, m_w_o, m_norm_mlp, m_w_up, m_w_down, m_norm_ple, m_w_ple_gate, m_w_ple_proj, m_norm_final, v_norm_mix, v_ssm_lambda_re, v_ssm_lambda_im, v_ssm_log_dt, v_ssm_b_re, v_ssm_b_im, v_ssm_c_re, v_ssm_c_im, v_ssm_d, v_ssm_w_glu, v_kv_norm, v_w_k, v_w_v, v_w_q, v_attn_sinks, v_w_o, v_norm_mlp, v_w_up, v_w_down, v_norm_ple, v_w_ple_gate, v_w_ple_proj, v_norm_final):
    names = ['norm_mix', 'ssm_lambda_re', 'ssm_lambda_im', 'ssm_log_dt', 'ssm_b_re', 'ssm_b_im', 'ssm_c_re',
             'ssm_c_im', 'ssm_d', 'ssm_w_glu', 'kv_norm', 'w_k', 'w_v', 'w_q', 'attn_sinks', 'w_o', 'norm_mlp',
             'w_up', 'w_down', 'norm_ple', 'w_ple_gate', 'w_ple_proj', 'norm_final']
    weights = dict(zip(names, (norm_mix, ssm_lambda_re, ssm_lambda_im, ssm_log_dt, ssm_b_re, ssm_b_im, ssm_c_re,
                               ssm_c_im, ssm_d, ssm_w_glu, kv_norm, w_k, w_v, w_q, attn_sinks, w_o, norm_mlp,
                               w_up, w_down, norm_ple, w_ple_gate, w_ple_proj, norm_final)))
    mom1 = dict(zip(names, (m_norm_mix, m_ssm_lambda_re, m_ssm_lambda_im, m_ssm_log_dt, m_ssm_b_re, m_ssm_b_im,
                            m_ssm_c_re, m_ssm_c_im, m_ssm_d, m_ssm_w_glu, m_kv_norm, m_w_k, m_w_v, m_w_q,
                            m_attn_sinks, m_w_o, m_norm_mlp, m_w_up, m_w_down, m_norm_ple, m_w_ple_gate,
                            m_w_ple_proj, m_norm_final)))
    mom2 = dict(zip(names, (v_norm_mix, v_ssm_lambda_re, v_ssm_lambda_im, v_ssm_log_dt, v_ssm_b_re, v_ssm_b_im,
                            v_ssm_c_re, v_ssm_c_im, v_ssm_d, v_ssm_w_glu, v_kv_norm, v_w_k, v_w_v, v_w_q,
                            v_attn_sinks, v_w_o, v_norm_mlp, v_w_up, v_w_down, v_norm_ple, v_w_ple_gate,
                            v_w_ple_proj, v_norm_final)))

    seqlen, d = x.shape[1], x.shape[2]
    depth = w_up.shape[0]
    n_ssm = ssm_w_glu.shape[0]
    n_att = w_q.shape[0]
    ng = d // SSM_GROUP
    nh = d // HEAD_DIM
    h0 = x[0]
    tgt = loss_target[0]
    tabs = _rope_tables(seqlen)

    sharded = ['w_up', 'w_down', 'w_ple_gate', 'w_ple_proj', 'ssm_w_glu', 'w_q', 'w_o', 'w_k', 'w_v']
    shards = {k: weights[k].astype(BF16) for k in sharded}
    shards['ssm_d'] = ssm_d
    dkv = w_k.shape[1]

    def layer_set(i):
        keys = [('w_up', i), ('w_down', i), ('w_ple_gate', i), ('w_ple_proj', i)]
        keys += [('ssm_w_glu', i), ('ssm_d', i)] if i < n_ssm else [('w_q', i - n_ssm), ('w_o', i - n_ssm)]
        if i == n_ssm:
            keys += [('w_k', None), ('w_v', None)]
        return keys

    def gather_of(i):
        keys = layer_set(i)
        return keys, _Exchange([shards[k] for k, _ in keys], False, [l for _, l in keys])

    def as_operands(keys, blocks):
        w = {}
        for (k, _), g in zip(keys, blocks):
            if k in ('w_ple_gate', 'w_q', 'w_o', 'w_k', 'w_v'):
                g = g.reshape(d, g.shape[2])
            elif k in ('w_ple_proj', 'ssm_w_glu'):
                g = g.transpose(1, 0, 2).reshape(g.shape[1], NDEV * g.shape[2])
            elif k == 'ssm_d':
                g = g.reshape(d)
            w[k] = g
        return w

    keys0, comm0 = gather_of(0)
    lw = {0: as_operands(keys0, _exchange(comm0.arrs, False, "gather_weights0", comm0.layers))}

    def ssm_params(i):
        n = SSM_STATE
        return (ssm_lambda_re[i].reshape(ng, 1, n), ssm_lambda_im[i].reshape(ng, 1, n),
                ssm_log_dt[i].reshape(ng, 1, 1), jnp.swapaxes(ssm_b_re[i], 1, 2), jnp.swapaxes(ssm_b_im[i], 1, 2),
                ssm_c_re[i], ssm_c_im[i])

    h = h0
    h_in, h_a, h_b = [], [], []
    ssm_saved, att_saved = {}, {}
    k_sh = v_sh = None
    for i in range(depth):
        h_in.append(h)
        if i < n_ssm:
            hn = _norm_fwd(h, norm_mix[i], f"norm_mix_fwd{i}")
            mats = _ssm_prep(ssm_params(i), f"ssm_prep{i}")
            mmat, atr, ati = mats[0], mats[5], mats[6]
            pmat = jnp.concatenate([mats[1], mats[2]], axis=2)
            qt = jnp.concatenate([mats[3], mats[4]], axis=2)
            s_in = _ssm_state_in(hn, pmat, f"ssm_state_in{i}")
            xp = _ssm_carry(s_in, *_ssm_rows(atr, ati, False), False, f"ssm_carry_fwd{i}")
            y = _ssm_out(hn, xp, mmat, qt, f"ssm_out{i}")
            ha = _glu_fwd(y, hn, h, lw[i]['ssm_d'], lw[i]['ssm_w_glu'], f"glu_fwd{i}")
            ssm_saved[i] = (hn, mmat, pmat, qt, atr, ati, xp, y)
        else:
            j = i - n_ssm
            q = _q_fwd(h, norm_mix[i], lw[i]['w_q'], tabs, f"q_fwd{j}")
            o = _attn_fwd(q, k_sh, v_sh, attn_sinks[j], f"attn_fwd{j}")
            ha = _lin_res(h, o, lw[i]['w_o'], f"attn_out{j}")
            att_saved[j] = (q, o)
        h_a.append(ha)
        nxt = gather_of(i + 1) if i + 1 < depth else None
        res = _mlp_fwd(ha, norm_mlp[i], lw[i]['w_up'], lw[i]['w_down'], f"mlp_fwd{i}", comm=nxt and nxt[1])
        hb = res[0]
        if nxt:
            lw[i + 1] = as_operands(nxt[0], res[1:])
        h_b.append(hb)
        h = _ple_fwd(hb, p[i, 0], norm_ple[i], lw[i]['w_ple_gate'], lw[i]['w_ple_proj'], f"ple_fwd{i}")
        if i == n_ssm - 1:
            k_sh, v_sh = _kv_fwd(h, kv_norm, lw[n_ssm]['w_k'], lw[n_ssm]['w_v'], tabs, "kv_fwd")
    h_kv = h_in[n_ssm] if n_ssm < depth else h
    dh, loss_row, g_norm_final = _loss_bwd(h, norm_final, tgt, "loss_bwd")
    loss = lax.psum(loss_row[0, 0], AXES)

    g_norm_mix, g_norm_mlp, g_norm_ple = [None] * depth, [None] * depth, [None] * depth
    g_ssm, g_sinks = [None] * n_ssm, [None] * n_att
    g_kv_norm = None
    dks, dvs = [], []
    recv = {}
    pending = None
    for i in range(depth - 1, -1, -1):
        gl = {}
        wk, wv = lw[n_ssm]['w_k'], lw[n_ssm]['w_v']
        if i == n_ssm - 1:
            dh, dkp, dvb, hkb, g_kv_norm = _kv_bwd(dks, dvs, h_kv, kv_norm, dh, wk, wv, tabs, "kv_bwd")
            gl['w_k', None] = _atb(hkb, dkp, False, "grad_w_k")
            gl['w_v', None] = _atb(hkb, dvb, False, "grad_w_v")
        dhb, dz, nb16, dpp, g_norm_ple[i] = _ple_bwd(h_b[i], p[i, 0], dh, norm_ple[i], lw[i]['w_ple_gate'],
                                                     lw[i]['w_ple_proj'], f"ple_bwd{i}")
        gl['w_ple_gate', i] = _atb(nb16, dz, False, f"grad_w_ple_gate{i}")
        gl['w_ple_proj', i] = _atb(p[i, 0], dpp, True, f"grad_w_ple_proj{i}")
        res = _mlp_bwd(h_a[i], dhb, norm_mlp[i], lw[i]['w_up'], lw[i]['w_down'], f"mlp_bwd{i}",
                       comm=pending and pending[1])
        dha, hmb, da, act, g_norm_mlp[i] = res[:5]
        if pending:
            recv.update(zip(pending[0], res[5:]))
        gl['w_up', i] = _atb(hmb, da, True, f"grad_w_up{i}")
        gl['w_down', i] = _atb(act, dhb, False, f"grad_w_down{i}")
        if i >= n_ssm:
            j = i - n_ssm
            q, o = att_saved[j]
            do = _lin_nt(dha, lw[i]['w_o'], f"attn_out_bwd{j}")
            gl['w_o', j] = _atb(o, dha, False, f"grad_w_o{j}")
            dq, dk_j, dv_j, dsink = _attn_bwd(q, k_sh, v_sh, do, attn_sinks[j], f"attn_bwd{j}")
            dks.append(dk_j)
            dvs.append(dv_j)
            g_sinks[j] = dsink[:, 0]
            dh, dqp, hnb, g_norm_mix[i] = _q_bwd(dq, h_in[i], norm_mix[i], dha, lw[i]['w_q'], tabs, f"q_bwd{j}")
            gl['w_q', j] = _atb(hnb, dqp, False, f"grad_w_q{j}")
        else:
            hn, mmat, pmat, qt, atr, ati, xp, y = ssm_saved[i]
            dyy, dhn_d, geb, dab, g_dskip = _glu_bwd(y, hn, dha, lw[i]['ssm_d'], lw[i]['ssm_w_glu'], f"glu_bwd{i}")
            gl['ssm_d', i] = g_dskip.reshape(NDEV, d // NDEV)
            gl['ssm_w_glu', i] = _atb(geb, dab, True, f"grad_ssm_w_glu{i}")
            dxp = _ssm_dstate(dyy, qt, f"ssm_dstate{i}")
            gs = _ssm_carry(dxp, *_ssm_rows(atr, ati, True), True, f"ssm_carry_bwd{i}")
            du, dm, dp, dqt, da_raw = _ssm_bwd(hn, dyy, xp, gs, mmat, pmat, f"ssm_bwd{i}")
            n = SSM_STATE
            cots = (dm, dp[:, :, :n], dp[:, :, n:], dqt[:, :, :n], dqt[:, :, n:],
                    (da_raw[:, 0:1, :n] + da_raw[:, 0:1, n:]), (da_raw[:, 1:2, :n] - da_raw[:, 1:2, n:]))
            g_ssm[i] = _ssm_prep_vjp(ssm_params(i), cots, f"ssm_prep_vjp{i}")
            dh, g_norm_mix[i] = _norm_bwd(h_in[i], norm_mix[i], dhn_d, du, dha, f"norm_mix_bwd{i}")
        pending = (list(gl), _Exchange(list(gl.values()), True))
    grad_x = dh[None]
    recv.update(zip(pending[0], _exchange(pending[1].arrs, True, "scatter_grads0")))

    out_g, out_d, out_m, out_v = {}, {}, {}, {}
    updated = {}
    for (k, l), parts in recv.items():
        pick = (lambda t: t) if l is None else (lambda t: t[l])
        shp = pick(weights[k]).shape
        r2 = (math.prod(shp[:-1]), shp[-1])
        res = _adamw(parts.reshape((NDEV,) + r2), pick(weights[k]).reshape(r2), pick(mom1[k]).reshape(r2),
                     pick(mom2[k]).reshape(r2), f"adamw_{k}{'' if l is None else l}")
        updated.setdefault(k, {})[l] = [t.reshape(shp) for t in res]
    for k, by_layer in updated.items():
        for n, dst in enumerate((out_g, out_d, out_m, out_v)):
            dst[k] = by_layer[None][n] if None in by_layer else jnp.stack([by_layer[l][n] for l in sorted(by_layer)])

    def ssm_grad(idx, unswap=False):
        g = jnp.stack([g_ssm[i][idx] for i in range(n_ssm)])
        return jnp.swapaxes(g, 2, 3) if unswap else g

    small = {'norm_mix': jnp.concatenate(g_norm_mix, axis=0),
             'ssm_lambda_re': ssm_grad(0), 'ssm_lambda_im': ssm_grad(1), 'ssm_log_dt': ssm_grad(2),
             'ssm_b_re': ssm_grad(3, True), 'ssm_b_im': ssm_grad(4, True),
             'ssm_c_re': ssm_grad(5), 'ssm_c_im': ssm_grad(6),
             'kv_norm': g_kv_norm, 'attn_sinks': jnp.stack(g_sinks),
             'norm_mlp': jnp.concatenate(g_norm_mlp, axis=0), 'norm_ple': jnp.concatenate(g_norm_ple, axis=0),
             'norm_final': g_norm_final}
    snames = list(small)
    sizes = [weights[k].size for k in snames]
    total = sum(sizes)
    lanes = 128
    padded = -(-total // (512 * lanes)) * (512 * lanes)

    def flat(parts):
        v = jnp.concatenate([t.reshape(-1) for t in parts] + [jnp.zeros((padded - total,), F32)])
        return v.reshape(padded // lanes, lanes)

    parts = _exchange([flat([small[k] for k in snames])], False, "gather_small_grads")[0]
    res = _adamw(parts, flat([weights[k] for k in snames]), flat([mom1[k] for k in snames]),
                 flat([mom2[k] for k in snames]), "adamw_small")
    off = 0
    for k, sz in zip(snames, sizes):
        for dst, t in zip((out_g, out_d, out_m, out_v), res):
            dst[k] = t.reshape(-1)[off:off + sz].reshape(weights[k].shape)
        off += sz

    return (loss, grad_x, *[out_g[k] for k in names], *[out_d[k] for k in names],
            *[out_m[k] for k in names], *[out_v[k] for k in names])
```

```python
import functools
import math

import jax
import jax.numpy as jnp
from jax import lax
from jax.experimental import pallas as pl
from jax.experimental.pallas import tpu as pltpu

F32 = jnp.float32
BF16 = jnp.bfloat16
SDS = jax.ShapeDtypeStruct
MESH = pl.DeviceIdType.MESH
AXES = ("x", "y", "c")
NDEV = 8

RMS_EPS = 1e-6
SSM_GROUP = 16
SSM_STATE = 64
SSM_T = 8
SSM_W = SSM_T * SSM_GROUP
HEAD_DIM = 64
GQA_GROUP = 4
ATTN_BLOCK = 128
ROT_DIM = 16
ROPE_THETA = 500000.0
NEG_INF = -1e30
ADAM_LR, ADAM_B1, ADAM_B2, ADAM_EPS, ADAM_WD, ADAM_STEP = 0.001, 0.9, 0.999, 1e-08, 0.01, 10

VMEM_CAP = 56 * 1024 * 1024
HI = lax.Precision.HIGHEST

NN = ((1,), (0,))
NT = ((1,), (1,))
TN = ((0,), (0,))


def _dot(a, b, dims=NN, precision=None):
    return lax.dot_general(a, b, (dims, ((), ())), preferred_element_type=F32, precision=precision)


def _split(a):
    if isinstance(a, tuple):
        return a
    hi = a.astype(BF16)
    return hi, (a - hi.astype(F32)).astype(BF16)


def _dot3(a, b, dims=NN):
    (ah, al), (bh, bl) = _split(a), _split(b)
    return _dot(ah, bh, dims) + (_dot(ah, bl, dims) + _dot(al, bh, dims))


@jax.custom_vjp
def _dot3_nt(a, b):
    return _dot3(a, b, NT)


def _dot3_nt_fwd(a, b):
    return _dot3(a, b, NT), (a, b)


def _dot3_nt_bwd(res, g):
    a, b = res
    return _dot3(g, b, NN), _dot3(g, a, TN)


_dot3_nt.defvjp(_dot3_nt_fwd, _dot3_nt_bwd)


def _tile(n, pref):
    t = min(n, pref)
    while n % t:
        t //= 2
    return t


def _nbytes(shape, dtype):
    return math.prod(s for s in shape if s is not None) * jnp.dtype(dtype).itemsize


def _vmem_limit(blocks, extra=0):
    need = sum(_nbytes(s, d) * n for s, d, n in blocks) + extra + (4 << 20)
    return int(min(VMEM_CAP, max(need, 16 << 20)))


def _pcall(body, *, name, out_shape, grid, in_specs, out_specs, scratch=(), vmem=None, comm=None):
    single = not isinstance(out_shape, (list, tuple))
    out_shape = [out_shape] if single else list(out_shape)
    out_specs = [out_specs] if single else list(out_specs)
    in_specs, scratch = list(in_specs), list(scratch)
    if comm is not None:
        n_in, n_out, n_scr, nx = len(in_specs), len(out_specs), len(scratch), len(comm.arrs)
        hbm = pl.BlockSpec(memory_space=pl.ANY)
        in_specs = in_specs + [hbm] * nx
        out_specs = out_specs + [hbm] * nx
        out_shape = out_shape + comm.out_shape()
        scratch = scratch + comm.semaphores()
        inner = body

        def body(*refs):
            ins, xin, rest = refs[:n_in], refs[n_in:n_in + nx], refs[n_in + nx:]
            outs, xout, rest = rest[:n_out], rest[n_out:n_out + nx], rest[n_out + nx:]
            scr, sems = rest[:n_scr], rest[n_scr:]
            first = functools.reduce(jnp.logical_and, [pl.program_id(a) == 0 for a in range(len(grid))])
            last = functools.reduce(jnp.logical_and, [pl.program_id(a) == g - 1 for a, g in enumerate(grid)])

            @pl.when(first)
            def _():
                comm.start(xin, xout, sems)
            inner(*ins, *outs, *scr)

            @pl.when(last)
            def _():
                comm.wait(xin, xout, sems)

    call = pl.pallas_call(
        body, out_shape=out_shape[0] if single and comm is None else out_shape, grid=grid, in_specs=in_specs,
        out_specs=out_specs[0] if single and comm is None else out_specs, scratch_shapes=scratch, name=name,
        compiler_params=pltpu.CompilerParams(
            dimension_semantics=("arbitrary",) * len(grid), vmem_limit_bytes=vmem),
        interpret=False)
    if comm is None:
        return call
    return lambda *args: call(*args, *comm.arrs)


def _rms(x, g):
    r = lax.rsqrt(jnp.mean(x * x, axis=-1, keepdims=True) + RMS_EPS)
    return x * r * g, r


def _rms_bwd(x, g, r, dy):
    xh = x * r
    dyg = dy * g
    dx = r * (dyg - xh * jnp.mean(dyg * xh, axis=-1, keepdims=True))
    return dx, jnp.sum(dy * xh, axis=0, keepdims=True)


_GELU_C = math.sqrt(2.0 / math.pi)


def _gelu_parts(x):
    t = jnp.tanh(_GELU_C * (x + 0.044715 * x * x * x))
    return 0.5 * x * (1.0 + t), t


def _gelu_grad(x, t):
    return 0.5 * (1.0 + t) + 0.5 * x * (1.0 - t * t) * _GELU_C * (1.0 + 3 * 0.044715 * x * x)


def _rope_tables(seqlen):
    half = ROT_DIM // 2
    inv = ROPE_THETA ** (-jnp.arange(0, ROT_DIM, 2, dtype=F32) / ROT_DIM)
    ang = jnp.arange(seqlen, dtype=jnp.int32).astype(F32)[:, None] * inv[None, :]
    cos, sin = jnp.cos(ang), jnp.sin(ang)
    zeros = jnp.zeros((seqlen, HEAD_DIM - ROT_DIM), F32)
    zh = jnp.zeros((seqlen, half), F32)
    c = jnp.concatenate([cos, cos, zeros + 1.0], axis=1)
    sa = jnp.concatenate([zh, sin, zeros], axis=1)
    sb = jnp.concatenate([-sin, zh, zeros], axis=1)
    return tuple(jnp.tile(t, (1, 128 // HEAD_DIM)) for t in (c, sa, sb))


def _rope(x, c, sa, sb):
    w = x.shape[1]
    reps = w // 128
    half = ROT_DIM // 2
    return (x * jnp.tile(c, (1, reps)) + pltpu.roll(x, half, 1) * jnp.tile(sa, (1, reps))
            + pltpu.roll(x, w - half, 1) * jnp.tile(sb, (1, reps)))


def _rope_bwd(dy, c, sa, sb):
    w = dy.shape[1]
    reps = w // 128
    half = ROT_DIM // 2
    return (dy * jnp.tile(c, (1, reps)) + pltpu.roll(dy * jnp.tile(sa, (1, reps)), w - half, 1)
            + pltpu.roll(dy * jnp.tile(sb, (1, reps)), half, 1))


def _rspec(tm, c):
    return pl.BlockSpec((tm, c), lambda i: (i, 0))


def _cspec(shape, idx=None):
    idx = tuple(idx) if idx is not None else (0,) * len(shape)
    return pl.BlockSpec(tuple(shape), lambda i: idx, pipeline_mode=pl.Buffered(1))


def _rowcall(body, name, seqlen, tm, rows_in, consts_in, rows_out, acc_out=(), extra_vmem=0, comm=None):
    in_specs = [_rspec(tm, a.shape[1]) for a in rows_in] + [_cspec(bs, ix) for _, bs, ix in consts_in]
    out_shape = [SDS((seqlen, c), d) for c, d in rows_out] + [SDS(s, F32) for s in acc_out]
    out_specs = [_rspec(tm, c) for c, _ in rows_out] + [pl.BlockSpec(s, lambda i: (0, 0)) for s in acc_out]
    blocks = ([((tm, a.shape[1]), a.dtype, 2) for a in rows_in] + [(bs, a.dtype, 1) for a, bs, _ in consts_in]
              + [((tm, c), d, 2) for c, d in rows_out])
    temporaries = 12 * tm * rows_in[0].shape[1] * 4
    return _pcall(body, name=name, out_shape=out_shape, grid=(seqlen // tm,), in_specs=in_specs,
                  out_specs=out_specs, vmem=_vmem_limit(blocks, extra_vmem + temporaries), comm=comm)(
                      *rows_in, *[a for a, _, _ in consts_in])


def _whole(a):
    return (a, a.shape, None)


def _norm_fwd(h, g, name, comm=None):
    seqlen, d = h.shape
    tm = _tile(seqlen, 1024)

    def body(h_ref, g_ref, o_ref):
        o_ref[...] = _rms(h_ref[...], g_ref[...])[0]

    return _rowcall(body, name, seqlen, tm, [h], [_whole(g.reshape(1, d))], [(d, F32)], comm=comm)


def _norm_bwd(h, g, dy1, dy2, dres, name, comm=None):
    seqlen, d = h.shape
    tm = _tile(seqlen, 512)

    def body(h_ref, dy1_ref, dy2_ref, dres_ref, g_ref, dh_ref, dg_ref):
        @pl.when(pl.program_id(0) == 0)
        def _():
            dg_ref[...] = jnp.zeros_like(dg_ref)
        x = h_ref[...]
        gv = g_ref[...]
        _, r = _rms(x, gv)
        dx, dg = _rms_bwd(x, gv, r, dy1_ref[...] + dy2_ref[...])
        dh_ref[...] = dres_ref[...] + dx
        dg_ref[...] += dg

    return _rowcall(body, name, seqlen, tm, [h, dy1, dy2, dres], [_whole(g.reshape(1, d))], [(d, F32)], [(1, d)],
                    comm=comm)


def _glu_fwd(y, hn, h, dskip, wglu, name):
    seqlen, d = h.shape
    tm = _tile(seqlen, 512)

    def body(y_ref, hn_ref, h_ref, d_ref, w_ref, o_ref):
        yy = y_ref[...] + d_ref[...] * hn_ref[...]
        ge, _ = _gelu_parts(yy)
        ab = _dot(ge.astype(BF16), w_ref[...])
        o_ref[...] = h_ref[...] + ab[:, :d] * jax.nn.sigmoid(ab[:, d:])

    return _rowcall(body, name, seqlen, tm, [y, hn, h], [_whole(dskip.reshape(1, d)), _whole(wglu)], [(d, F32)],
                    extra_vmem=tm * d * 4 * 6)[0]


def _glu_bwd(y, hn, dmix, dskip, wglu, name):
    seqlen, d = hn.shape
    tm = _tile(seqlen, 512)

    def body(y_ref, hn_ref, dm_ref, d_ref, w_ref, dyy_ref, dhn_ref, ge_ref, dab_ref, dd_ref):
        @pl.when(pl.program_id(0) == 0)
        def _():
            dd_ref[...] = jnp.zeros_like(dd_ref)
        hn_v = hn_ref[...]
        dsk = d_ref[...]
        yy = y_ref[...] + dsk * hn_v
        ge, t = _gelu_parts(yy)
        geb = ge.astype(BF16)
        ab = _dot(geb, w_ref[...])
        a = ab[:, :d]
        sg = jax.nn.sigmoid(ab[:, d:])
        dm = dm_ref[...]
        dab_ref[:, :d] = (dm * sg).astype(BF16)
        dab_ref[:, d:] = (dm * a * sg * (1.0 - sg)).astype(BF16)
        dge = _dot(dab_ref[...], w_ref[...], NT)
        dyy = dge * _gelu_grad(yy, t)
        dyy_ref[...] = dyy
        dhn_ref[...] = dyy * dsk
        ge_ref[...] = geb
        dd_ref[...] += jnp.sum(dyy * hn_v, axis=0, keepdims=True)

    return _rowcall(body, name, seqlen, tm, [y, hn, dmix], [_whole(dskip.reshape(1, d)), _whole(wglu)],
                    [(d, F32), (d, F32), (d, BF16), (2 * d, BF16)], [(1, d)], extra_vmem=tm * d * 4 * 8)


def _q_fwd(h, g, wq, tabs, name):
    seqlen, d = h.shape
    tm = _tile(seqlen, 512)

    def body(h_ref, c_ref, sa_ref, sb_ref, g_ref, w_ref, q_ref):
        hn, _ = _rms(h_ref[...], g_ref[...])
        qp = _dot(hn.astype(BF16), w_ref[...])
        q_ref[...] = _rope(qp, c_ref[...], sa_ref[...], sb_ref[...]).astype(BF16)

    return _rowcall(body, name, seqlen, tm, [h, *tabs], [_whole(g.reshape(1, d)), _whole(wq)], [(d, BF16)],
                    extra_vmem=tm * d * 4 * 6)[0]


def _q_bwd(dq, h, g, dres, wq, tabs, name):
    seqlen, d = h.shape
    tm = _tile(seqlen, 512)

    def body(dq_ref, h_ref, dres_ref, c_ref, sa_ref, sb_ref, g_ref, w_ref, dh_ref, dqp_ref, hn_ref, dg_ref):
        @pl.when(pl.program_id(0) == 0)
        def _():
            dg_ref[...] = jnp.zeros_like(dg_ref)
        dqp = _rope_bwd(dq_ref[...], c_ref[...], sa_ref[...], sb_ref[...]).astype(BF16)
        x = h_ref[...]
        gv = g_ref[...]
        hn, r = _rms(x, gv)
        dhn = _dot(dqp, w_ref[...], NT)
        dx, dg = _rms_bwd(x, gv, r, dhn)
        dh_ref[...] = dres_ref[...] + dx
        dqp_ref[...] = dqp
        hn_ref[...] = hn.astype(BF16)
        dg_ref[...] += dg

    return _rowcall(body, name, seqlen, tm, [dq, h, dres, *tabs], [_whole(g.reshape(1, d)), _whole(wq)],
                    [(d, F32), (d, BF16), (d, BF16)], [(1, d)], extra_vmem=tm * d * 4 * 6)


def _kv_fwd(h, g, wk, wv, tabs, name):
    seqlen, d = h.shape
    dk = wk.shape[1]
    tm = _tile(seqlen, 512)

    def body(h_ref, c_ref, sa_ref, sb_ref, g_ref, wk_ref, wv_ref, k_ref, v_ref):
        hk = _rms(h_ref[...], g_ref[...])[0].astype(BF16)
        k_ref[...] = _rope(_dot(hk, wk_ref[...]), c_ref[...], sa_ref[...], sb_ref[...]).astype(BF16)
        v_ref[...] = _dot(hk, wv_ref[...]).astype(BF16)

    return _rowcall(body, name, seqlen, tm, [h, *tabs], [_whole(g.reshape(1, d)), _whole(wk), _whole(wv)],
                    [(dk, BF16), (dk, BF16)], extra_vmem=tm * d * 4 * 4)


def _kv_bwd(dks, dvs, h, g, dres, wk, wv, tabs, name):
    seqlen, d = h.shape
    dkw = wk.shape[1]
    tm = _tile(seqlen, 512)

    def body(dk0_ref, dk1_ref, dv0_ref, dv1_ref, h_ref, dres_ref, c_ref, sa_ref, sb_ref, g_ref, wk_ref, wv_ref,
             dh_ref, dkp_ref, dvb_ref, hk_ref, dg_ref):
        @pl.when(pl.program_id(0) == 0)
        def _():
            dg_ref[...] = jnp.zeros_like(dg_ref)
        dkp = _rope_bwd(dk0_ref[...] + dk1_ref[...], c_ref[...], sa_ref[...], sb_ref[...]).astype(BF16)
        dvb = (dv0_ref[...] + dv1_ref[...]).astype(BF16)
        x = h_ref[...]
        gv = g_ref[...]
        hk, r = _rms(x, gv)
        dhk = _dot(dkp, wk_ref[...], NT) + _dot(dvb, wv_ref[...], NT)
        dx, dg = _rms_bwd(x, gv, r, dhk)
        dh_ref[...] = dres_ref[...] + dx
        dkp_ref[...] = dkp
        dvb_ref[...] = dvb
        hk_ref[...] = hk.astype(BF16)
        dg_ref[...] += dg

    return _rowcall(body, name, seqlen, tm, [dks[0], dks[1], dvs[0], dvs[1], h, dres, *tabs],
                    [_whole(g.reshape(1, d)), _whole(wk), _whole(wv)],
                    [(d, F32), (dkw, BF16), (dkw, BF16), (d, BF16)], [(1, d)], extra_vmem=tm * d * 4 * 6)


def _lin_res(h, xb, w, name):
    seqlen, d = h.shape
    tm = _tile(seqlen, 512)

    def body(h_ref, x_ref, w_ref, o_ref):
        o_ref[...] = h_ref[...] + _dot(x_ref[...], w_ref[...])

    return _rowcall(body, name, seqlen, tm, [h, xb], [_whole(w)], [(d, F32)], extra_vmem=tm * d * 4 * 2)[0]


def _lin_nt(dy, w, name):
    seqlen, d = dy.shape
    tm = _tile(seqlen, 512)

    def body(dy_ref, w_ref, o_ref):
        o_ref[...] = _dot(dy_ref[...].astype(BF16), w_ref[...], NT).astype(BF16)

    return _rowcall(body, name, seqlen, tm, [dy], [_whole(w)], [(w.shape[0], BF16)], extra_vmem=tm * d * 4 * 2)[0]


def _mlp_fwd(h, g, wup_g, wdn_g, name, comm=None):
    seqlen, d = h.shape
    nb, _, fb = wup_g.shape
    tm = _tile(seqlen, 256)

    def body(h_ref, g_ref, wup_ref, wdn_ref, o_ref, act_ref):
        x = h_ref[...]
        hm = _rms(x, g_ref[...])[0].astype(BF16)
        acc = x
        for j in range(nb):
            r = jnp.maximum(_dot(hm, wup_ref[j]), 0.0)
            act = (r * r).astype(BF16)
            act_ref[:, j * fb:(j + 1) * fb] = act
            acc = acc + _dot(act, wdn_ref[j])
        o_ref[...] = acc

    consts = [_whole(g.reshape(1, d)), _whole(wup_g), _whole(wdn_g)]
    return _rowcall(body, name, seqlen, tm, [h], consts, [(d, F32), (nb * fb, BF16)],
                    extra_vmem=tm * (d + fb) * 4 * 4, comm=comm)


def _mlp_bwd(h, act, dh, g, wup_g, wdn_g, name, comm=None):
    seqlen, d = h.shape
    nb, _, fb = wup_g.shape
    tm = _tile(seqlen, 256)

    def body(h_ref, act_ref, dh_ref, g_ref, wup_ref, wdn_ref, dhin_ref, hm_ref, da_ref, dg_ref):
        @pl.when(pl.program_id(0) == 0)
        def _():
            dg_ref[...] = jnp.zeros_like(dg_ref)
        x = h_ref[...]
        gv = g_ref[...]
        dy = dh_ref[...]
        hm, r = _rms(x, gv)
        dyb = dy.astype(BF16)
        dhm = jnp.zeros_like(x)
        for j in range(nb):
            rl2 = 2.0 * jnp.sqrt(act_ref[:, j * fb:(j + 1) * fb].astype(F32))
            da = (_dot(dyb, wdn_ref[j], NT) * rl2).astype(BF16)
            da_ref[:, j * fb:(j + 1) * fb] = da
            dhm = dhm + _dot(da, wup_ref[j], NT)
        dx, dg = _rms_bwd(x, gv, r, dhm)
        dhin_ref[...] = dy + dx
        hm_ref[...] = hm.astype(BF16)
        dg_ref[...] += dg

    consts = [_whole(g.reshape(1, d)), _whole(wup_g), _whole(wdn_g)]
    return _rowcall(body, name, seqlen, tm, [h, act, dh], consts,
                    [(d, F32), (d, BF16), (nb * fb, BF16)], [(1, d)],
                    extra_vmem=tm * (d + fb) * 4 * 6, comm=comm)


def _ple_fwd(h, p, g, wg, wpp, name):
    seqlen, d = h.shape
    tm = _tile(seqlen, 512)

    def body(h_ref, p_ref, g_ref, wg_ref, wpp_ref, o_ref):
        x = h_ref[...]
        n = _rms(x, g_ref[...])[0].astype(BF16)
        gate = jax.nn.sigmoid(_dot(n, wg_ref[...]))
        o_ref[...] = x + gate * _dot(p_ref[...].astype(BF16), wpp_ref[...])

    return _rowcall(body, name, seqlen, tm, [h, p], [_whole(g.reshape(1, d)), _whole(wg), _whole(wpp)], [(d, F32)],
                    extra_vmem=tm * d * 4 * 5)[0]


def _ple_bwd(h, p, dh, g, wg, wpp, name):
    seqlen, d = h.shape
    tm = _tile(seqlen, 512)

    def body(h_ref, p_ref, dh_ref, g_ref, wg_ref, wpp_ref, dhin_ref, dz_ref, n_ref, dpp_ref, dg_ref):
        @pl.when(pl.program_id(0) == 0)
        def _():
            dg_ref[...] = jnp.zeros_like(dg_ref)
        x = h_ref[...]
        gv = g_ref[...]
        dy = dh_ref[...]
        n, r = _rms(x, gv)
        nb16 = n.astype(BF16)
        gate = jax.nn.sigmoid(_dot(nb16, wg_ref[...]))
        pp = _dot(p_ref[...].astype(BF16), wpp_ref[...])
        dz = (dy * pp * gate * (1.0 - gate)).astype(BF16)
        dn = _dot(dz, wg_ref[...], NT)
        dx, dg = _rms_bwd(x, gv, r, dn)
        dhin_ref[...] = dy + dx
        dz_ref[...] = dz
        n_ref[...] = nb16
        dpp_ref[...] = (dy * gate).astype(BF16)
        dg_ref[...] += dg

    return _rowcall(body, name, seqlen, tm, [h, p, dh], [_whole(g.reshape(1, d)), _whole(wg), _whole(wpp)],
                    [(d, F32), (d, BF16), (d, BF16), (d, BF16)], [(1, d)], extra_vmem=tm * d * 4 * 8)


def _loss_bwd(h, g, tgt, name):
    seqlen, d = h.shape
    tm = _tile(seqlen, 512)

    def body(h_ref, t_ref, g_ref, dh_ref, loss_ref, dg_ref):
        @pl.when(pl.program_id(0) == 0)
        def _():
            dg_ref[...] = jnp.zeros_like(dg_ref)
            loss_ref[...] = jnp.zeros_like(loss_ref)
        x = h_ref[...]
        gv = g_ref[...]
        y, r = _rms(x, gv)
        diff = y - t_ref[...]
        loss_ref[...] += (0.5 / d) * jnp.sum(jnp.sum(diff * diff, axis=1, keepdims=True), axis=0, keepdims=True)
        dx, dg = _rms_bwd(x, gv, r, diff * (1.0 / d))
        dh_ref[...] = dx
        dg_ref[...] += dg

    return _rowcall(body, name, seqlen, tm, [h, tgt], [_whole(g.reshape(1, d))], [(d, F32)], [(1, 128), (1, d)])


def _atb(a, b, col_blocked, name):
    seqlen, k1 = a.shape
    k2 = b.shape[1]
    if col_blocked:
        cs = k2 // NDEV
        t1 = _tile(k1, 512)
        nblk = _tile(NDEV, max(1, 2048 // cs))
        t2 = nblk * cs
        oshape = (NDEV, k1, cs)
        oblock = (nblk, t1, cs)
        omap = lambda i, j, l: (j, i, 0)
    else:
        rs = k1 // NDEV
        t2 = _tile(k2, 2048)
        nblk = _tile(NDEV, max(1, 1024 // rs))
        t1 = nblk * rs
        oshape = (NDEV, rs, k2)
        oblock = (nblk, rs, t2)
        omap = lambda i, j, l: (i, 0, j)
    tl = _tile(seqlen, 2048 if b.dtype == BF16 else 1024)

    def body(a_ref, b_ref, o_ref):
        @pl.when(pl.program_id(2) == 0)
        def _():
            o_ref[...] = jnp.zeros_like(o_ref)
        res = _dot(a_ref[...].astype(BF16), b_ref[...].astype(BF16), TN)
        for n in range(nblk):
            if col_blocked:
                o_ref[n] += res[:, n * cs:(n + 1) * cs]
            else:
                o_ref[n] += res[n * rs:(n + 1) * rs, :]

    blocks = [((tl, t1), a.dtype, 2), ((tl, t2), b.dtype, 2), ((t1, t2), F32, 2)]
    return _pcall(
        body, name=name, out_shape=SDS(oshape, F32), grid=(k1 // t1, k2 // t2, seqlen // tl),
        in_specs=[pl.BlockSpec((tl, t1), lambda i, j, l: (l, i)), pl.BlockSpec((tl, t2), lambda i, j, l: (l, j))],
        out_specs=pl.BlockSpec(oblock, omap),
        vmem=_vmem_limit(blocks, extra=t1 * t2 * 4 + tl * (t1 + t2) * 2))(a, b)


def _attn_probs(q4s, kks, sink_col, has_prev):
    s = jnp.concatenate([_dot(q4, kk, NT) for q4, kk in zip(q4s, kks)], axis=0) * (HEAD_DIM ** -0.5)
    rows = s.shape[0]
    qi = lax.broadcasted_iota(jnp.int32, (rows, 2 * ATTN_BLOCK), 0) % ATTN_BLOCK + ATTN_BLOCK
    kj = lax.broadcasted_iota(jnp.int32, (rows, 2 * ATTN_BLOCK), 1)
    mask = (kj <= qi) & (qi - kj < ATTN_BLOCK) & ((kj >= ATTN_BLOCK) | has_prev)
    s = jnp.where(mask, s, NEG_INF)
    m = jnp.maximum(jnp.max(s, axis=1, keepdims=True), sink_col)
    pr = jnp.exp(s - m)
    es = jnp.exp(sink_col - m)
    inv = 1.0 / (jnp.sum(pr, axis=1, keepdims=True) + es)
    return pr * inv, es * inv


def _sink_col(sink_ref, nheads):
    return jnp.concatenate([jnp.full((ATTN_BLOCK, 1), sink_ref[hq], F32) for hq in range(nheads)], axis=0)


def _kv_pair(p_ref, c_ref, kh):
    sl = slice(kh * HEAD_DIM, (kh + 1) * HEAD_DIM)
    return jnp.concatenate([p_ref[:, sl], c_ref[:, sl]], axis=0)


def _stack_heads(ref, kh):
    return jnp.concatenate(
        [ref[:, (kh * GQA_GROUP + g) * HEAD_DIM:(kh * GQA_GROUP + g + 1) * HEAD_DIM] for g in range(GQA_GROUP)], axis=0)


def _attn_fwd(q, k, v, sinks, name):
    seqlen, d = q.shape
    dkv = k.shape[1]
    nkv = dkv // HEAD_DIM
    nb = seqlen // ATTN_BLOCK
    blk = ATTN_BLOCK

    def body(sink_ref, q_ref, kc_ref, kp_ref, vc_ref, vp_ref, o_ref):
        has_prev = pl.program_id(0) > 0
        q4s = [_stack_heads(q_ref, kh) for kh in range(nkv)]
        kks = [_kv_pair(kp_ref, kc_ref, kh) for kh in range(nkv)]
        w, _ = _attn_probs(q4s, kks, _sink_col(sink_ref, nkv * GQA_GROUP), has_prev)
        wb = w.astype(BF16)
        for kh in range(nkv):
            o4 = _dot(wb[kh * GQA_GROUP * blk:(kh + 1) * GQA_GROUP * blk, :], _kv_pair(vp_ref, vc_ref, kh))
            for g in range(GQA_GROUP):
                hq = kh * GQA_GROUP + g
                o_ref[:, hq * HEAD_DIM:(hq + 1) * HEAD_DIM] = o4[g * blk:(g + 1) * blk, :].astype(BF16)

    cur = lambda n: (n, 0)
    prev = lambda n: (jnp.maximum(n - 1, 0), 0)
    return _pcall(
        body, name=name, out_shape=SDS((seqlen, d), BF16), grid=(nb,),
        in_specs=[pl.BlockSpec(memory_space=pltpu.SMEM), pl.BlockSpec((blk, d), cur),
                  pl.BlockSpec((blk, dkv), cur), pl.BlockSpec((blk, dkv), prev),
                  pl.BlockSpec((blk, dkv), cur), pl.BlockSpec((blk, dkv), prev)],
        out_specs=pl.BlockSpec((blk, d), cur), vmem=32 << 20)(sinks, q, k, k, v, v)


def _attn_bwd(q, k, v, do, sinks, name):
    seqlen, d = q.shape
    dkv = k.shape[1]
    nkv = dkv // HEAD_DIM
    nh = d // HEAD_DIM
    nb = seqlen // ATTN_BLOCK
    blk = ATTN_BLOCK
    scale = HEAD_DIM ** -0.5

    def body(sink_ref, q_ref, kc_ref, kp_ref, vc_ref, vp_ref, do_ref, dq_ref, dk_ref, dv_ref, ds_ref,
             ck_ref, cv_ref):
        n = pl.program_id(0)

        @pl.when(n == 0)
        def _():
            ck_ref[...] = jnp.zeros_like(ck_ref)
            cv_ref[...] = jnp.zeros_like(cv_ref)
            ds_ref[...] = jnp.zeros_like(ds_ref)

        @pl.when(n == nb)
        def _():
            dk_ref[...] = ck_ref[...]
            dv_ref[...] = cv_ref[...]

        @pl.when(n < nb)
        def _():
            has_prev = n > 0
            q4s = [_stack_heads(q_ref, kh) for kh in range(nkv)]
            do4s = [_stack_heads(do_ref, kh) for kh in range(nkv)]
            kks = [_kv_pair(kp_ref, kc_ref, kh) for kh in range(nkv)]
            w, wsink = _attn_probs(q4s, kks, _sink_col(sink_ref, nh), has_prev)
            dw = jnp.concatenate([_dot(do4s[kh], _kv_pair(vp_ref, vc_ref, kh), NT) for kh in range(nkv)], axis=0)
            dsum = jnp.sum(w * dw, axis=1, keepdims=True)
            ds_all = (w * (dw - dsum) * scale).astype(BF16)
            wb = w.astype(BF16)
            dsk = -wsink * dsum
            for kh in range(nkv):
                sl = slice(kh * HEAD_DIM, (kh + 1) * HEAD_DIM)
                rows = slice(kh * GQA_GROUP * blk, (kh + 1) * GQA_GROUP * blk)
                ds = ds_all[rows, :]
                dq4 = _dot(ds, kks[kh])
                dkk = _dot(ds, q4s[kh], TN)
                dvv = _dot(wb[rows, :], do4s[kh], TN)
                for g in range(GQA_GROUP):
                    hq = kh * GQA_GROUP + g
                    dq_ref[:, hq * HEAD_DIM:(hq + 1) * HEAD_DIM] = dq4[g * blk:(g + 1) * blk, :]
                    ds_ref[hq:hq + 1, :] += jnp.sum(dsk[hq * blk:(hq + 1) * blk, :], axis=0, keepdims=True)
                dk_ref[:, sl] = ck_ref[:, sl] + dkk[:blk, :]
                dv_ref[:, sl] = cv_ref[:, sl] + dvv[:blk, :]
                ck_ref[:, sl] = dkk[blk:, :]
                cv_ref[:, sl] = dvv[blk:, :]

    cur = lambda n: (jnp.minimum(n, nb - 1), 0)
    prev = lambda n: (jnp.clip(n - 1, 0, nb - 1), 0)
    lag = lambda n: (jnp.maximum(n - 1, 0), 0)
    return _pcall(
        body, name=name,
        out_shape=[SDS((seqlen, d), F32), SDS((seqlen, dkv), F32), SDS((seqlen, dkv), F32), SDS((nh, 128), F32)],
        grid=(nb + 1,),
        in_specs=[pl.BlockSpec(memory_space=pltpu.SMEM), pl.BlockSpec((blk, d), cur),
                  pl.BlockSpec((blk, dkv), cur), pl.BlockSpec((blk, dkv), prev),
                  pl.BlockSpec((blk, dkv), cur), pl.BlockSpec((blk, dkv), prev), pl.BlockSpec((blk, d), cur)],
        out_specs=[pl.BlockSpec((blk, d), cur), pl.BlockSpec((blk, dkv), lag), pl.BlockSpec((blk, dkv), lag),
                   pl.BlockSpec((nh, 128), lambda n: (0, 0))],
        scratch=[pltpu.VMEM((blk, dkv), F32)] * 2, vmem=32 << 20)(sinks, q, k, k, v, v, do)


def _ssm_mats(lre, lim, ldt, btr, bti, cr, ci):
    dt = jnp.exp(ldt)
    mag = jnp.exp(lre * dt)
    ar = mag * jnp.cos(lim * dt)
    ai = mag * jnp.sin(lim * dt)
    den = lre * lre + lim * lim
    nr = ar - 1.0
    cfr = (nr * lre + ai * lim) / den
    cfi = (ai * lre - nr * lim) / den
    bbr = cfr * btr - cfi * bti
    bbi = cfr * bti + cfi * btr
    pr = [jnp.ones_like(ar)]
    pi = [jnp.zeros_like(ai)]
    for _ in range(SSM_T):
        pr.append(pr[-1] * ar - pi[-1] * ai)
        pi.append(pr[-2] * ai + pi[-1] * ar)
    last = SSM_T - 1
    p_re = jnp.concatenate([pr[last - s] * bbr - pi[last - s] * bbi for s in range(SSM_T)], axis=0)
    p_im = jnp.concatenate([pr[last - s] * bbi + pi[last - s] * bbr for s in range(SSM_T)], axis=0)
    qt_re = jnp.concatenate([pr[t + 1] * cr - pi[t + 1] * ci for t in range(SSM_T)], axis=0)
    qt_im = jnp.concatenate([-(pr[t + 1] * ci + pi[t + 1] * cr) for t in range(SSM_T)], axis=0)
    ctr = jnp.concatenate([cr] * SSM_T, axis=0)
    cti = jnp.concatenate([ci] * SSM_T, axis=0)
    lag = (lax.broadcasted_iota(jnp.int32, (SSM_W, SSM_W), 1) // SSM_GROUP
           - lax.broadcasted_iota(jnp.int32, (SSM_W, SSM_W), 0) // SSM_GROUP)
    m = jnp.zeros((SSM_W, SSM_W), F32)
    for l in range(SSM_T):
        zr = jnp.concatenate([pr[l] * bbr - pi[l] * bbi] * SSM_T, axis=0)
        zi = jnp.concatenate([pr[l] * bbi + pi[l] * bbr] * SSM_T, axis=0)
        kl = _dot3_nt(zr, ctr) - _dot3_nt(zi, cti)
        m = m + jnp.where(lag == l, kl, 0.0)
    return m, p_re, p_im, qt_re, qt_im, pr[SSM_T], pi[SSM_T]


_SSM_GB = 8


def _ssm_param_specs(ng):
    n, hh = SSM_STATE, SSM_GROUP
    gb = _tile(ng, _SSM_GB)
    row = pl.BlockSpec((gb, 1, n), lambda i: (i, 0, 0))
    one = pl.BlockSpec((gb, 1, 1), lambda i: (i, 0, 0))
    mat = pl.BlockSpec((gb, hh, n), lambda i: (i, 0, 0))
    big = pl.BlockSpec((gb, SSM_W, SSM_W), lambda i: (i, 0, 0))
    half = pl.BlockSpec((gb, SSM_W, n), lambda i: (i, 0, 0))
    return gb, row, one, mat, big, half


def _ssm_prep(params, name):
    ng = params[0].shape[0]
    n = SSM_STATE
    gb, row, one, mat, big, half = _ssm_param_specs(ng)

    def body(lre, lim, ldt, btr, bti, cr, ci, m_ref, pre_ref, pim_ref, qre_ref, qim_ref, atr_ref, ati_ref):
        for gi in range(gb):
            outs = _ssm_mats(lre[gi], lim[gi], ldt[gi], btr[gi], bti[gi], cr[gi], ci[gi])
            for ref, val in zip((m_ref, pre_ref, pim_ref, qre_ref, qim_ref, atr_ref, ati_ref), outs):
                ref[gi] = val

    return _pcall(
        body, name=name,
        out_shape=[SDS((ng, SSM_W, SSM_W), F32)] + [SDS((ng, SSM_W, n), F32)] * 4 + [SDS((ng, 1, n), F32)] * 2,
        grid=(ng // gb,), in_specs=[row, row, one, mat, mat, mat, mat],
        out_specs=[big, half, half, half, half, row, row], vmem=40 << 20)(*params)


def _ssm_prep_vjp(params, cots, name):
    ng = params[0].shape[0]
    n, hh = SSM_STATE, SSM_GROUP
    gb, row, one, mat, big, half = _ssm_param_specs(ng)

    def body(lre, lim, ldt, btr, bti, cr, ci, dm, dpre, dpim, dqre, dqim, datr, dati,
             o_lre, o_lim, o_ldt, o_btr, o_bti, o_cr, o_ci):
        for gi in range(gb):
            prm = (lre[gi], lim[gi], ldt[gi], btr[gi], bti[gi], cr[gi], ci[gi])
            _, pull = jax.vjp(_ssm_mats, *prm)
            grads = pull((dm[gi], dpre[gi], dpim[gi], dqre[gi], dqim[gi], datr[gi], dati[gi]))
            for ref, val in zip((o_lre, o_lim, o_ldt, o_btr, o_bti, o_cr, o_ci), grads):
                ref[gi] = val

    return _pcall(
        body, name=name,
        out_shape=[SDS((ng, 1, n), F32)] * 2 + [SDS((ng, 1, 1), F32)] + [SDS((ng, hh, n), F32)] * 4,
        grid=(ng // gb,), in_specs=[row, row, one, mat, mat, mat, mat, big, half, half, half, half, row, row],
        out_specs=[row, row, one, mat, mat, mat, mat], vmem=48 << 20)(*params, *cots)


_SSM_GT = SSM_W // SSM_GROUP


def _blk_transpose(xs):
    assert len(xs) == SSM_T == _SSM_GT
    blk = lax.broadcasted_iota(jnp.int32, xs[0].shape, 1) // SSM_GROUP
    xs = list(xs)
    k = SSM_T // 2
    while k:
        high = (blk // k) % 2 == 1
        nxt = []
        for i in range(SSM_T):
            if i & k:
                nxt.append(jnp.where(high, xs[i], pltpu.roll(xs[i ^ k], SSM_W - SSM_GROUP * k, 1)))
            else:
                nxt.append(jnp.where(high, pltpu.roll(xs[i ^ k], SSM_GROUP * k, 1), xs[i]))
        xs = nxt
        k //= 2
    return xs


def _tile_groups(x_ref, ncb):
    return _blk_transpose([x_ref[pl.ds(t, ncb, stride=SSM_T), :] for t in range(SSM_T)])


def _groups_tile(ys, o_ref, ncb):
    for t, y in enumerate(_blk_transpose(ys)):
        o_ref[pl.ds(t, ncb, stride=SSM_T), :] = y


def _ssm_specs(seqlen, d):
    ncb = _tile(seqlen // SSM_T, 512)
    grid = (d // SSM_W, seqlen // (SSM_T * ncb))
    act = pl.BlockSpec((SSM_T * ncb, SSM_W), lambda j, r: (r, j))
    state = pl.BlockSpec((ncb, _SSM_GT * SSM_W), lambda j, r: (r, j))
    mats = pl.BlockSpec((_SSM_GT, SSM_W, SSM_W), lambda j, r: (j, 0, 0))
    return ncb, grid, act, state, mats


def _gsl(gl):
    return slice(gl * SSM_W, (gl + 1) * SSM_W)


def _ssm_state_in(hn, pmat, name):
    seqlen, d = hn.shape
    ncb, grid, act, state, mats = _ssm_specs(seqlen, d)

    def body(x_ref, p_ref, s_ref):
        us = _tile_groups(x_ref, ncb)
        for gl in range(_SSM_GT):
            s_ref[:, _gsl(gl)] = _dot3(us[gl], p_ref[gl], NN)

    return _pcall(body, name=name, out_shape=SDS((seqlen // SSM_T, d * SSM_T), F32), grid=grid,
                  in_specs=[act, mats], out_specs=state, vmem=40 << 20)(hn, pmat)


def _ssm_out(hn, xp, mmat, qt, name, comm=None):
    seqlen, d = hn.shape
    ncb, grid, act, state, mats = _ssm_specs(seqlen, d)

    def body(x_ref, xp_ref, m_ref, q_ref, y_ref):
        us = _tile_groups(x_ref, ncb)
        ys = [_dot3(us[gl], m_ref[gl], NN) + _dot3(xp_ref[:, _gsl(gl)], q_ref[gl], NT)
              for gl in range(_SSM_GT)]
        _groups_tile(ys, y_ref, ncb)

    return _pcall(body, name=name, out_shape=[SDS((seqlen, d), F32)], grid=grid,
                  in_specs=[act, state, mats, mats], out_specs=[act], vmem=40 << 20, comm=comm)(hn, xp, mmat, qt)


def _ssm_dstate(dy, qt, name):
    seqlen, d = dy.shape
    ncb, grid, act, state, mats = _ssm_specs(seqlen, d)

    def body(dy_ref, q_ref, o_ref):
        dys = _tile_groups(dy_ref, ncb)
        for gl in range(_SSM_GT):
            o_ref[:, _gsl(gl)] = _dot3(dys[gl], q_ref[gl], NN)

    return _pcall(body, name=name, out_shape=SDS((seqlen // SSM_T, d * SSM_T), F32), grid=grid,
                  in_specs=[act, mats], out_specs=state, vmem=40 << 20)(dy, qt)


def _ssm_bwd(hn, dy, xp, gs, mmat, pmat, name, comm=None):
    seqlen, d = hn.shape
    ng = d // SSM_GROUP
    ncb, grid, act, state, mats = _ssm_specs(seqlen, d)

    def body(x_ref, dy_ref, xp_ref, g_ref, m_ref, p_ref, du_ref, dm_ref, dp_ref, dq_ref, da_ref):
        @pl.when(pl.program_id(1) == 0)
        def _():
            for ref in (dm_ref, dp_ref, dq_ref, da_ref):
                ref[...] = jnp.zeros_like(ref)
        us = _tile_groups(x_ref, ncb)
        dys = _tile_groups(dy_ref, ncb)
        dus = []
        for gl in range(_SSM_GT):
            xv, gv = xp_ref[:, _gsl(gl)], g_ref[:, _gsl(gl)]
            u2, dy2, x2, g2 = _split(us[gl]), _split(dys[gl]), _split(xv), _split(gv)
            dus.append(_dot3(dy2, m_ref[gl], NT) + _dot3(g2, p_ref[gl], NT))
            dm_ref[gl] += _dot3(u2, dy2, TN)
            dp_ref[gl] += _dot3(u2, g2, TN)
            dq_ref[gl] += _dot3(dy2, x2, TN)
            da_ref[gl, 0:1, :] += jnp.sum(xv * gv, axis=0, keepdims=True)
            da_ref[gl, 1:2, :] += jnp.sum(xv * pltpu.roll(gv, SSM_STATE, 1), axis=0, keepdims=True)
        _groups_tile(dus, du_ref, ncb)

    return _pcall(
        body, name=name,
        out_shape=[SDS((seqlen, d), F32)] + [SDS((ng, SSM_W, SSM_W), F32)] * 3 + [SDS((ng, 2, SSM_W), F32)],
        grid=grid, in_specs=[act, act, state, state, mats, mats],
        out_specs=[act, mats, mats, mats, pl.BlockSpec((_SSM_GT, 2, SSM_W), lambda j, r: (j, 0, 0))],
        vmem=48 << 20, comm=comm)(hn, dy, xp, gs, mmat, pmat)


def _ssm_carry(s, a1, a2, reverse, name):
    nc, w = s.shape
    tc = _tile(nc, 256)
    nblk = nc // tc
    sub = 8

    def body(s_ref, a1_ref, a2_ref, o_ref, st_ref, sw_ref):
        @pl.when(pl.program_id(0) == 0)
        def _():
            st_ref[...] = jnp.zeros_like(st_ref)
            sw_ref[...] = jnp.zeros_like(sw_ref)
        a1v = jnp.broadcast_to(a1_ref[...], (sub, w))
        a2v = jnp.broadcast_to(a2_ref[...], (sub, w))
        first = lax.broadcasted_iota(jnp.int32, (sub, w), 1) % SSM_W < SSM_STATE
        row = lax.broadcasted_iota(jnp.int32, (sub, w), 0)

        def step(t, carry):
            x, xs = carry
            tt = (tc // sub - 1 - t) if reverse else t
            base = pl.multiple_of(tt * sub, sub)
            blk = s_ref[pl.ds(base, sub), :]
            blks = jnp.where(first, pltpu.roll(blk, w - SSM_STATE, 1), pltpu.roll(blk, SSM_STATE, 1))
            out = jnp.zeros((sub, w), F32)
            for r in (range(sub - 1, -1, -1) if reverse else range(sub)):
                out = jnp.where(row == r, x, out)
                sr = jnp.broadcast_to(blk[r:r + 1, :], (sub, w))
                ssr = jnp.broadcast_to(blks[r:r + 1, :], (sub, w))
                x, xs = a1v * x + a2v * xs + sr, a1v * xs - a2v * x + ssr
            o_ref[pl.ds(base, sub), :] = out
            return x, xs

        x, xs = lax.fori_loop(0, tc // sub, step, (st_ref[...], sw_ref[...]))
        st_ref[...] = x
        sw_ref[...] = xs

    imap = (lambda i: (nblk - 1 - i, 0)) if reverse else (lambda i: (i, 0))
    cst = pl.BlockSpec((1, w), lambda i: (0, 0))
    return _pcall(body, name=name, out_shape=SDS((nc, w), F32), grid=(nblk,),
                  in_specs=[pl.BlockSpec((tc, w), imap), cst, cst], out_specs=pl.BlockSpec((tc, w), imap),
                  scratch=[pltpu.VMEM((sub, w), F32)] * 2,
                  vmem=_vmem_limit([((tc, w), F32, 4)], extra=8 << 20))(s, a1, a2)


def _ssm_rows(atr, ati, conj):
    ng = atr.shape[0]
    ai = -ati if conj else ati
    a1 = jnp.concatenate([atr, atr], axis=2).reshape(1, ng * SSM_W)
    a2 = jnp.concatenate([-ai, ai], axis=2).reshape(1, ng * SSM_W)
    return a1, a2


def _peers():
    x, y, c = (lax.axis_index(a) for a in AXES)
    me = 4 * x + 2 * y + c
    peers = []
    for dx, dy, dc in [(0, 0, 1), (0, 1, 0), (0, 1, 1), (1, 0, 0), (1, 0, 1), (1, 1, 0), (1, 1, 1)]:
        px, py, pc = (1 - x) if dx else x, (1 - y) if dy else y, (1 - c) if dc else c
        peers.append(((px, py, pc), 4 * px + 2 * py + pc))
    return me, peers


class _Exchange:
    def __init__(self, arrs, scatter, layers=None):
        self.arrs = list(arrs)
        self.scatter = scatter
        self.layers = list(layers) if layers is not None else [None] * len(self.arrs)

    def out_shape(self):
        shapes = []
        for arr, layer in zip(self.arrs, self.layers):
            block = arr.shape[1:] if (self.scatter or layer is not None) else arr.shape
            shapes.append(SDS((NDEV,) + tuple(block), arr.dtype))
        return shapes

    def semaphores(self):
        n = len(self.arrs)
        return [pltpu.SemaphoreType.DMA((n * (NDEV - 1),)), pltpu.SemaphoreType.DMA((n * (NDEV - 1),)),
                pltpu.SemaphoreType.DMA((n,))]

    def _src(self, ref, a, block):
        if self.scatter:
            return ref.at[block]
        return ref if self.layers[a] is None else ref.at[self.layers[a]]

    def _remote(self, xin, xout, sems, a, k, peer, landing):
        pid, pidx = peer
        slot = a * (NDEV - 1) + k
        return pltpu.make_async_remote_copy(
            src_ref=self._src(xin[a], a, pidx), dst_ref=xout[a].at[landing],
            send_sem=sems[0].at[slot], recv_sem=sems[1].at[slot], device_id=pid, device_id_type=MESH)

    def _local(self, xin, xout, sems, a, me):
        return pltpu.make_async_copy(self._src(xin[a], a, me), xout[a].at[me], sems[2].at[a])

    def start(self, xin, xout, sems):
        me, peers = _peers()
        for a in range(len(self.arrs)):
            self._local(xin, xout, sems, a, me).start()
        for k, peer in enumerate(peers):
            for a in range(len(self.arrs)):
                self._remote(xin, xout, sems, a, k, peer, me).start()

    def wait(self, xin, xout, sems):
        me, peers = _peers()
        for a in range(len(self.arrs)):
            self._local(xin, xout, sems, a, me).wait()
        for k, peer in enumerate(peers):
            for a in range(len(self.arrs)):
                cp = self._remote(xin, xout, sems, a, k, peer, peer[1])
                cp.wait_send()
                cp.wait_recv()


def _exchange(arrs, scatter, name, layers=None):
    comm = _Exchange(arrs, scatter, layers)
    n = len(comm.arrs)

    def body(*refs):
        xin, xout, sems = refs[:n], refs[n:2 * n], refs[2 * n:]
        comm.start(xin, xout, sems)
        comm.wait(xin, xout, sems)

    hbm = pl.BlockSpec(memory_space=pl.ANY)
    return pl.pallas_call(
        body, out_shape=comm.out_shape(), in_specs=[hbm] * n, out_specs=[hbm] * n,
        scratch_shapes=comm.semaphores(), name=name, interpret=False)(*comm.arrs)


def _adamw(parts, w, m, v, name):
    rows, cols = w.shape
    tr = _tile(rows, max(8, (1 << 17) // cols))
    c1 = 1.0 - ADAM_B1 ** ADAM_STEP
    c2 = 1.0 - ADAM_B2 ** ADAM_STEP

    def body(p_ref, w_ref, m_ref, v_ref, g_ref, d_ref, nm_ref, nv_ref):
        g = p_ref[0]
        for j in range(1, NDEV):
            g = g + p_ref[j]
        mm = ADAM_B1 * m_ref[...] + (1.0 - ADAM_B1) * g
        vv = ADAM_B2 * v_ref[...] + (1.0 - ADAM_B2) * (g * g)
        g_ref[...] = g
        nm_ref[...] = mm
        nv_ref[...] = vv
        d_ref[...] = -ADAM_LR * ((mm / c1) / (jnp.sqrt(vv / c2) + ADAM_EPS) + ADAM_WD * w_ref[...])

    spec = pl.BlockSpec((tr, cols), lambda i: (i, 0))
    return _pcall(
        body, name=name, out_shape=[SDS((rows, cols), F32)] * 4, grid=(rows // tr,),
        in_specs=[pl.BlockSpec((NDEV, tr, cols), lambda i: (0, i, 0)), spec, spec, spec], out_specs=[spec] * 4,
        vmem=_vmem_limit([((NDEV + 7, tr, cols), F32, 2)]))(parts, w, m, v)


def kernel(x, p, norm_mix, ssm_lambda_re, ssm_lambda_im, ssm_log_dt, ssm_b_re, ssm_b_im, ssm_c_re, ssm_c_im, ssm_d, ssm_w_glu, kv_norm, w_k, w_v, w_q, attn_sinks, w_o, norm_mlp, w_up, w_down, norm_ple, w_ple_gate, w_ple_proj, norm_final, loss_target, m_norm_mix, m_ssm_lambda_re, m_ssm_lambda_im, m_ssm_log_dt, m_ssm_b_re, m_ssm_b_im, m_ssm_c_re, m_ssm_c_im, m_ssm_d, m_ssm_w_glu, m_kv_norm, m_w_k, m_w_v, m_w_q, m_attn_sinks, m_w_o, m_norm_mlp, m_w_up, m_w_down, m_norm_ple, m_w_ple_gate, m_w_ple_proj, m_norm_final, v_norm_mix, v_ssm_lambda_re, v_ssm_lambda_im, v_ssm_log_dt, v_ssm_b_re, v_ssm_b_im, v_ssm_c_re, v_ssm_c_im, v_ssm_d, v_ssm_w_glu, v_kv_norm, v_w_k, v_w_v, v_w_q, v_attn_sinks, v_w_o, v_norm_mlp, v_w_up, v_w_down, v_norm_ple, v_w_ple_gate, v_w_ple_proj, v_norm_final):
    names = ['norm_mix', 'ssm_lambda_re', 'ssm_lambda_im', 'ssm_log_dt', 'ssm_b_re', 'ssm_b_im', 'ssm_c_re',
             'ssm_c_im', 'ssm_d', 'ssm_w_glu', 'kv_norm', 'w_k', 'w_v', 'w_q', 'attn_sinks', 'w_o', 'norm_mlp',
             'w_up', 'w_down', 'norm_ple', 'w_ple_gate', 'w_ple_proj', 'norm_final']
    weights = dict(zip(names, (norm_mix, ssm_lambda_re, ssm_lambda_im, ssm_log_dt, ssm_b_re, ssm_b_im, ssm_c_re,
                               ssm_c_im, ssm_d, ssm_w_glu, kv_norm, w_k, w_v, w_q, attn_sinks, w_o, norm_mlp,
                               w_up, w_down, norm_ple, w_ple_gate, w_ple_proj, norm_final)))
    mom1 = dict(zip(names, (m_norm_mix, m_ssm_lambda_re, m_ssm_lambda_im, m_ssm_log_dt, m_ssm_b_re, m_ssm_b_im,
                            m_ssm_c_re, m_ssm_c_im, m_ssm_d, m_ssm_w_glu, m_kv_norm, m_w_k, m_w_v, m_w_q,
                            m_attn_sinks, m_w_o, m_norm_mlp, m_w_up, m_w_down, m_norm_ple, m_w_ple_gate,
                            m_w_ple_proj, m_norm_final)))
    mom2 = dict(zip(names, (v_norm_mix, v_ssm_lambda_re, v_ssm_lambda_im, v_ssm_log_dt, v_ssm_b_re, v_ssm_b_im,
                            v_ssm_c_re, v_ssm_c_im, v_ssm_d, v_ssm_w_glu, v_kv_norm, v_w_k, v_w_v, v_w_q,
                            v_attn_sinks, v_w_o, v_norm_mlp, v_w_up, v_w_down, v_norm_ple, v_w_ple_gate,
                            v_w_ple_proj, v_norm_final)))

    seqlen, d = x.shape[1], x.shape[2]
    depth = w_up.shape[0]
    n_ssm = ssm_w_glu.shape[0]
    n_att = w_q.shape[0]
    ng = d // SSM_GROUP
    nh = d // HEAD_DIM
    h0 = x[0]
    tgt = loss_target[0]
    tabs = _rope_tables(seqlen)

    sharded = ['w_up', 'w_down', 'w_ple_gate', 'w_ple_proj', 'ssm_w_glu', 'w_q', 'w_o', 'w_k', 'w_v']
    shards = {k: weights[k].astype(BF16) for k in sharded}
    shards['ssm_d'] = ssm_d
    dkv = w_k.shape[1]

    def layer_set(i):
        keys = [('w_up', i), ('w_down', i), ('w_ple_gate', i), ('w_ple_proj', i)]
        keys += [('ssm_w_glu', i), ('ssm_d', i)] if i < n_ssm else [('w_q', i - n_ssm), ('w_o', i - n_ssm)]
        if i == n_ssm:
            keys += [('w_k', None), ('w_v', None)]
        return keys

    def gather_of(i, only=None):
        keys = [kl for kl in layer_set(i) if only is None or kl[0] in only]
        return keys, _Exchange([shards[k] for k, _ in keys], False, [l for _, l in keys])

    def as_operands(keys, blocks):
        w = {}
        for (k, _), g in zip(keys, blocks):
            if k in ('w_ple_gate', 'w_q', 'w_o', 'w_k', 'w_v'):
                g = g.reshape(d, g.shape[2])
            elif k in ('w_ple_proj', 'ssm_w_glu'):
                g = g.transpose(1, 0, 2).reshape(g.shape[1], NDEV * g.shape[2])
            elif k == 'ssm_d':
                g = g.reshape(d)
            w[k] = g
        return w

    lw = {}

    def ssm_params(i):
        n = SSM_STATE
        return (ssm_lambda_re[i].reshape(ng, 1, n), ssm_lambda_im[i].reshape(ng, 1, n),
                ssm_log_dt[i].reshape(ng, 1, 1), jnp.swapaxes(ssm_b_re[i], 1, 2), jnp.swapaxes(ssm_b_im[i], 1, 2),
                ssm_c_re[i], ssm_c_im[i])

    h = h0
    h_in, h_a, h_b, acts = [], [], [], []
    ssm_saved, att_saved = {}, {}
    k_sh = v_sh = None
    for i in range(depth):
        h_in.append(h)
        if i < n_ssm:
            first = gather_of(0, ('ssm_w_glu', 'ssm_d')) if i == 0 else None
            hn, *got = _norm_fwd(h, norm_mix[i], f"norm_mix_fwd{i}", comm=first and first[1])
            if first:
                lw[0] = as_operands(first[0], got)
            mats = _ssm_prep(ssm_params(i), f"ssm_prep{i}")
            mmat, atr, ati = mats[0], mats[5], mats[6]
            pmat = jnp.concatenate([mats[1], mats[2]], axis=2)
            qt = jnp.concatenate([mats[3], mats[4]], axis=2)
            s_in = _ssm_state_in(hn, pmat, f"ssm_state_in{i}")
            xp = _ssm_carry(s_in, *_ssm_rows(atr, ati, False), False, f"ssm_carry_fwd{i}")
            rest = gather_of(0, ('w_up', 'w_down', 'w_ple_gate', 'w_ple_proj')) if i == 0 else None
            y, *got = _ssm_out(hn, xp, mmat, qt, f"ssm_out{i}", comm=rest and rest[1])
            if rest:
                lw[0].update(as_operands(rest[0], got))
            ha = _glu_fwd(y, hn, h, lw[i]['ssm_d'], lw[i]['ssm_w_glu'], f"glu_fwd{i}")
            ssm_saved[i] = (hn, mmat, pmat, qt, atr, ati, xp, y)
        else:
            j = i - n_ssm
            q = _q_fwd(h, norm_mix[i], lw[i]['w_q'], tabs, f"q_fwd{j}")
            o = _attn_fwd(q, k_sh, v_sh, attn_sinks[j], f"attn_fwd{j}")
            ha = _lin_res(h, o, lw[i]['w_o'], f"attn_out{j}")
            att_saved[j] = (q, o)
        h_a.append(ha)
        nxt = gather_of(i + 1) if i + 1 < depth else None
        res = _mlp_fwd(ha, norm_mlp[i], lw[i]['w_up'], lw[i]['w_down'], f"mlp_fwd{i}", comm=nxt and nxt[1])
        hb = res[0]
        acts.append(res[1])
        if nxt:
            lw[i + 1] = as_operands(nxt[0], res[2:])
        h_b.append(hb)
        h = _ple_fwd(hb, p[i, 0], norm_ple[i], lw[i]['w_ple_gate'], lw[i]['w_ple_proj'], f"ple_fwd{i}")
        if i == n_ssm - 1:
            k_sh, v_sh = _kv_fwd(h, kv_norm, lw[n_ssm]['w_k'], lw[n_ssm]['w_v'], tabs, "kv_fwd")
    h_kv = h_in[n_ssm] if n_ssm < depth else h
    dh, loss_row, g_norm_final = _loss_bwd(h, norm_final, tgt, "loss_bwd")
    loss = lax.psum(loss_row[0, 0], AXES)

    g_norm_mix, g_norm_mlp, g_norm_ple = [None] * depth, [None] * depth, [None] * depth
    g_ssm, g_sinks = [None] * n_ssm, [None] * n_att
    g_kv_norm = None
    dks, dvs = [], []
    recv = {}
    pending = None
    for i in range(depth - 1, -1, -1):
        gl = {}
        wk, wv = lw[n_ssm]['w_k'], lw[n_ssm]['w_v']
        if i == n_ssm - 1:
            dh, dkp, dvb, hkb, g_kv_norm = _kv_bwd(dks, dvs, h_kv, kv_norm, dh, wk, wv, tabs, "kv_bwd")
            gl['w_k', None] = _atb(hkb, dkp, False, "grad_w_k")
            gl['w_v', None] = _atb(hkb, dvb, False, "grad_w_v")
        dhb, dz, nb16, dpp, g_norm_ple[i] = _ple_bwd(h_b[i], p[i, 0], dh, norm_ple[i], lw[i]['w_ple_gate'],
                                                     lw[i]['w_ple_proj'], f"ple_bwd{i}")
        gl['w_ple_gate', i] = _atb(nb16, dz, False, f"grad_w_ple_gate{i}")
        gl['w_ple_proj', i] = _atb(p[i, 0], dpp, True, f"grad_w_ple_proj{i}")
        res = _mlp_bwd(h_a[i], acts[i], dhb, norm_mlp[i], lw[i]['w_up'], lw[i]['w_down'], f"mlp_bwd{i}",
                       comm=pending and pending[1])
        dha, hmb, da, g_norm_mlp[i] = res[:4]
        if pending:
            recv.update(zip(pending[0], res[4:]))
            pending = None
        gl['w_up', i] = _atb(hmb, da, True, f"grad_w_up{i}")
        gl['w_down', i] = _atb(acts[i], dhb, False, f"grad_w_down{i}")
        if i >= n_ssm:
            j = i - n_ssm
            q, o = att_saved[j]
            do = _lin_nt(dha, lw[i]['w_o'], f"attn_out_bwd{j}")
            gl['w_o', j] = _atb(o, dha, False, f"grad_w_o{j}")
            dq, dk_j, dv_j, dsink = _attn_bwd(q, k_sh, v_sh, do, attn_sinks[j], f"attn_bwd{j}")
            dks.append(dk_j)
            dvs.append(dv_j)
            g_sinks[j] = dsink[:, 0]
            dh, dqp, hnb, g_norm_mix[i] = _q_bwd(dq, h_in[i], norm_mix[i], dha, lw[i]['w_q'], tabs, f"q_bwd{j}")
            gl['w_q', j] = _atb(hnb, dqp, False, f"grad_w_q{j}")
        else:
            hn, mmat, pmat, qt, atr, ati, xp, y = ssm_saved[i]
            dyy, dhn_d, geb, dab, g_dskip = _glu_bwd(y, hn, dha, lw[i]['ssm_d'], lw[i]['ssm_w_glu'], f"glu_bwd{i}")
            gl['ssm_d', i] = g_dskip.reshape(NDEV, d // NDEV)
            gl['ssm_w_glu', i] = _atb(geb, dab, True, f"grad_ssm_w_glu{i}")
            dxp = _ssm_dstate(dyy, qt, f"ssm_dstate{i}")
            gs = _ssm_carry(dxp, *_ssm_rows(atr, ati, True), True, f"ssm_carry_bwd{i}")
            last = (list(gl), _Exchange(list(gl.values()), True)) if i == 0 else None
            du, dm, dp, dqt, da_raw, *got = _ssm_bwd(hn, dyy, xp, gs, mmat, pmat, f"ssm_bwd{i}",
                                                     comm=last and last[1])
            if last:
                recv.update(zip(last[0], got))
                gl = {}
            n = SSM_STATE
            cots = (dm, dp[:, :, :n], dp[:, :, n:], dqt[:, :, :n], dqt[:, :, n:],
                    (da_raw[:, 0:1, :n] + da_raw[:, 0:1, n:]), (da_raw[:, 1:2, :n] - da_raw[:, 1:2, n:]))
            g_ssm[i] = _ssm_prep_vjp(ssm_params(i), cots, f"ssm_prep_vjp{i}")
            dh, g_norm_mix[i] = _norm_bwd(h_in[i], norm_mix[i], dhn_d, du, dha, f"norm_mix_bwd{i}")
        if gl:
            pending = (list(gl), _Exchange(list(gl.values()), True))
    grad_x = dh[None]
    if pending:
        recv.update(zip(pending[0], _exchange(pending[1].arrs, True, "scatter_grads0")))

    out_g, out_d, out_m, out_v = {}, {}, {}, {}
    updated = {}
    for (k, l), parts in recv.items():
        pick = (lambda t: t) if l is None else (lambda t: t[l])
        shp = pick(weights[k]).shape
        r2 = (math.prod(shp[:-1]), shp[-1])
        res = _adamw(parts.reshape((NDEV,) + r2), pick(weights[k]).reshape(r2), pick(mom1[k]).reshape(r2),
                     pick(mom2[k]).reshape(r2), f"adamw_{k}{'' if l is None else l}")
        updated.setdefault(k, {})[l] = [t.reshape(shp) for t in res]
    for k, by_layer in updated.items():
        for n, dst in enumerate((out_g, out_d, out_m, out_v)):
            dst[k] = by_layer[None][n] if None in by_layer else jnp.stack([by_layer[l][n] for l in sorted(by_layer)])

    def ssm_grad(idx, unswap=False):
        g = jnp.stack([g_ssm[i][idx] for i in range(n_ssm)])
        return jnp.swapaxes(g, 2, 3) if unswap else g

    small = {'norm_mix': jnp.concatenate(g_norm_mix, axis=0),
             'ssm_lambda_re': ssm_grad(0), 'ssm_lambda_im': ssm_grad(1), 'ssm_log_dt': ssm_grad(2),
             'ssm_b_re': ssm_grad(3, True), 'ssm_b_im': ssm_grad(4, True),
             'ssm_c_re': ssm_grad(5), 'ssm_c_im': ssm_grad(6),
             'kv_norm': g_kv_norm, 'attn_sinks': jnp.stack(g_sinks),
             'norm_mlp': jnp.concatenate(g_norm_mlp, axis=0), 'norm_ple': jnp.concatenate(g_norm_ple, axis=0),
             'norm_final': g_norm_final}
    snames = list(small)
    sizes = [weights[k].size for k in snames]
    total = sum(sizes)
    lanes = 128
    padded = -(-total // (512 * lanes)) * (512 * lanes)

    def flat(parts):
        v = jnp.concatenate([t.reshape(-1) for t in parts] + [jnp.zeros((padded - total,), F32)])
        return v.reshape(padded // lanes, lanes)

    parts = _exchange([flat([small[k] for k in snames])], False, "gather_small_grads")[0]
    res = _adamw(parts, flat([weights[k] for k in snames]), flat([mom1[k] for k in snames]),
                 flat([mom2[k] for k in snames]), "adamw_small")
    off = 0
    for k, sz in zip(snames, sizes):
        for dst, t in zip((out_g, out_d, out_m, out_v), res):
            dst[k] = t.reshape(-1)[off:off + sz].reshape(weights[k].shape)
        off += sz

    return (loss, grad_x, *[out_g[k] for k in names], *[out_d[k] for k in names],
            *[out_m[k] for k in names], *[out_v[k] for k in names])
```

```python
import functools
import math

import jax
import jax.numpy as jnp
from jax import lax
from jax.experimental import pallas as pl
from jax.experimental.pallas import tpu as pltpu

F32 = jnp.float32
BF16 = jnp.bfloat16
SDS = jax.ShapeDtypeStruct
MESH = pl.DeviceIdType.MESH
AXES = ("x", "y", "c")
NDEV = 8

RMS_EPS = 1e-6
SSM_GROUP = 16
SSM_STATE = 64
SSM_T = 8
SSM_W = SSM_T * SSM_GROUP
HEAD_DIM = 64
GQA_GROUP = 4
ATTN_BLOCK = 128
ROT_DIM = 16
ROPE_THETA = 500000.0
NEG_INF = -1e30
ADAM_LR, ADAM_B1, ADAM_B2, ADAM_EPS, ADAM_WD, ADAM_STEP = 0.001, 0.9, 0.999, 1e-08, 0.01, 10

VMEM_CAP = 56 * 1024 * 1024
HI = lax.Precision.HIGHEST

NN = ((1,), (0,))
NT = ((1,), (1,))
TN = ((0,), (0,))


def _dot(a, b, dims=NN, precision=None):
    return lax.dot_general(a, b, (dims, ((), ())), preferred_element_type=F32, precision=precision)


def _split(a):
    if isinstance(a, tuple):
        return a
    hi = a.astype(BF16)
    return hi, (a - hi.astype(F32)).astype(BF16)


def _dot3(a, b, dims=NN):
    (ah, al), (bh, bl) = _split(a), _split(b)
    return _dot(ah, bh, dims) + (_dot(ah, bl, dims) + _dot(al, bh, dims))


@jax.custom_vjp
def _dot3_nt(a, b):
    return _dot3(a, b, NT)


def _dot3_nt_fwd(a, b):
    return _dot3(a, b, NT), (a, b)


def _dot3_nt_bwd(res, g):
    a, b = res
    return _dot3(g, b, NN), _dot3(g, a, TN)


_dot3_nt.defvjp(_dot3_nt_fwd, _dot3_nt_bwd)


def _tile(n, pref):
    t = min(n, pref)
    while n % t:
        t //= 2
    return t


def _nbytes(shape, dtype):
    return math.prod(s for s in shape if s is not None) * jnp.dtype(dtype).itemsize


def _vmem_limit(blocks, extra=0):
    need = sum(_nbytes(s, d) * n for s, d, n in blocks) + extra + (4 << 20)
    return int(min(VMEM_CAP, max(need, 16 << 20)))


def _pcall(body, *, name, out_shape, grid, in_specs, out_specs, scratch=(), vmem=None, comm=None):
    single = not isinstance(out_shape, (list, tuple))
    out_shape = [out_shape] if single else list(out_shape)
    out_specs = [out_specs] if single else list(out_specs)
    in_specs, scratch = list(in_specs), list(scratch)
    if comm is not None:
        n_in, n_out, n_scr, nx = len(in_specs), len(out_specs), len(scratch), len(comm.arrs)
        hbm = pl.BlockSpec(memory_space=pl.ANY)
        in_specs = in_specs + [hbm] * nx
        out_specs = out_specs + [hbm] * nx
        out_shape = out_shape + comm.out_shape()
        scratch = scratch + comm.semaphores()
        inner = body

        def body(*refs):
            ins, xin, rest = refs[:n_in], refs[n_in:n_in + nx], refs[n_in + nx:]
            outs, xout, rest = rest[:n_out], rest[n_out:n_out + nx], rest[n_out + nx:]
            scr, sems = rest[:n_scr], rest[n_scr:]
            first = functools.reduce(jnp.logical_and, [pl.program_id(a) == 0 for a in range(len(grid))])
            last = functools.reduce(jnp.logical_and, [pl.program_id(a) == g - 1 for a, g in enumerate(grid)])

            @pl.when(first)
            def _():
                comm.start(xin, xout, sems)
            inner(*ins, *outs, *scr)

            @pl.when(last)
            def _():
                comm.wait(xin, xout, sems)

    call = pl.pallas_call(
        body, out_shape=out_shape[0] if single and comm is None else out_shape, grid=grid, in_specs=in_specs,
        out_specs=out_specs[0] if single and comm is None else out_specs, scratch_shapes=scratch, name=name,
        compiler_params=pltpu.CompilerParams(
            dimension_semantics=("arbitrary",) * len(grid), vmem_limit_bytes=vmem),
        interpret=False)
    if comm is None:
        return call
    return lambda *args: call(*args, *comm.arrs)


def _rms(x, g):
    r = lax.rsqrt(jnp.mean(x * x, axis=-1, keepdims=True) + RMS_EPS)
    return x * r * g, r


def _rms_bwd(x, g, r, dy):
    xh = x * r
    dyg = dy * g
    dx = r * (dyg - xh * jnp.mean(dyg * xh, axis=-1, keepdims=True))
    return dx, jnp.sum(dy * xh, axis=0, keepdims=True)


_GELU_C = math.sqrt(2.0 / math.pi)


def _gelu_parts(x):
    t = jnp.tanh(_GELU_C * (x + 0.044715 * x * x * x))
    return 0.5 * x * (1.0 + t), t


def _gelu_grad(x, t):
    return 0.5 * (1.0 + t) + 0.5 * x * (1.0 - t * t) * _GELU_C * (1.0 + 3 * 0.044715 * x * x)


def _rope_tables(seqlen):
    half = ROT_DIM // 2
    inv = ROPE_THETA ** (-jnp.arange(0, ROT_DIM, 2, dtype=F32) / ROT_DIM)
    ang = jnp.arange(seqlen, dtype=jnp.int32).astype(F32)[:, None] * inv[None, :]
    cos, sin = jnp.cos(ang), jnp.sin(ang)
    zeros = jnp.zeros((seqlen, HEAD_DIM - ROT_DIM), F32)
    zh = jnp.zeros((seqlen, half), F32)
    c = jnp.concatenate([cos, cos, zeros + 1.0], axis=1)
    sa = jnp.concatenate([zh, sin, zeros], axis=1)
    sb = jnp.concatenate([-sin, zh, zeros], axis=1)
    return tuple(jnp.tile(t, (1, 128 // HEAD_DIM)) for t in (c, sa, sb))


def _rope(x, c, sa, sb):
    w = x.shape[1]
    reps = w // 128
    half = ROT_DIM // 2
    return (x * jnp.tile(c, (1, reps)) + pltpu.roll(x, half, 1) * jnp.tile(sa, (1, reps))
            + pltpu.roll(x, w - half, 1) * jnp.tile(sb, (1, reps)))


def _rope_bwd(dy, c, sa, sb):
    w = dy.shape[1]
    reps = w // 128
    half = ROT_DIM // 2
    return (dy * jnp.tile(c, (1, reps)) + pltpu.roll(dy * jnp.tile(sa, (1, reps)), w - half, 1)
            + pltpu.roll(dy * jnp.tile(sb, (1, reps)), half, 1))


def _rspec(tm, c):
    return pl.BlockSpec((tm, c), lambda i: (i, 0))


def _cspec(shape, idx=None):
    idx = tuple(idx) if idx is not None else (0,) * len(shape)
    return pl.BlockSpec(tuple(shape), lambda i: idx, pipeline_mode=pl.Buffered(1))


def _rowcall(body, name, seqlen, tm, rows_in, consts_in, rows_out, acc_out=(), extra_vmem=0, comm=None):
    in_specs = [_rspec(tm, a.shape[1]) for a in rows_in] + [_cspec(bs, ix) for _, bs, ix in consts_in]
    out_shape = [SDS((seqlen, c), d) for c, d in rows_out] + [SDS(s, F32) for s in acc_out]
    out_specs = [_rspec(tm, c) for c, _ in rows_out] + [pl.BlockSpec(s, lambda i: (0, 0)) for s in acc_out]
    blocks = ([((tm, a.shape[1]), a.dtype, 2) for a in rows_in] + [(bs, a.dtype, 1) for a, bs, _ in consts_in]
              + [((tm, c), d, 2) for c, d in rows_out])
    temporaries = 12 * tm * rows_in[0].shape[1] * 4
    return _pcall(body, name=name, out_shape=out_shape, grid=(seqlen // tm,), in_specs=in_specs,
                  out_specs=out_specs, vmem=_vmem_limit(blocks, extra_vmem + temporaries), comm=comm)(
                      *rows_in, *[a for a, _, _ in consts_in])


def _whole(a):
    return (a, a.shape, None)


def _norm_fwd(h, g, name, comm=None):
    seqlen, d = h.shape
    tm = _tile(seqlen, 1024)

    def body(h_ref, g_ref, o_ref):
        o_ref[...] = _rms(h_ref[...], g_ref[...])[0]

    return _rowcall(body, name, seqlen, tm, [h], [_whole(g.reshape(1, d))], [(d, F32)], comm=comm)


def _norm_bwd(h, g, dy1, dy2, dres, name, comm=None):
    seqlen, d = h.shape
    tm = _tile(seqlen, 512)

    def body(h_ref, dy1_ref, dy2_ref, dres_ref, g_ref, dh_ref, dg_ref):
        @pl.when(pl.program_id(0) == 0)
        def _():
            dg_ref[...] = jnp.zeros_like(dg_ref)
        x = h_ref[...]
        gv = g_ref[...]
        _, r = _rms(x, gv)
        dx, dg = _rms_bwd(x, gv, r, dy1_ref[...] + dy2_ref[...])
        dh_ref[...] = dres_ref[...] + dx
        dg_ref[...] += dg

    return _rowcall(body, name, seqlen, tm, [h, dy1, dy2, dres], [_whole(g.reshape(1, d))], [(d, F32)], [(1, d)],
                    comm=comm)


def _glu_fwd(y, hn, h, dskip, wglu, name):
    seqlen, d = h.shape
    tm = _tile(seqlen, 512)

    def body(y_ref, hn_ref, h_ref, d_ref, w_ref, o_ref):
        yy = y_ref[...] + d_ref[...] * hn_ref[...]
        ge, _ = _gelu_parts(yy)
        ab = _dot(ge.astype(BF16), w_ref[...])
        o_ref[...] = h_ref[...] + ab[:, :d] * jax.nn.sigmoid(ab[:, d:])

    return _rowcall(body, name, seqlen, tm, [y, hn, h], [_whole(dskip.reshape(1, d)), _whole(wglu)], [(d, F32)],
                    extra_vmem=tm * d * 4 * 6)[0]


def _glu_bwd(y, hn, dmix, dskip, wglu, name):
    seqlen, d = hn.shape
    tm = _tile(seqlen, 512)

    def body(y_ref, hn_ref, dm_ref, d_ref, w_ref, dyy_ref, dhn_ref, ge_ref, dab_ref, dd_ref):
        @pl.when(pl.program_id(0) == 0)
        def _():
            dd_ref[...] = jnp.zeros_like(dd_ref)
        hn_v = hn_ref[...]
        dsk = d_ref[...]
        yy = y_ref[...] + dsk * hn_v
        ge, t = _gelu_parts(yy)
        geb = ge.astype(BF16)
        ab = _dot(geb, w_ref[...])
        a = ab[:, :d]
        sg = jax.nn.sigmoid(ab[:, d:])
        dm = dm_ref[...]
        dab_ref[:, :d] = (dm * sg).astype(BF16)
        dab_ref[:, d:] = (dm * a * sg * (1.0 - sg)).astype(BF16)
        dge = _dot(dab_ref[...], w_ref[...], NT)
        dyy = dge * _gelu_grad(yy, t)
        dyy_ref[...] = dyy
        dhn_ref[...] = dyy * dsk
        ge_ref[...] = geb
        dd_ref[...] += jnp.sum(dyy * hn_v, axis=0, keepdims=True)

    return _rowcall(body, name, seqlen, tm, [y, hn, dmix], [_whole(dskip.reshape(1, d)), _whole(wglu)],
                    [(d, F32), (d, F32), (d, BF16), (2 * d, BF16)], [(1, d)], extra_vmem=tm * d * 4 * 8)


def _q_fwd(h, g, wq, tabs, name):
    seqlen, d = h.shape
    tm = _tile(seqlen, 512)

    def body(h_ref, c_ref, sa_ref, sb_ref, g_ref, w_ref, q_ref):
        hn, _ = _rms(h_ref[...], g_ref[...])
        qp = _dot(hn.astype(BF16), w_ref[...])
        q_ref[...] = _rope(qp, c_ref[...], sa_ref[...], sb_ref[...]).astype(BF16)

    return _rowcall(body, name, seqlen, tm, [h, *tabs], [_whole(g.reshape(1, d)), _whole(wq)], [(d, BF16)],
                    extra_vmem=tm * d * 4 * 6)[0]


def _q_bwd(dq, h, g, dres, wq, tabs, name):
    seqlen, d = h.shape
    tm = _tile(seqlen, 512)

    def body(dq_ref, h_ref, dres_ref, c_ref, sa_ref, sb_ref, g_ref, w_ref, dh_ref, dqp_ref, hn_ref, dg_ref):
        @pl.when(pl.program_id(0) == 0)
        def _():
            dg_ref[...] = jnp.zeros_like(dg_ref)
        dqp = _rope_bwd(dq_ref[...], c_ref[...], sa_ref[...], sb_ref[...]).astype(BF16)
        x = h_ref[...]
        gv = g_ref[...]
        hn, r = _rms(x, gv)
        dhn = _dot(dqp, w_ref[...], NT)
        dx, dg = _rms_bwd(x, gv, r, dhn)
        dh_ref[...] = dres_ref[...] + dx
        dqp_ref[...] = dqp
        hn_ref[...] = hn.astype(BF16)
        dg_ref[...] += dg

    return _rowcall(body, name, seqlen, tm, [dq, h, dres, *tabs], [_whole(g.reshape(1, d)), _whole(wq)],
                    [(d, F32), (d, BF16), (d, BF16)], [(1, d)], extra_vmem=tm * d * 4 * 6)


def _kv_fwd(h, g, wk, wv, tabs, name):
    seqlen, d = h.shape
    dk = wk.shape[1]
    tm = _tile(seqlen, 512)

    def body(h_ref, c_ref, sa_ref, sb_ref, g_ref, wk_ref, wv_ref, k_ref, v_ref):
        hk = _rms(h_ref[...], g_ref[...])[0].astype(BF16)
        k_ref[...] = _rope(_dot(hk, wk_ref[...]), c_ref[...], sa_ref[...], sb_ref[...]).astype(BF16)
        v_ref[...] = _dot(hk, wv_ref[...]).astype(BF16)

    return _rowcall(body, name, seqlen, tm, [h, *tabs], [_whole(g.reshape(1, d)), _whole(wk), _whole(wv)],
                    [(dk, BF16), (dk, BF16)], extra_vmem=tm * d * 4 * 4)


def _kv_bwd(dks, dvs, h, g, dres, wk, wv, tabs, name):
    seqlen, d = h.shape
    dkw = wk.shape[1]
    tm = _tile(seqlen, 512)

    def body(dk0_ref, dk1_ref, dv0_ref, dv1_ref, h_ref, dres_ref, c_ref, sa_ref, sb_ref, g_ref, wk_ref, wv_ref,
             dh_ref, dkp_ref, dvb_ref, hk_ref, dg_ref):
        @pl.when(pl.program_id(0) == 0)
        def _():
            dg_ref[...] = jnp.zeros_like(dg_ref)
        dkp = _rope_bwd(dk0_ref[...] + dk1_ref[...], c_ref[...], sa_ref[...], sb_ref[...]).astype(BF16)
        dvb = (dv0_ref[...] + dv1_ref[...]).astype(BF16)
        x = h_ref[...]
        gv = g_ref[...]
        hk, r = _rms(x, gv)
        dhk = _dot(dkp, wk_ref[...], NT) + _dot(dvb, wv_ref[...], NT)
        dx, dg = _rms_bwd(x, gv, r, dhk)
        dh_ref[...] = dres_ref[...] + dx
        dkp_ref[...] = dkp
        dvb_ref[...] = dvb
        hk_ref[...] = hk.astype(BF16)
        dg_ref[...] += dg

    return _rowcall(body, name, seqlen, tm, [dks[0], dks[1], dvs[0], dvs[1], h, dres, *tabs],
                    [_whole(g.reshape(1, d)), _whole(wk), _whole(wv)],
                    [(d, F32), (dkw, BF16), (dkw, BF16), (d, BF16)], [(1, d)], extra_vmem=tm * d * 4 * 6)


def _lin_res(h, xb, w, name):
    seqlen, d = h.shape
    tm = _tile(seqlen, 512)

    def body(h_ref, x_ref, w_ref, o_ref):
        o_ref[...] = h_ref[...] + _dot(x_ref[...], w_ref[...])

    return _rowcall(body, name, seqlen, tm, [h, xb], [_whole(w)], [(d, F32)], extra_vmem=tm * d * 4 * 2)[0]


def _lin_nt(dy, w, name):
    seqlen, d = dy.shape
    tm = _tile(seqlen, 512)

    def body(dy_ref, w_ref, o_ref):
        o_ref[...] = _dot(dy_ref[...].astype(BF16), w_ref[...], NT).astype(BF16)

    return _rowcall(body, name, seqlen, tm, [dy], [_whole(w)], [(w.shape[0], BF16)], extra_vmem=tm * d * 4 * 2)[0]


def _mlp_fwd(h, g, wup, wdn, name, comm=None):
    seqlen, d = h.shape
    f = wup.shape[1]
    tm = _tile(seqlen, 512)

    def body(h_ref, g_ref, wup_ref, wdn_ref, o_ref, act_ref):
        x = h_ref[...]
        hm = _rms(x, g_ref[...])[0].astype(BF16)
        r = jnp.maximum(_dot(hm, wup_ref[...]), 0.0)
        act = (r * r).astype(BF16)
        act_ref[...] = act
        o_ref[...] = x + _dot(act, wdn_ref[...])

    consts = [_whole(g.reshape(1, d)), _whole(wup), _whole(wdn)]
    return _rowcall(body, name, seqlen, tm, [h], consts, [(d, F32), (f, BF16)], extra_vmem=tm * f * 4 * 3, comm=comm)


def _mlp_bwd(h, act, dh, g, wup, wdn, name, comm=None):
    seqlen, d = h.shape
    f = wup.shape[1]
    tm = _tile(seqlen, 512)

    def body(h_ref, act_ref, dh_ref, g_ref, wup_ref, wdn_ref, dhin_ref, hm_ref, da_ref, dg_ref):
        @pl.when(pl.program_id(0) == 0)
        def _():
            dg_ref[...] = jnp.zeros_like(dg_ref)
        x = h_ref[...]
        gv = g_ref[...]
        dy = dh_ref[...]
        hm, r = _rms(x, gv)
        rl2 = 2.0 * jnp.sqrt(act_ref[...].astype(F32))
        da = (_dot(dy.astype(BF16), wdn_ref[...], NT) * rl2).astype(BF16)
        da_ref[...] = da
        dx, dg = _rms_bwd(x, gv, r, _dot(da, wup_ref[...], NT))
        dhin_ref[...] = dy + dx
        hm_ref[...] = hm.astype(BF16)
        dg_ref[...] += dg

    consts = [_whole(g.reshape(1, d)), _whole(wup), _whole(wdn)]
    return _rowcall(body, name, seqlen, tm, [h, act, dh], consts, [(d, F32), (d, BF16), (f, BF16)], [(1, d)],
                    extra_vmem=tm * f * 4 * 3, comm=comm)


def _ple_fwd(h, p, g, wg, wpp, name):
    seqlen, d = h.shape
    tm = _tile(seqlen, 512)

    def body(h_ref, p_ref, g_ref, wg_ref, wpp_ref, o_ref):
        x = h_ref[...]
        n = _rms(x, g_ref[...])[0].astype(BF16)
        gate = jax.nn.sigmoid(_dot(n, wg_ref[...]))
        o_ref[...] = x + gate * _dot(p_ref[...].astype(BF16), wpp_ref[...])

    return _rowcall(body, name, seqlen, tm, [h, p], [_whole(g.reshape(1, d)), _whole(wg), _whole(wpp)], [(d, F32)],
                    extra_vmem=tm * d * 4 * 5)[0]


def _ple_bwd(h, p, dh, g, wg, wpp, name):
    seqlen, d = h.shape
    tm = _tile(seqlen, 512)

    def body(h_ref, p_ref, dh_ref, g_ref, wg_ref, wpp_ref, dhin_ref, dz_ref, n_ref, dpp_ref, dg_ref):
        @pl.when(pl.program_id(0) == 0)
        def _():
            dg_ref[...] = jnp.zeros_like(dg_ref)
        x = h_ref[...]
        gv = g_ref[...]
        dy = dh_ref[...]
        n, r = _rms(x, gv)
        nb16 = n.astype(BF16)
        gate = jax.nn.sigmoid(_dot(nb16, wg_ref[...]))
        pp = _dot(p_ref[...].astype(BF16), wpp_ref[...])
        dz = (dy * pp * gate * (1.0 - gate)).astype(BF16)
        dn = _dot(dz, wg_ref[...], NT)
        dx, dg = _rms_bwd(x, gv, r, dn)
        dhin_ref[...] = dy + dx
        dz_ref[...] = dz
        n_ref[...] = nb16
        dpp_ref[...] = (dy * gate).astype(BF16)
        dg_ref[...] += dg

    return _rowcall(body, name, seqlen, tm, [h, p, dh], [_whole(g.reshape(1, d)), _whole(wg), _whole(wpp)],
                    [(d, F32), (d, BF16), (d, BF16), (d, BF16)], [(1, d)], extra_vmem=tm * d * 4 * 8)


def _loss_bwd(h, g, tgt, name):
    seqlen, d = h.shape
    tm = _tile(seqlen, 512)

    def body(h_ref, t_ref, g_ref, dh_ref, loss_ref, dg_ref):
        @pl.when(pl.program_id(0) == 0)
        def _():
            dg_ref[...] = jnp.zeros_like(dg_ref)
            loss_ref[...] = jnp.zeros_like(loss_ref)
        x = h_ref[...]
        gv = g_ref[...]
        y, r = _rms(x, gv)
        diff = y - t_ref[...]
        loss_ref[...] += (0.5 / d) * jnp.sum(jnp.sum(diff * diff, axis=1, keepdims=True), axis=0, keepdims=True)
        dx, dg = _rms_bwd(x, gv, r, diff * (1.0 / d))
        dh_ref[...] = dx
        dg_ref[...] += dg

    return _rowcall(body, name, seqlen, tm, [h, tgt], [_whole(g.reshape(1, d))], [(d, F32)], [(1, 128), (1, d)])


def _atb(a, b, col_blocked, name):
    seqlen, k1 = a.shape
    k2 = b.shape[1]
    if col_blocked:
        cs = k2 // NDEV
        t1 = _tile(k1, 512)
        nblk = _tile(NDEV, max(1, 2048 // cs))
        t2 = nblk * cs
        oshape = (NDEV, k1, cs)
        oblock = (nblk, t1, cs)
        omap = lambda i, j, l: (j, i, 0)
    else:
        rs = k1 // NDEV
        t2 = _tile(k2, 2048)
        nblk = _tile(NDEV, max(1, 1024 // rs))
        t1 = nblk * rs
        oshape = (NDEV, rs, k2)
        oblock = (nblk, rs, t2)
        omap = lambda i, j, l: (i, 0, j)
    tl = _tile(seqlen, 2048 if b.dtype == BF16 else 1024)

    def body(a_ref, b_ref, o_ref):
        @pl.when(pl.program_id(2) == 0)
        def _():
            o_ref[...] = jnp.zeros_like(o_ref)
        res = _dot(a_ref[...].astype(BF16), b_ref[...].astype(BF16), TN)
        for n in range(nblk):
            if col_blocked:
                o_ref[n] += res[:, n * cs:(n + 1) * cs]
            else:
                o_ref[n] += res[n * rs:(n + 1) * rs, :]

    blocks = [((tl, t1), a.dtype, 2), ((tl, t2), b.dtype, 2), ((t1, t2), F32, 2)]
    return _pcall(
        body, name=name, out_shape=SDS(oshape, F32), grid=(k1 // t1, k2 // t2, seqlen // tl),
        in_specs=[pl.BlockSpec((tl, t1), lambda i, j, l: (l, i)), pl.BlockSpec((tl, t2), lambda i, j, l: (l, j))],
        out_specs=pl.BlockSpec(oblock, omap),
        vmem=_vmem_limit(blocks, extra=t1 * t2 * 4 + tl * (t1 + t2) * 2))(a, b)


_ATTN_SCALE = HEAD_DIM ** -0.5


def _attn_bias():
    qi = lax.broadcasted_iota(jnp.int32, (ATTN_BLOCK, 2 * ATTN_BLOCK), 0) + ATTN_BLOCK
    kj = lax.broadcasted_iota(jnp.int32, (ATTN_BLOCK, 2 * ATTN_BLOCK), 1)
    band = (kj <= qi) & (qi - kj < ATTN_BLOCK)
    return jnp.where(jnp.stack([band & (kj >= ATTN_BLOCK), band]), 0.0, NEG_INF).astype(F32)


def _bias_spec():
    return pl.BlockSpec((None, ATTN_BLOCK, 2 * ATTN_BLOCK), lambda n: (jnp.minimum(n, 1), 0, 0))


def _attn_probs(q4s, kks, sink_col, bias):
    s = jnp.concatenate([_dot(q4, kk, NT) for q4, kk in zip(q4s, kks)], axis=0)
    rows = s.shape[0]
    s = (s.reshape(rows // ATTN_BLOCK, ATTN_BLOCK, 2 * ATTN_BLOCK) + bias).reshape(rows, 2 * ATTN_BLOCK)
    m = jnp.maximum(jnp.max(s, axis=1, keepdims=True), sink_col)
    pr = jnp.exp(s - m)
    es = jnp.exp(sink_col - m)
    inv = 1.0 / (jnp.sum(pr, axis=1, keepdims=True) + es)
    return pr * inv, es * inv


def _sink_col(sink_ref, nheads):
    return jnp.concatenate([jnp.full((ATTN_BLOCK, 1), sink_ref[hq], F32) for hq in range(nheads)], axis=0)


def _kv_pair(p_ref, c_ref, kh):
    sl = slice(kh * HEAD_DIM, (kh + 1) * HEAD_DIM)
    return jnp.concatenate([p_ref[:, sl], c_ref[:, sl]], axis=0)


def _stack_heads(ref, kh, scale=None):
    x = jnp.concatenate(
        [ref[:, (kh * GQA_GROUP + g) * HEAD_DIM:(kh * GQA_GROUP + g + 1) * HEAD_DIM] for g in range(GQA_GROUP)], axis=0)
    return x if scale is None else x * scale


def _attn_fwd(q, k, v, sinks, name):
    seqlen, d = q.shape
    dkv = k.shape[1]
    nkv = dkv // HEAD_DIM
    nb = seqlen // ATTN_BLOCK
    blk = ATTN_BLOCK

    def body(sink_ref, bias_ref, q_ref, kc_ref, kp_ref, vc_ref, vp_ref, o_ref):
        q4s = [_stack_heads(q_ref, kh, _ATTN_SCALE) for kh in range(nkv)]
        kks = [_kv_pair(kp_ref, kc_ref, kh) for kh in range(nkv)]
        w, _ = _attn_probs(q4s, kks, _sink_col(sink_ref, nkv * GQA_GROUP), bias_ref[...])
        wb = w.astype(BF16)
        for kh in range(nkv):
            o4 = _dot(wb[kh * GQA_GROUP * blk:(kh + 1) * GQA_GROUP * blk, :], _kv_pair(vp_ref, vc_ref, kh))
            for g in range(GQA_GROUP):
                hq = kh * GQA_GROUP + g
                o_ref[:, hq * HEAD_DIM:(hq + 1) * HEAD_DIM] = o4[g * blk:(g + 1) * blk, :].astype(BF16)

    cur = lambda n: (n, 0)
    prev = lambda n: (jnp.maximum(n - 1, 0), 0)
    return _pcall(
        body, name=name, out_shape=SDS((seqlen, d), BF16), grid=(nb,),
        in_specs=[pl.BlockSpec(memory_space=pltpu.SMEM), _bias_spec(), pl.BlockSpec((blk, d), cur),
                  pl.BlockSpec((blk, dkv), cur), pl.BlockSpec((blk, dkv), prev),
                  pl.BlockSpec((blk, dkv), cur), pl.BlockSpec((blk, dkv), prev)],
        out_specs=pl.BlockSpec((blk, d), cur), vmem=32 << 20)(sinks, _attn_bias(), q, k, k, v, v)


def _attn_bwd(q, k, v, do, sinks, name):
    seqlen, d = q.shape
    dkv = k.shape[1]
    nkv = dkv // HEAD_DIM
    nh = d // HEAD_DIM
    nb = seqlen // ATTN_BLOCK
    blk = ATTN_BLOCK

    def body(sink_ref, bias_ref, q_ref, kc_ref, kp_ref, vc_ref, vp_ref, do_ref, dq_ref, dk_ref, dv_ref, ds_ref,
             ck_ref, cv_ref):
        n = pl.program_id(0)

        @pl.when(n == 0)
        def _():
            ck_ref[...] = jnp.zeros_like(ck_ref)
            cv_ref[...] = jnp.zeros_like(cv_ref)
            ds_ref[...] = jnp.zeros_like(ds_ref)

        @pl.when(n == nb)
        def _():
            dk_ref[...] = ck_ref[...]
            dv_ref[...] = cv_ref[...]

        @pl.when(n < nb)
        def _():
            q4s = [_stack_heads(q_ref, kh, _ATTN_SCALE) for kh in range(nkv)]
            do4s = [_stack_heads(do_ref, kh) for kh in range(nkv)]
            kks = [_kv_pair(kp_ref, kc_ref, kh) for kh in range(nkv)]
            w, wsink = _attn_probs(q4s, kks, _sink_col(sink_ref, nh), bias_ref[...])
            dw = jnp.concatenate([_dot(do4s[kh], _kv_pair(vp_ref, vc_ref, kh), NT) for kh in range(nkv)], axis=0)
            dsum = jnp.sum(w * dw, axis=1, keepdims=True)
            ds_all = (w * (dw - dsum)).astype(BF16)
            wb = w.astype(BF16)
            dsk = -wsink * dsum
            for kh in range(nkv):
                sl = slice(kh * HEAD_DIM, (kh + 1) * HEAD_DIM)
                rows = slice(kh * GQA_GROUP * blk, (kh + 1) * GQA_GROUP * blk)
                ds = ds_all[rows, :]
                dq4 = _dot(ds, kks[kh]) * _ATTN_SCALE
                dkk = _dot(ds, q4s[kh], TN)
                dvv = _dot(wb[rows, :], do4s[kh], TN)
                for g in range(GQA_GROUP):
                    hq = kh * GQA_GROUP + g
                    dq_ref[:, hq * HEAD_DIM:(hq + 1) * HEAD_DIM] = dq4[g * blk:(g + 1) * blk, :]
                    ds_ref[hq:hq + 1, :] += jnp.sum(dsk[hq * blk:(hq + 1) * blk, :], axis=0, keepdims=True)
                dk_ref[:, sl] = ck_ref[:, sl] + dkk[:blk, :]
                dv_ref[:, sl] = cv_ref[:, sl] + dvv[:blk, :]
                ck_ref[:, sl] = dkk[blk:, :]
                cv_ref[:, sl] = dvv[blk:, :]

    cur = lambda n: (jnp.minimum(n, nb - 1), 0)
    prev = lambda n: (jnp.clip(n - 1, 0, nb - 1), 0)
    lag = lambda n: (jnp.maximum(n - 1, 0), 0)
    return _pcall(
        body, name=name,
        out_shape=[SDS((seqlen, d), F32), SDS((seqlen, dkv), F32), SDS((seqlen, dkv), F32), SDS((nh, 128), F32)],
        grid=(nb + 1,),
        in_specs=[pl.BlockSpec(memory_space=pltpu.SMEM), _bias_spec(), pl.BlockSpec((blk, d), cur),
                  pl.BlockSpec((blk, dkv), cur), pl.BlockSpec((blk, dkv), prev),
                  pl.BlockSpec((blk, dkv), cur), pl.BlockSpec((blk, dkv), prev), pl.BlockSpec((blk, d), cur)],
        out_specs=[pl.BlockSpec((blk, d), cur), pl.BlockSpec((blk, dkv), lag), pl.BlockSpec((blk, dkv), lag),
                   pl.BlockSpec((nh, 128), lambda n: (0, 0))],
        scratch=[pltpu.VMEM((blk, dkv), F32)] * 2, vmem=32 << 20)(sinks, _attn_bias(), q, k, k, v, v, do)


def _ssm_mats(lre, lim, ldt, btr, bti, cr, ci):
    dt = jnp.exp(ldt)
    mag = jnp.exp(lre * dt)
    ar = mag * jnp.cos(lim * dt)
    ai = mag * jnp.sin(lim * dt)
    den = lre * lre + lim * lim
    nr = ar - 1.0
    cfr = (nr * lre + ai * lim) / den
    cfi = (ai * lre - nr * lim) / den
    bbr = cfr * btr - cfi * bti
    bbi = cfr * bti + cfi * btr
    pr = [jnp.ones_like(ar)]
    pi = [jnp.zeros_like(ai)]
    for _ in range(SSM_T):
        pr.append(pr[-1] * ar - pi[-1] * ai)
        pi.append(pr[-2] * ai + pi[-1] * ar)
    last = SSM_T - 1
    p_re = jnp.concatenate([pr[last - s] * bbr - pi[last - s] * bbi for s in range(SSM_T)], axis=0)
    p_im = jnp.concatenate([pr[last - s] * bbi + pi[last - s] * bbr for s in range(SSM_T)], axis=0)
    qt_re = jnp.concatenate([pr[t + 1] * cr - pi[t + 1] * ci for t in range(SSM_T)], axis=0)
    qt_im = jnp.concatenate([-(pr[t + 1] * ci + pi[t + 1] * cr) for t in range(SSM_T)], axis=0)
    ctr = jnp.concatenate([cr] * SSM_T, axis=0)
    cti = jnp.concatenate([ci] * SSM_T, axis=0)
    lag = (lax.broadcasted_iota(jnp.int32, (SSM_W, SSM_W), 1) // SSM_GROUP
           - lax.broadcasted_iota(jnp.int32, (SSM_W, SSM_W), 0) // SSM_GROUP)
    m = jnp.zeros((SSM_W, SSM_W), F32)
    for l in range(SSM_T):
        zr = jnp.concatenate([pr[l] * bbr - pi[l] * bbi] * SSM_T, axis=0)
        zi = jnp.concatenate([pr[l] * bbi + pi[l] * bbr] * SSM_T, axis=0)
        kl = _dot3_nt(zr, ctr) - _dot3_nt(zi, cti)
        m = m + jnp.where(lag == l, kl, 0.0)
    return m, p_re, p_im, qt_re, qt_im, pr[SSM_T], pi[SSM_T]


_SSM_GB = 8


def _ssm_param_specs(ng):
    n, hh = SSM_STATE, SSM_GROUP
    gb = _tile(ng, _SSM_GB)
    row = pl.BlockSpec((gb, 1, n), lambda i: (i, 0, 0))
    one = pl.BlockSpec((gb, 1, 1), lambda i: (i, 0, 0))
    mat = pl.BlockSpec((gb, hh, n), lambda i: (i, 0, 0))
    big = pl.BlockSpec((gb, SSM_W, SSM_W), lambda i: (i, 0, 0))
    half = pl.BlockSpec((gb, SSM_W, n), lambda i: (i, 0, 0))
    return gb, row, one, mat, big, half


def _ssm_prep(params, name):
    ng = params[0].shape[0]
    n = SSM_STATE
    gb, row, one, mat, big, half = _ssm_param_specs(ng)

    def body(lre, lim, ldt, btr, bti, cr, ci, m_ref, pre_ref, pim_ref, qre_ref, qim_ref, atr_ref, ati_ref):
        for gi in range(gb):
            outs = _ssm_mats(lre[gi], lim[gi], ldt[gi], btr[gi], bti[gi], cr[gi], ci[gi])
            for ref, val in zip((m_ref, pre_ref, pim_ref, qre_ref, qim_ref, atr_ref, ati_ref), outs):
                ref[gi] = val

    return _pcall(
        body, name=name,
        out_shape=[SDS((ng, SSM_W, SSM_W), F32)] + [SDS((ng, SSM_W, n), F32)] * 4 + [SDS((ng, 1, n), F32)] * 2,
        grid=(ng // gb,), in_specs=[row, row, one, mat, mat, mat, mat],
        out_specs=[big, half, half, half, half, row, row], vmem=40 << 20)(*params)


def _ssm_prep_vjp(params, cots, name):
    ng = params[0].shape[0]
    n, hh = SSM_STATE, SSM_GROUP
    gb, row, one, mat, big, half = _ssm_param_specs(ng)

    def body(lre, lim, ldt, btr, bti, cr, ci, dm, dpre, dpim, dqre, dqim, datr, dati,
             o_lre, o_lim, o_ldt, o_btr, o_bti, o_cr, o_ci):
        for gi in range(gb):
            prm = (lre[gi], lim[gi], ldt[gi], btr[gi], bti[gi], cr[gi], ci[gi])
            _, pull = jax.vjp(_ssm_mats, *prm)
            grads = pull((dm[gi], dpre[gi], dpim[gi], dqre[gi], dqim[gi], datr[gi], dati[gi]))
            for ref, val in zip((o_lre, o_lim, o_ldt, o_btr, o_bti, o_cr, o_ci), grads):
                ref[gi] = val

    return _pcall(
        body, name=name,
        out_shape=[SDS((ng, 1, n), F32)] * 2 + [SDS((ng, 1, 1), F32)] + [SDS((ng, hh, n), F32)] * 4,
        grid=(ng // gb,), in_specs=[row, row, one, mat, mat, mat, mat, big, half, half, half, half, row, row],
        out_specs=[row, row, one, mat, mat, mat, mat], vmem=48 << 20)(*params, *cots)


_SSM_GT = SSM_W // SSM_GROUP


def _blk_transpose(xs):
    assert len(xs) == SSM_T == _SSM_GT
    blk = lax.broadcasted_iota(jnp.int32, xs[0].shape, 1) // SSM_GROUP
    xs = list(xs)
    k = SSM_T // 2
    while k:
        high = (blk // k) % 2 == 1
        nxt = []
        for i in range(SSM_T):
            if i & k:
                nxt.append(jnp.where(high, xs[i], pltpu.roll(xs[i ^ k], SSM_W - SSM_GROUP * k, 1)))
            else:
                nxt.append(jnp.where(high, pltpu.roll(xs[i ^ k], SSM_GROUP * k, 1), xs[i]))
        xs = nxt
        k //= 2
    return xs


def _tile_groups(x_ref, ncb):
    return _blk_transpose([x_ref[pl.ds(t, ncb, stride=SSM_T), :] for t in range(SSM_T)])


def _groups_tile(ys, o_ref, ncb):
    for t, y in enumerate(_blk_transpose(ys)):
        o_ref[pl.ds(t, ncb, stride=SSM_T), :] = y


def _ssm_specs(seqlen, d):
    ncb = _tile(seqlen // SSM_T, 512)
    grid = (d // SSM_W, seqlen // (SSM_T * ncb))
    act = pl.BlockSpec((SSM_T * ncb, SSM_W), lambda j, r: (r, j))
    state = pl.BlockSpec((ncb, _SSM_GT * SSM_W), lambda j, r: (r, j))
    mats = pl.BlockSpec((_SSM_GT, SSM_W, SSM_W), lambda j, r: (j, 0, 0))
    return ncb, grid, act, state, mats


def _gsl(gl):
    return slice(gl * SSM_W, (gl + 1) * SSM_W)


def _ssm_state_in(hn, pmat, name):
    seqlen, d = hn.shape
    ncb, grid, act, state, mats = _ssm_specs(seqlen, d)

    def body(x_ref, p_ref, s_ref):
        us = _tile_groups(x_ref, ncb)
        for gl in range(_SSM_GT):
            s_ref[:, _gsl(gl)] = _dot3(us[gl], p_ref[gl], NN)

    return _pcall(body, name=name, out_shape=SDS((seqlen // SSM_T, d * SSM_T), F32), grid=grid,
                  in_specs=[act, mats], out_specs=state, vmem=40 << 20)(hn, pmat)


def _ssm_out(hn, xp, mmat, qt, name, comm=None):
    seqlen, d = hn.shape
    ncb, grid, act, state, mats = _ssm_specs(seqlen, d)

    def body(x_ref, xp_ref, m_ref, q_ref, y_ref):
        us = _tile_groups(x_ref, ncb)
        ys = [_dot3(us[gl], m_ref[gl], NN) + _dot3(xp_ref[:, _gsl(gl)], q_ref[gl], NT)
              for gl in range(_SSM_GT)]
        _groups_tile(ys, y_ref, ncb)

    return _pcall(body, name=name, out_shape=[SDS((seqlen, d), F32)], grid=grid,
                  in_specs=[act, state, mats, mats], out_specs=[act], vmem=40 << 20, comm=comm)(hn, xp, mmat, qt)


def _ssm_dstate(dy, qt, name):
    seqlen, d = dy.shape
    ncb, grid, act, state, mats = _ssm_specs(seqlen, d)

    def body(dy_ref, q_ref, o_ref):
        dys = _tile_groups(dy_ref, ncb)
        for gl in range(_SSM_GT):
            o_ref[:, _gsl(gl)] = _dot3(dys[gl], q_ref[gl], NN)

    return _pcall(body, name=name, out_shape=SDS((seqlen // SSM_T, d * SSM_T), F32), grid=grid,
                  in_specs=[act, mats], out_specs=state, vmem=40 << 20)(dy, qt)


def _ssm_bwd(hn, dy, xp, gs, mmat, pmat, name, comm=None):
    seqlen, d = hn.shape
    ng = d // SSM_GROUP
    ncb, grid, act, state, mats = _ssm_specs(seqlen, d)

    def body(x_ref, dy_ref, xp_ref, g_ref, m_ref, p_ref, du_ref, dm_ref, dp_ref, dq_ref, da_ref):
        @pl.when(pl.program_id(1) == 0)
        def _():
            for ref in (dm_ref, dp_ref, dq_ref, da_ref):
                ref[...] = jnp.zeros_like(ref)
        us = _tile_groups(x_ref, ncb)
        dys = _tile_groups(dy_ref, ncb)
        dus = []
        for gl in range(_SSM_GT):
            xv, gv = xp_ref[:, _gsl(gl)], g_ref[:, _gsl(gl)]
            u2, dy2, x2, g2 = _split(us[gl]), _split(dys[gl]), _split(xv), _split(gv)
            dus.append(_dot3(dy2, m_ref[gl], NT) + _dot3(g2, p_ref[gl], NT))
            dm_ref[gl] += _dot3(u2, dy2, TN)
            dp_ref[gl] += _dot3(u2, g2, TN)
            dq_ref[gl] += _dot3(dy2, x2, TN)
            da_ref[gl, 0:1, :] += jnp.sum(xv * gv, axis=0, keepdims=True)
            da_ref[gl, 1:2, :] += jnp.sum(xv * pltpu.roll(gv, SSM_STATE, 1), axis=0, keepdims=True)
        _groups_tile(dus, du_ref, ncb)

    return _pcall(
        body, name=name,
        out_shape=[SDS((seqlen, d), F32)] + [SDS((ng, SSM_W, SSM_W), F32)] * 3 + [SDS((ng, 2, SSM_W), F32)],
        grid=grid, in_specs=[act, act, state, state, mats, mats],
        out_specs=[act, mats, mats, mats, pl.BlockSpec((_SSM_GT, 2, SSM_W), lambda j, r: (j, 0, 0))],
        vmem=48 << 20, comm=comm)(hn, dy, xp, gs, mmat, pmat)


def _ssm_carry(s, a1, a2, reverse, name):
    nc, w = s.shape
    nseg = 8
    seg = nc // nseg
    wb = SSM_W
    shape = (nseg, wb)

    def body(s_ref, a1_ref, a2_ref, o_ref):
        first = lax.broadcasted_iota(jnp.int32, shape, 1) % SSM_W < SSM_STATE
        row = lax.broadcasted_iota(jnp.int32, shape, 0)
        sign = jnp.where(first, -1.0, 1.0)
        a1v = jnp.broadcast_to(a1_ref[...], shape)
        a2v = jnp.broadcast_to(a2_ref[...], shape)
        zero = jnp.zeros(shape, F32)

        def swap(v):
            return pltpu.roll(v, SSM_STATE, 1)

        def rows_of(j):
            return pl.ds((seg - 1 - j) if reverse else j, nseg, stride=seg)

        def scan(j, carry):
            x, xs = carry
            v = s_ref[rows_of(j), :]
            o_ref[rows_of(j), :] = x
            return a1v * x + a2v * xs + v, a1v * xs - a2v * x + swap(v)

        ends, _ = lax.fori_loop(0, seg, scan, (zero, zero), unroll=8)

        ar, ai = a1v, a2v * sign
        pr, pi, br, bi, n = zero + 1.0, zero, ar, ai, seg
        while n:
            if n & 1:
                pr, pi = pr * br - pi * bi, pr * bi + pi * br
            br, bi, n = br * br - bi * bi, 2.0 * br * bi, n >> 1
        start, t = zero, zero
        for k in (range(nseg - 1, -1, -1) if reverse else range(nseg)):
            start = jnp.where(row == k, t, start)
            t = pr * t + (sign * pi) * swap(t) + jnp.broadcast_to(ends[k:k + 1, :], shape)
        start_rot = sign * swap(start)

        def fix(j, carry):
            qr, qi = carry
            o_ref[rows_of(j), :] = o_ref[rows_of(j), :] + qr * start + qi * start_rot
            return qr * ar - qi * ai, qr * ai + qi * ar

        lax.fori_loop(0, seg, fix, (zero + 1.0, zero), unroll=8)

    blk = pl.BlockSpec((nc, wb), lambda i: (0, i))
    cst = pl.BlockSpec((1, wb), lambda i: (0, i))
    return _pcall(body, name=name, out_shape=SDS((nc, w), F32), grid=(w // wb,), in_specs=[blk, cst, cst],
                  out_specs=blk, vmem=_vmem_limit([((nc, wb), F32, 4)], extra=8 << 20))(s, a1, a2)


def _ssm_rows(atr, ati, conj):
    ng = atr.shape[0]
    ai = -ati if conj else ati
    a1 = jnp.concatenate([atr, atr], axis=2).reshape(1, ng * SSM_W)
    a2 = jnp.concatenate([-ai, ai], axis=2).reshape(1, ng * SSM_W)
    return a1, a2


def _peers():
    x, y, c = (lax.axis_index(a) for a in AXES)
    me = 4 * x + 2 * y + c
    peers = []
    for dx, dy, dc in [(0, 0, 1), (0, 1, 0), (0, 1, 1), (1, 0, 0), (1, 0, 1), (1, 1, 0), (1, 1, 1)]:
        px, py, pc = (1 - x) if dx else x, (1 - y) if dy else y, (1 - c) if dc else c
        peers.append(((px, py, pc), 4 * px + 2 * py + pc))
    return me, peers


class _Exchange:
    def __init__(self, arrs, scatter, layers=None):
        self.arrs = list(arrs)
        self.scatter = scatter
        self.layers = list(layers) if layers is not None else [None] * len(self.arrs)

    def out_shape(self):
        shapes = []
        for arr, layer in zip(self.arrs, self.layers):
            block = arr.shape[1:] if (self.scatter or layer is not None) else arr.shape
            shapes.append(SDS((NDEV,) + tuple(block), arr.dtype))
        return shapes

    def semaphores(self):
        n = len(self.arrs)
        return [pltpu.SemaphoreType.DMA((n * (NDEV - 1),)), pltpu.SemaphoreType.DMA((n * (NDEV - 1),)),
                pltpu.SemaphoreType.DMA((n,))]

    def _src(self, ref, a, block):
        if self.scatter:
            return ref.at[block]
        return ref if self.layers[a] is None else ref.at[self.layers[a]]

    def _remote(self, xin, xout, sems, a, k, peer, landing):
        pid, pidx = peer
        slot = a * (NDEV - 1) + k
        return pltpu.make_async_remote_copy(
            src_ref=self._src(xin[a], a, pidx), dst_ref=xout[a].at[landing],
            send_sem=sems[0].at[slot], recv_sem=sems[1].at[slot], device_id=pid, device_id_type=MESH)

    def _local(self, xin, xout, sems, a, me):
        return pltpu.make_async_copy(self._src(xin[a], a, me), xout[a].at[me], sems[2].at[a])

    def start(self, xin, xout, sems):
        me, peers = _peers()
        for a in range(len(self.arrs)):
            self._local(xin, xout, sems, a, me).start()
        for k, peer in enumerate(peers):
            for a in range(len(self.arrs)):
                self._remote(xin, xout, sems, a, k, peer, me).start()

    def wait(self, xin, xout, sems):
        me, peers = _peers()
        for a in range(len(self.arrs)):
            self._local(xin, xout, sems, a, me).wait()
        for k, peer in enumerate(peers):
            for a in range(len(self.arrs)):
                cp = self._remote(xin, xout, sems, a, k, peer, peer[1])
                cp.wait_send()
                cp.wait_recv()


def _exchange(arrs, scatter, name, layers=None):
    comm = _Exchange(arrs, scatter, layers)
    n = len(comm.arrs)

    def body(*refs):
        xin, xout, sems = refs[:n], refs[n:2 * n], refs[2 * n:]
        comm.start(xin, xout, sems)
        comm.wait(xin, xout, sems)

    hbm = pl.BlockSpec(memory_space=pl.ANY)
    return pl.pallas_call(
        body, out_shape=comm.out_shape(), in_specs=[hbm] * n, out_specs=[hbm] * n,
        scratch_shapes=comm.semaphores(), name=name, interpret=False)(*comm.arrs)


def _adamw(parts, w, m, v, name):
    rows, cols = w.shape
    tr = _tile(rows, max(8, (1 << 17) // cols))
    c1 = 1.0 - ADAM_B1 ** ADAM_STEP
    c2 = 1.0 - ADAM_B2 ** ADAM_STEP

    def body(p_ref, w_ref, m_ref, v_ref, g_ref, d_ref, nm_ref, nv_ref):
        g = p_ref[0]
        for j in range(1, NDEV):
            g = g + p_ref[j]
        mm = ADAM_B1 * m_ref[...] + (1.0 - ADAM_B1) * g
        vv = ADAM_B2 * v_ref[...] + (1.0 - ADAM_B2) * (g * g)
        g_ref[...] = g
        nm_ref[...] = mm
        nv_ref[...] = vv
        d_ref[...] = -ADAM_LR * ((mm / c1) / (jnp.sqrt(vv / c2) + ADAM_EPS) + ADAM_WD * w_ref[...])

    spec = pl.BlockSpec((tr, cols), lambda i: (i, 0))
    return _pcall(
        body, name=name, out_shape=[SDS((rows, cols), F32)] * 4, grid=(rows // tr,),
        in_specs=[pl.BlockSpec((NDEV, tr, cols), lambda i: (0, i, 0)), spec, spec, spec], out_specs=[spec] * 4,
        vmem=_vmem_limit([((NDEV + 7, tr, cols), F32, 2)]))(parts, w, m, v)


def kernel(x, p, norm_mix, ssm_lambda_re, ssm_lambda_im, ssm_log_dt, ssm_b_re, ssm_b_im, ssm_c_re, ssm_c_im, ssm_d, ssm_w_glu, kv_norm, w_k, w_v, w_q, attn_sinks, w_o, norm_mlp, w_up, w_down, norm_ple, w_ple_gate, w_ple_proj, norm_final, loss_target, m_norm_mix, m_ssm_lambda_re, m_ssm_lambda_im, m_ssm_log_dt, m_ssm_b_re, m_ssm_b_im, m_ssm_c_re, m_ssm_c_im, m_ssm_d, m_ssm_w_glu, m_kv_norm, m_w_k, m_w_v, m_w_q, m_attn_sinks, m_w_o, m_norm_mlp, m_w_up, m_w_down, m_norm_ple, m_w_ple_gate, m_w_ple_proj, m_norm_final, v_norm_mix, v_ssm_lambda_re, v_ssm_lambda_im, v_ssm_log_dt, v_ssm_b_re, v_ssm_b_im, v_ssm_c_re, v_ssm_c_im, v_ssm_d, v_ssm_w_glu, v_kv_norm, v_w_k, v_w_v, v_w_q, v_attn_sinks, v_w_o, v_norm_mlp, v_w_up, v_w_down, v_norm_ple, v_w_ple_gate, v_w_ple_proj, v_norm_final):
    names = ['norm_mix', 'ssm_lambda_re', 'ssm_lambda_im', 'ssm_log_dt', 'ssm_b_re', 'ssm_b_im', 'ssm_c_re',
             'ssm_c_im', 'ssm_d', 'ssm_w_glu', 'kv_norm', 'w_k', 'w_v', 'w_q', 'attn_sinks', 'w_o', 'norm_mlp',
             'w_up', 'w_down', 'norm_ple', 'w_ple_gate', 'w_ple_proj', 'norm_final']
    weights = dict(zip(names, (norm_mix, ssm_lambda_re, ssm_lambda_im, ssm_log_dt, ssm_b_re, ssm_b_im, ssm_c_re,
                               ssm_c_im, ssm_d, ssm_w_glu, kv_norm, w_k, w_v, w_q, attn_sinks, w_o, norm_mlp,
                               w_up, w_down, norm_ple, w_ple_gate, w_ple_proj, norm_final)))
    mom1 = dict(zip(names, (m_norm_mix, m_ssm_lambda_re, m_ssm_lambda_im, m_ssm_log_dt, m_ssm_b_re, m_ssm_b_im,
                            m_ssm_c_re, m_ssm_c_im, m_ssm_d, m_ssm_w_glu, m_kv_norm, m_w_k, m_w_v, m_w_q,
                            m_attn_sinks, m_w_o, m_norm_mlp, m_w_up, m_w_down, m_norm_ple, m_w_ple_gate,
                            m_w_ple_proj, m_norm_final)))
    mom2 = dict(zip(names, (v_norm_mix, v_ssm_lambda_re, v_ssm_lambda_im, v_ssm_log_dt, v_ssm_b_re, v_ssm_b_im,
                            v_ssm_c_re, v_ssm_c_im, v_ssm_d, v_ssm_w_glu, v_kv_norm, v_w_k, v_w_v, v_w_q,
                            v_attn_sinks, v_w_o, v_norm_mlp, v_w_up, v_w_down, v_norm_ple, v_w_ple_gate,
                            v_w_ple_proj, v_norm_final)))

    seqlen, d = x.shape[1], x.shape[2]
    depth = w_up.shape[0]
    n_ssm = ssm_w_glu.shape[0]
    n_att = w_q.shape[0]
    ng = d // SSM_GROUP
    nh = d // HEAD_DIM
    h0 = x[0]
    tgt = loss_target[0]
    tabs = _rope_tables(seqlen)

    sharded = ['w_up', 'w_down', 'w_ple_gate', 'w_ple_proj', 'ssm_w_glu', 'w_q', 'w_o', 'w_k', 'w_v']
    shards = {k: weights[k].astype(BF16) for k in sharded}
    shards['ssm_d'] = ssm_d
    dkv = w_k.shape[1]

    def layer_set(i):
        keys = [('w_up', i), ('w_down', i), ('w_ple_gate', i), ('w_ple_proj', i)]
        keys += [('ssm_w_glu', i), ('ssm_d', i)] if i < n_ssm else [('w_q', i - n_ssm), ('w_o', i - n_ssm)]
        if i == n_ssm:
            keys += [('w_k', None), ('w_v', None)]
        return keys

    def gather_of(i, only=None):
        keys = [kl for kl in layer_set(i) if only is None or kl[0] in only]
        return keys, _Exchange([shards[k] for k, _ in keys], False, [l for _, l in keys])

    def as_operands(keys, blocks):
        w = {}
        for (k, _), g in zip(keys, blocks):
            if k in ('w_ple_gate', 'w_q', 'w_o', 'w_k', 'w_v', 'w_down'):
                g = g.reshape(NDEV * g.shape[1], g.shape[2])
            elif k in ('w_ple_proj', 'ssm_w_glu', 'w_up'):
                g = g.transpose(1, 0, 2).reshape(g.shape[1], NDEV * g.shape[2])
            elif k == 'ssm_d':
                g = g.reshape(d)
            w[k] = g
        return w

    lw = {}

    def ssm_params(i):
        n = SSM_STATE
        return (ssm_lambda_re[i].reshape(ng, 1, n), ssm_lambda_im[i].reshape(ng, 1, n),
                ssm_log_dt[i].reshape(ng, 1, 1), jnp.swapaxes(ssm_b_re[i], 1, 2), jnp.swapaxes(ssm_b_im[i], 1, 2),
                ssm_c_re[i], ssm_c_im[i])

    h = h0
    h_in, h_a, h_b, acts = [], [], [], []
    ssm_saved, att_saved = {}, {}
    k_sh = v_sh = None
    for i in range(depth):
        h_in.append(h)
        if i < n_ssm:
            first = gather_of(0, ('ssm_w_glu', 'ssm_d')) if i == 0 else None
            hn, *got = _norm_fwd(h, norm_mix[i], f"norm_mix_fwd{i}", comm=first and first[1])
            if first:
                lw[0] = as_operands(first[0], got)
            mats = _ssm_prep(ssm_params(i), f"ssm_prep{i}")
            mmat, atr, ati = mats[0], mats[5], mats[6]
            pmat = jnp.concatenate([mats[1], mats[2]], axis=2)
            qt = jnp.concatenate([mats[3], mats[4]], axis=2)
            s_in = _ssm_state_in(hn, pmat, f"ssm_state_in{i}")
            xp = _ssm_carry(s_in, *_ssm_rows(atr, ati, False), False, f"ssm_carry_fwd{i}")
            rest = gather_of(0, ('w_up', 'w_down', 'w_ple_gate', 'w_ple_proj')) if i == 0 else None
            y, *got = _ssm_out(hn, xp, mmat, qt, f"ssm_out{i}", comm=rest and rest[1])
            if rest:
                lw[0].update(as_operands(rest[0], got))
            ha = _glu_fwd(y, hn, h, lw[i]['ssm_d'], lw[i]['ssm_w_glu'], f"glu_fwd{i}")
            ssm_saved[i] = (hn, mmat, pmat, qt, atr, ati, xp, y)
        else:
            j = i - n_ssm
            q = _q_fwd(h, norm_mix[i], lw[i]['w_q'], tabs, f"q_fwd{j}")
            o = _attn_fwd(q, k_sh, v_sh, attn_sinks[j], f"attn_fwd{j}")
            ha = _lin_res(h, o, lw[i]['w_o'], f"attn_out{j}")
            att_saved[j] = (q, o)
        h_a.append(ha)
        nxt = gather_of(i + 1) if i + 1 < depth else None
        res = _mlp_fwd(ha, norm_mlp[i], lw[i]['w_up'], lw[i]['w_down'], f"mlp_fwd{i}", comm=nxt and nxt[1])
        hb = res[0]
        acts.append(res[1])
        if nxt:
            lw[i + 1] = as_operands(nxt[0], res[2:])
        h_b.append(hb)
        h = _ple_fwd(hb, p[i, 0], norm_ple[i], lw[i]['w_ple_gate'], lw[i]['w_ple_proj'], f"ple_fwd{i}")
        if i == n_ssm - 1:
            k_sh, v_sh = _kv_fwd(h, kv_norm, lw[n_ssm]['w_k'], lw[n_ssm]['w_v'], tabs, "kv_fwd")
    h_kv = h_in[n_ssm] if n_ssm < depth else h
    dh, loss_row, g_norm_final = _loss_bwd(h, norm_final, tgt, "loss_bwd")
    loss = lax.psum(loss_row[0, 0], AXES)

    g_norm_mix, g_norm_mlp, g_norm_ple = [None] * depth, [None] * depth, [None] * depth
    g_ssm, g_sinks = [None] * n_ssm, [None] * n_att
    g_kv_norm = None
    dks, dvs = [], []
    recv = {}
    pending = None
    for i in range(depth - 1, -1, -1):
        gl = {}
        wk, wv = lw[n_ssm]['w_k'], lw[n_ssm]['w_v']
        if i == n_ssm - 1:
            dh, dkp, dvb, hkb, g_kv_norm = _kv_bwd(dks, dvs, h_kv, kv_norm, dh, wk, wv, tabs, "kv_bwd")
            gl['w_k', None] = _atb(hkb, dkp, False, "grad_w_k")
            gl['w_v', None] = _atb(hkb, dvb, False, "grad_w_v")
        dhb, dz, nb16, dpp, g_norm_ple[i] = _ple_bwd(h_b[i], p[i, 0], dh, norm_ple[i], lw[i]['w_ple_gate'],
                                                     lw[i]['w_ple_proj'], f"ple_bwd{i}")
        gl['w_ple_gate', i] = _atb(nb16, dz, False, f"grad_w_ple_gate{i}")
        gl['w_ple_proj', i] = _atb(p[i, 0], dpp, True, f"grad_w_ple_proj{i}")
        res = _mlp_bwd(h_a[i], acts[i], dhb, norm_mlp[i], lw[i]['w_up'], lw[i]['w_down'], f"mlp_bwd{i}",
                       comm=pending and pending[1])
        dha, hmb, da, g_norm_mlp[i] = res[:4]
        if pending:
            recv.update(zip(pending[0], res[4:]))
            pending = None
        gl['w_up', i] = _atb(hmb, da, True, f"grad_w_up{i}")
        gl['w_down', i] = _atb(acts[i], dhb, False, f"grad_w_down{i}")
        if i >= n_ssm:
            j = i - n_ssm
            q, o = att_saved[j]
            do = _lin_nt(dha, lw[i]['w_o'], f"attn_out_bwd{j}")
            gl['w_o', j] = _atb(o, dha, False, f"grad_w_o{j}")
            dq, dk_j, dv_j, dsink = _attn_bwd(q, k_sh, v_sh, do, attn_sinks[j], f"attn_bwd{j}")
            dks.append(dk_j)
            dvs.append(dv_j)
            g_sinks[j] = dsink[:, 0]
            dh, dqp, hnb, g_norm_mix[i] = _q_bwd(dq, h_in[i], norm_mix[i], dha, lw[i]['w_q'], tabs, f"q_bwd{j}")
            gl['w_q', j] = _atb(hnb, dqp, False, f"grad_w_q{j}")
        else:
            hn, mmat, pmat, qt, atr, ati, xp, y = ssm_saved[i]
            dyy, dhn_d, geb, dab, g_dskip = _glu_bwd(y, hn, dha, lw[i]['ssm_d'], lw[i]['ssm_w_glu'], f"glu_bwd{i}")
            gl['ssm_d', i] = g_dskip.reshape(NDEV, d // NDEV)
            gl['ssm_w_glu', i] = _atb(geb, dab, True, f"grad_ssm_w_glu{i}")
            dxp = _ssm_dstate(dyy, qt, f"ssm_dstate{i}")
            gs = _ssm_carry(dxp, *_ssm_rows(atr, ati, True), True, f"ssm_carry_bwd{i}")
            last = (list(gl), _Exchange(list(gl.values()), True)) if i == 0 else None
            du, dm, dp, dqt, da_raw, *got = _ssm_bwd(hn, dyy, xp, gs, mmat, pmat, f"ssm_bwd{i}",
                                                     comm=last and last[1])
            if last:
                recv.update(zip(last[0], got))
                gl = {}
            n = SSM_STATE
            cots = (dm, dp[:, :, :n], dp[:, :, n:], dqt[:, :, :n], dqt[:, :, n:],
                    (da_raw[:, 0:1, :n] + da_raw[:, 0:1, n:]), (da_raw[:, 1:2, :n] - da_raw[:, 1:2, n:]))
            g_ssm[i] = _ssm_prep_vjp(ssm_params(i), cots, f"ssm_prep_vjp{i}")
            dh, g_norm_mix[i] = _norm_bwd(h_in[i], norm_mix[i], dhn_d, du, dha, f"norm_mix_bwd{i}")
        if gl:
            pending = (list(gl), _Exchange(list(gl.values()), True))
    grad_x = dh[None]
    if pending:
        recv.update(zip(pending[0], _exchange(pending[1].arrs, True, "scatter_grads0")))

    out_g, out_d, out_m, out_v = {}, {}, {}, {}
    updated = {}
    for (k, l), parts in recv.items():
        pick = (lambda t: t) if l is None else (lambda t: t[l])
        shp = pick(weights[k]).shape
        r2 = (math.prod(shp[:-1]), shp[-1])
        res = _adamw(parts.reshape((NDEV,) + r2), pick(weights[k]).reshape(r2), pick(mom1[k]).reshape(r2),
                     pick(mom2[k]).reshape(r2), f"adamw_{k}{'' if l is None else l}")
        updated.setdefault(k, {})[l] = [t.reshape(shp) for t in res]
    for k, by_layer in updated.items():
        for n, dst in enumerate((out_g, out_d, out_m, out_v)):
            dst[k] = by_layer[None][n] if None in by_layer else jnp.stack([by_layer[l][n] for l in sorted(by_layer)])

    def ssm_grad(idx, unswap=False):
        g = jnp.stack([g_ssm[i][idx] for i in range(n_ssm)])
        return jnp.swapaxes(g, 2, 3) if unswap else g

    small = {'norm_mix': jnp.concatenate(g_norm_mix, axis=0),
             'ssm_lambda_re': ssm_grad(0), 'ssm_lambda_im': ssm_grad(1), 'ssm_log_dt': ssm_grad(2),
             'ssm_b_re': ssm_grad(3, True), 'ssm_b_im': ssm_grad(4, True),
             'ssm_c_re': ssm_grad(5), 'ssm_c_im': ssm_grad(6),
             'kv_norm': g_kv_norm, 'attn_sinks': jnp.stack(g_sinks),
             'norm_mlp': jnp.concatenate(g_norm_mlp, axis=0), 'norm_ple': jnp.concatenate(g_norm_ple, axis=0),
             'norm_final': g_norm_final}
    snames = list(small)
    sizes = [weights[k].size for k in snames]
    total = sum(sizes)
    lanes = 128
    padded = -(-total // (512 * lanes)) * (512 * lanes)

    def flat(parts):
        v = jnp.concatenate([t.reshape(-1) for t in parts] + [jnp.zeros((padded - total,), F32)])
        return v.reshape(padded // lanes, lanes)

    parts = _exchange([flat([small[k] for k in snames])], False, "gather_small_grads")[0]
    res = _adamw(parts, flat([weights[k] for k in snames]), flat([mom1[k] for k in snames]),
                 flat([mom2[k] for k in snames]), "adamw_small")
    off = 0
    for k, sz in zip(snames, sizes):
        for dst, t in zip((out_g, out_d, out_m, out_v), res):
            dst[k] = t.reshape(-1)[off:off + sz].reshape(weights[k].shape)
        off += sz

    return (loss, grad_x, *[out_g[k] for k in names], *[out_d[k] for k in names],
            *[out_m[k] for k in names], *[out_v[k] for k in names])
```

```python
import functools
import math

import jax
import jax.numpy as jnp
from jax import lax
from jax.experimental import pallas as pl
from jax.experimental.pallas import tpu as pltpu

F32 = jnp.float32
BF16 = jnp.bfloat16
SDS = jax.ShapeDtypeStruct
MESH = pl.DeviceIdType.MESH
AXES = ("x", "y", "c")
NDEV = 8

RMS_EPS = 1e-6
SSM_GROUP = 16
SSM_STATE = 64
SSM_T = 8
SSM_W = SSM_T * SSM_GROUP
HEAD_DIM = 64
GQA_GROUP = 4
ATTN_BLOCK = 128
ROT_DIM = 16
ROPE_THETA = 500000.0
NEG_INF = -1e30
ADAM_LR, ADAM_B1, ADAM_B2, ADAM_EPS, ADAM_WD, ADAM_STEP = 0.001, 0.9, 0.999, 1e-08, 0.01, 10

VMEM_CAP = 56 * 1024 * 1024
HI = lax.Precision.HIGHEST

NN = ((1,), (0,))
NT = ((1,), (1,))
TN = ((0,), (0,))


def _dot(a, b, dims=NN, precision=None):
    return lax.dot_general(a, b, (dims, ((), ())), preferred_element_type=F32, precision=precision)


def _split(a):
    if isinstance(a, tuple):
        return a
    hi = a.astype(BF16)
    return hi, (a - hi.astype(F32)).astype(BF16)


def _dot3(a, b, dims=NN):
    (ah, al), (bh, bl) = _split(a), _split(b)
    return _dot(ah, bh, dims) + (_dot(ah, bl, dims) + _dot(al, bh, dims))


@jax.custom_vjp
def _dot3_nt(a, b):
    return _dot3(a, b, NT)


def _dot3_nt_fwd(a, b):
    return _dot3(a, b, NT), (a, b)


def _dot3_nt_bwd(res, g):
    a, b = res
    return _dot3(g, b, NN), _dot3(g, a, TN)


_dot3_nt.defvjp(_dot3_nt_fwd, _dot3_nt_bwd)


def _tile(n, pref):
    t = min(n, pref)
    while n % t:
        t //= 2
    return t


def _nbytes(shape, dtype):
    return math.prod(s for s in shape if s is not None) * jnp.dtype(dtype).itemsize


def _vmem_limit(blocks, extra=0):
    need = sum(_nbytes(s, d) * n for s, d, n in blocks) + extra + (4 << 20)
    return int(min(VMEM_CAP, max(need, 16 << 20)))


def _pcall(body, *, name, out_shape, grid, in_specs, out_specs, scratch=(), vmem=None, comm=None):
    single = not isinstance(out_shape, (list, tuple))
    out_shape = [out_shape] if single else list(out_shape)
    out_specs = [out_specs] if single else list(out_specs)
    in_specs, scratch = list(in_specs), list(scratch)
    if comm is not None:
        n_in, n_out, n_scr, nx = len(in_specs), len(out_specs), len(scratch), len(comm.arrs)
        hbm = pl.BlockSpec(memory_space=pl.ANY)
        in_specs = in_specs + [hbm] * nx
        out_specs = out_specs + [hbm] * nx
        out_shape = out_shape + comm.out_shape()
        scratch = scratch + comm.semaphores()
        inner = body

        def body(*refs):
            ins, xin, rest = refs[:n_in], refs[n_in:n_in + nx], refs[n_in + nx:]
            outs, xout, rest = rest[:n_out], rest[n_out:n_out + nx], rest[n_out + nx:]
            scr, sems = rest[:n_scr], rest[n_scr:]
            first = functools.reduce(jnp.logical_and, [pl.program_id(a) == 0 for a in range(len(grid))])
            last = functools.reduce(jnp.logical_and, [pl.program_id(a) == g - 1 for a, g in enumerate(grid)])

            @pl.when(first)
            def _():
                comm.start(xin, xout, sems)
            inner(*ins, *outs, *scr)

            @pl.when(last)
            def _():
                comm.wait(xin, xout, sems)

    call = pl.pallas_call(
        body, out_shape=out_shape[0] if single and comm is None else out_shape, grid=grid, in_specs=in_specs,
        out_specs=out_specs[0] if single and comm is None else out_specs, scratch_shapes=scratch, name=name,
        compiler_params=pltpu.CompilerParams(
            dimension_semantics=("arbitrary",) * len(grid), vmem_limit_bytes=vmem),
        interpret=False)
    if comm is None:
        return call
    return lambda *args: call(*args, *comm.arrs)


def _rms(x, g):
    r = lax.rsqrt(jnp.mean(x * x, axis=-1, keepdims=True) + RMS_EPS)
    return x * r * g, r


def _rms_bwd(x, g, r, dy):
    xh = x * r
    dyg = dy * g
    dx = r * (dyg - xh * jnp.mean(dyg * xh, axis=-1, keepdims=True))
    return dx, jnp.sum(dy * xh, axis=0, keepdims=True)


_GELU_C = math.sqrt(2.0 / math.pi)


def _gelu_parts(x):
    t = jnp.tanh(_GELU_C * (x + 0.044715 * x * x * x))
    return 0.5 * x * (1.0 + t), t


def _gelu_grad(x, t):
    return 0.5 * (1.0 + t) + 0.5 * x * (1.0 - t * t) * _GELU_C * (1.0 + 3 * 0.044715 * x * x)


def _rope_tables(seqlen):
    half = ROT_DIM // 2
    inv = ROPE_THETA ** (-jnp.arange(0, ROT_DIM, 2, dtype=F32) / ROT_DIM)
    ang = jnp.arange(seqlen, dtype=jnp.int32).astype(F32)[:, None] * inv[None, :]
    cos, sin = jnp.cos(ang), jnp.sin(ang)
    zeros = jnp.zeros((seqlen, HEAD_DIM - ROT_DIM), F32)
    zh = jnp.zeros((seqlen, half), F32)
    c = jnp.concatenate([cos, cos, zeros + 1.0], axis=1)
    sa = jnp.concatenate([zh, sin, zeros], axis=1)
    sb = jnp.concatenate([-sin, zh, zeros], axis=1)
    return tuple(jnp.tile(t, (1, 128 // HEAD_DIM)) for t in (c, sa, sb))


def _rope(x, c, sa, sb):
    w = x.shape[1]
    reps = w // 128
    half = ROT_DIM // 2
    return (x * jnp.tile(c, (1, reps)) + pltpu.roll(x, half, 1) * jnp.tile(sa, (1, reps))
            + pltpu.roll(x, w - half, 1) * jnp.tile(sb, (1, reps)))


def _rope_bwd(dy, c, sa, sb):
    w = dy.shape[1]
    reps = w // 128
    half = ROT_DIM // 2
    return (dy * jnp.tile(c, (1, reps)) + pltpu.roll(dy * jnp.tile(sa, (1, reps)), w - half, 1)
            + pltpu.roll(dy * jnp.tile(sb, (1, reps)), half, 1))


def _rspec(tm, c):
    return pl.BlockSpec((tm, c), lambda i: (i, 0))


def _cspec(shape, idx=None):
    idx = tuple(idx) if idx is not None else (0,) * len(shape)
    return pl.BlockSpec(tuple(shape), lambda i: idx, pipeline_mode=pl.Buffered(1))


def _rowcall(body, name, seqlen, tm, rows_in, consts_in, rows_out, acc_out=(), extra_vmem=0, comm=None):
    in_specs = [_rspec(tm, a.shape[1]) for a in rows_in] + [_cspec(bs, ix) for _, bs, ix in consts_in]
    out_shape = [SDS((seqlen, c), d) for c, d in rows_out] + [SDS(s, F32) for s in acc_out]
    out_specs = [_rspec(tm, c) for c, _ in rows_out] + [pl.BlockSpec(s, lambda i: (0, 0)) for s in acc_out]
    blocks = ([((tm, a.shape[1]), a.dtype, 2) for a in rows_in] + [(bs, a.dtype, 1) for a, bs, _ in consts_in]
              + [((tm, c), d, 2) for c, d in rows_out])
    temporaries = 12 * tm * rows_in[0].shape[1] * 4
    return _pcall(body, name=name, out_shape=out_shape, grid=(seqlen // tm,), in_specs=in_specs,
                  out_specs=out_specs, vmem=_vmem_limit(blocks, extra_vmem + temporaries), comm=comm)(
                      *rows_in, *[a for a, _, _ in consts_in])


def _whole(a):
    return (a, a.shape, None)


def _norm_fwd(h, g, name, comm=None):
    seqlen, d = h.shape
    tm = _tile(seqlen, 1024)

    def body(h_ref, g_ref, o_ref):
        o_ref[...] = _rms(h_ref[...], g_ref[...])[0]

    return _rowcall(body, name, seqlen, tm, [h], [_whole(g.reshape(1, d))], [(d, F32)], comm=comm)


def _norm_bwd(h, g, dy1, dy2, dres, name, comm=None):
    seqlen, d = h.shape
    tm = _tile(seqlen, 512)

    def body(h_ref, dy1_ref, dy2_ref, dres_ref, g_ref, dh_ref, dg_ref):
        @pl.when(pl.program_id(0) == 0)
        def _():
            dg_ref[...] = jnp.zeros_like(dg_ref)
        x = h_ref[...]
        gv = g_ref[...]
        _, r = _rms(x, gv)
        dx, dg = _rms_bwd(x, gv, r, dy1_ref[...] + dy2_ref[...])
        dh_ref[...] = dres_ref[...] + dx
        dg_ref[...] += dg

    return _rowcall(body, name, seqlen, tm, [h, dy1, dy2, dres], [_whole(g.reshape(1, d))], [(d, F32)], [(1, d)],
                    comm=comm)


def _glu_fwd(y, hn, h, dskip, wglu, name):
    seqlen, d = h.shape
    tm = _tile(seqlen, 512)

    def body(y_ref, hn_ref, h_ref, d_ref, w_ref, o_ref):
        yy = y_ref[...] + d_ref[...] * hn_ref[...]
        ge, _ = _gelu_parts(yy)
        ab = _dot(ge.astype(BF16), w_ref[...])
        o_ref[...] = h_ref[...] + ab[:, :d] * jax.nn.sigmoid(ab[:, d:])

    return _rowcall(body, name, seqlen, tm, [y, hn, h], [_whole(dskip.reshape(1, d)), _whole(wglu)], [(d, F32)],
                    extra_vmem=tm * d * 4 * 6)[0]


def _glu_bwd(y, hn, dmix, dskip, wglu, wglu_t, name):
    seqlen, d = hn.shape
    tm = _tile(seqlen, 512)

    def body(y_ref, hn_ref, dm_ref, d_ref, w_ref, wt_ref, dyy_ref, dhn_ref, ge_ref, dab_ref, dd_ref):
        @pl.when(pl.program_id(0) == 0)
        def _():
            dd_ref[...] = jnp.zeros_like(dd_ref)
        hn_v = hn_ref[...]
        dsk = d_ref[...]
        yy = y_ref[...] + dsk * hn_v
        ge, t = _gelu_parts(yy)
        geb = ge.astype(BF16)
        ab = _dot(geb, w_ref[...])
        a = ab[:, :d]
        sg = jax.nn.sigmoid(ab[:, d:])
        dm = dm_ref[...]
        dab_ref[:, :d] = (dm * sg).astype(BF16)
        dab_ref[:, d:] = (dm * a * sg * (1.0 - sg)).astype(BF16)
        dge = _dot(dab_ref[...], wt_ref[...])
        dyy = dge * _gelu_grad(yy, t)
        dyy_ref[...] = dyy
        dhn_ref[...] = dyy * dsk
        ge_ref[...] = geb
        dd_ref[...] += jnp.sum(dyy * hn_v, axis=0, keepdims=True)

    return _rowcall(body, name, seqlen, tm, [y, hn, dmix],
                    [_whole(dskip.reshape(1, d)), _whole(wglu), _whole(wglu_t)],
                    [(d, F32), (d, F32), (d, BF16), (2 * d, BF16)], [(1, d)], extra_vmem=tm * d * 4 * 8)


def _q_fwd(h, g, wq, tabs, name):
    seqlen, d = h.shape
    tm = _tile(seqlen, 512)

    def body(h_ref, c_ref, sa_ref, sb_ref, g_ref, w_ref, q_ref):
        hn, _ = _rms(h_ref[...], g_ref[...])
        qp = _dot(hn.astype(BF16), w_ref[...])
        q_ref[...] = _rope(qp, c_ref[...], sa_ref[...], sb_ref[...]).astype(BF16)

    return _rowcall(body, name, seqlen, tm, [h, *tabs], [_whole(g.reshape(1, d)), _whole(wq)], [(d, BF16)],
                    extra_vmem=tm * d * 4 * 6)[0]


def _q_bwd(dq, h, g, dres, wq, tabs, name):
    seqlen, d = h.shape
    tm = _tile(seqlen, 512)

    def body(dq_ref, h_ref, dres_ref, c_ref, sa_ref, sb_ref, g_ref, w_ref, dh_ref, dqp_ref, hn_ref, dg_ref):
        @pl.when(pl.program_id(0) == 0)
        def _():
            dg_ref[...] = jnp.zeros_like(dg_ref)
        dqp = _rope_bwd(dq_ref[...], c_ref[...], sa_ref[...], sb_ref[...]).astype(BF16)
        x = h_ref[...]
        gv = g_ref[...]
        hn, r = _rms(x, gv)
        dhn = _dot(dqp, w_ref[...])
        dx, dg = _rms_bwd(x, gv, r, dhn)
        dh_ref[...] = dres_ref[...] + dx
        dqp_ref[...] = dqp
        hn_ref[...] = hn.astype(BF16)
        dg_ref[...] += dg

    return _rowcall(body, name, seqlen, tm, [dq, h, dres, *tabs], [_whole(g.reshape(1, d)), _whole(wq)],
                    [(d, F32), (d, BF16), (d, BF16)], [(1, d)], extra_vmem=tm * d * 4 * 6)


def _kv_fwd(h, g, wk, wv, tabs, name):
    seqlen, d = h.shape
    dk = wk.shape[1]
    tm = _tile(seqlen, 512)

    def body(h_ref, c_ref, sa_ref, sb_ref, g_ref, wk_ref, wv_ref, k_ref, v_ref):
        hk = _rms(h_ref[...], g_ref[...])[0].astype(BF16)
        k_ref[...] = _rope(_dot(hk, wk_ref[...]), c_ref[...], sa_ref[...], sb_ref[...]).astype(BF16)
        v_ref[...] = _dot(hk, wv_ref[...]).astype(BF16)

    return _rowcall(body, name, seqlen, tm, [h, *tabs], [_whole(g.reshape(1, d)), _whole(wk), _whole(wv)],
                    [(dk, BF16), (dk, BF16)], extra_vmem=tm * d * 4 * 4)


def _kv_bwd(dks, dvs, h, g, dres, wk, wv, tabs, name):
    seqlen, d = h.shape
    dkw = wk.shape[0]
    tm = _tile(seqlen, 512)

    def body(dk0_ref, dk1_ref, dv0_ref, dv1_ref, h_ref, dres_ref, c_ref, sa_ref, sb_ref, g_ref, wk_ref, wv_ref,
             dh_ref, dkp_ref, dvb_ref, hk_ref, dg_ref):
        @pl.when(pl.program_id(0) == 0)
        def _():
            dg_ref[...] = jnp.zeros_like(dg_ref)
        dkp = _rope_bwd(dk0_ref[...] + dk1_ref[...], c_ref[...], sa_ref[...], sb_ref[...]).astype(BF16)
        dvb = (dv0_ref[...] + dv1_ref[...]).astype(BF16)
        x = h_ref[...]
        gv = g_ref[...]
        hk, r = _rms(x, gv)
        dhk = _dot(dkp, wk_ref[...]) + _dot(dvb, wv_ref[...])
        dx, dg = _rms_bwd(x, gv, r, dhk)
        dh_ref[...] = dres_ref[...] + dx
        dkp_ref[...] = dkp
        dvb_ref[...] = dvb
        hk_ref[...] = hk.astype(BF16)
        dg_ref[...] += dg

    return _rowcall(body, name, seqlen, tm, [dks[0], dks[1], dvs[0], dvs[1], h, dres, *tabs],
                    [_whole(g.reshape(1, d)), _whole(wk), _whole(wv)],
                    [(d, F32), (dkw, BF16), (dkw, BF16), (d, BF16)], [(1, d)], extra_vmem=tm * d * 4 * 6)


def _lin_res(h, xb, w, name):
    seqlen, d = h.shape
    tm = _tile(seqlen, 512)

    def body(h_ref, x_ref, w_ref, o_ref):
        o_ref[...] = h_ref[...] + _dot(x_ref[...], w_ref[...])

    return _rowcall(body, name, seqlen, tm, [h, xb], [_whole(w)], [(d, F32)], extra_vmem=tm * d * 4 * 2)[0]


def _lin_bf16(dy, w, name):
    seqlen, d = dy.shape
    tm = _tile(seqlen, 512)

    def body(dy_ref, w_ref, o_ref):
        o_ref[...] = _dot(dy_ref[...].astype(BF16), w_ref[...]).astype(BF16)

    return _rowcall(body, name, seqlen, tm, [dy], [_whole(w)], [(w.shape[1], BF16)], extra_vmem=tm * d * 4 * 2)[0]


def _mlp_fwd(h, g, wup, wdn, name, comm=None):
    seqlen, d = h.shape
    f = wup.shape[1]
    tm = _tile(seqlen, 512)

    def body(h_ref, g_ref, wup_ref, wdn_ref, o_ref, act_ref):
        x = h_ref[...]
        hm = _rms(x, g_ref[...])[0].astype(BF16)
        r = jnp.maximum(_dot(hm, wup_ref[...]), 0.0)
        act = (r * r).astype(BF16)
        act_ref[...] = act
        o_ref[...] = x + _dot(act, wdn_ref[...])

    consts = [_whole(g.reshape(1, d)), _whole(wup), _whole(wdn)]
    return _rowcall(body, name, seqlen, tm, [h], consts, [(d, F32), (f, BF16)], extra_vmem=tm * f * 4 * 3, comm=comm)


def _mlp_bwd(h, act, dh, g, wup_t, wdn_t, name, comm=None):
    seqlen, d = h.shape
    f = wup_t.shape[0]
    tm = _tile(seqlen, 512)

    def body(h_ref, act_ref, dh_ref, g_ref, wup_ref, wdn_ref, dhin_ref, hm_ref, da_ref, dg_ref):
        @pl.when(pl.program_id(0) == 0)
        def _():
            dg_ref[...] = jnp.zeros_like(dg_ref)
        x = h_ref[...]
        gv = g_ref[...]
        dy = dh_ref[...]
        hm, r = _rms(x, gv)
        rl2 = 2.0 * jnp.sqrt(act_ref[...].astype(F32))
        da = (_dot(dy.astype(BF16), wdn_ref[...]) * rl2).astype(BF16)
        da_ref[...] = da
        dx, dg = _rms_bwd(x, gv, r, _dot(da, wup_ref[...]))
        dhin_ref[...] = dy + dx
        hm_ref[...] = hm.astype(BF16)
        dg_ref[...] += dg

    consts = [_whole(g.reshape(1, d)), _whole(wup_t), _whole(wdn_t)]
    return _rowcall(body, name, seqlen, tm, [h, act, dh], consts, [(d, F32), (d, BF16), (f, BF16)], [(1, d)],
                    extra_vmem=tm * f * 4 * 3, comm=comm)


def _ple_fwd(h, p, g, wg, wpp, name):
    seqlen, d = h.shape
    tm = _tile(seqlen, 512)

    def body(h_ref, p_ref, g_ref, wg_ref, wpp_ref, o_ref):
        x = h_ref[...]
        n = _rms(x, g_ref[...])[0].astype(BF16)
        gate = jax.nn.sigmoid(_dot(n, wg_ref[...]))
        o_ref[...] = x + gate * _dot(p_ref[...].astype(BF16), wpp_ref[...])

    return _rowcall(body, name, seqlen, tm, [h, p], [_whole(g.reshape(1, d)), _whole(wg), _whole(wpp)], [(d, F32)],
                    extra_vmem=tm * d * 4 * 5)[0]


def _ple_bwd(h, p, dh, g, wg, wg_t, wpp, name):
    seqlen, d = h.shape
    tm = _tile(seqlen, 512)

    def body(h_ref, p_ref, dh_ref, g_ref, wg_ref, wgt_ref, wpp_ref, dhin_ref, dz_ref, n_ref, dpp_ref, dg_ref):
        @pl.when(pl.program_id(0) == 0)
        def _():
            dg_ref[...] = jnp.zeros_like(dg_ref)
        x = h_ref[...]
        gv = g_ref[...]
        dy = dh_ref[...]
        n, r = _rms(x, gv)
        nb16 = n.astype(BF16)
        gate = jax.nn.sigmoid(_dot(nb16, wg_ref[...]))
        pp = _dot(p_ref[...].astype(BF16), wpp_ref[...])
        dz = (dy * pp * gate * (1.0 - gate)).astype(BF16)
        dn = _dot(dz, wgt_ref[...])
        dx, dg = _rms_bwd(x, gv, r, dn)
        dhin_ref[...] = dy + dx
        dz_ref[...] = dz
        n_ref[...] = nb16
        dpp_ref[...] = (dy * gate).astype(BF16)
        dg_ref[...] += dg

    return _rowcall(body, name, seqlen, tm, [h, p, dh],
                    [_whole(g.reshape(1, d)), _whole(wg), _whole(wg_t), _whole(wpp)],
                    [(d, F32), (d, BF16), (d, BF16), (d, BF16)], [(1, d)], extra_vmem=tm * d * 4 * 8)


def _loss_bwd(h, g, tgt, name):
    seqlen, d = h.shape
    tm = _tile(seqlen, 512)

    def body(h_ref, t_ref, g_ref, dh_ref, loss_ref, dg_ref):
        @pl.when(pl.program_id(0) == 0)
        def _():
            dg_ref[...] = jnp.zeros_like(dg_ref)
            loss_ref[...] = jnp.zeros_like(loss_ref)
        x = h_ref[...]
        gv = g_ref[...]
        y, r = _rms(x, gv)
        diff = y - t_ref[...]
        loss_ref[...] += (0.5 / d) * jnp.sum(jnp.sum(diff * diff, axis=1, keepdims=True), axis=0, keepdims=True)
        dx, dg = _rms_bwd(x, gv, r, diff * (1.0 / d))
        dh_ref[...] = dx
        dg_ref[...] += dg

    return _rowcall(body, name, seqlen, tm, [h, tgt], [_whole(g.reshape(1, d))], [(d, F32)], [(1, 128), (1, d)])


def _atb(a, b, col_blocked, name):
    seqlen, k1 = a.shape
    k2 = b.shape[1]
    if col_blocked:
        cs = k2 // NDEV
        t1 = _tile(k1, 512)
        nblk = _tile(NDEV, max(1, 2048 // cs))
        t2 = nblk * cs
        oshape = (NDEV, k1, cs)
        oblock = (nblk, t1, cs)
        omap = lambda i, j, l: (j, i, 0)
    else:
        rs = k1 // NDEV
        t2 = _tile(k2, 2048)
        nblk = _tile(NDEV, max(1, 1024 // rs))
        t1 = nblk * rs
        oshape = (NDEV, rs, k2)
        oblock = (nblk, rs, t2)
        omap = lambda i, j, l: (i, 0, j)
    tl = _tile(seqlen, 2048 if b.dtype == BF16 else 1024)

    def body(a_ref, b_ref, o_ref):
        @pl.when(pl.program_id(2) == 0)
        def _():
            o_ref[...] = jnp.zeros_like(o_ref)
        res = _dot(a_ref[...].astype(BF16), b_ref[...].astype(BF16), TN)
        for n in range(nblk):
            if col_blocked:
                o_ref[n] += res[:, n * cs:(n + 1) * cs]
            else:
                o_ref[n] += res[n * rs:(n + 1) * rs, :]

    blocks = [((tl, t1), a.dtype, 2), ((tl, t2), b.dtype, 2), ((t1, t2), F32, 2)]
    return _pcall(
        body, name=name, out_shape=SDS(oshape, F32), grid=(k1 // t1, k2 // t2, seqlen // tl),
        in_specs=[pl.BlockSpec((tl, t1), lambda i, j, l: (l, i)), pl.BlockSpec((tl, t2), lambda i, j, l: (l, j))],
        out_specs=pl.BlockSpec(oblock, omap),
        vmem=_vmem_limit(blocks, extra=t1 * t2 * 4 + tl * (t1 + t2) * 2))(a, b)


_ATTN_SCALE = HEAD_DIM ** -0.5


def _attn_bias():
    qi = lax.broadcasted_iota(jnp.int32, (ATTN_BLOCK, 2 * ATTN_BLOCK), 0) + ATTN_BLOCK
    kj = lax.broadcasted_iota(jnp.int32, (ATTN_BLOCK, 2 * ATTN_BLOCK), 1)
    band = (kj <= qi) & (qi - kj < ATTN_BLOCK)
    return jnp.where(jnp.stack([band & (kj >= ATTN_BLOCK), band]), 0.0, NEG_INF).astype(F32)


def _bias_spec():
    return pl.BlockSpec((None, ATTN_BLOCK, 2 * ATTN_BLOCK), lambda n: (jnp.minimum(n, 1), 0, 0))


def _attn_probs(q4s, kks, sink_col, bias):
    s = jnp.concatenate([_dot(q4, kk, NT) for q4, kk in zip(q4s, kks)], axis=0)
    rows = s.shape[0]
    s = (s.reshape(rows // ATTN_BLOCK, ATTN_BLOCK, 2 * ATTN_BLOCK) + bias).reshape(rows, 2 * ATTN_BLOCK)
    m = jnp.maximum(jnp.max(s, axis=1, keepdims=True), sink_col)
    pr = jnp.exp(s - m)
    es = jnp.exp(sink_col - m)
    inv = 1.0 / (jnp.sum(pr, axis=1, keepdims=True) + es)
    return pr * inv, es * inv


def _sink_col(sink_ref, nheads):
    return jnp.concatenate([jnp.full((ATTN_BLOCK, 1), sink_ref[hq], F32) for hq in range(nheads)], axis=0)


def _kv_pair(p_ref, c_ref, kh):
    sl = slice(kh * HEAD_DIM, (kh + 1) * HEAD_DIM)
    return jnp.concatenate([p_ref[:, sl], c_ref[:, sl]], axis=0)


def _stack_heads(ref, kh, scale=None):
    x = jnp.concatenate(
        [ref[:, (kh * GQA_GROUP + g) * HEAD_DIM:(kh * GQA_GROUP + g + 1) * HEAD_DIM] for g in range(GQA_GROUP)], axis=0)
    return x if scale is None else x * scale


def _attn_fwd(q, k, v, sinks, name):
    seqlen, d = q.shape
    dkv = k.shape[1]
    nkv = dkv // HEAD_DIM
    nb = seqlen // ATTN_BLOCK
    blk = ATTN_BLOCK

    def body(sink_ref, bias_ref, q_ref, kc_ref, kp_ref, vc_ref, vp_ref, o_ref):
        q4s = [_stack_heads(q_ref, kh, _ATTN_SCALE) for kh in range(nkv)]
        kks = [_kv_pair(kp_ref, kc_ref, kh) for kh in range(nkv)]
        w, _ = _attn_probs(q4s, kks, _sink_col(sink_ref, nkv * GQA_GROUP), bias_ref[...])
        wb = w.astype(BF16)
        for kh in range(nkv):
            o4 = _dot(wb[kh * GQA_GROUP * blk:(kh + 1) * GQA_GROUP * blk, :], _kv_pair(vp_ref, vc_ref, kh))
            for g in range(GQA_GROUP):
                hq = kh * GQA_GROUP + g
                o_ref[:, hq * HEAD_DIM:(hq + 1) * HEAD_DIM] = o4[g * blk:(g + 1) * blk, :].astype(BF16)

    cur = lambda n: (n, 0)
    prev = lambda n: (jnp.maximum(n - 1, 0), 0)
    return _pcall(
        body, name=name, out_shape=SDS((seqlen, d), BF16), grid=(nb,),
        in_specs=[pl.BlockSpec(memory_space=pltpu.SMEM), _bias_spec(), pl.BlockSpec((blk, d), cur),
                  pl.BlockSpec((blk, dkv), cur), pl.BlockSpec((blk, dkv), prev),
                  pl.BlockSpec((blk, dkv), cur), pl.BlockSpec((blk, dkv), prev)],
        out_specs=pl.BlockSpec((blk, d), cur), vmem=32 << 20)(sinks, _attn_bias(), q, k, k, v, v)


def _attn_bwd(q, k, v, do, sinks, name):
    seqlen, d = q.shape
    dkv = k.shape[1]
    nkv = dkv // HEAD_DIM
    nh = d // HEAD_DIM
    nb = seqlen // ATTN_BLOCK
    blk = ATTN_BLOCK

    def body(sink_ref, bias_ref, q_ref, kc_ref, kp_ref, vc_ref, vp_ref, do_ref, dq_ref, dk_ref, dv_ref, ds_ref,
             ck_ref, cv_ref):
        n = pl.program_id(0)

        @pl.when(n == 0)
        def _():
            ck_ref[...] = jnp.zeros_like(ck_ref)
            cv_ref[...] = jnp.zeros_like(cv_ref)
            ds_ref[...] = jnp.zeros_like(ds_ref)

        @pl.when(n == nb)
        def _():
            dk_ref[...] = ck_ref[...]
            dv_ref[...] = cv_ref[...]

        @pl.when(n < nb)
        def _():
            q4s = [_stack_heads(q_ref, kh, _ATTN_SCALE) for kh in range(nkv)]
            do4s = [_stack_heads(do_ref, kh) for kh in range(nkv)]
            kks = [_kv_pair(kp_ref, kc_ref, kh) for kh in range(nkv)]
            w, wsink = _attn_probs(q4s, kks, _sink_col(sink_ref, nh), bias_ref[...])
            dw = jnp.concatenate([_dot(do4s[kh], _kv_pair(vp_ref, vc_ref, kh), NT) for kh in range(nkv)], axis=0)
            dsum = jnp.sum(w * dw, axis=1, keepdims=True)
            ds_all = (w * (dw - dsum)).astype(BF16)
            wb = w.astype(BF16)
            dsk = -wsink * dsum
            for kh in range(nkv):
                sl = slice(kh * HEAD_DIM, (kh + 1) * HEAD_DIM)
                rows = slice(kh * GQA_GROUP * blk, (kh + 1) * GQA_GROUP * blk)
                ds = ds_all[rows, :]
                dq4 = _dot(ds, kks[kh]) * _ATTN_SCALE
                dkk = _dot(ds, q4s[kh], TN)
                dvv = _dot(wb[rows, :], do4s[kh], TN)
                for g in range(GQA_GROUP):
                    hq = kh * GQA_GROUP + g
                    dq_ref[:, hq * HEAD_DIM:(hq + 1) * HEAD_DIM] = dq4[g * blk:(g + 1) * blk, :]
                    ds_ref[hq:hq + 1, :] += jnp.sum(dsk[hq * blk:(hq + 1) * blk, :], axis=0, keepdims=True)
                dk_ref[:, sl] = ck_ref[:, sl] + dkk[:blk, :]
                dv_ref[:, sl] = cv_ref[:, sl] + dvv[:blk, :]
                ck_ref[:, sl] = dkk[blk:, :]
                cv_ref[:, sl] = dvv[blk:, :]

    cur = lambda n: (jnp.minimum(n, nb - 1), 0)
    prev = lambda n: (jnp.clip(n - 1, 0, nb - 1), 0)
    lag = lambda n: (jnp.maximum(n - 1, 0), 0)
    return _pcall(
        body, name=name,
        out_shape=[SDS((seqlen, d), F32), SDS((seqlen, dkv), F32), SDS((seqlen, dkv), F32), SDS((nh, 128), F32)],
        grid=(nb + 1,),
        in_specs=[pl.BlockSpec(memory_space=pltpu.SMEM), _bias_spec(), pl.BlockSpec((blk, d), cur),
                  pl.BlockSpec((blk, dkv), cur), pl.BlockSpec((blk, dkv), prev),
                  pl.BlockSpec((blk, dkv), cur), pl.BlockSpec((blk, dkv), prev), pl.BlockSpec((blk, d), cur)],
        out_specs=[pl.BlockSpec((blk, d), cur), pl.BlockSpec((blk, dkv), lag), pl.BlockSpec((blk, dkv), lag),
                   pl.BlockSpec((nh, 128), lambda n: (0, 0))],
        scratch=[pltpu.VMEM((blk, dkv), F32)] * 2, vmem=32 << 20)(sinks, _attn_bias(), q, k, k, v, v, do)


def _ssm_mats(lre, lim, ldt, btr, bti, cr, ci):
    dt = jnp.exp(ldt)
    mag = jnp.exp(lre * dt)
    ar = mag * jnp.cos(lim * dt)
    ai = mag * jnp.sin(lim * dt)
    den = lre * lre + lim * lim
    nr = ar - 1.0
    cfr = (nr * lre + ai * lim) / den
    cfi = (ai * lre - nr * lim) / den
    bbr = cfr * btr - cfi * bti
    bbi = cfr * bti + cfi * btr
    pr = [jnp.ones_like(ar)]
    pi = [jnp.zeros_like(ai)]
    for _ in range(SSM_T):
        pr.append(pr[-1] * ar - pi[-1] * ai)
        pi.append(pr[-2] * ai + pi[-1] * ar)
    last = SSM_T - 1
    p_re = jnp.concatenate([pr[last - s] * bbr - pi[last - s] * bbi for s in range(SSM_T)], axis=0)
    p_im = jnp.concatenate([pr[last - s] * bbi + pi[last - s] * bbr for s in range(SSM_T)], axis=0)
    qt_re = jnp.concatenate([pr[t + 1] * cr - pi[t + 1] * ci for t in range(SSM_T)], axis=0)
    qt_im = jnp.concatenate([-(pr[t + 1] * ci + pi[t + 1] * cr) for t in range(SSM_T)], axis=0)
    ctr = jnp.concatenate([cr] * SSM_T, axis=0)
    cti = jnp.concatenate([ci] * SSM_T, axis=0)
    lag = (lax.broadcasted_iota(jnp.int32, (SSM_W, SSM_W), 1) // SSM_GROUP
           - lax.broadcasted_iota(jnp.int32, (SSM_W, SSM_W), 0) // SSM_GROUP)
    m = jnp.zeros((SSM_W, SSM_W), F32)
    for l in range(SSM_T):
        zr = jnp.concatenate([pr[l] * bbr - pi[l] * bbi] * SSM_T, axis=0)
        zi = jnp.concatenate([pr[l] * bbi + pi[l] * bbr] * SSM_T, axis=0)
        kl = _dot3_nt(zr, ctr) - _dot3_nt(zi, cti)
        m = m + jnp.where(lag == l, kl, 0.0)
    return m, p_re, p_im, qt_re, qt_im, pr[SSM_T], pi[SSM_T]


_SSM_GB = 8


def _ssm_param_specs(ng):
    n, hh = SSM_STATE, SSM_GROUP
    gb = _tile(ng, _SSM_GB)
    row = pl.BlockSpec((gb, 1, n), lambda i: (i, 0, 0))
    one = pl.BlockSpec((gb, 1, 1), lambda i: (i, 0, 0))
    mat = pl.BlockSpec((gb, hh, n), lambda i: (i, 0, 0))
    big = pl.BlockSpec((gb, SSM_W, SSM_W), lambda i: (i, 0, 0))
    half = pl.BlockSpec((gb, SSM_W, n), lambda i: (i, 0, 0))
    return gb, row, one, mat, big, half


def _ssm_prep(params, name):
    ng = params[0].shape[0]
    n = SSM_STATE
    gb, row, one, mat, big, half = _ssm_param_specs(ng)

    def body(lre, lim, ldt, btr, bti, cr, ci, m_ref, pre_ref, pim_ref, qre_ref, qim_ref, atr_ref, ati_ref):
        for gi in range(gb):
            outs = _ssm_mats(lre[gi], lim[gi], ldt[gi], btr[gi], bti[gi], cr[gi], ci[gi])
            for ref, val in zip((m_ref, pre_ref, pim_ref, qre_ref, qim_ref, atr_ref, ati_ref), outs):
                ref[gi] = val

    return _pcall(
        body, name=name,
        out_shape=[SDS((ng, SSM_W, SSM_W), F32)] + [SDS((ng, SSM_W, n), F32)] * 4 + [SDS((ng, 1, n), F32)] * 2,
        grid=(ng // gb,), in_specs=[row, row, one, mat, mat, mat, mat],
        out_specs=[big, half, half, half, half, row, row], vmem=40 << 20)(*params)


def _ssm_prep_vjp(params, cots, name):
    ng = params[0].shape[0]
    n, hh = SSM_STATE, SSM_GROUP
    gb, row, one, mat, big, half = _ssm_param_specs(ng)

    def body(lre, lim, ldt, btr, bti, cr, ci, dm, dpre, dpim, dqre, dqim, datr, dati,
             o_lre, o_lim, o_ldt, o_btr, o_bti, o_cr, o_ci):
        for gi in range(gb):
            prm = (lre[gi], lim[gi], ldt[gi], btr[gi], bti[gi], cr[gi], ci[gi])
            _, pull = jax.vjp(_ssm_mats, *prm)
            grads = pull((dm[gi], dpre[gi], dpim[gi], dqre[gi], dqim[gi], datr[gi], dati[gi]))
            for ref, val in zip((o_lre, o_lim, o_ldt, o_btr, o_bti, o_cr, o_ci), grads):
                ref[gi] = val

    return _pcall(
        body, name=name,
        out_shape=[SDS((ng, 1, n), F32)] * 2 + [SDS((ng, 1, 1), F32)] + [SDS((ng, hh, n), F32)] * 4,
        grid=(ng // gb,), in_specs=[row, row, one, mat, mat, mat, mat, big, half, half, half, half, row, row],
        out_specs=[row, row, one, mat, mat, mat, mat], vmem=48 << 20)(*params, *cots)


_SSM_GT = SSM_W // SSM_GROUP


def _blk_transpose(xs):
    assert len(xs) == SSM_T == _SSM_GT
    blk = lax.broadcasted_iota(jnp.int32, xs[0].shape, 1) // SSM_GROUP
    xs = list(xs)
    k = SSM_T // 2
    while k:
        high = (blk // k) % 2 == 1
        nxt = []
        for i in range(SSM_T):
            if i & k:
                nxt.append(jnp.where(high, xs[i], pltpu.roll(xs[i ^ k], SSM_W - SSM_GROUP * k, 1)))
            else:
                nxt.append(jnp.where(high, pltpu.roll(xs[i ^ k], SSM_GROUP * k, 1), xs[i]))
        xs = nxt
        k //= 2
    return xs


def _tile_groups(x_ref, ncb):
    return _blk_transpose([x_ref[pl.ds(t, ncb, stride=SSM_T), :] for t in range(SSM_T)])


def _groups_tile(ys, o_ref, ncb):
    for t, y in enumerate(_blk_transpose(ys)):
        o_ref[pl.ds(t, ncb, stride=SSM_T), :] = y


def _ssm_specs(seqlen, d):
    ncb = _tile(seqlen // SSM_T, 512)
    grid = (d // SSM_W, seqlen // (SSM_T * ncb))
    act = pl.BlockSpec((SSM_T * ncb, SSM_W), lambda j, r: (r, j))
    state = pl.BlockSpec((ncb, _SSM_GT * SSM_W), lambda j, r: (r, j))
    mats = pl.BlockSpec((_SSM_GT, SSM_W, SSM_W), lambda j, r: (j, 0, 0))
    return ncb, grid, act, state, mats


def _gsl(gl):
    return slice(gl * SSM_W, (gl + 1) * SSM_W)


def _ssm_state_in(hn, pmat, name):
    seqlen, d = hn.shape
    ncb, grid, act, state, mats = _ssm_specs(seqlen, d)

    def body(x_ref, p_ref, s_ref):
        us = _tile_groups(x_ref, ncb)
        for gl in range(_SSM_GT):
            s_ref[:, _gsl(gl)] = _dot3(us[gl], p_ref[gl], NN)

    return _pcall(body, name=name, out_shape=SDS((seqlen // SSM_T, d * SSM_T), F32), grid=grid,
                  in_specs=[act, mats], out_specs=state, vmem=40 << 20)(hn, pmat)


def _ssm_out(hn, xp, mmat, qt, name, comm=None):
    seqlen, d = hn.shape
    ncb, grid, act, state, mats = _ssm_specs(seqlen, d)

    def body(x_ref, xp_ref, m_ref, q_ref, y_ref):
        us = _tile_groups(x_ref, ncb)
        ys = [_dot3(us[gl], m_ref[gl], NN) + _dot3(xp_ref[:, _gsl(gl)], q_ref[gl], NT)
              for gl in range(_SSM_GT)]
        _groups_tile(ys, y_ref, ncb)

    return _pcall(body, name=name, out_shape=[SDS((seqlen, d), F32)], grid=grid,
                  in_specs=[act, state, mats, mats], out_specs=[act], vmem=40 << 20, comm=comm)(hn, xp, mmat, qt)


def _ssm_dstate(dy, qt, name):
    seqlen, d = dy.shape
    ncb, grid, act, state, mats = _ssm_specs(seqlen, d)

    def body(dy_ref, q_ref, o_ref):
        dys = _tile_groups(dy_ref, ncb)
        for gl in range(_SSM_GT):
            o_ref[:, _gsl(gl)] = _dot3(dys[gl], q_ref[gl], NN)

    return _pcall(body, name=name, out_shape=SDS((seqlen // SSM_T, d * SSM_T), F32), grid=grid,
                  in_specs=[act, mats], out_specs=state, vmem=40 << 20)(dy, qt)


def _ssm_bwd(hn, dy, xp, gs, mmat, pmat, name, comm=None):
    seqlen, d = hn.shape
    ng = d // SSM_GROUP
    ncb, grid, act, state, mats = _ssm_specs(seqlen, d)

    def body(x_ref, dy_ref, xp_ref, g_ref, m_ref, p_ref, du_ref, dm_ref, dp_ref, dq_ref, da_ref):
        @pl.when(pl.program_id(1) == 0)
        def _():
            for ref in (dm_ref, dp_ref, dq_ref, da_ref):
                ref[...] = jnp.zeros_like(ref)
        us = _tile_groups(x_ref, ncb)
        dys = _tile_groups(dy_ref, ncb)
        dus = []
        for gl in range(_SSM_GT):
            xv, gv = xp_ref[:, _gsl(gl)], g_ref[:, _gsl(gl)]
            u2, dy2, x2, g2 = _split(us[gl]), _split(dys[gl]), _split(xv), _split(gv)
            dus.append(_dot3(dy2, m_ref[gl], NT) + _dot3(g2, p_ref[gl], NT))
            dm_ref[gl] += _dot3(u2, dy2, TN)
            dp_ref[gl] += _dot3(u2, g2, TN)
            dq_ref[gl] += _dot3(dy2, x2, TN)
            da_ref[gl, 0:1, :] += jnp.sum(xv * gv, axis=0, keepdims=True)
            da_ref[gl, 1:2, :] += jnp.sum(xv * pltpu.roll(gv, SSM_STATE, 1), axis=0, keepdims=True)
        _groups_tile(dus, du_ref, ncb)

    return _pcall(
        body, name=name,
        out_shape=[SDS((seqlen, d), F32)] + [SDS((ng, SSM_W, SSM_W), F32)] * 3 + [SDS((ng, 2, SSM_W), F32)],
        grid=grid, in_specs=[act, act, state, state, mats, mats],
        out_specs=[act, mats, mats, mats, pl.BlockSpec((_SSM_GT, 2, SSM_W), lambda j, r: (j, 0, 0))],
        vmem=48 << 20, comm=comm)(hn, dy, xp, gs, mmat, pmat)


def _ssm_carry(s, a1, a2, reverse, name):
    nc, w = s.shape
    tc = _tile(nc, 256)
    nblk = nc // tc
    sub = 8

    def body(s_ref, a1_ref, a2_ref, o_ref, st_ref, sw_ref):
        @pl.when(pl.program_id(0) == 0)
        def _():
            st_ref[...] = jnp.zeros_like(st_ref)
            sw_ref[...] = jnp.zeros_like(sw_ref)
        a1v = jnp.broadcast_to(a1_ref[...], (sub, w))
        a2v = jnp.broadcast_to(a2_ref[...], (sub, w))
        first = lax.broadcasted_iota(jnp.int32, (sub, w), 1) % SSM_W < SSM_STATE
        row = lax.broadcasted_iota(jnp.int32, (sub, w), 0)

        def step(t, carry):
            x, xs = carry
            tt = (tc // sub - 1 - t) if reverse else t
            base = pl.multiple_of(tt * sub, sub)
            blk = s_ref[pl.ds(base, sub), :]
            blks = jnp.where(first, pltpu.roll(blk, w - SSM_STATE, 1), pltpu.roll(blk, SSM_STATE, 1))
            out = jnp.zeros((sub, w), F32)
            for r in (range(sub - 1, -1, -1) if reverse else range(sub)):
                out = jnp.where(row == r, x, out)
                sr = jnp.broadcast_to(blk[r:r + 1, :], (sub, w))
                ssr = jnp.broadcast_to(blks[r:r + 1, :], (sub, w))
                x, xs = a1v * x + a2v * xs + sr, a1v * xs - a2v * x + ssr
            o_ref[pl.ds(base, sub), :] = out
            return x, xs

        x, xs = lax.fori_loop(0, tc // sub, step, (st_ref[...], sw_ref[...]))
        st_ref[...] = x
        sw_ref[...] = xs

    imap = (lambda i: (nblk - 1 - i, 0)) if reverse else (lambda i: (i, 0))
    cst = pl.BlockSpec((1, w), lambda i: (0, 0))
    return _pcall(body, name=name, out_shape=SDS((nc, w), F32), grid=(nblk,),
                  in_specs=[pl.BlockSpec((tc, w), imap), cst, cst], out_specs=pl.BlockSpec((tc, w), imap),
                  scratch=[pltpu.VMEM((sub, w), F32)] * 2,
                  vmem=_vmem_limit([((tc, w), F32, 4)], extra=8 << 20))(s, a1, a2)


def _ssm_rows(atr, ati, conj):
    ng = atr.shape[0]
    ai = -ati if conj else ati
    a1 = jnp.concatenate([atr, atr], axis=2).reshape(1, ng * SSM_W)
    a2 = jnp.concatenate([-ai, ai], axis=2).reshape(1, ng * SSM_W)
    return a1, a2


def _peers():
    x, y, c = (lax.axis_index(a) for a in AXES)
    me = 4 * x + 2 * y + c
    peers = []
    for dx, dy, dc in [(0, 0, 1), (0, 1, 0), (0, 1, 1), (1, 0, 0), (1, 0, 1), (1, 1, 0), (1, 1, 1)]:
        px, py, pc = (1 - x) if dx else x, (1 - y) if dy else y, (1 - c) if dc else c
        peers.append(((px, py, pc), 4 * px + 2 * py + pc))
    return me, peers


class _Exchange:
    def __init__(self, arrs, scatter, layers=None):
        self.arrs = list(arrs)
        self.scatter = scatter
        self.layers = list(layers) if layers is not None else [None] * len(self.arrs)

    def out_shape(self):
        shapes = []
        for arr, layer in zip(self.arrs, self.layers):
            block = arr.shape[1:] if (self.scatter or layer is not None) else arr.shape
            shapes.append(SDS((NDEV,) + tuple(block), arr.dtype))
        return shapes

    def semaphores(self):
        n = len(self.arrs)
        return [pltpu.SemaphoreType.DMA((n * (NDEV - 1),)), pltpu.SemaphoreType.DMA((n * (NDEV - 1),)),
                pltpu.SemaphoreType.DMA((n,))]

    def _src(self, ref, a, block):
        if self.scatter:
            return ref.at[block]
        return ref if self.layers[a] is None else ref.at[self.layers[a]]

    def _remote(self, xin, xout, sems, a, k, peer, landing):
        pid, pidx = peer
        slot = a * (NDEV - 1) + k
        return pltpu.make_async_remote_copy(
            src_ref=self._src(xin[a], a, pidx), dst_ref=xout[a].at[landing],
            send_sem=sems[0].at[slot], recv_sem=sems[1].at[slot], device_id=pid, device_id_type=MESH)

    def _local(self, xin, xout, sems, a, me):
        return pltpu.make_async_copy(self._src(xin[a], a, me), xout[a].at[me], sems[2].at[a])

    def start(self, xin, xout, sems):
        me, peers = _peers()
        for a in range(len(self.arrs)):
            self._local(xin, xout, sems, a, me).start()
        for k, peer in enumerate(peers):
            for a in range(len(self.arrs)):
                self._remote(xin, xout, sems, a, k, peer, me).start()

    def wait(self, xin, xout, sems):
        me, peers = _peers()
        for a in range(len(self.arrs)):
            self._local(xin, xout, sems, a, me).wait()
        for k, peer in enumerate(peers):
            for a in range(len(self.arrs)):
                cp = self._remote(xin, xout, sems, a, k, peer, peer[1])
                cp.wait_send()
                cp.wait_recv()


def _exchange(arrs, scatter, name, layers=None):
    comm = _Exchange(arrs, scatter, layers)
    n = len(comm.arrs)

    def body(*refs):
        xin, xout, sems = refs[:n], refs[n:2 * n], refs[2 * n:]
        comm.start(xin, xout, sems)
        comm.wait(xin, xout, sems)

    hbm = pl.BlockSpec(memory_space=pl.ANY)
    return pl.pallas_call(
        body, out_shape=comm.out_shape(), in_specs=[hbm] * n, out_specs=[hbm] * n,
        scratch_shapes=comm.semaphores(), name=name, interpret=False)(*comm.arrs)


def _adamw(parts, w, m, v, name):
    rows, cols = w.shape
    tr = _tile(rows, max(8, (1 << 17) // cols))
    c1 = 1.0 - ADAM_B1 ** ADAM_STEP
    c2 = 1.0 - ADAM_B2 ** ADAM_STEP

    def body(p_ref, w_ref, m_ref, v_ref, g_ref, d_ref, nm_ref, nv_ref):
        g = p_ref[0]
        for j in range(1, NDEV):
            g = g + p_ref[j]
        mm = ADAM_B1 * m_ref[...] + (1.0 - ADAM_B1) * g
        vv = ADAM_B2 * v_ref[...] + (1.0 - ADAM_B2) * (g * g)
        g_ref[...] = g
        nm_ref[...] = mm
        nv_ref[...] = vv
        d_ref[...] = -ADAM_LR * ((mm / c1) / (jnp.sqrt(vv / c2) + ADAM_EPS) + ADAM_WD * w_ref[...])

    spec = pl.BlockSpec((tr, cols), lambda i: (i, 0))
    return _pcall(
        body, name=name, out_shape=[SDS((rows, cols), F32)] * 4, grid=(rows // tr,),
        in_specs=[pl.BlockSpec((NDEV, tr, cols), lambda i: (0, i, 0)), spec, spec, spec], out_specs=[spec] * 4,
        vmem=_vmem_limit([((NDEV + 7, tr, cols), F32, 2)]))(parts, w, m, v)


def kernel(x, p, norm_mix, ssm_lambda_re, ssm_lambda_im, ssm_log_dt, ssm_b_re, ssm_b_im, ssm_c_re, ssm_c_im, ssm_d, ssm_w_glu, kv_norm, w_k, w_v, w_q, attn_sinks, w_o, norm_mlp, w_up, w_down, norm_ple, w_ple_gate, w_ple_proj, norm_final, loss_target, m_norm_mix, m_ssm_lambda_re, m_ssm_lambda_im, m_ssm_log_dt, m_ssm_b_re, m_ssm_b_im, m_ssm_c_re, m_ssm_c_im, m_ssm_d, m_ssm_w_glu, m_kv_norm, m_w_k, m_w_v, m_w_q, m_attn_sinks, m_w_o, m_norm_mlp, m_w_up, m_w_down, m_norm_ple, m_w_ple_gate, m_w_ple_proj, m_norm_final, v_norm_mix, v_ssm_lambda_re, v_ssm_lambda_im, v_ssm_log_dt, v_ssm_b_re, v_ssm_b_im, v_ssm_c_re, v_ssm_c_im, v_ssm_d, v_ssm_w_glu, v_kv_norm, v_w_k, v_w_v, v_w_q, v_attn_sinks, v_w_o, v_norm_mlp, v_w_up, v_w_down, v_norm_ple, v_w_ple_gate, v_w_ple_proj, v_norm_final):
    names = ['norm_mix', 'ssm_lambda_re', 'ssm_lambda_im', 'ssm_log_dt', 'ssm_b_re', 'ssm_b_im', 'ssm_c_re',
             'ssm_c_im', 'ssm_d', 'ssm_w_glu', 'kv_norm', 'w_k', 'w_v', 'w_q', 'attn_sinks', 'w_o', 'norm_mlp',
             'w_up', 'w_down', 'norm_ple', 'w_ple_gate', 'w_ple_proj', 'norm_final']
    weights = dict(zip(names, (norm_mix, ssm_lambda_re, ssm_lambda_im, ssm_log_dt, ssm_b_re, ssm_b_im, ssm_c_re,
                               ssm_c_im, ssm_d, ssm_w_glu, kv_norm, w_k, w_v, w_q, attn_sinks, w_o, norm_mlp,
                               w_up, w_down, norm_ple, w_ple_gate, w_ple_proj, norm_final)))
    mom1 = dict(zip(names, (m_norm_mix, m_ssm_lambda_re, m_ssm_lambda_im, m_ssm_log_dt, m_ssm_b_re, m_ssm_b_im,
                            m_ssm_c_re, m_ssm_c_im, m_ssm_d, m_ssm_w_glu, m_kv_norm, m_w_k, m_w_v, m_w_q,
                            m_attn_sinks, m_w_o, m_norm_mlp, m_w_up, m_w_down, m_norm_ple, m_w_ple_gate,
                            m_w_ple_proj, m_norm_final)))
    mom2 = dict(zip(names, (v_norm_mix, v_ssm_lambda_re, v_ssm_lambda_im, v_ssm_log_dt, v_ssm_b_re, v_ssm_b_im,
                            v_ssm_c_re, v_ssm_c_im, v_ssm_d, v_ssm_w_glu, v_kv_norm, v_w_k, v_w_v, v_w_q,
                            v_attn_sinks, v_w_o, v_norm_mlp, v_w_up, v_w_down, v_norm_ple, v_w_ple_gate,
                            v_w_ple_proj, v_norm_final)))

    seqlen, d = x.shape[1], x.shape[2]
    depth = w_up.shape[0]
    n_ssm = ssm_w_glu.shape[0]
    n_att = w_q.shape[0]
    ng = d // SSM_GROUP
    nh = d // HEAD_DIM
    h0 = x[0]
    tgt = loss_target[0]
    tabs = _rope_tables(seqlen)

    sharded = ['w_up', 'w_down', 'w_ple_gate', 'w_ple_proj', 'ssm_w_glu', 'w_q', 'w_o', 'w_k', 'w_v']
    shards = {k: weights[k].astype(BF16) for k in sharded}
    shards['ssm_d'] = ssm_d
    dkv = w_k.shape[1]

    def layer_set(i):
        keys = [('w_up', i), ('w_down', i), ('w_ple_gate', i), ('w_ple_proj', i)]
        keys += [('ssm_w_glu', i), ('ssm_d', i)] if i < n_ssm else [('w_q', i - n_ssm), ('w_o', i - n_ssm)]
        if i == n_ssm:
            keys += [('w_k', None), ('w_v', None)]
        return keys

    def gather_of(i, only=None):
        keys = [kl for kl in layer_set(i) if only is None or kl[0] in only]
        return keys, _Exchange([shards[k] for k, _ in keys], False, [l for _, l in keys])

    def as_operands(keys, blocks):
        w = {}
        for (k, _), g in zip(keys, blocks):
            if k == 'ssm_d':
                w[k] = g.reshape(d)
            elif k in ('w_ple_proj', 'ssm_w_glu', 'w_up'):
                w[k] = g.transpose(1, 0, 2).reshape(g.shape[1], NDEV * g.shape[2])
                w[k + '_t'] = g.transpose(0, 2, 1).reshape(NDEV * g.shape[2], g.shape[1])
            else:
                w[k] = g.reshape(NDEV * g.shape[1], g.shape[2])
                w[k + '_t'] = g.transpose(2, 0, 1).reshape(g.shape[2], NDEV * g.shape[1])
        return w

    lw = {}

    def ssm_params(i):
        n = SSM_STATE
        return (ssm_lambda_re[i].reshape(ng, 1, n), ssm_lambda_im[i].reshape(ng, 1, n),
                ssm_log_dt[i].reshape(ng, 1, 1), jnp.swapaxes(ssm_b_re[i], 1, 2), jnp.swapaxes(ssm_b_im[i], 1, 2),
                ssm_c_re[i], ssm_c_im[i])

    h = h0
    h_in, h_a, h_b, acts = [], [], [], []
    ssm_saved, att_saved = {}, {}
    k_sh = v_sh = None
    for i in range(depth):
        h_in.append(h)
        if i < n_ssm:
            first = gather_of(0, ('ssm_w_glu', 'ssm_d')) if i == 0 else None
            hn, *got = _norm_fwd(h, norm_mix[i], f"norm_mix_fwd{i}", comm=first and first[1])
            if first:
                lw[0] = as_operands(first[0], got)
            mats = _ssm_prep(ssm_params(i), f"ssm_prep{i}")
            mmat, atr, ati = mats[0], mats[5], mats[6]
            pmat = jnp.concatenate([mats[1], mats[2]], axis=2)
            qt = jnp.concatenate([mats[3], mats[4]], axis=2)
            s_in = _ssm_state_in(hn, pmat, f"ssm_state_in{i}")
            xp = _ssm_carry(s_in, *_ssm_rows(atr, ati, False), False, f"ssm_carry_fwd{i}")
            rest = gather_of(0, ('w_up', 'w_down', 'w_ple_gate', 'w_ple_proj')) if i == 0 else None
            y, *got = _ssm_out(hn, xp, mmat, qt, f"ssm_out{i}", comm=rest and rest[1])
            if rest:
                lw[0].update(as_operands(rest[0], got))
            ha = _glu_fwd(y, hn, h, lw[i]['ssm_d'], lw[i]['ssm_w_glu'], f"glu_fwd{i}")
            ssm_saved[i] = (hn, mmat, pmat, qt, atr, ati, xp, y)
        else:
            j = i - n_ssm
            q = _q_fwd(h, norm_mix[i], lw[i]['w_q'], tabs, f"q_fwd{j}")
            o = _attn_fwd(q, k_sh, v_sh, attn_sinks[j], f"attn_fwd{j}")
            ha = _lin_res(h, o, lw[i]['w_o'], f"attn_out{j}")
            att_saved[j] = (q, o)
        h_a.append(ha)
        nxt = gather_of(i + 1) if i + 1 < depth else None
        res = _mlp_fwd(ha, norm_mlp[i], lw[i]['w_up'], lw[i]['w_down'], f"mlp_fwd{i}", comm=nxt and nxt[1])
        hb = res[0]
        acts.append(res[1])
        if nxt:
            lw[i + 1] = as_operands(nxt[0], res[2:])
        h_b.append(hb)
        h = _ple_fwd(hb, p[i, 0], norm_ple[i], lw[i]['w_ple_gate'], lw[i]['w_ple_proj'], f"ple_fwd{i}")
        if i == n_ssm - 1:
            k_sh, v_sh = _kv_fwd(h, kv_norm, lw[n_ssm]['w_k'], lw[n_ssm]['w_v'], tabs, "kv_fwd")
    h_kv = h_in[n_ssm] if n_ssm < depth else h
    dh, loss_row, g_norm_final = _loss_bwd(h, norm_final, tgt, "loss_bwd")
    loss = lax.psum(loss_row[0, 0], AXES)

    g_norm_mix, g_norm_mlp, g_norm_ple = [None] * depth, [None] * depth, [None] * depth
    g_ssm, g_sinks = [None] * n_ssm, [None] * n_att
    g_kv_norm = None
    dks, dvs = [], []
    recv = {}
    pending = None
    for i in range(depth - 1, -1, -1):
        gl = {}
        if i == n_ssm - 1:
            dh, dkp, dvb, hkb, g_kv_norm = _kv_bwd(dks, dvs, h_kv, kv_norm, dh, lw[n_ssm]['w_k_t'],
                                                   lw[n_ssm]['w_v_t'], tabs, "kv_bwd")
            gl['w_k', None] = _atb(hkb, dkp, False, "grad_w_k")
            gl['w_v', None] = _atb(hkb, dvb, False, "grad_w_v")
        dhb, dz, nb16, dpp, g_norm_ple[i] = _ple_bwd(h_b[i], p[i, 0], dh, norm_ple[i], lw[i]['w_ple_gate'],
                                                     lw[i]['w_ple_gate_t'], lw[i]['w_ple_proj'], f"ple_bwd{i}")
        gl['w_ple_gate', i] = _atb(nb16, dz, False, f"grad_w_ple_gate{i}")
        gl['w_ple_proj', i] = _atb(p[i, 0], dpp, True, f"grad_w_ple_proj{i}")
        res = _mlp_bwd(h_a[i], acts[i], dhb, norm_mlp[i], lw[i]['w_up_t'], lw[i]['w_down_t'], f"mlp_bwd{i}",
                       comm=pending and pending[1])
        dha, hmb, da, g_norm_mlp[i] = res[:4]
        if pending:
            recv.update(zip(pending[0], res[4:]))
            pending = None
        gl['w_up', i] = _atb(hmb, da, True, f"grad_w_up{i}")
        gl['w_down', i] = _atb(acts[i], dhb, False, f"grad_w_down{i}")
        if i >= n_ssm:
            j = i - n_ssm
            q, o = att_saved[j]
            do = _lin_bf16(dha, lw[i]['w_o_t'], f"attn_out_bwd{j}")
            gl['w_o', j] = _atb(o, dha, False, f"grad_w_o{j}")
            dq, dk_j, dv_j, dsink = _attn_bwd(q, k_sh, v_sh, do, attn_sinks[j], f"attn_bwd{j}")
            dks.append(dk_j)
            dvs.append(dv_j)
            g_sinks[j] = dsink[:, 0]
            dh, dqp, hnb, g_norm_mix[i] = _q_bwd(dq, h_in[i], norm_mix[i], dha, lw[i]['w_q_t'], tabs, f"q_bwd{j}")
            gl['w_q', j] = _atb(hnb, dqp, False, f"grad_w_q{j}")
        else:
            hn, mmat, pmat, qt, atr, ati, xp, y = ssm_saved[i]
            dyy, dhn_d, geb, dab, g_dskip = _glu_bwd(y, hn, dha, lw[i]['ssm_d'], lw[i]['ssm_w_glu'],
                                                     lw[i]['ssm_w_glu_t'], f"glu_bwd{i}")
            gl['ssm_d', i] = g_dskip.reshape(NDEV, d // NDEV)
            gl['ssm_w_glu', i] = _atb(geb, dab, True, f"grad_ssm_w_glu{i}")
            dxp = _ssm_dstate(dyy, qt, f"ssm_dstate{i}")
            gs = _ssm_carry(dxp, *_ssm_rows(atr, ati, True), True, f"ssm_carry_bwd{i}")
            last = (list(gl), _Exchange(list(gl.values()), True)) if i == 0 else None
            du, dm, dp, dqt, da_raw, *got = _ssm_bwd(hn, dyy, xp, gs, mmat, pmat, f"ssm_bwd{i}",
                                                     comm=last and last[1])
            if last:
                recv.update(zip(last[0], got))
                gl = {}
            n = SSM_STATE
            cots = (dm, dp[:, :, :n], dp[:, :, n:], dqt[:, :, :n], dqt[:, :, n:],
                    (da_raw[:, 0:1, :n] + da_raw[:, 0:1, n:]), (da_raw[:, 1:2, :n] - da_raw[:, 1:2, n:]))
            g_ssm[i] = _ssm_prep_vjp(ssm_params(i), cots, f"ssm_prep_vjp{i}")
            dh, g_norm_mix[i] = _norm_bwd(h_in[i], norm_mix[i], dhn_d, du, dha, f"norm_mix_bwd{i}")
        if gl:
            pending = (list(gl), _Exchange(list(gl.values()), True))
    grad_x = dh[None]
    if pending:
        recv.update(zip(pending[0], _exchange(pending[1].arrs, True, "scatter_grads0")))

    out_g, out_d, out_m, out_v = {}, {}, {}, {}
    updated = {}
    for (k, l), parts in recv.items():
        pick = (lambda t: t) if l is None else (lambda t: t[l])
        shp = pick(weights[k]).shape
        r2 = (math.prod(shp[:-1]), shp[-1])
        res = _adamw(parts.reshape((NDEV,) + r2), pick(weights[k]).reshape(r2), pick(mom1[k]).reshape(r2),
                     pick(mom2[k]).reshape(r2), f"adamw_{k}{'' if l is None else l}")
        updated.setdefault(k, {})[l] = [t.reshape(shp) for t in res]
    for k, by_layer in updated.items():
        for n, dst in enumerate((out_g, out_d, out_m, out_v)):
            dst[k] = by_layer[None][n] if None in by_layer else jnp.stack([by_layer[l][n] for l in sorted(by_layer)])

    def ssm_grad(idx, unswap=False):
        g = jnp.stack([g_ssm[i][idx] for i in range(n_ssm)])
        return jnp.swapaxes(g, 2, 3) if unswap else g

    small = {'norm_mix': jnp.concatenate(g_norm_mix, axis=0),
             'ssm_lambda_re': ssm_grad(0), 'ssm_lambda_im': ssm_grad(1), 'ssm_log_dt': ssm_grad(2),
             'ssm_b_re': ssm_grad(3, True), 'ssm_b_im': ssm_grad(4, True),
             'ssm_c_re': ssm_grad(5), 'ssm_c_im': ssm_grad(6),
             'kv_norm': g_kv_norm, 'attn_sinks': jnp.stack(g_sinks),
             'norm_mlp': jnp.concatenate(g_norm_mlp, axis=0), 'norm_ple': jnp.concatenate(g_norm_ple, axis=0),
             'norm_final': g_norm_final}
    snames = list(small)
    sizes = [weights[k].size for k in snames]
    total = sum(sizes)
    lanes = 128
    padded = -(-total // (512 * lanes)) * (512 * lanes)

    def flat(parts):
        v = jnp.concatenate([t.reshape(-1) for t in parts] + [jnp.zeros((padded - total,), F32)])
        return v.reshape(padded // lanes, lanes)

    parts = _exchange([flat([small[k] for k in snames])], False, "gather_small_grads")[0]
    res = _adamw(parts, flat([weights[k] for k in snames]), flat([mom1[k] for k in snames]),
                 flat([mom2[k] for k in snames]), "adamw_small")
    off = 0
    for k, sz in zip(snames, sizes):
        for dst, t in zip((out_g, out_d, out_m, out_v), res):
            dst[k] = t.reshape(-1)[off:off + sz].reshape(weights[k].shape)
        off += sz

    return (loss, grad_x, *[out_g[k] for k in names], *[out_d[k] for k in names],
            *[out_m[k] for k in names], *[out_v[k] for k in names])
```

```python
import functools
import math

import jax
import jax.numpy as jnp
from jax import lax
from jax.experimental import pallas as pl
from jax.experimental.pallas import tpu as pltpu

F32 = jnp.float32
BF16 = jnp.bfloat16
SDS = jax.ShapeDtypeStruct
MESH = pl.DeviceIdType.MESH
AXES = ("x", "y", "c")
NDEV = 8

RMS_EPS = 1e-6
SSM_GROUP = 16
SSM_STATE = 64
SSM_T = 8
SSM_W = SSM_T * SSM_GROUP
HEAD_DIM = 64
GQA_GROUP = 4
ATTN_BLOCK = 128
ROT_DIM = 16
ROPE_THETA = 500000.0
NEG_INF = -1e30
ADAM_LR, ADAM_B1, ADAM_B2, ADAM_EPS, ADAM_WD, ADAM_STEP = 0.001, 0.9, 0.999, 1e-08, 0.01, 10

VMEM_CAP = 56 * 1024 * 1024
HI = lax.Precision.HIGHEST

NN = ((1,), (0,))
NT = ((1,), (1,))
TN = ((0,), (0,))


def _dot(a, b, dims=NN, precision=None):
    return lax.dot_general(a, b, (dims, ((), ())), preferred_element_type=F32, precision=precision)


def _split(a):
    if isinstance(a, tuple):
        return a
    hi = a.astype(BF16)
    return hi, (a - hi.astype(F32)).astype(BF16)


def _dot3(a, b, dims=NN):
    (ah, al), (bh, bl) = _split(a), _split(b)
    return _dot(ah, bh, dims) + (_dot(ah, bl, dims) + _dot(al, bh, dims))


@jax.custom_vjp
def _dot3_nt(a, b):
    return _dot3(a, b, NT)


def _dot3_nt_fwd(a, b):
    return _dot3(a, b, NT), (a, b)


def _dot3_nt_bwd(res, g):
    a, b = res
    return _dot3(g, b, NN), _dot3(g, a, TN)


_dot3_nt.defvjp(_dot3_nt_fwd, _dot3_nt_bwd)


def _tile(n, pref):
    t = min(n, pref)
    while n % t:
        t //= 2
    return t


def _nbytes(shape, dtype):
    return math.prod(s for s in shape if s is not None) * jnp.dtype(dtype).itemsize


def _vmem_limit(blocks, extra=0):
    need = sum(_nbytes(s, d) * n for s, d, n in blocks) + extra + (4 << 20)
    return int(min(VMEM_CAP, max(need, 16 << 20)))


def _pcall(body, *, name, out_shape, grid, in_specs, out_specs, scratch=(), vmem=None, comm=None):
    single = not isinstance(out_shape, (list, tuple))
    out_shape = [out_shape] if single else list(out_shape)
    out_specs = [out_specs] if single else list(out_specs)
    in_specs, scratch = list(in_specs), list(scratch)
    if comm is not None:
        n_in, n_out, n_scr, nx = len(in_specs), len(out_specs), len(scratch), len(comm.arrs)
        hbm = pl.BlockSpec(memory_space=pl.ANY)
        in_specs = in_specs + [hbm] * nx
        out_specs = out_specs + [hbm] * nx
        out_shape = out_shape + comm.out_shape()
        scratch = scratch + comm.semaphores()
        inner = body

        def body(*refs):
            ins, xin, rest = refs[:n_in], refs[n_in:n_in + nx], refs[n_in + nx:]
            outs, xout, rest = rest[:n_out], rest[n_out:n_out + nx], rest[n_out + nx:]
            scr, sems = rest[:n_scr], rest[n_scr:]
            first = functools.reduce(jnp.logical_and, [pl.program_id(a) == 0 for a in range(len(grid))])
            last = functools.reduce(jnp.logical_and, [pl.program_id(a) == g - 1 for a, g in enumerate(grid)])

            @pl.when(first)
            def _():
                comm.start(xin, xout, sems)
            inner(*ins, *outs, *scr)

            @pl.when(last)
            def _():
                comm.wait(xin, xout, sems)

    call = pl.pallas_call(
        body, out_shape=out_shape[0] if single and comm is None else out_shape, grid=grid, in_specs=in_specs,
        out_specs=out_specs[0] if single and comm is None else out_specs, scratch_shapes=scratch, name=name,
        compiler_params=pltpu.CompilerParams(
            dimension_semantics=("arbitrary",) * len(grid), vmem_limit_bytes=vmem),
        interpret=False)
    if comm is None:
        return call
    return lambda *args: call(*args, *comm.arrs)


def _rms(x, g):
    r = lax.rsqrt(jnp.mean(x * x, axis=-1, keepdims=True) + RMS_EPS)
    return x * r * g, r


def _rms_bwd(x, g, r, dy):
    xh = x * r
    dyg = dy * g
    dx = r * (dyg - xh * jnp.mean(dyg * xh, axis=-1, keepdims=True))
    return dx, jnp.sum(dy * xh, axis=0, keepdims=True)


_GELU_C = math.sqrt(2.0 / math.pi)


def _gelu_parts(x):
    t = jnp.tanh(_GELU_C * (x + 0.044715 * x * x * x))
    return 0.5 * x * (1.0 + t), t


def _gelu_grad(x, t):
    return 0.5 * (1.0 + t) + 0.5 * x * (1.0 - t * t) * _GELU_C * (1.0 + 3 * 0.044715 * x * x)


def _rope_tables(seqlen):
    half = ROT_DIM // 2
    inv = ROPE_THETA ** (-jnp.arange(0, ROT_DIM, 2, dtype=F32) / ROT_DIM)
    ang = jnp.arange(seqlen, dtype=jnp.int32).astype(F32)[:, None] * inv[None, :]
    cos, sin = jnp.cos(ang), jnp.sin(ang)
    zeros = jnp.zeros((seqlen, HEAD_DIM - ROT_DIM), F32)
    zh = jnp.zeros((seqlen, half), F32)
    c = jnp.concatenate([cos, cos, zeros + 1.0], axis=1)
    sa = jnp.concatenate([zh, sin, zeros], axis=1)
    sb = jnp.concatenate([-sin, zh, zeros], axis=1)
    return tuple(jnp.tile(t, (1, 128 // HEAD_DIM)) for t in (c, sa, sb))


def _rope(x, c, sa, sb):
    w = x.shape[1]
    reps = w // 128
    half = ROT_DIM // 2
    return (x * jnp.tile(c, (1, reps)) + pltpu.roll(x, half, 1) * jnp.tile(sa, (1, reps))
            + pltpu.roll(x, w - half, 1) * jnp.tile(sb, (1, reps)))


def _rope_bwd(dy, c, sa, sb):
    w = dy.shape[1]
    reps = w // 128
    half = ROT_DIM // 2
    return (dy * jnp.tile(c, (1, reps)) + pltpu.roll(dy * jnp.tile(sa, (1, reps)), w - half, 1)
            + pltpu.roll(dy * jnp.tile(sb, (1, reps)), half, 1))


def _rspec(tm, c):
    return pl.BlockSpec((tm, c), lambda i: (i, 0))


def _cspec(shape, idx=None):
    idx = tuple(idx) if idx is not None else (0,) * len(shape)
    return pl.BlockSpec(tuple(shape), lambda i: idx, pipeline_mode=pl.Buffered(1))


def _rowcall(body, name, seqlen, tm, rows_in, consts_in, rows_out, acc_out=(), extra_vmem=0, comm=None):
    in_specs = [_rspec(tm, a.shape[1]) for a in rows_in] + [_cspec(bs, ix) for _, bs, ix in consts_in]
    out_shape = [SDS((seqlen, c), d) for c, d in rows_out] + [SDS(s, F32) for s in acc_out]
    out_specs = [_rspec(tm, c) for c, _ in rows_out] + [pl.BlockSpec(s, lambda i: (0, 0)) for s in acc_out]
    blocks = ([((tm, a.shape[1]), a.dtype, 2) for a in rows_in] + [(bs, a.dtype, 1) for a, bs, _ in consts_in]
              + [((tm, c), d, 2) for c, d in rows_out])
    temporaries = 12 * tm * rows_in[0].shape[1] * 4
    return _pcall(body, name=name, out_shape=out_shape, grid=(seqlen // tm,), in_specs=in_specs,
                  out_specs=out_specs, vmem=_vmem_limit(blocks, extra_vmem + temporaries), comm=comm)(
                      *rows_in, *[a for a, _, _ in consts_in])


def _whole(a):
    return (a, a.shape, None)


def _norm_fwd(h, g, name, comm=None):
    seqlen, d = h.shape
    tm = _tile(seqlen, 1024)

    def body(h_ref, g_ref, o_ref):
        o_ref[...] = _rms(h_ref[...], g_ref[...])[0]

    return _rowcall(body, name, seqlen, tm, [h], [_whole(g.reshape(1, d))], [(d, F32)], comm=comm)


def _norm_bwd(h, g, dy1, dy2, dres, name, comm=None):
    seqlen, d = h.shape
    tm = _tile(seqlen, 512)

    def body(h_ref, dy1_ref, dy2_ref, dres_ref, g_ref, dh_ref, dg_ref):
        @pl.when(pl.program_id(0) == 0)
        def _():
            dg_ref[...] = jnp.zeros_like(dg_ref)
        x = h_ref[...]
        gv = g_ref[...]
        _, r = _rms(x, gv)
        dx, dg = _rms_bwd(x, gv, r, dy1_ref[...] + dy2_ref[...])
        dh_ref[...] = dres_ref[...] + dx
        dg_ref[...] += dg

    return _rowcall(body, name, seqlen, tm, [h, dy1, dy2, dres], [_whole(g.reshape(1, d))], [(d, F32)], [(1, d)],
                    comm=comm)


def _glu_fwd(y, hn, h, dskip, wglu, name):
    seqlen, d = h.shape
    tm = _tile(seqlen, 512)

    def body(y_ref, hn_ref, h_ref, d_ref, w_ref, o_ref):
        yy = y_ref[...] + d_ref[...] * hn_ref[...]
        ge, _ = _gelu_parts(yy)
        ab = _dot(ge.astype(BF16), w_ref[...])
        o_ref[...] = h_ref[...] + ab[:, :d] * jax.nn.sigmoid(ab[:, d:])

    return _rowcall(body, name, seqlen, tm, [y, hn, h], [_whole(dskip.reshape(1, d)), _whole(wglu)], [(d, F32)],
                    extra_vmem=tm * d * 4 * 6)[0]


def _glu_bwd(y, hn, dmix, dskip, wglu, wglu_t, name):
    seqlen, d = hn.shape
    tm = _tile(seqlen, 512)

    def body(y_ref, hn_ref, dm_ref, d_ref, w_ref, wt_ref, dyy_ref, dhn_ref, ge_ref, dab_ref, dd_ref):
        @pl.when(pl.program_id(0) == 0)
        def _():
            dd_ref[...] = jnp.zeros_like(dd_ref)
        hn_v = hn_ref[...]
        dsk = d_ref[...]
        yy = y_ref[...] + dsk * hn_v
        ge, t = _gelu_parts(yy)
        geb = ge.astype(BF16)
        ab = _dot(geb, w_ref[...])
        a = ab[:, :d]
        sg = jax.nn.sigmoid(ab[:, d:])
        dm = dm_ref[...]
        dab_ref[:, :d] = (dm * sg).astype(BF16)
        dab_ref[:, d:] = (dm * a * sg * (1.0 - sg)).astype(BF16)
        dge = _dot(dab_ref[...], wt_ref[...])
        dyy = dge * _gelu_grad(yy, t)
        dyy_ref[...] = dyy
        dhn_ref[...] = dyy * dsk
        ge_ref[...] = geb
        dd_ref[...] += jnp.sum(dyy * hn_v, axis=0, keepdims=True)

    return _rowcall(body, name, seqlen, tm, [y, hn, dmix],
                    [_whole(dskip.reshape(1, d)), _whole(wglu), _whole(wglu_t)],
                    [(d, F32), (d, F32), (d, BF16), (2 * d, BF16)], [(1, d)], extra_vmem=tm * d * 4 * 8)


def _q_fwd(h, g, wq, tabs, name):
    seqlen, d = h.shape
    tm = _tile(seqlen, 512)

    def body(h_ref, c_ref, sa_ref, sb_ref, g_ref, w_ref, q_ref):
        hn, _ = _rms(h_ref[...], g_ref[...])
        qp = _dot(hn.astype(BF16), w_ref[...])
        q_ref[...] = _rope(qp, c_ref[...], sa_ref[...], sb_ref[...]).astype(BF16)

    return _rowcall(body, name, seqlen, tm, [h, *tabs], [_whole(g.reshape(1, d)), _whole(wq)], [(d, BF16)],
                    extra_vmem=tm * d * 4 * 6)[0]


def _q_bwd(dq, h, g, dres, wq, tabs, name):
    seqlen, d = h.shape
    tm = _tile(seqlen, 512)

    def body(dq_ref, h_ref, dres_ref, c_ref, sa_ref, sb_ref, g_ref, w_ref, dh_ref, dqp_ref, hn_ref, dg_ref):
        @pl.when(pl.program_id(0) == 0)
        def _():
            dg_ref[...] = jnp.zeros_like(dg_ref)
        dqp = _rope_bwd(dq_ref[...], c_ref[...], sa_ref[...], sb_ref[...]).astype(BF16)
        x = h_ref[...]
        gv = g_ref[...]
        hn, r = _rms(x, gv)
        dhn = _dot(dqp, w_ref[...])
        dx, dg = _rms_bwd(x, gv, r, dhn)
        dh_ref[...] = dres_ref[...] + dx
        dqp_ref[...] = dqp
        hn_ref[...] = hn.astype(BF16)
        dg_ref[...] += dg

    return _rowcall(body, name, seqlen, tm, [dq, h, dres, *tabs], [_whole(g.reshape(1, d)), _whole(wq)],
                    [(d, F32), (d, BF16), (d, BF16)], [(1, d)], extra_vmem=tm * d * 4 * 6)


def _kv_fwd(h, g, wk, wv, tabs, name):
    seqlen, d = h.shape
    dk = wk.shape[1]
    tm = _tile(seqlen, 512)

    def body(h_ref, c_ref, sa_ref, sb_ref, g_ref, wk_ref, wv_ref, k_ref, v_ref):
        hk = _rms(h_ref[...], g_ref[...])[0].astype(BF16)
        k_ref[...] = _rope(_dot(hk, wk_ref[...]), c_ref[...], sa_ref[...], sb_ref[...]).astype(BF16)
        v_ref[...] = _dot(hk, wv_ref[...]).astype(BF16)

    return _rowcall(body, name, seqlen, tm, [h, *tabs], [_whole(g.reshape(1, d)), _whole(wk), _whole(wv)],
                    [(dk, BF16), (dk, BF16)], extra_vmem=tm * d * 4 * 4)


def _kv_bwd(dks, dvs, h, g, dres, wk, wv, tabs, name):
    seqlen, d = h.shape
    dkw = wk.shape[0]
    tm = _tile(seqlen, 512)

    def body(dk0_ref, dk1_ref, dv0_ref, dv1_ref, h_ref, dres_ref, c_ref, sa_ref, sb_ref, g_ref, wk_ref, wv_ref,
             dh_ref, dkp_ref, dvb_ref, hk_ref, dg_ref):
        @pl.when(pl.program_id(0) == 0)
        def _():
            dg_ref[...] = jnp.zeros_like(dg_ref)
        dkp = _rope_bwd(dk0_ref[...] + dk1_ref[...], c_ref[...], sa_ref[...], sb_ref[...]).astype(BF16)
        dvb = (dv0_ref[...] + dv1_ref[...]).astype(BF16)
        x = h_ref[...]
        gv = g_ref[...]
        hk, r = _rms(x, gv)
        dhk = _dot(dkp, wk_ref[...]) + _dot(dvb, wv_ref[...])
        dx, dg = _rms_bwd(x, gv, r, dhk)
        dh_ref[...] = dres_ref[...] + dx
        dkp_ref[...] = dkp
        dvb_ref[...] = dvb
        hk_ref[...] = hk.astype(BF16)
        dg_ref[...] += dg

    return _rowcall(body, name, seqlen, tm, [dks[0], dks[1], dvs[0], dvs[1], h, dres, *tabs],
                    [_whole(g.reshape(1, d)), _whole(wk), _whole(wv)],
                    [(d, F32), (dkw, BF16), (dkw, BF16), (d, BF16)], [(1, d)], extra_vmem=tm * d * 4 * 6)


def _lin_res(h, xb, w, name):
    seqlen, d = h.shape
    tm = _tile(seqlen, 512)

    def body(h_ref, x_ref, w_ref, o_ref):
        o_ref[...] = h_ref[...] + _dot(x_ref[...], w_ref[...])

    return _rowcall(body, name, seqlen, tm, [h, xb], [_whole(w)], [(d, F32)], extra_vmem=tm * d * 4 * 2)[0]


def _lin_bf16(dy, w, name):
    seqlen, d = dy.shape
    tm = _tile(seqlen, 512)

    def body(dy_ref, w_ref, o_ref):
        o_ref[...] = _dot(dy_ref[...].astype(BF16), w_ref[...]).astype(BF16)

    return _rowcall(body, name, seqlen, tm, [dy], [_whole(w)], [(w.shape[1], BF16)], extra_vmem=tm * d * 4 * 2)[0]


def _mlp_fwd(h, g, wup, wdn, name, comm=None):
    seqlen, d = h.shape
    f = wup.shape[1]
    tm = _tile(seqlen, 512)

    def body(h_ref, g_ref, wup_ref, wdn_ref, o_ref, act_ref):
        x = h_ref[...]
        hm = _rms(x, g_ref[...])[0].astype(BF16)
        r = jnp.maximum(_dot(hm, wup_ref[...]), 0.0)
        act = (r * r).astype(BF16)
        act_ref[...] = act
        o_ref[...] = x + _dot(act, wdn_ref[...])

    consts = [_whole(g.reshape(1, d)), _whole(wup), _whole(wdn)]
    return _rowcall(body, name, seqlen, tm, [h], consts, [(d, F32), (f, BF16)], extra_vmem=tm * f * 4 * 3, comm=comm)


def _mlp_bwd(h, act, dh, g, wup_t, wdn_t, name, comm=None):
    seqlen, d = h.shape
    f = wup_t.shape[0]
    tm = _tile(seqlen, 512)

    def body(h_ref, act_ref, dh_ref, g_ref, wup_ref, wdn_ref, dhin_ref, hm_ref, da_ref, dg_ref):
        @pl.when(pl.program_id(0) == 0)
        def _():
            dg_ref[...] = jnp.zeros_like(dg_ref)
        x = h_ref[...]
        gv = g_ref[...]
        dy = dh_ref[...]
        hm, r = _rms(x, gv)
        rl2 = 2.0 * jnp.sqrt(act_ref[...].astype(F32))
        da = (_dot(dy.astype(BF16), wdn_ref[...]) * rl2).astype(BF16)
        da_ref[...] = da
        dx, dg = _rms_bwd(x, gv, r, _dot(da, wup_ref[...]))
        dhin_ref[...] = dy + dx
        hm_ref[...] = hm.astype(BF16)
        dg_ref[...] += dg

    consts = [_whole(g.reshape(1, d)), _whole(wup_t), _whole(wdn_t)]
    return _rowcall(body, name, seqlen, tm, [h, act, dh], consts, [(d, F32), (d, BF16), (f, BF16)], [(1, d)],
                    extra_vmem=tm * f * 4 * 3, comm=comm)


def _ple_fwd(h, p, g, wg, wpp, name):
    seqlen, d = h.shape
    tm = _tile(seqlen, 512)

    def body(h_ref, p_ref, g_ref, wg_ref, wpp_ref, o_ref):
        x = h_ref[...]
        n = _rms(x, g_ref[...])[0].astype(BF16)
        gate = jax.nn.sigmoid(_dot(n, wg_ref[...]))
        o_ref[...] = x + gate * _dot(p_ref[...].astype(BF16), wpp_ref[...])

    return _rowcall(body, name, seqlen, tm, [h, p], [_whole(g.reshape(1, d)), _whole(wg), _whole(wpp)], [(d, F32)],
                    extra_vmem=tm * d * 4 * 5)[0]


def _ple_bwd(h, p, dh, g, wg, wg_t, wpp, name):
    seqlen, d = h.shape
    tm = _tile(seqlen, 512)

    def body(h_ref, p_ref, dh_ref, g_ref, wg_ref, wgt_ref, wpp_ref, dhin_ref, dz_ref, n_ref, dpp_ref, dg_ref):
        @pl.when(pl.program_id(0) == 0)
        def _():
            dg_ref[...] = jnp.zeros_like(dg_ref)
        x = h_ref[...]
        gv = g_ref[...]
        dy = dh_ref[...]
        n, r = _rms(x, gv)
        nb16 = n.astype(BF16)
        gate = jax.nn.sigmoid(_dot(nb16, wg_ref[...]))
        pp = _dot(p_ref[...].astype(BF16), wpp_ref[...])
        dz = (dy * pp * gate * (1.0 - gate)).astype(BF16)
        dn = _dot(dz, wgt_ref[...])
        dx, dg = _rms_bwd(x, gv, r, dn)
        dhin_ref[...] = dy + dx
        dz_ref[...] = dz
        n_ref[...] = nb16
        dpp_ref[...] = (dy * gate).astype(BF16)
        dg_ref[...] += dg

    return _rowcall(body, name, seqlen, tm, [h, p, dh],
                    [_whole(g.reshape(1, d)), _whole(wg), _whole(wg_t), _whole(wpp)],
                    [(d, F32), (d, BF16), (d, BF16), (d, BF16)], [(1, d)], extra_vmem=tm * d * 4 * 8)


def _loss_bwd(h, g, tgt, name):
    seqlen, d = h.shape
    tm = _tile(seqlen, 512)

    def body(h_ref, t_ref, g_ref, dh_ref, loss_ref, dg_ref):
        @pl.when(pl.program_id(0) == 0)
        def _():
            dg_ref[...] = jnp.zeros_like(dg_ref)
            loss_ref[...] = jnp.zeros_like(loss_ref)
        x = h_ref[...]
        gv = g_ref[...]
        y, r = _rms(x, gv)
        diff = y - t_ref[...]
        loss_ref[...] += (0.5 / d) * jnp.sum(jnp.sum(diff * diff, axis=1, keepdims=True), axis=0, keepdims=True)
        dx, dg = _rms_bwd(x, gv, r, diff * (1.0 / d))
        dh_ref[...] = dx
        dg_ref[...] += dg

    return _rowcall(body, name, seqlen, tm, [h, tgt], [_whole(g.reshape(1, d))], [(d, F32)], [(1, 128), (1, d)])


def _atb(a, b, col_blocked, name):
    seqlen, k1 = a.shape
    k2 = b.shape[1]
    if col_blocked:
        cs = k2 // NDEV
        t1 = _tile(k1, 512)
        nblk = _tile(NDEV, max(1, 2048 // cs))
        t2 = nblk * cs
        oshape = (NDEV, k1, cs)
        oblock = (nblk, t1, cs)
        omap = lambda i, j, l: (j, i, 0)
    else:
        rs = k1 // NDEV
        t2 = _tile(k2, 2048)
        nblk = _tile(NDEV, max(1, 1024 // rs))
        t1 = nblk * rs
        oshape = (NDEV, rs, k2)
        oblock = (nblk, rs, t2)
        omap = lambda i, j, l: (i, 0, j)
    tl = _tile(seqlen, 2048 if b.dtype == BF16 else 1024)

    def body(a_ref, b_ref, o_ref):
        @pl.when(pl.program_id(2) == 0)
        def _():
            o_ref[...] = jnp.zeros_like(o_ref)
        res = _dot(a_ref[...].astype(BF16), b_ref[...].astype(BF16), TN)
        for n in range(nblk):
            if col_blocked:
                o_ref[n] += res[:, n * cs:(n + 1) * cs]
            else:
                o_ref[n] += res[n * rs:(n + 1) * rs, :]

    blocks = [((tl, t1), a.dtype, 2), ((tl, t2), b.dtype, 2), ((t1, t2), F32, 2)]
    return _pcall(
        body, name=name, out_shape=SDS(oshape, F32), grid=(k1 // t1, k2 // t2, seqlen // tl),
        in_specs=[pl.BlockSpec((tl, t1), lambda i, j, l: (l, i)), pl.BlockSpec((tl, t2), lambda i, j, l: (l, j))],
        out_specs=pl.BlockSpec(oblock, omap),
        vmem=_vmem_limit(blocks, extra=t1 * t2 * 4 + tl * (t1 + t2) * 2))(a, b)


_ATTN_SCALE = HEAD_DIM ** -0.5


def _attn_bias():
    qi = lax.broadcasted_iota(jnp.int32, (ATTN_BLOCK, 2 * ATTN_BLOCK), 0) + ATTN_BLOCK
    kj = lax.broadcasted_iota(jnp.int32, (ATTN_BLOCK, 2 * ATTN_BLOCK), 1)
    band = (kj <= qi) & (qi - kj < ATTN_BLOCK)
    return jnp.where(jnp.stack([band & (kj >= ATTN_BLOCK), band]), 0.0, NEG_INF).astype(F32)


def _bias_spec():
    return pl.BlockSpec((None, ATTN_BLOCK, 2 * ATTN_BLOCK), lambda n: (jnp.minimum(n, 1), 0, 0))


def _attn_probs(q4s, kks, sink_col, bias):
    s = jnp.concatenate([_dot(q4, kk, NT) for q4, kk in zip(q4s, kks)], axis=0)
    rows = s.shape[0]
    s = (s.reshape(rows // ATTN_BLOCK, ATTN_BLOCK, 2 * ATTN_BLOCK) + bias).reshape(rows, 2 * ATTN_BLOCK)
    m = jnp.maximum(jnp.max(s, axis=1, keepdims=True), sink_col)
    pr = jnp.exp(s - m)
    es = jnp.exp(sink_col - m)
    inv = 1.0 / (jnp.sum(pr, axis=1, keepdims=True) + es)
    return pr * inv, es * inv


def _sink_col(sink_ref, nheads):
    return jnp.concatenate([jnp.full((ATTN_BLOCK, 1), sink_ref[hq], F32) for hq in range(nheads)], axis=0)


def _kv_pair(p_ref, c_ref, kh):
    sl = slice(kh * HEAD_DIM, (kh + 1) * HEAD_DIM)
    return jnp.concatenate([p_ref[:, sl], c_ref[:, sl]], axis=0)


def _stack_heads(ref, kh, scale=None):
    x = jnp.concatenate(
        [ref[:, (kh * GQA_GROUP + g) * HEAD_DIM:(kh * GQA_GROUP + g + 1) * HEAD_DIM] for g in range(GQA_GROUP)], axis=0)
    return x if scale is None else x * scale


def _attn_fwd(q, k, v, sinks, name):
    seqlen, d = q.shape
    dkv = k.shape[1]
    nkv = dkv // HEAD_DIM
    nb = seqlen // ATTN_BLOCK
    blk = ATTN_BLOCK

    def body(sink_ref, bias_ref, q_ref, kc_ref, kp_ref, vc_ref, vp_ref, o_ref):
        q4s = [_stack_heads(q_ref, kh, _ATTN_SCALE) for kh in range(nkv)]
        kks = [_kv_pair(kp_ref, kc_ref, kh) for kh in range(nkv)]
        w, _ = _attn_probs(q4s, kks, _sink_col(sink_ref, nkv * GQA_GROUP), bias_ref[...])
        wb = w.astype(BF16)
        for kh in range(nkv):
            o4 = _dot(wb[kh * GQA_GROUP * blk:(kh + 1) * GQA_GROUP * blk, :], _kv_pair(vp_ref, vc_ref, kh))
            for g in range(GQA_GROUP):
                hq = kh * GQA_GROUP + g
                o_ref[:, hq * HEAD_DIM:(hq + 1) * HEAD_DIM] = o4[g * blk:(g + 1) * blk, :].astype(BF16)

    cur = lambda n: (n, 0)
    prev = lambda n: (jnp.maximum(n - 1, 0), 0)
    return _pcall(
        body, name=name, out_shape=SDS((seqlen, d), BF16), grid=(nb,),
        in_specs=[pl.BlockSpec(memory_space=pltpu.SMEM), _bias_spec(), pl.BlockSpec((blk, d), cur),
                  pl.BlockSpec((blk, dkv), cur), pl.BlockSpec((blk, dkv), prev),
                  pl.BlockSpec((blk, dkv), cur), pl.BlockSpec((blk, dkv), prev)],
        out_specs=pl.BlockSpec((blk, d), cur), vmem=32 << 20)(sinks, _attn_bias(), q, k, k, v, v)


def _attn_bwd(q, k, v, do, sinks, name, comm=None):
    seqlen, d = q.shape
    dkv = k.shape[1]
    nkv = dkv // HEAD_DIM
    nh = d // HEAD_DIM
    nb = seqlen // ATTN_BLOCK
    blk = ATTN_BLOCK

    def body(sink_ref, bias_ref, q_ref, kc_ref, kp_ref, vc_ref, vp_ref, do_ref, dq_ref, dk_ref, dv_ref, ds_ref,
             ck_ref, cv_ref):
        n = pl.program_id(0)

        @pl.when(n == 0)
        def _():
            ck_ref[...] = jnp.zeros_like(ck_ref)
            cv_ref[...] = jnp.zeros_like(cv_ref)
            ds_ref[...] = jnp.zeros_like(ds_ref)

        @pl.when(n == nb)
        def _():
            dk_ref[...] = ck_ref[...]
            dv_ref[...] = cv_ref[...]

        @pl.when(n < nb)
        def _():
            q4s = [_stack_heads(q_ref, kh, _ATTN_SCALE) for kh in range(nkv)]
            do4s = [_stack_heads(do_ref, kh) for kh in range(nkv)]
            kks = [_kv_pair(kp_ref, kc_ref, kh) for kh in range(nkv)]
            w, wsink = _attn_probs(q4s, kks, _sink_col(sink_ref, nh), bias_ref[...])
            dw = jnp.concatenate([_dot(do4s[kh], _kv_pair(vp_ref, vc_ref, kh), NT) for kh in range(nkv)], axis=0)
            dsum = jnp.sum(w * dw, axis=1, keepdims=True)
            ds_all = (w * (dw - dsum)).astype(BF16)
            wb = w.astype(BF16)
            dsk = -wsink * dsum
            for kh in range(nkv):
                sl = slice(kh * HEAD_DIM, (kh + 1) * HEAD_DIM)
                rows = slice(kh * GQA_GROUP * blk, (kh + 1) * GQA_GROUP * blk)
                ds = ds_all[rows, :]
                dq4 = _dot(ds, kks[kh]) * _ATTN_SCALE
                dkk = _dot(ds, q4s[kh], TN)
                dvv = _dot(wb[rows, :], do4s[kh], TN)
                for g in range(GQA_GROUP):
                    hq = kh * GQA_GROUP + g
                    dq_ref[:, hq * HEAD_DIM:(hq + 1) * HEAD_DIM] = dq4[g * blk:(g + 1) * blk, :]
                    ds_ref[hq:hq + 1, :] += jnp.sum(dsk[hq * blk:(hq + 1) * blk, :], axis=0, keepdims=True)
                dk_ref[:, sl] = ck_ref[:, sl] + dkk[:blk, :]
                dv_ref[:, sl] = cv_ref[:, sl] + dvv[:blk, :]
                ck_ref[:, sl] = dkk[blk:, :]
                cv_ref[:, sl] = dvv[blk:, :]

    cur = lambda n: (jnp.minimum(n, nb - 1), 0)
    prev = lambda n: (jnp.clip(n - 1, 0, nb - 1), 0)
    lag = lambda n: (jnp.maximum(n - 1, 0), 0)
    return _pcall(
        body, name=name,
        out_shape=[SDS((seqlen, d), F32), SDS((seqlen, dkv), F32), SDS((seqlen, dkv), F32), SDS((nh, 128), F32)],
        grid=(nb + 1,),
        in_specs=[pl.BlockSpec(memory_space=pltpu.SMEM), _bias_spec(), pl.BlockSpec((blk, d), cur),
                  pl.BlockSpec((blk, dkv), cur), pl.BlockSpec((blk, dkv), prev),
                  pl.BlockSpec((blk, dkv), cur), pl.BlockSpec((blk, dkv), prev), pl.BlockSpec((blk, d), cur)],
        out_specs=[pl.BlockSpec((blk, d), cur), pl.BlockSpec((blk, dkv), lag), pl.BlockSpec((blk, dkv), lag),
                   pl.BlockSpec((nh, 128), lambda n: (0, 0))],
        scratch=[pltpu.VMEM((blk, dkv), F32)] * 2, vmem=32 << 20, comm=comm)(sinks, _attn_bias(), q, k, k, v, v, do)


def _ssm_mats(lre, lim, ldt, btr, bti, cr, ci):
    dt = jnp.exp(ldt)
    mag = jnp.exp(lre * dt)
    ar = mag * jnp.cos(lim * dt)
    ai = mag * jnp.sin(lim * dt)
    den = lre * lre + lim * lim
    nr = ar - 1.0
    cfr = (nr * lre + ai * lim) / den
    cfi = (ai * lre - nr * lim) / den
    bbr = cfr * btr - cfi * bti
    bbi = cfr * bti + cfi * btr
    pr = [jnp.ones_like(ar)]
    pi = [jnp.zeros_like(ai)]
    for _ in range(SSM_T):
        pr.append(pr[-1] * ar - pi[-1] * ai)
        pi.append(pr[-2] * ai + pi[-1] * ar)
    last = SSM_T - 1
    p_re = jnp.concatenate([pr[last - s] * bbr - pi[last - s] * bbi for s in range(SSM_T)], axis=0)
    p_im = jnp.concatenate([pr[last - s] * bbi + pi[last - s] * bbr for s in range(SSM_T)], axis=0)
    qt_re = jnp.concatenate([pr[t + 1] * cr - pi[t + 1] * ci for t in range(SSM_T)], axis=0)
    qt_im = jnp.concatenate([-(pr[t + 1] * ci + pi[t + 1] * cr) for t in range(SSM_T)], axis=0)
    ctr = jnp.concatenate([cr] * SSM_T, axis=0)
    cti = jnp.concatenate([ci] * SSM_T, axis=0)
    lag = (lax.broadcasted_iota(jnp.int32, (SSM_W, SSM_W), 1) // SSM_GROUP
           - lax.broadcasted_iota(jnp.int32, (SSM_W, SSM_W), 0) // SSM_GROUP)
    m = jnp.zeros((SSM_W, SSM_W), F32)
    for l in range(SSM_T):
        zr = jnp.concatenate([pr[l] * bbr - pi[l] * bbi] * SSM_T, axis=0)
        zi = jnp.concatenate([pr[l] * bbi + pi[l] * bbr] * SSM_T, axis=0)
        kl = _dot3_nt(zr, ctr) - _dot3_nt(zi, cti)
        m = m + jnp.where(lag == l, kl, 0.0)
    return m, p_re, p_im, qt_re, qt_im, pr[SSM_T], pi[SSM_T]


_SSM_GB = 8


def _ssm_param_specs(ng):
    n, hh = SSM_STATE, SSM_GROUP
    gb = _tile(ng, _SSM_GB)
    row = pl.BlockSpec((gb, 1, n), lambda i: (i, 0, 0))
    one = pl.BlockSpec((gb, 1, 1), lambda i: (i, 0, 0))
    mat = pl.BlockSpec((gb, hh, n), lambda i: (i, 0, 0))
    big = pl.BlockSpec((gb, SSM_W, SSM_W), lambda i: (i, 0, 0))
    half = pl.BlockSpec((gb, SSM_W, n), lambda i: (i, 0, 0))
    return gb, row, one, mat, big, half


def _ssm_prep(params, name):
    ng = params[0].shape[0]
    n = SSM_STATE
    gb, row, one, mat, big, half = _ssm_param_specs(ng)

    def body(lre, lim, ldt, btr, bti, cr, ci, m_ref, pre_ref, pim_ref, qre_ref, qim_ref, atr_ref, ati_ref):
        for gi in range(gb):
            outs = _ssm_mats(lre[gi], lim[gi], ldt[gi], btr[gi], bti[gi], cr[gi], ci[gi])
            for ref, val in zip((m_ref, pre_ref, pim_ref, qre_ref, qim_ref, atr_ref, ati_ref), outs):
                ref[gi] = val

    return _pcall(
        body, name=name,
        out_shape=[SDS((ng, SSM_W, SSM_W), F32)] + [SDS((ng, SSM_W, n), F32)] * 4 + [SDS((ng, 1, n), F32)] * 2,
        grid=(ng // gb,), in_specs=[row, row, one, mat, mat, mat, mat],
        out_specs=[big, half, half, half, half, row, row], vmem=40 << 20)(*params)


def _ssm_prep_vjp(params, cots, name):
    ng = params[0].shape[0]
    n, hh = SSM_STATE, SSM_GROUP
    gb, row, one, mat, big, half = _ssm_param_specs(ng)

    def body(lre, lim, ldt, btr, bti, cr, ci, dm, dpre, dpim, dqre, dqim, datr, dati,
             o_lre, o_lim, o_ldt, o_btr, o_bti, o_cr, o_ci):
        for gi in range(gb):
            prm = (lre[gi], lim[gi], ldt[gi], btr[gi], bti[gi], cr[gi], ci[gi])
            _, pull = jax.vjp(_ssm_mats, *prm)
            grads = pull((dm[gi], dpre[gi], dpim[gi], dqre[gi], dqim[gi], datr[gi], dati[gi]))
            for ref, val in zip((o_lre, o_lim, o_ldt, o_btr, o_bti, o_cr, o_ci), grads):
                ref[gi] = val

    return _pcall(
        body, name=name,
        out_shape=[SDS((ng, 1, n), F32)] * 2 + [SDS((ng, 1, 1), F32)] + [SDS((ng, hh, n), F32)] * 4,
        grid=(ng // gb,), in_specs=[row, row, one, mat, mat, mat, mat, big, half, half, half, half, row, row],
        out_specs=[row, row, one, mat, mat, mat, mat], vmem=48 << 20)(*params, *cots)


_SSM_GT = SSM_W // SSM_GROUP


def _blk_transpose(xs):
    assert len(xs) == SSM_T == _SSM_GT
    blk = lax.broadcasted_iota(jnp.int32, xs[0].shape, 1) // SSM_GROUP
    xs = list(xs)
    k = SSM_T // 2
    while k:
        high = (blk // k) % 2 == 1
        nxt = []
        for i in range(SSM_T):
            if i & k:
                nxt.append(jnp.where(high, xs[i], pltpu.roll(xs[i ^ k], SSM_W - SSM_GROUP * k, 1)))
            else:
                nxt.append(jnp.where(high, pltpu.roll(xs[i ^ k], SSM_GROUP * k, 1), xs[i]))
        xs = nxt
        k //= 2
    return xs


def _tile_groups(x_ref, ncb):
    return _blk_transpose([x_ref[pl.ds(t, ncb, stride=SSM_T), :] for t in range(SSM_T)])


def _groups_tile(ys, o_ref, ncb):
    for t, y in enumerate(_blk_transpose(ys)):
        o_ref[pl.ds(t, ncb, stride=SSM_T), :] = y


def _ssm_specs(seqlen, d):
    ncb = _tile(seqlen // SSM_T, 512)
    grid = (d // SSM_W, seqlen // (SSM_T * ncb))
    act = pl.BlockSpec((SSM_T * ncb, SSM_W), lambda j, r: (r, j))
    state = pl.BlockSpec((ncb, _SSM_GT * SSM_W), lambda j, r: (r, j))
    mats = pl.BlockSpec((_SSM_GT, SSM_W, SSM_W), lambda j, r: (j, 0, 0))
    return ncb, grid, act, state, mats


def _gsl(gl):
    return slice(gl * SSM_W, (gl + 1) * SSM_W)


def _ssm_state_in(hn, pmat, name):
    seqlen, d = hn.shape
    ncb, grid, act, state, mats = _ssm_specs(seqlen, d)

    def body(x_ref, p_ref, s_ref):
        us = _tile_groups(x_ref, ncb)
        for gl in range(_SSM_GT):
            s_ref[:, _gsl(gl)] = _dot3(us[gl], p_ref[gl], NN)

    return _pcall(body, name=name, out_shape=SDS((seqlen // SSM_T, d * SSM_T), F32), grid=grid,
                  in_specs=[act, mats], out_specs=state, vmem=40 << 20)(hn, pmat)


def _ssm_out(hn, xp, mmat, qt, name, comm=None):
    seqlen, d = hn.shape
    ncb, grid, act, state, mats = _ssm_specs(seqlen, d)

    def body(x_ref, xp_ref, m_ref, q_ref, y_ref):
        us = _tile_groups(x_ref, ncb)
        ys = [_dot3(us[gl], m_ref[gl], NN) + _dot3(xp_ref[:, _gsl(gl)], q_ref[gl], NT)
              for gl in range(_SSM_GT)]
        _groups_tile(ys, y_ref, ncb)

    return _pcall(body, name=name, out_shape=[SDS((seqlen, d), F32)], grid=grid,
                  in_specs=[act, state, mats, mats], out_specs=[act], vmem=40 << 20, comm=comm)(hn, xp, mmat, qt)


def _ssm_dstate(dy, qt, name):
    seqlen, d = dy.shape
    ncb, grid, act, state, mats = _ssm_specs(seqlen, d)

    def body(dy_ref, q_ref, o_ref):
        dys = _tile_groups(dy_ref, ncb)
        for gl in range(_SSM_GT):
            o_ref[:, _gsl(gl)] = _dot3(dys[gl], q_ref[gl], NN)

    return _pcall(body, name=name, out_shape=SDS((seqlen // SSM_T, d * SSM_T), F32), grid=grid,
                  in_specs=[act, mats], out_specs=state, vmem=40 << 20)(dy, qt)


def _ssm_bwd(hn, dy, xp, gs, mmat, pmat, name, comm=None):
    seqlen, d = hn.shape
    ng = d // SSM_GROUP
    ncb, grid, act, state, mats = _ssm_specs(seqlen, d)

    def body(x_ref, dy_ref, xp_ref, g_ref, m_ref, p_ref, du_ref, dm_ref, dp_ref, dq_ref, da_ref):
        @pl.when(pl.program_id(1) == 0)
        def _():
            for ref in (dm_ref, dp_ref, dq_ref, da_ref):
                ref[...] = jnp.zeros_like(ref)
        us = _tile_groups(x_ref, ncb)
        dys = _tile_groups(dy_ref, ncb)
        dus = []
        for gl in range(_SSM_GT):
            xv, gv = xp_ref[:, _gsl(gl)], g_ref[:, _gsl(gl)]
            u2, dy2, x2, g2 = _split(us[gl]), _split(dys[gl]), _split(xv), _split(gv)
            dus.append(_dot3(dy2, m_ref[gl], NT) + _dot3(g2, p_ref[gl], NT))
            dm_ref[gl] += _dot3(u2, dy2, TN)
            dp_ref[gl] += _dot3(u2, g2, TN)
            dq_ref[gl] += _dot3(dy2, x2, TN)
            da_ref[gl, 0:1, :] += jnp.sum(xv * gv, axis=0, keepdims=True)
            da_ref[gl, 1:2, :] += jnp.sum(xv * pltpu.roll(gv, SSM_STATE, 1), axis=0, keepdims=True)
        _groups_tile(dus, du_ref, ncb)

    return _pcall(
        body, name=name,
        out_shape=[SDS((seqlen, d), F32)] + [SDS((ng, SSM_W, SSM_W), F32)] * 3 + [SDS((ng, 2, SSM_W), F32)],
        grid=grid, in_specs=[act, act, state, state, mats, mats],
        out_specs=[act, mats, mats, mats, pl.BlockSpec((_SSM_GT, 2, SSM_W), lambda j, r: (j, 0, 0))],
        vmem=48 << 20, comm=comm)(hn, dy, xp, gs, mmat, pmat)


def _ssm_carry(s, a1, a2, reverse, name, comm=None):
    nc, w = s.shape
    tc = _tile(nc, 256)
    nblk = nc // tc
    sub = 8

    def body(s_ref, a1_ref, a2_ref, o_ref, st_ref, sw_ref):
        @pl.when(pl.program_id(0) == 0)
        def _():
            st_ref[...] = jnp.zeros_like(st_ref)
            sw_ref[...] = jnp.zeros_like(sw_ref)
        a1v = jnp.broadcast_to(a1_ref[...], (sub, w))
        a2v = jnp.broadcast_to(a2_ref[...], (sub, w))
        first = lax.broadcasted_iota(jnp.int32, (sub, w), 1) % SSM_W < SSM_STATE
        row = lax.broadcasted_iota(jnp.int32, (sub, w), 0)

        def step(t, carry):
            x, xs = carry
            tt = (tc // sub - 1 - t) if reverse else t
            base = pl.multiple_of(tt * sub, sub)
            blk = s_ref[pl.ds(base, sub), :]
            blks = jnp.where(first, pltpu.roll(blk, w - SSM_STATE, 1), pltpu.roll(blk, SSM_STATE, 1))
            out = jnp.zeros((sub, w), F32)
            for r in (range(sub - 1, -1, -1) if reverse else range(sub)):
                out = jnp.where(row == r, x, out)
                sr = jnp.broadcast_to(blk[r:r + 1, :], (sub, w))
                ssr = jnp.broadcast_to(blks[r:r + 1, :], (sub, w))
                x, xs = a1v * x + a2v * xs + sr, a1v * xs - a2v * x + ssr
            o_ref[pl.ds(base, sub), :] = out
            return x, xs

        x, xs = lax.fori_loop(0, tc // sub, step, (st_ref[...], sw_ref[...]))
        st_ref[...] = x
        sw_ref[...] = xs

    imap = (lambda i: (nblk - 1 - i, 0)) if reverse else (lambda i: (i, 0))
    cst = pl.BlockSpec((1, w), lambda i: (0, 0))
    return _pcall(body, name=name, out_shape=[SDS((nc, w), F32)], grid=(nblk,),
                  in_specs=[pl.BlockSpec((tc, w), imap), cst, cst], out_specs=[pl.BlockSpec((tc, w), imap)],
                  scratch=[pltpu.VMEM((sub, w), F32)] * 2,
                  vmem=_vmem_limit([((tc, w), F32, 4)], extra=8 << 20), comm=comm)(s, a1, a2)


def _ssm_rows(atr, ati, conj):
    ng = atr.shape[0]
    ai = -ati if conj else ati
    a1 = jnp.concatenate([atr, atr], axis=2).reshape(1, ng * SSM_W)
    a2 = jnp.concatenate([-ai, ai], axis=2).reshape(1, ng * SSM_W)
    return a1, a2


def _peers():
    x, y, c = (lax.axis_index(a) for a in AXES)
    me = 4 * x + 2 * y + c
    peers = []
    for dx, dy, dc in [(0, 0, 1), (0, 1, 0), (0, 1, 1), (1, 0, 0), (1, 0, 1), (1, 1, 0), (1, 1, 1)]:
        px, py, pc = (1 - x) if dx else x, (1 - y) if dy else y, (1 - c) if dc else c
        peers.append(((px, py, pc), 4 * px + 2 * py + pc))
    return me, peers


class _Exchange:
    def __init__(self, arrs, scatter, layers=None):
        self.arrs = list(arrs)
        self.scatter = scatter
        self.layers = list(layers) if layers is not None else [None] * len(self.arrs)

    def out_shape(self):
        shapes = []
        for arr, layer in zip(self.arrs, self.layers):
            block = arr.shape[1:] if (self.scatter or layer is not None) else arr.shape
            shapes.append(SDS((NDEV,) + tuple(block), arr.dtype))
        return shapes

    def semaphores(self):
        n = len(self.arrs)
        return [pltpu.SemaphoreType.DMA((n * (NDEV - 1),)), pltpu.SemaphoreType.DMA((n * (NDEV - 1),)),
                pltpu.SemaphoreType.DMA((n,))]

    def _src(self, ref, a, block):
        if self.scatter:
            return ref.at[block]
        return ref if self.layers[a] is None else ref.at[self.layers[a]]

    def _remote(self, xin, xout, sems, a, k, peer, landing):
        pid, pidx = peer
        slot = a * (NDEV - 1) + k
        return pltpu.make_async_remote_copy(
            src_ref=self._src(xin[a], a, pidx), dst_ref=xout[a].at[landing],
            send_sem=sems[0].at[slot], recv_sem=sems[1].at[slot], device_id=pid, device_id_type=MESH)

    def _local(self, xin, xout, sems, a, me):
        return pltpu.make_async_copy(self._src(xin[a], a, me), xout[a].at[me], sems[2].at[a])

    def start(self, xin, xout, sems):
        me, peers = _peers()
        for a in range(len(self.arrs)):
            self._local(xin, xout, sems, a, me).start()
        for k, peer in enumerate(peers):
            for a in range(len(self.arrs)):
                self._remote(xin, xout, sems, a, k, peer, me).start()

    def wait(self, xin, xout, sems):
        me, peers = _peers()
        for a in range(len(self.arrs)):
            self._local(xin, xout, sems, a, me).wait()
        for k, peer in enumerate(peers):
            for a in range(len(self.arrs)):
                cp = self._remote(xin, xout, sems, a, k, peer, peer[1])
                cp.wait_send()
                cp.wait_recv()


def _exchange(arrs, scatter, name, layers=None):
    comm = _Exchange(arrs, scatter, layers)
    n = len(comm.arrs)

    def body(*refs):
        xin, xout, sems = refs[:n], refs[n:2 * n], refs[2 * n:]
        comm.start(xin, xout, sems)
        comm.wait(xin, xout, sems)

    hbm = pl.BlockSpec(memory_space=pl.ANY)
    return pl.pallas_call(
        body, out_shape=comm.out_shape(), in_specs=[hbm] * n, out_specs=[hbm] * n,
        scratch_shapes=comm.semaphores(), name=name, interpret=False)(*comm.arrs)


def _adamw(parts, w, m, v, name):
    rows, cols = w.shape
    tr = _tile(rows, max(8, (1 << 17) // cols))
    c1 = 1.0 - ADAM_B1 ** ADAM_STEP
    c2 = 1.0 - ADAM_B2 ** ADAM_STEP

    def body(p_ref, w_ref, m_ref, v_ref, g_ref, d_ref, nm_ref, nv_ref):
        g = p_ref[0]
        for j in range(1, NDEV):
            g = g + p_ref[j]
        mm = ADAM_B1 * m_ref[...] + (1.0 - ADAM_B1) * g
        vv = ADAM_B2 * v_ref[...] + (1.0 - ADAM_B2) * (g * g)
        g_ref[...] = g
        nm_ref[...] = mm
        nv_ref[...] = vv
        d_ref[...] = -ADAM_LR * ((mm / c1) / (jnp.sqrt(vv / c2) + ADAM_EPS) + ADAM_WD * w_ref[...])

    spec = pl.BlockSpec((tr, cols), lambda i: (i, 0))
    return _pcall(
        body, name=name, out_shape=[SDS((rows, cols), F32)] * 4, grid=(rows // tr,),
        in_specs=[pl.BlockSpec((NDEV, tr, cols), lambda i: (0, i, 0)), spec, spec, spec], out_specs=[spec] * 4,
        vmem=_vmem_limit([((NDEV + 7, tr, cols), F32, 2)]))(parts, w, m, v)


def kernel(x, p, norm_mix, ssm_lambda_re, ssm_lambda_im, ssm_log_dt, ssm_b_re, ssm_b_im, ssm_c_re, ssm_c_im, ssm_d, ssm_w_glu, kv_norm, w_k, w_v, w_q, attn_sinks, w_o, norm_mlp, w_up, w_down, norm_ple, w_ple_gate, w_ple_proj, norm_final, loss_target, m_norm_mix, m_ssm_lambda_re, m_ssm_lambda_im, m_ssm_log_dt, m_ssm_b_re, m_ssm_b_im, m_ssm_c_re, m_ssm_c_im, m_ssm_d, m_ssm_w_glu, m_kv_norm, m_w_k, m_w_v, m_w_q, m_attn_sinks, m_w_o, m_norm_mlp, m_w_up, m_w_down, m_norm_ple, m_w_ple_gate, m_w_ple_proj, m_norm_final, v_norm_mix, v_ssm_lambda_re, v_ssm_lambda_im, v_ssm_log_dt, v_ssm_b_re, v_ssm_b_im, v_ssm_c_re, v_ssm_c_im, v_ssm_d, v_ssm_w_glu, v_kv_norm, v_w_k, v_w_v, v_w_q, v_attn_sinks, v_w_o, v_norm_mlp, v_w_up, v_w_down, v_norm_ple, v_w_ple_gate, v_w_ple_proj, v_norm_final):
    names = ['norm_mix', 'ssm_lambda_re', 'ssm_lambda_im', 'ssm_log_dt', 'ssm_b_re', 'ssm_b_im', 'ssm_c_re',
             'ssm_c_im', 'ssm_d', 'ssm_w_glu', 'kv_norm', 'w_k', 'w_v', 'w_q', 'attn_sinks', 'w_o', 'norm_mlp',
             'w_up', 'w_down', 'norm_ple', 'w_ple_gate', 'w_ple_proj', 'norm_final']
    weights = dict(zip(names, (norm_mix, ssm_lambda_re, ssm_lambda_im, ssm_log_dt, ssm_b_re, ssm_b_im, ssm_c_re,
                               ssm_c_im, ssm_d, ssm_w_glu, kv_norm, w_k, w_v, w_q, attn_sinks, w_o, norm_mlp,
                               w_up, w_down, norm_ple, w_ple_gate, w_ple_proj, norm_final)))
    mom1 = dict(zip(names, (m_norm_mix, m_ssm_lambda_re, m_ssm_lambda_im, m_ssm_log_dt, m_ssm_b_re, m_ssm_b_im,
                            m_ssm_c_re, m_ssm_c_im, m_ssm_d, m_ssm_w_glu, m_kv_norm, m_w_k, m_w_v, m_w_q,
                            m_attn_sinks, m_w_o, m_norm_mlp, m_w_up, m_w_down, m_norm_ple, m_w_ple_gate,
                            m_w_ple_proj, m_norm_final)))
    mom2 = dict(zip(names, (v_norm_mix, v_ssm_lambda_re, v_ssm_lambda_im, v_ssm_log_dt, v_ssm_b_re, v_ssm_b_im,
                            v_ssm_c_re, v_ssm_c_im, v_ssm_d, v_ssm_w_glu, v_kv_norm, v_w_k, v_w_v, v_w_q,
                            v_attn_sinks, v_w_o, v_norm_mlp, v_w_up, v_w_down, v_norm_ple, v_w_ple_gate,
                            v_w_ple_proj, v_norm_final)))

    seqlen, d = x.shape[1], x.shape[2]
    depth = w_up.shape[0]
    n_ssm = ssm_w_glu.shape[0]
    n_att = w_q.shape[0]
    ng = d // SSM_GROUP
    nh = d // HEAD_DIM
    h0 = x[0]
    tgt = loss_target[0]
    tabs = _rope_tables(seqlen)

    sharded = ['w_up', 'w_down', 'w_ple_gate', 'w_ple_proj', 'ssm_w_glu', 'w_q', 'w_o', 'w_k', 'w_v']
    shards = {k: weights[k].astype(BF16) for k in sharded}
    shards['ssm_d'] = ssm_d
    dkv = w_k.shape[1]

    def layer_set(i):
        keys = [('w_up', i), ('w_down', i), ('w_ple_gate', i), ('w_ple_proj', i)]
        keys += [('ssm_w_glu', i), ('ssm_d', i)] if i < n_ssm else [('w_q', i - n_ssm), ('w_o', i - n_ssm)]
        if i == n_ssm:
            keys += [('w_k', None), ('w_v', None)]
        return keys

    def gather_of(i, only=None):
        keys = [kl for kl in layer_set(i) if only is None or kl[0] in only]
        return keys, _Exchange([shards[k] for k, _ in keys], False, [l for _, l in keys])

    def as_operands(keys, blocks):
        w = {}
        for (k, _), g in zip(keys, blocks):
            if k == 'ssm_d':
                w[k] = g.reshape(d)
            elif k in ('w_ple_proj', 'ssm_w_glu', 'w_up'):
                w[k] = g.transpose(1, 0, 2).reshape(g.shape[1], NDEV * g.shape[2])
                w[k + '_t'] = g.transpose(0, 2, 1).reshape(NDEV * g.shape[2], g.shape[1])
            else:
                w[k] = g.reshape(NDEV * g.shape[1], g.shape[2])
                w[k + '_t'] = g.transpose(2, 0, 1).reshape(g.shape[2], NDEV * g.shape[1])
        return w

    lw = {}

    def ssm_params(i):
        n = SSM_STATE
        return (ssm_lambda_re[i].reshape(ng, 1, n), ssm_lambda_im[i].reshape(ng, 1, n),
                ssm_log_dt[i].reshape(ng, 1, 1), jnp.swapaxes(ssm_b_re[i], 1, 2), jnp.swapaxes(ssm_b_im[i], 1, 2),
                ssm_c_re[i], ssm_c_im[i])

    h = h0
    h_in, h_a, h_b, acts = [], [], [], []
    ssm_saved, att_saved = {}, {}
    k_sh = v_sh = None
    for i in range(depth):
        h_in.append(h)
        if i < n_ssm:
            first = gather_of(0, ('ssm_w_glu', 'ssm_d')) if i == 0 else None
            hn, *got = _norm_fwd(h, norm_mix[i], f"norm_mix_fwd{i}", comm=first and first[1])
            if first:
                lw[0] = as_operands(first[0], got)
            mats = _ssm_prep(ssm_params(i), f"ssm_prep{i}")
            mmat, atr, ati = mats[0], mats[5], mats[6]
            pmat = jnp.concatenate([mats[1], mats[2]], axis=2)
            qt = jnp.concatenate([mats[3], mats[4]], axis=2)
            s_in = _ssm_state_in(hn, pmat, f"ssm_state_in{i}")
            xp = _ssm_carry(s_in, *_ssm_rows(atr, ati, False), False, f"ssm_carry_fwd{i}")[0]
            rest = gather_of(0, ('w_up', 'w_down', 'w_ple_gate', 'w_ple_proj')) if i == 0 else None
            y, *got = _ssm_out(hn, xp, mmat, qt, f"ssm_out{i}", comm=rest and rest[1])
            if rest:
                lw[0].update(as_operands(rest[0], got))
            ha = _glu_fwd(y, hn, h, lw[i]['ssm_d'], lw[i]['ssm_w_glu'], f"glu_fwd{i}")
            ssm_saved[i] = (hn, mmat, pmat, qt, atr, ati, xp, y)
        else:
            j = i - n_ssm
            q = _q_fwd(h, norm_mix[i], lw[i]['w_q'], tabs, f"q_fwd{j}")
            o = _attn_fwd(q, k_sh, v_sh, attn_sinks[j], f"attn_fwd{j}")
            ha = _lin_res(h, o, lw[i]['w_o'], f"attn_out{j}")
            att_saved[j] = (q, o)
        h_a.append(ha)
        nxt = gather_of(i + 1) if i + 1 < depth else None
        res = _mlp_fwd(ha, norm_mlp[i], lw[i]['w_up'], lw[i]['w_down'], f"mlp_fwd{i}", comm=nxt and nxt[1])
        hb = res[0]
        acts.append(res[1])
        if nxt:
            lw[i + 1] = as_operands(nxt[0], res[2:])
        h_b.append(hb)
        h = _ple_fwd(hb, p[i, 0], norm_ple[i], lw[i]['w_ple_gate'], lw[i]['w_ple_proj'], f"ple_fwd{i}")
        if i == n_ssm - 1:
            k_sh, v_sh = _kv_fwd(h, kv_norm, lw[n_ssm]['w_k'], lw[n_ssm]['w_v'], tabs, "kv_fwd")
    h_kv = h_in[n_ssm] if n_ssm < depth else h
    dh, loss_row, g_norm_final = _loss_bwd(h, norm_final, tgt, "loss_bwd")
    loss = lax.psum(loss_row[0, 0], AXES)

    g_norm_mix, g_norm_mlp, g_norm_ple = [None] * depth, [None] * depth, [None] * depth
    g_ssm, g_sinks = [None] * n_ssm, [None] * n_att
    g_kv_norm = None
    dks, dvs = [], []
    recv = {}

    def riding(stacks):
        if not stacks:
            return None
        ride = (list(stacks), _Exchange(list(stacks.values()), True))
        stacks.clear()
        return ride

    def landed(ride, blocks):
        if ride:
            recv.update(zip(ride[0], blocks))

    gl = {}
    for i in range(depth - 1, -1, -1):
        if i == n_ssm - 1:
            dh, dkp, dvb, hkb, g_kv_norm = _kv_bwd(dks, dvs, h_kv, kv_norm, dh, lw[n_ssm]['w_k_t'],
                                                   lw[n_ssm]['w_v_t'], tabs, "kv_bwd")
            gl['w_k', None] = _atb(hkb, dkp, False, "grad_w_k")
            gl['w_v', None] = _atb(hkb, dvb, False, "grad_w_v")
        dhb, dz, nb16, dpp, g_norm_ple[i] = _ple_bwd(h_b[i], p[i, 0], dh, norm_ple[i], lw[i]['w_ple_gate'],
                                                     lw[i]['w_ple_gate_t'], lw[i]['w_ple_proj'], f"ple_bwd{i}")
        gl['w_ple_gate', i] = _atb(nb16, dz, False, f"grad_w_ple_gate{i}")
        gl['w_ple_proj', i] = _atb(p[i, 0], dpp, True, f"grad_w_ple_proj{i}")
        ride = riding(gl)
        res = _mlp_bwd(h_a[i], acts[i], dhb, norm_mlp[i], lw[i]['w_up_t'], lw[i]['w_down_t'], f"mlp_bwd{i}",
                       comm=ride and ride[1])
        dha, hmb, da, g_norm_mlp[i] = res[:4]
        landed(ride, res[4:])
        g_up = {('w_up', i): _atb(hmb, da, True, f"grad_w_up{i}")}
        g_down = {('w_down', i): _atb(acts[i], dhb, False, f"grad_w_down{i}")}
        if i >= n_ssm:
            j = i - n_ssm
            q, o = att_saved[j]
            do = _lin_bf16(dha, lw[i]['w_o_t'], f"attn_out_bwd{j}")
            ride = riding({**g_up, **g_down, ('w_o', j): _atb(o, dha, False, f"grad_w_o{j}")})
            dq, dk_j, dv_j, dsink, *got = _attn_bwd(q, k_sh, v_sh, do, attn_sinks[j], f"attn_bwd{j}",
                                                    comm=ride[1])
            landed(ride, got)
            dks.append(dk_j)
            dvs.append(dv_j)
            g_sinks[j] = dsink[:, 0]
            dh, dqp, hnb, g_norm_mix[i] = _q_bwd(dq, h_in[i], norm_mix[i], dha, lw[i]['w_q_t'], tabs, f"q_bwd{j}")
            gl['w_q', j] = _atb(hnb, dqp, False, f"grad_w_q{j}")
        else:
            hn, mmat, pmat, qt, atr, ati, xp, y = ssm_saved[i]
            dyy, dhn_d, geb, dab, g_dskip = _glu_bwd(y, hn, dha, lw[i]['ssm_d'], lw[i]['ssm_w_glu'],
                                                     lw[i]['ssm_w_glu_t'], f"glu_bwd{i}")
            g_down['ssm_d', i] = g_dskip.reshape(NDEV, d // NDEV)
            g_down['ssm_w_glu', i] = _atb(geb, dab, True, f"grad_ssm_w_glu{i}")
            dxp = _ssm_dstate(dyy, qt, f"ssm_dstate{i}")
            ride = riding(g_up)
            gs, *got = _ssm_carry(dxp, *_ssm_rows(atr, ati, True), True, f"ssm_carry_bwd{i}", comm=ride[1])
            landed(ride, got)
            ride = riding(g_down)
            du, dm, dp, dqt, da_raw, *got = _ssm_bwd(hn, dyy, xp, gs, mmat, pmat, f"ssm_bwd{i}", comm=ride[1])
            landed(ride, got)
            n = SSM_STATE
            cots = (dm, dp[:, :, :n], dp[:, :, n:], dqt[:, :, :n], dqt[:, :, n:],
                    (da_raw[:, 0:1, :n] + da_raw[:, 0:1, n:]), (da_raw[:, 1:2, :n] - da_raw[:, 1:2, n:]))
            g_ssm[i] = _ssm_prep_vjp(ssm_params(i), cots, f"ssm_prep_vjp{i}")
            dh, g_norm_mix[i] = _norm_bwd(h_in[i], norm_mix[i], dhn_d, du, dha, f"norm_mix_bwd{i}")
    grad_x = dh[None]
    if gl:
        ride = riding(gl)
        landed(ride, _exchange(ride[1].arrs, True, "scatter_grads_rest"))

    out_g, out_d, out_m, out_v = {}, {}, {}, {}
    updated = {}
    for (k, l), parts in recv.items():
        pick = (lambda t: t) if l is None else (lambda t: t[l])
        shp = pick(weights[k]).shape
        r2 = (math.prod(shp[:-1]), shp[-1])
        res = _adamw(parts.reshape((NDEV,) + r2), pick(weights[k]).reshape(r2), pick(mom1[k]).reshape(r2),
                     pick(mom2[k]).reshape(r2), f"adamw_{k}{'' if l is None else l}")
        updated.setdefault(k, {})[l] = [t.reshape(shp) for t in res]
    for k, by_layer in updated.items():
        for n, dst in enumerate((out_g, out_d, out_m, out_v)):
            dst[k] = by_layer[None][n] if None in by_layer else jnp.stack([by_layer[l][n] for l in sorted(by_layer)])

    def ssm_grad(idx, unswap=False):
        g = jnp.stack([g_ssm[i][idx] for i in range(n_ssm)])
        return jnp.swapaxes(g, 2, 3) if unswap else g

    small = {'norm_mix': jnp.concatenate(g_norm_mix, axis=0),
             'ssm_lambda_re': ssm_grad(0), 'ssm_lambda_im': ssm_grad(1), 'ssm_log_dt': ssm_grad(2),
             'ssm_b_re': ssm_grad(3, True), 'ssm_b_im': ssm_grad(4, True),
             'ssm_c_re': ssm_grad(5), 'ssm_c_im': ssm_grad(6),
             'kv_norm': g_kv_norm, 'attn_sinks': jnp.stack(g_sinks),
             'norm_mlp': jnp.concatenate(g_norm_mlp, axis=0), 'norm_ple': jnp.concatenate(g_norm_ple, axis=0),
             'norm_final': g_norm_final}
    snames = list(small)
    sizes = [weights[k].size for k in snames]
    total = sum(sizes)
    lanes = 128
    padded = -(-total // (512 * lanes)) * (512 * lanes)

    def flat(parts):
        v = jnp.concatenate([t.reshape(-1) for t in parts] + [jnp.zeros((padded - total,), F32)])
        return v.reshape(padded // lanes, lanes)

    parts = _exchange([flat([small[k] for k in snames])], False, "gather_small_grads")[0]
    res = _adamw(parts, flat([weights[k] for k in snames]), flat([mom1[k] for k in snames]),
                 flat([mom2[k] for k in snames]), "adamw_small")
    off = 0
    for k, sz in zip(snames, sizes):
        for dst, t in zip((out_g, out_d, out_m, out_v), res):
            dst[k] = t.reshape(-1)[off:off + sz].reshape(weights[k].shape)
        off += sz

    return (loss, grad_x, *[out_g[k] for k in names], *[out_d[k] for k in names],
            *[out_m[k] for k in names], *[out_v[k] for k in names])
```

```python
import functools
import math

import jax
import jax.numpy as jnp
from jax import lax
from jax.experimental import pallas as pl
from jax.experimental.pallas import tpu as pltpu

F32 = jnp.float32
BF16 = jnp.bfloat16
SDS = jax.ShapeDtypeStruct
MESH = pl.DeviceIdType.MESH
AXES = ("x", "y", "c")
NDEV = 8

RMS_EPS = 1e-6
SSM_GROUP = 16
SSM_STATE = 64
SSM_T = 8
SSM_W = SSM_T * SSM_GROUP
HEAD_DIM = 64
GQA_GROUP = 4
ATTN_BLOCK = 128
ROT_DIM = 16
ROPE_THETA = 500000.0
NEG_INF = -1e30
ADAM_LR, ADAM_B1, ADAM_B2, ADAM_EPS, ADAM_WD, ADAM_STEP = 0.001, 0.9, 0.999, 1e-08, 0.01, 10

VMEM_CAP = 56 * 1024 * 1024
HI = lax.Precision.HIGHEST

NN = ((1,), (0,))
NT = ((1,), (1,))
TN = ((0,), (0,))


def _dot(a, b, dims=NN, precision=None):
    return lax.dot_general(a, b, (dims, ((), ())), preferred_element_type=F32, precision=precision)


def _split(a):
    if isinstance(a, tuple):
        return a
    hi = a.astype(BF16)
    return hi, (a - hi.astype(F32)).astype(BF16)


def _dot3(a, b, dims=NN):
    (ah, al), (bh, bl) = _split(a), _split(b)
    return _dot(ah, bh, dims) + (_dot(ah, bl, dims) + _dot(al, bh, dims))


@jax.custom_vjp
def _dot3_nt(a, b):
    return _dot3(a, b, NT)


def _dot3_nt_fwd(a, b):
    return _dot3(a, b, NT), (a, b)


def _dot3_nt_bwd(res, g):
    a, b = res
    return _dot3(g, b, NN), _dot3(g, a, TN)


_dot3_nt.defvjp(_dot3_nt_fwd, _dot3_nt_bwd)


def _tile(n, pref):
    t = min(n, pref)
    while n % t:
        t //= 2
    return t


def _nbytes(shape, dtype):
    return math.prod(s for s in shape if s is not None) * jnp.dtype(dtype).itemsize


def _vmem_limit(blocks, extra=0):
    need = sum(_nbytes(s, d) * n for s, d, n in blocks) + extra + (4 << 20)
    return int(min(VMEM_CAP, max(need, 16 << 20)))


def _pcall(body, *, name, out_shape, grid, in_specs, out_specs, scratch=(), vmem=None, comm=None):
    single = not isinstance(out_shape, (list, tuple))
    out_shape = [out_shape] if single else list(out_shape)
    out_specs = [out_specs] if single else list(out_specs)
    in_specs, scratch = list(in_specs), list(scratch)
    if comm is not None:
        n_in, n_out, n_scr, nx = len(in_specs), len(out_specs), len(scratch), len(comm.arrs)
        hbm = pl.BlockSpec(memory_space=pl.ANY)
        in_specs = in_specs + [hbm] * nx
        out_specs = out_specs + [hbm] * nx
        out_shape = out_shape + comm.out_shape()
        scratch = scratch + comm.semaphores()
        inner = body

        def body(*refs):
            ins, xin, rest = refs[:n_in], refs[n_in:n_in + nx], refs[n_in + nx:]
            outs, xout, rest = rest[:n_out], rest[n_out:n_out + nx], rest[n_out + nx:]
            scr, sems = rest[:n_scr], rest[n_scr:]
            first = functools.reduce(jnp.logical_and, [pl.program_id(a) == 0 for a in range(len(grid))])
            last = functools.reduce(jnp.logical_and, [pl.program_id(a) == g - 1 for a, g in enumerate(grid)])

            @pl.when(first)
            def _():
                comm.start(xin, xout, sems)
            inner(*ins, *outs, *scr)

            @pl.when(last)
            def _():
                comm.wait(xin, xout, sems)

    call = pl.pallas_call(
        body, out_shape=out_shape[0] if single and comm is None else out_shape, grid=grid, in_specs=in_specs,
        out_specs=out_specs[0] if single and comm is None else out_specs, scratch_shapes=scratch, name=name,
        compiler_params=pltpu.CompilerParams(
            dimension_semantics=("arbitrary",) * len(grid), vmem_limit_bytes=vmem),
        interpret=False)
    if comm is None:
        return call
    return lambda *args: call(*args, *comm.arrs)


def _rms(x, g):
    r = lax.rsqrt(jnp.mean(x * x, axis=-1, keepdims=True) + RMS_EPS)
    return x * r * g, r


def _rms_bwd(x, g, r, dy):
    xh = x * r
    dyg = dy * g
    dx = r * (dyg - xh * jnp.mean(dyg * xh, axis=-1, keepdims=True))
    return dx, jnp.sum(dy * xh, axis=0, keepdims=True)


_GELU_C = math.sqrt(2.0 / math.pi)


def _gelu_parts(x):
    t = jnp.tanh(_GELU_C * (x + 0.044715 * x * x * x))
    return 0.5 * x * (1.0 + t), t


def _gelu_grad(x, t):
    return 0.5 * (1.0 + t) + 0.5 * x * (1.0 - t * t) * _GELU_C * (1.0 + 3 * 0.044715 * x * x)


def _rope_tables(seqlen):
    half = ROT_DIM // 2
    inv = ROPE_THETA ** (-jnp.arange(0, ROT_DIM, 2, dtype=F32) / ROT_DIM)
    ang = jnp.arange(seqlen, dtype=jnp.int32).astype(F32)[:, None] * inv[None, :]
    cos, sin = jnp.cos(ang), jnp.sin(ang)
    zeros = jnp.zeros((seqlen, HEAD_DIM - ROT_DIM), F32)
    zh = jnp.zeros((seqlen, half), F32)
    c = jnp.concatenate([cos, cos, zeros + 1.0], axis=1)
    sa = jnp.concatenate([zh, sin, zeros], axis=1)
    sb = jnp.concatenate([-sin, zh, zeros], axis=1)
    return tuple(jnp.tile(t, (1, 128 // HEAD_DIM)) for t in (c, sa, sb))


def _rope(x, c, sa, sb):
    w = x.shape[1]
    reps = w // 128
    half = ROT_DIM // 2
    return (x * jnp.tile(c, (1, reps)) + pltpu.roll(x, half, 1) * jnp.tile(sa, (1, reps))
            + pltpu.roll(x, w - half, 1) * jnp.tile(sb, (1, reps)))


def _rope_bwd(dy, c, sa, sb):
    w = dy.shape[1]
    reps = w // 128
    half = ROT_DIM // 2
    return (dy * jnp.tile(c, (1, reps)) + pltpu.roll(dy * jnp.tile(sa, (1, reps)), w - half, 1)
            + pltpu.roll(dy * jnp.tile(sb, (1, reps)), half, 1))


def _rspec(tm, c):
    return pl.BlockSpec((tm, c), lambda i: (i, 0))


def _cspec(shape, idx=None):
    idx = tuple(idx) if idx is not None else (0,) * len(shape)
    return pl.BlockSpec(tuple(shape), lambda i: idx, pipeline_mode=pl.Buffered(1))


def _rowcall(body, name, seqlen, tm, rows_in, consts_in, rows_out, acc_out=(), extra_vmem=0, comm=None):
    in_specs = [_rspec(tm, a.shape[1]) for a in rows_in] + [_cspec(bs, ix) for _, bs, ix in consts_in]
    out_shape = [SDS((seqlen, c), d) for c, d in rows_out] + [SDS(s, F32) for s in acc_out]
    out_specs = [_rspec(tm, c) for c, _ in rows_out] + [pl.BlockSpec(s, lambda i: (0, 0)) for s in acc_out]
    blocks = ([((tm, a.shape[1]), a.dtype, 2) for a in rows_in] + [(bs, a.dtype, 1) for a, bs, _ in consts_in]
              + [((tm, c), d, 2) for c, d in rows_out])
    temporaries = 12 * tm * rows_in[0].shape[1] * 4
    return _pcall(body, name=name, out_shape=out_shape, grid=(seqlen // tm,), in_specs=in_specs,
                  out_specs=out_specs, vmem=_vmem_limit(blocks, extra_vmem + temporaries), comm=comm)(
                      *rows_in, *[a for a, _, _ in consts_in])


def _whole(a):
    return (a, a.shape, None)


def _norm_fwd(h, g, name, comm=None):
    seqlen, d = h.shape
    tm = _tile(seqlen, 1024)

    def body(h_ref, g_ref, o_ref):
        o_ref[...] = _rms(h_ref[...], g_ref[...])[0]

    return _rowcall(body, name, seqlen, tm, [h], [_whole(g.reshape(1, d))], [(d, F32)], comm=comm)


def _norm_bwd(h, g, dy1, dy2, dres, name, comm=None):
    seqlen, d = h.shape
    tm = _tile(seqlen, 512)

    def body(h_ref, dy1_ref, dy2_ref, dres_ref, g_ref, dh_ref, dg_ref):
        @pl.when(pl.program_id(0) == 0)
        def _():
            dg_ref[...] = jnp.zeros_like(dg_ref)
        x = h_ref[...]
        gv = g_ref[...]
        _, r = _rms(x, gv)
        dx, dg = _rms_bwd(x, gv, r, dy1_ref[...] + dy2_ref[...])
        dh_ref[...] = dres_ref[...] + dx
        dg_ref[...] += dg

    return _rowcall(body, name, seqlen, tm, [h, dy1, dy2, dres], [_whole(g.reshape(1, d))], [(d, F32)], [(1, d)],
                    comm=comm)


def _glu_fwd(y, hn, h, dskip, wglu, name):
    seqlen, d = h.shape
    tm = _tile(seqlen, 512)

    def body(y_ref, hn_ref, h_ref, d_ref, w_ref, o_ref):
        yy = y_ref[...] + d_ref[...] * hn_ref[...]
        ge, _ = _gelu_parts(yy)
        ab = _dot(ge.astype(BF16), w_ref[...])
        o_ref[...] = h_ref[...] + ab[:, :d] * jax.nn.sigmoid(ab[:, d:])

    return _rowcall(body, name, seqlen, tm, [y, hn, h], [_whole(dskip.reshape(1, d)), _whole(wglu)], [(d, F32)],
                    extra_vmem=tm * d * 4 * 6)[0]


def _glu_bwd(y, hn, dmix, dskip, wglu, wglu_t, name):
    seqlen, d = hn.shape
    tm = _tile(seqlen, 512)

    def body(y_ref, hn_ref, dm_ref, d_ref, w_ref, wt_ref, dyy_ref, dhn_ref, ge_ref, dab_ref, dd_ref):
        @pl.when(pl.program_id(0) == 0)
        def _():
            dd_ref[...] = jnp.zeros_like(dd_ref)
        hn_v = hn_ref[...]
        dsk = d_ref[...]
        yy = y_ref[...] + dsk * hn_v
        ge, t = _gelu_parts(yy)
        geb = ge.astype(BF16)
        ab = _dot(geb, w_ref[...])
        a = ab[:, :d]
        sg = jax.nn.sigmoid(ab[:, d:])
        dm = dm_ref[...]
        dab_ref[:, :d] = (dm * sg).astype(BF16)
        dab_ref[:, d:] = (dm * a * sg * (1.0 - sg)).astype(BF16)
        dge = _dot(dab_ref[...], wt_ref[...])
        dyy = dge * _gelu_grad(yy, t)
        dyy_ref[...] = dyy
        dhn_ref[...] = dyy * dsk
        ge_ref[...] = geb
        dd_ref[...] += jnp.sum(dyy * hn_v, axis=0, keepdims=True)

    return _rowcall(body, name, seqlen, tm, [y, hn, dmix],
                    [_whole(dskip.reshape(1, d)), _whole(wglu), _whole(wglu_t)],
                    [(d, F32), (d, F32), (d, BF16), (2 * d, BF16)], [(1, d)], extra_vmem=tm * d * 4 * 8)


def _q_fwd(h, g, wq, tabs, name):
    seqlen, d = h.shape
    tm = _tile(seqlen, 512)

    def body(h_ref, c_ref, sa_ref, sb_ref, g_ref, w_ref, q_ref):
        hn, _ = _rms(h_ref[...], g_ref[...])
        qp = _dot(hn.astype(BF16), w_ref[...])
        q_ref[...] = _rope(qp, c_ref[...], sa_ref[...], sb_ref[...]).astype(BF16)

    return _rowcall(body, name, seqlen, tm, [h, *tabs], [_whole(g.reshape(1, d)), _whole(wq)], [(d, BF16)],
                    extra_vmem=tm * d * 4 * 6)[0]


def _q_bwd(dq, h, g, dres, wq, tabs, name):
    seqlen, d = h.shape
    tm = _tile(seqlen, 512)

    def body(dq_ref, h_ref, dres_ref, c_ref, sa_ref, sb_ref, g_ref, w_ref, dh_ref, dqp_ref, hn_ref, dg_ref):
        @pl.when(pl.program_id(0) == 0)
        def _():
            dg_ref[...] = jnp.zeros_like(dg_ref)
        dqp = _rope_bwd(dq_ref[...], c_ref[...], sa_ref[...], sb_ref[...]).astype(BF16)
        x = h_ref[...]
        gv = g_ref[...]
        hn, r = _rms(x, gv)
        dhn = _dot(dqp, w_ref[...])
        dx, dg = _rms_bwd(x, gv, r, dhn)
        dh_ref[...] = dres_ref[...] + dx
        dqp_ref[...] = dqp
        hn_ref[...] = hn.astype(BF16)
        dg_ref[...] += dg

    return _rowcall(body, name, seqlen, tm, [dq, h, dres, *tabs], [_whole(g.reshape(1, d)), _whole(wq)],
                    [(d, F32), (d, BF16), (d, BF16)], [(1, d)], extra_vmem=tm * d * 4 * 6)


def _kv_fwd(h, g, wk, wv, tabs, name):
    seqlen, d = h.shape
    dk = wk.shape[1]
    tm = _tile(seqlen, 512)

    def body(h_ref, c_ref, sa_ref, sb_ref, g_ref, wk_ref, wv_ref, k_ref, v_ref):
        hk = _rms(h_ref[...], g_ref[...])[0].astype(BF16)
        k_ref[...] = _rope(_dot(hk, wk_ref[...]), c_ref[...], sa_ref[...], sb_ref[...]).astype(BF16)
        v_ref[...] = _dot(hk, wv_ref[...]).astype(BF16)

    return _rowcall(body, name, seqlen, tm, [h, *tabs], [_whole(g.reshape(1, d)), _whole(wk), _whole(wv)],
                    [(dk, BF16), (dk, BF16)], extra_vmem=tm * d * 4 * 4)


def _kv_bwd(dks, dvs, h, g, dres, wk, wv, tabs, name):
    seqlen, d = h.shape
    dkw = wk.shape[0]
    tm = _tile(seqlen, 512)

    def body(dk0_ref, dk1_ref, dv0_ref, dv1_ref, h_ref, dres_ref, c_ref, sa_ref, sb_ref, g_ref, wk_ref, wv_ref,
             dh_ref, dkp_ref, dvb_ref, hk_ref, dg_ref):
        @pl.when(pl.program_id(0) == 0)
        def _():
            dg_ref[...] = jnp.zeros_like(dg_ref)
        dkp = _rope_bwd(dk0_ref[...] + dk1_ref[...], c_ref[...], sa_ref[...], sb_ref[...]).astype(BF16)
        dvb = (dv0_ref[...] + dv1_ref[...]).astype(BF16)
        x = h_ref[...]
        gv = g_ref[...]
        hk, r = _rms(x, gv)
        dhk = _dot(dkp, wk_ref[...]) + _dot(dvb, wv_ref[...])
        dx, dg = _rms_bwd(x, gv, r, dhk)
        dh_ref[...] = dres_ref[...] + dx
        dkp_ref[...] = dkp
        dvb_ref[...] = dvb
        hk_ref[...] = hk.astype(BF16)
        dg_ref[...] += dg

    return _rowcall(body, name, seqlen, tm, [dks[0], dks[1], dvs[0], dvs[1], h, dres, *tabs],
                    [_whole(g.reshape(1, d)), _whole(wk), _whole(wv)],
                    [(d, F32), (dkw, BF16), (dkw, BF16), (d, BF16)], [(1, d)], extra_vmem=tm * d * 4 * 6)


def _lin_res(h, xb, w, name):
    seqlen, d = h.shape
    tm = _tile(seqlen, 512)

    def body(h_ref, x_ref, w_ref, o_ref):
        o_ref[...] = h_ref[...] + _dot(x_ref[...], w_ref[...])

    return _rowcall(body, name, seqlen, tm, [h, xb], [_whole(w)], [(d, F32)], extra_vmem=tm * d * 4 * 2)[0]


def _lin_bf16(dy, w, name):
    seqlen, d = dy.shape
    tm = _tile(seqlen, 512)

    def body(dy_ref, w_ref, o_ref):
        o_ref[...] = _dot(dy_ref[...].astype(BF16), w_ref[...]).astype(BF16)

    return _rowcall(body, name, seqlen, tm, [dy], [_whole(w)], [(w.shape[1], BF16)], extra_vmem=tm * d * 4 * 2)[0]


def _mlp_fwd(h, g, wup, wdn, name, comm=None):
    seqlen, d = h.shape
    f = wup.shape[1]
    tm = _tile(seqlen, 512)

    def body(h_ref, g_ref, wup_ref, wdn_ref, o_ref, act_ref):
        x = h_ref[...]
        hm = _rms(x, g_ref[...])[0].astype(BF16)
        r = jnp.maximum(_dot(hm, wup_ref[...]), 0.0)
        act = (r * r).astype(BF16)
        act_ref[...] = act
        o_ref[...] = x + _dot(act, wdn_ref[...])

    consts = [_whole(g.reshape(1, d)), _whole(wup), _whole(wdn)]
    return _rowcall(body, name, seqlen, tm, [h], consts, [(d, F32), (f, BF16)], extra_vmem=tm * f * 4 * 3, comm=comm)


def _mlp_bwd(h, act, dh, g, wup_t, wdn_t, name, comm=None):
    seqlen, d = h.shape
    f = wup_t.shape[0]
    tm = _tile(seqlen, 512)

    def body(h_ref, act_ref, dh_ref, g_ref, wup_ref, wdn_ref, dhin_ref, hm_ref, da_ref, dg_ref):
        @pl.when(pl.program_id(0) == 0)
        def _():
            dg_ref[...] = jnp.zeros_like(dg_ref)
        x = h_ref[...]
        gv = g_ref[...]
        dy = dh_ref[...]
        hm, r = _rms(x, gv)
        rl2 = 2.0 * jnp.sqrt(act_ref[...].astype(F32))
        da = (_dot(dy.astype(BF16), wdn_ref[...]) * rl2).astype(BF16)
        da_ref[...] = da
        dx, dg = _rms_bwd(x, gv, r, _dot(da, wup_ref[...]))
        dhin_ref[...] = dy + dx
        hm_ref[...] = hm.astype(BF16)
        dg_ref[...] += dg

    consts = [_whole(g.reshape(1, d)), _whole(wup_t), _whole(wdn_t)]
    return _rowcall(body, name, seqlen, tm, [h, act, dh], consts, [(d, F32), (d, BF16), (f, BF16)], [(1, d)],
                    extra_vmem=tm * f * 4 * 3, comm=comm)


def _ple_fwd(h, p, g, wg, wpp, name):
    seqlen, d = h.shape
    tm = _tile(seqlen, 512)

    def body(h_ref, p_ref, g_ref, wg_ref, wpp_ref, o_ref):
        x = h_ref[...]
        n = _rms(x, g_ref[...])[0].astype(BF16)
        gate = jax.nn.sigmoid(_dot(n, wg_ref[...]))
        o_ref[...] = x + gate * _dot(p_ref[...].astype(BF16), wpp_ref[...])

    return _rowcall(body, name, seqlen, tm, [h, p], [_whole(g.reshape(1, d)), _whole(wg), _whole(wpp)], [(d, F32)],
                    extra_vmem=tm * d * 4 * 5)[0]


def _ple_bwd(h, p, dh, g, wg, wg_t, wpp, name):
    seqlen, d = h.shape
    tm = _tile(seqlen, 512)

    def body(h_ref, p_ref, dh_ref, g_ref, wg_ref, wgt_ref, wpp_ref, dhin_ref, dz_ref, n_ref, dpp_ref, dg_ref):
        @pl.when(pl.program_id(0) == 0)
        def _():
            dg_ref[...] = jnp.zeros_like(dg_ref)
        x = h_ref[...]
        gv = g_ref[...]
        dy = dh_ref[...]
        n, r = _rms(x, gv)
        nb16 = n.astype(BF16)
        gate = jax.nn.sigmoid(_dot(nb16, wg_ref[...]))
        pp = _dot(p_ref[...].astype(BF16), wpp_ref[...])
        dz = (dy * pp * gate * (1.0 - gate)).astype(BF16)
        dn = _dot(dz, wgt_ref[...])
        dx, dg = _rms_bwd(x, gv, r, dn)
        dhin_ref[...] = dy + dx
        dz_ref[...] = dz
        n_ref[...] = nb16
        dpp_ref[...] = (dy * gate).astype(BF16)
        dg_ref[...] += dg

    return _rowcall(body, name, seqlen, tm, [h, p, dh],
                    [_whole(g.reshape(1, d)), _whole(wg), _whole(wg_t), _whole(wpp)],
                    [(d, F32), (d, BF16), (d, BF16), (d, BF16)], [(1, d)], extra_vmem=tm * d * 4 * 8)


def _loss_bwd(h, g, tgt, name):
    seqlen, d = h.shape
    tm = _tile(seqlen, 512)

    def body(h_ref, t_ref, g_ref, dh_ref, loss_ref, dg_ref):
        @pl.when(pl.program_id(0) == 0)
        def _():
            dg_ref[...] = jnp.zeros_like(dg_ref)
            loss_ref[...] = jnp.zeros_like(loss_ref)
        x = h_ref[...]
        gv = g_ref[...]
        y, r = _rms(x, gv)
        diff = y - t_ref[...]
        loss_ref[...] += (0.5 / d) * jnp.sum(jnp.sum(diff * diff, axis=1, keepdims=True), axis=0, keepdims=True)
        dx, dg = _rms_bwd(x, gv, r, diff * (1.0 / d))
        dh_ref[...] = dx
        dg_ref[...] += dg

    return _rowcall(body, name, seqlen, tm, [h, tgt], [_whole(g.reshape(1, d))], [(d, F32)], [(1, 128), (1, d)])


def _atb(a, b, col_blocked, name):
    seqlen, k1 = a.shape
    k2 = b.shape[1]
    if col_blocked:
        cs = k2 // NDEV
        t1 = _tile(k1, 512)
        nblk = _tile(NDEV, max(1, 2048 // cs))
        t2 = nblk * cs
        oshape = (NDEV, k1, cs)
        oblock = (nblk, t1, cs)
        omap = lambda i, j, l: (j, i, 0)
    else:
        rs = k1 // NDEV
        t2 = _tile(k2, 2048)
        nblk = _tile(NDEV, max(1, 1024 // rs))
        t1 = nblk * rs
        oshape = (NDEV, rs, k2)
        oblock = (nblk, rs, t2)
        omap = lambda i, j, l: (i, 0, j)
    tl = _tile(seqlen, 2048 if b.dtype == BF16 else 1024)

    def body(a_ref, b_ref, o_ref):
        @pl.when(pl.program_id(2) == 0)
        def _():
            o_ref[...] = jnp.zeros_like(o_ref)
        res = _dot(a_ref[...].astype(BF16), b_ref[...].astype(BF16), TN)
        for n in range(nblk):
            if col_blocked:
                o_ref[n] += res[:, n * cs:(n + 1) * cs]
            else:
                o_ref[n] += res[n * rs:(n + 1) * rs, :]

    blocks = [((tl, t1), a.dtype, 2), ((tl, t2), b.dtype, 2), ((t1, t2), F32, 2)]
    return _pcall(
        body, name=name, out_shape=SDS(oshape, F32), grid=(k1 // t1, k2 // t2, seqlen // tl),
        in_specs=[pl.BlockSpec((tl, t1), lambda i, j, l: (l, i)), pl.BlockSpec((tl, t2), lambda i, j, l: (l, j))],
        out_specs=pl.BlockSpec(oblock, omap),
        vmem=_vmem_limit(blocks, extra=t1 * t2 * 4 + tl * (t1 + t2) * 2))(a, b)


_ATTN_SCALE = HEAD_DIM ** -0.5


def _attn_bias():
    qi = lax.broadcasted_iota(jnp.int32, (ATTN_BLOCK, 2 * ATTN_BLOCK), 0) + ATTN_BLOCK
    kj = lax.broadcasted_iota(jnp.int32, (ATTN_BLOCK, 2 * ATTN_BLOCK), 1)
    band = (kj <= qi) & (qi - kj < ATTN_BLOCK)
    return jnp.where(jnp.stack([band & (kj >= ATTN_BLOCK), band]), 0.0, NEG_INF).astype(F32)


def _bias_spec():
    return pl.BlockSpec((None, ATTN_BLOCK, 2 * ATTN_BLOCK), lambda n: (jnp.minimum(n, 1), 0, 0))


def _attn_probs(q4s, kks, sink_col, bias):
    s = jnp.concatenate([_dot(q4, kk, NT) for q4, kk in zip(q4s, kks)], axis=0)
    rows = s.shape[0]
    s = (s.reshape(rows // ATTN_BLOCK, ATTN_BLOCK, 2 * ATTN_BLOCK) + bias).reshape(rows, 2 * ATTN_BLOCK)
    m = jnp.maximum(jnp.max(s, axis=1, keepdims=True), sink_col)
    pr = jnp.exp(s - m)
    es = jnp.exp(sink_col - m)
    inv = 1.0 / (jnp.sum(pr, axis=1, keepdims=True) + es)
    return pr * inv, es * inv


def _sink_col(sink_ref, nheads):
    return jnp.concatenate([jnp.full((ATTN_BLOCK, 1), sink_ref[hq], F32) for hq in range(nheads)], axis=0)


def _kv_pair(p_ref, c_ref, kh):
    sl = slice(kh * HEAD_DIM, (kh + 1) * HEAD_DIM)
    return jnp.concatenate([p_ref[:, sl], c_ref[:, sl]], axis=0)


def _stack_heads(ref, kh, scale=None):
    x = jnp.concatenate(
        [ref[:, (kh * GQA_GROUP + g) * HEAD_DIM:(kh * GQA_GROUP + g + 1) * HEAD_DIM] for g in range(GQA_GROUP)], axis=0)
    return x if scale is None else x * scale


def _attn_fwd(q, k, v, sinks, name):
    seqlen, d = q.shape
    dkv = k.shape[1]
    nkv = dkv // HEAD_DIM
    nb = seqlen // ATTN_BLOCK
    blk = ATTN_BLOCK

    def body(sink_ref, bias_ref, q_ref, kc_ref, kp_ref, vc_ref, vp_ref, o_ref):
        q4s = [_stack_heads(q_ref, kh, _ATTN_SCALE) for kh in range(nkv)]
        kks = [_kv_pair(kp_ref, kc_ref, kh) for kh in range(nkv)]
        w, _ = _attn_probs(q4s, kks, _sink_col(sink_ref, nkv * GQA_GROUP), bias_ref[...])
        wb = w.astype(BF16)
        for kh in range(nkv):
            o4 = _dot(wb[kh * GQA_GROUP * blk:(kh + 1) * GQA_GROUP * blk, :], _kv_pair(vp_ref, vc_ref, kh))
            for g in range(GQA_GROUP):
                hq = kh * GQA_GROUP + g
                o_ref[:, hq * HEAD_DIM:(hq + 1) * HEAD_DIM] = o4[g * blk:(g + 1) * blk, :].astype(BF16)

    cur = lambda n: (n, 0)
    prev = lambda n: (jnp.maximum(n - 1, 0), 0)
    return _pcall(
        body, name=name, out_shape=SDS((seqlen, d), BF16), grid=(nb,),
        in_specs=[pl.BlockSpec(memory_space=pltpu.SMEM), _bias_spec(), pl.BlockSpec((blk, d), cur),
                  pl.BlockSpec((blk, dkv), cur), pl.BlockSpec((blk, dkv), prev),
                  pl.BlockSpec((blk, dkv), cur), pl.BlockSpec((blk, dkv), prev)],
        out_specs=pl.BlockSpec((blk, d), cur), vmem=32 << 20)(sinks, _attn_bias(), q, k, k, v, v)


def _attn_bwd(q, k, v, do, sinks, name, comm=None):
    seqlen, d = q.shape
    dkv = k.shape[1]
    nkv = dkv // HEAD_DIM
    nh = d // HEAD_DIM
    nb = seqlen // ATTN_BLOCK
    blk = ATTN_BLOCK

    def body(sink_ref, bias_ref, q_ref, kc_ref, kp_ref, vc_ref, vp_ref, do_ref, dq_ref, dk_ref, dv_ref, ds_ref,
             ck_ref, cv_ref):
        n = pl.program_id(0)

        @pl.when(n == 0)
        def _():
            ck_ref[...] = jnp.zeros_like(ck_ref)
            cv_ref[...] = jnp.zeros_like(cv_ref)
            ds_ref[...] = jnp.zeros_like(ds_ref)

        @pl.when(n == nb)
        def _():
            dk_ref[...] = ck_ref[...]
            dv_ref[...] = cv_ref[...]

        @pl.when(n < nb)
        def _():
            q4s = [_stack_heads(q_ref, kh, _ATTN_SCALE) for kh in range(nkv)]
            do4s = [_stack_heads(do_ref, kh) for kh in range(nkv)]
            kks = [_kv_pair(kp_ref, kc_ref, kh) for kh in range(nkv)]
            w, wsink = _attn_probs(q4s, kks, _sink_col(sink_ref, nh), bias_ref[...])
            dw = jnp.concatenate([_dot(do4s[kh], _kv_pair(vp_ref, vc_ref, kh), NT) for kh in range(nkv)], axis=0)
            dsum = jnp.sum(w * dw, axis=1, keepdims=True)
            ds_all = (w * (dw - dsum)).astype(BF16)
            wb = w.astype(BF16)
            dsk = -wsink * dsum
            for kh in range(nkv):
                sl = slice(kh * HEAD_DIM, (kh + 1) * HEAD_DIM)
                rows = slice(kh * GQA_GROUP * blk, (kh + 1) * GQA_GROUP * blk)
                ds = ds_all[rows, :]
                dq4 = _dot(ds, kks[kh]) * _ATTN_SCALE
                dkk = _dot(ds, q4s[kh], TN)
                dvv = _dot(wb[rows, :], do4s[kh], TN)
                for g in range(GQA_GROUP):
                    hq = kh * GQA_GROUP + g
                    dq_ref[:, hq * HEAD_DIM:(hq + 1) * HEAD_DIM] = dq4[g * blk:(g + 1) * blk, :]
                    ds_ref[hq:hq + 1, :] += jnp.sum(dsk[hq * blk:(hq + 1) * blk, :], axis=0, keepdims=True)
                dk_ref[:, sl] = ck_ref[:, sl] + dkk[:blk, :]
                dv_ref[:, sl] = cv_ref[:, sl] + dvv[:blk, :]
                ck_ref[:, sl] = dkk[blk:, :]
                cv_ref[:, sl] = dvv[blk:, :]

    cur = lambda n: (jnp.minimum(n, nb - 1), 0)
    prev = lambda n: (jnp.clip(n - 1, 0, nb - 1), 0)
    lag = lambda n: (jnp.maximum(n - 1, 0), 0)
    return _pcall(
        body, name=name,
        out_shape=[SDS((seqlen, d), F32), SDS((seqlen, dkv), F32), SDS((seqlen, dkv), F32), SDS((nh, 128), F32)],
        grid=(nb + 1,),
        in_specs=[pl.BlockSpec(memory_space=pltpu.SMEM), _bias_spec(), pl.BlockSpec((blk, d), cur),
                  pl.BlockSpec((blk, dkv), cur), pl.BlockSpec((blk, dkv), prev),
                  pl.BlockSpec((blk, dkv), cur), pl.BlockSpec((blk, dkv), prev), pl.BlockSpec((blk, d), cur)],
        out_specs=[pl.BlockSpec((blk, d), cur), pl.BlockSpec((blk, dkv), lag), pl.BlockSpec((blk, dkv), lag),
                   pl.BlockSpec((nh, 128), lambda n: (0, 0))],
        scratch=[pltpu.VMEM((blk, dkv), F32)] * 2, vmem=32 << 20, comm=comm)(sinks, _attn_bias(), q, k, k, v, v, do)


def _ssm_mats(lre, lim, ldt, btr, bti, cr, ci):
    dt = jnp.exp(ldt)
    mag = jnp.exp(lre * dt)
    ar = mag * jnp.cos(lim * dt)
    ai = mag * jnp.sin(lim * dt)
    den = lre * lre + lim * lim
    nr = ar - 1.0
    cfr = (nr * lre + ai * lim) / den
    cfi = (ai * lre - nr * lim) / den
    bbr = cfr * btr - cfi * bti
    bbi = cfr * bti + cfi * btr
    pr = [jnp.ones_like(ar)]
    pi = [jnp.zeros_like(ai)]
    for _ in range(SSM_T):
        pr.append(pr[-1] * ar - pi[-1] * ai)
        pi.append(pr[-2] * ai + pi[-1] * ar)
    last = SSM_T - 1
    p_re = jnp.concatenate([pr[last - s] * bbr - pi[last - s] * bbi for s in range(SSM_T)], axis=0)
    p_im = jnp.concatenate([pr[last - s] * bbi + pi[last - s] * bbr for s in range(SSM_T)], axis=0)
    qt_re = jnp.concatenate([pr[t + 1] * cr - pi[t + 1] * ci for t in range(SSM_T)], axis=0)
    qt_im = jnp.concatenate([-(pr[t + 1] * ci + pi[t + 1] * cr) for t in range(SSM_T)], axis=0)
    ctr = jnp.concatenate([cr] * SSM_T, axis=0)
    cti = jnp.concatenate([ci] * SSM_T, axis=0)
    lag = (lax.broadcasted_iota(jnp.int32, (SSM_W, SSM_W), 1) // SSM_GROUP
           - lax.broadcasted_iota(jnp.int32, (SSM_W, SSM_W), 0) // SSM_GROUP)
    m = jnp.zeros((SSM_W, SSM_W), F32)
    for l in range(SSM_T):
        zr = jnp.concatenate([pr[l] * bbr - pi[l] * bbi] * SSM_T, axis=0)
        zi = jnp.concatenate([pr[l] * bbi + pi[l] * bbr] * SSM_T, axis=0)
        kl = _dot3_nt(zr, ctr) - _dot3_nt(zi, cti)
        m = m + jnp.where(lag == l, kl, 0.0)
    return m, p_re, p_im, qt_re, qt_im, pr[SSM_T], pi[SSM_T]


_SSM_GB = 8


def _ssm_param_specs(ng):
    n, hh = SSM_STATE, SSM_GROUP
    gb = _tile(ng, _SSM_GB)
    row = pl.BlockSpec((gb, 1, n), lambda i: (i, 0, 0))
    one = pl.BlockSpec((gb, 1, 1), lambda i: (i, 0, 0))
    mat = pl.BlockSpec((gb, hh, n), lambda i: (i, 0, 0))
    big = pl.BlockSpec((gb, SSM_W, SSM_W), lambda i: (i, 0, 0))
    half = pl.BlockSpec((gb, SSM_W, n), lambda i: (i, 0, 0))
    return gb, row, one, mat, big, half


def _ssm_prep(params, name):
    ng = params[0].shape[0]
    n = SSM_STATE
    gb, row, one, mat, big, half = _ssm_param_specs(ng)

    def body(lre, lim, ldt, btr, bti, cr, ci, m_ref, pre_ref, pim_ref, qre_ref, qim_ref, atr_ref, ati_ref):
        for gi in range(gb):
            outs = _ssm_mats(lre[gi], lim[gi], ldt[gi], btr[gi], bti[gi], cr[gi], ci[gi])
            for ref, val in zip((m_ref, pre_ref, pim_ref, qre_ref, qim_ref, atr_ref, ati_ref), outs):
                ref[gi] = val

    return _pcall(
        body, name=name,
        out_shape=[SDS((ng, SSM_W, SSM_W), F32)] + [SDS((ng, SSM_W, n), F32)] * 4 + [SDS((ng, 1, n), F32)] * 2,
        grid=(ng // gb,), in_specs=[row, row, one, mat, mat, mat, mat],
        out_specs=[big, half, half, half, half, row, row], vmem=40 << 20)(*params)


def _ssm_prep_vjp(params, cots, name):
    ng = params[0].shape[0]
    n, hh = SSM_STATE, SSM_GROUP
    gb, row, one, mat, big, half = _ssm_param_specs(ng)

    def body(lre, lim, ldt, btr, bti, cr, ci, dm, dpre, dpim, dqre, dqim, datr, dati,
             o_lre, o_lim, o_ldt, o_btr, o_bti, o_cr, o_ci):
        for gi in range(gb):
            prm = (lre[gi], lim[gi], ldt[gi], btr[gi], bti[gi], cr[gi], ci[gi])
            _, pull = jax.vjp(_ssm_mats, *prm)
            grads = pull((dm[gi], dpre[gi], dpim[gi], dqre[gi], dqim[gi], datr[gi], dati[gi]))
            for ref, val in zip((o_lre, o_lim, o_ldt, o_btr, o_bti, o_cr, o_ci), grads):
                ref[gi] = val

    return _pcall(
        body, name=name,
        out_shape=[SDS((ng, 1, n), F32)] * 2 + [SDS((ng, 1, 1), F32)] + [SDS((ng, hh, n), F32)] * 4,
        grid=(ng // gb,), in_specs=[row, row, one, mat, mat, mat, mat, big, half, half, half, half, row, row],
        out_specs=[row, row, one, mat, mat, mat, mat], vmem=48 << 20)(*params, *cots)


_SSM_GT = SSM_W // SSM_GROUP


def _blk_transpose(xs):
    assert len(xs) == SSM_T == _SSM_GT
    blk = lax.broadcasted_iota(jnp.int32, xs[0].shape, 1) // SSM_GROUP
    xs = list(xs)
    k = SSM_T // 2
    while k:
        high = (blk // k) % 2 == 1
        nxt = []
        for i in range(SSM_T):
            if i & k:
                nxt.append(jnp.where(high, xs[i], pltpu.roll(xs[i ^ k], SSM_W - SSM_GROUP * k, 1)))
            else:
                nxt.append(jnp.where(high, pltpu.roll(xs[i ^ k], SSM_GROUP * k, 1), xs[i]))
        xs = nxt
        k //= 2
    return xs


def _tile_groups(x_ref, ncb):
    return _blk_transpose([x_ref[pl.ds(t, ncb, stride=SSM_T), :] for t in range(SSM_T)])


def _groups_tile(ys, o_ref, ncb):
    for t, y in enumerate(_blk_transpose(ys)):
        o_ref[pl.ds(t, ncb, stride=SSM_T), :] = y


def _ssm_specs(seqlen, d):
    ncb = _tile(seqlen // SSM_T, 512)
    grid = (d // SSM_W, seqlen // (SSM_T * ncb))
    act = pl.BlockSpec((SSM_T * ncb, SSM_W), lambda j, r: (r, j))
    state = pl.BlockSpec((ncb, _SSM_GT * SSM_W), lambda j, r: (r, j))
    mats = pl.BlockSpec((_SSM_GT, SSM_W, SSM_W), lambda j, r: (j, 0, 0))
    return ncb, grid, act, state, mats


def _gsl(gl):
    return slice(gl * SSM_W, (gl + 1) * SSM_W)


def _ssm_state_in(hn, pmat, name):
    seqlen, d = hn.shape
    ncb, grid, act, state, mats = _ssm_specs(seqlen, d)

    def body(x_ref, p_ref, s_ref):
        us = _tile_groups(x_ref, ncb)
        for gl in range(_SSM_GT):
            s_ref[:, _gsl(gl)] = _dot3(us[gl], p_ref[gl], NN)

    return _pcall(body, name=name, out_shape=SDS((seqlen // SSM_T, d * SSM_T), F32), grid=grid,
                  in_specs=[act, mats], out_specs=state, vmem=40 << 20)(hn, pmat)


def _ssm_out(hn, xp, mmat, qt, name, comm=None):
    seqlen, d = hn.shape
    ncb, grid, act, state, mats = _ssm_specs(seqlen, d)

    def body(x_ref, xp_ref, m_ref, q_ref, y_ref):
        us = _tile_groups(x_ref, ncb)
        ys = [_dot3(us[gl], m_ref[gl], NN) + _dot3(xp_ref[:, _gsl(gl)], q_ref[gl], NT)
              for gl in range(_SSM_GT)]
        _groups_tile(ys, y_ref, ncb)

    return _pcall(body, name=name, out_shape=[SDS((seqlen, d), F32)], grid=grid,
                  in_specs=[act, state, mats, mats], out_specs=[act], vmem=40 << 20, comm=comm)(hn, xp, mmat, qt)


def _ssm_dstate(dy, qt, name):
    seqlen, d = dy.shape
    ncb, grid, act, state, mats = _ssm_specs(seqlen, d)

    def body(dy_ref, q_ref, o_ref):
        dys = _tile_groups(dy_ref, ncb)
        for gl in range(_SSM_GT):
            o_ref[:, _gsl(gl)] = _dot3(dys[gl], q_ref[gl], NN)

    return _pcall(body, name=name, out_shape=SDS((seqlen // SSM_T, d * SSM_T), F32), grid=grid,
                  in_specs=[act, mats], out_specs=state, vmem=40 << 20)(dy, qt)


def _ssm_bwd(hn, dy, xp, gs, mmat, pmat, name, comm=None):
    seqlen, d = hn.shape
    ng = d // SSM_GROUP
    ncb, grid, act, state, mats = _ssm_specs(seqlen, d)

    def body(x_ref, dy_ref, xp_ref, g_ref, m_ref, p_ref, du_ref, dm_ref, dp_ref, dq_ref, da_ref):
        @pl.when(pl.program_id(1) == 0)
        def _():
            for ref in (dm_ref, dp_ref, dq_ref, da_ref):
                ref[...] = jnp.zeros_like(ref)
        us = _tile_groups(x_ref, ncb)
        dys = _tile_groups(dy_ref, ncb)
        dus = []
        for gl in range(_SSM_GT):
            xv, gv = xp_ref[:, _gsl(gl)], g_ref[:, _gsl(gl)]
            u2, dy2, x2, g2 = _split(us[gl]), _split(dys[gl]), _split(xv), _split(gv)
            dus.append(_dot3(dy2, m_ref[gl], NT) + _dot3(g2, p_ref[gl], NT))
            dm_ref[gl] += _dot3(u2, dy2, TN)
            dp_ref[gl] += _dot3(u2, g2, TN)
            dq_ref[gl] += _dot3(dy2, x2, TN)
            da_ref[gl, 0:1, :] += jnp.sum(xv * gv, axis=0, keepdims=True)
            da_ref[gl, 1:2, :] += jnp.sum(xv * pltpu.roll(gv, SSM_STATE, 1), axis=0, keepdims=True)
        _groups_tile(dus, du_ref, ncb)

    return _pcall(
        body, name=name,
        out_shape=[SDS((seqlen, d), F32)] + [SDS((ng, SSM_W, SSM_W), F32)] * 3 + [SDS((ng, 2, SSM_W), F32)],
        grid=grid, in_specs=[act, act, state, state, mats, mats],
        out_specs=[act, mats, mats, mats, pl.BlockSpec((_SSM_GT, 2, SSM_W), lambda j, r: (j, 0, 0))],
        vmem=48 << 20, comm=comm)(hn, dy, xp, gs, mmat, pmat)


def _ssm_carry(s, a1, a2, reverse, name, comm=None):
    nc, w = s.shape
    tc = _tile(nc, 256)
    nblk = nc // tc
    sub = 8
    shape = (sub, SSM_W)

    def body(s_ref, a1_ref, a2_ref, o_ref, st_ref, pw_ref, p1_ref, p2_ref):
        rows = lax.broadcasted_iota(jnp.int32, (sub, w), 0)
        sign = jnp.where(lax.broadcasted_iota(jnp.int32, (1, w), 1) % SSM_W < SSM_STATE, -1.0, 1.0)

        @pl.when(pl.program_id(0) == 0)
        def _():
            st_ref[...] = jnp.zeros_like(st_ref)
            cr, ci = a1_ref[...], a2_ref[...] * sign
            qr, qi = jnp.ones_like(cr), jnp.zeros_like(ci)
            p1, p2 = jnp.zeros((sub, w), F32), jnp.zeros((sub, w), F32)
            for r in range(sub):
                at = (sub - 1 - r) if reverse else r
                p1 = jnp.where(rows == at, qr, p1)
                p2 = jnp.where(rows == at, qi * sign, p2)
                qr, qi = qr * cr - qi * ci, qr * ci + qi * cr
            p1_ref[...] = p1
            p2_ref[...] = p2
            for k in range(4):
                pw_ref[2 * k:2 * k + 1, :] = cr
                pw_ref[2 * k + 1:2 * k + 2, :] = ci * sign
                cr, ci = cr * cr - ci * ci, 2.0 * cr * ci

        row = lax.broadcasted_iota(jnp.int32, shape, 0)

        def cmul(c1, c2, v):
            return c1 * v + c2 * pltpu.roll(v, SSM_STATE, 1)

        def shifted(v, k):
            if reverse:
                return jnp.where(row < sub - k, pltpu.roll(v, sub - k, 0), 0.0)
            return jnp.where(row >= k, pltpu.roll(v, k, 0), 0.0)

        def step(t, carry):
            tt = (tc // sub - 1 - t) if reverse else t
            base = pl.multiple_of(tt * sub, sub)
            for g in range(w // SSM_W):
                lanes = slice(g * SSM_W, (g + 1) * SSM_W)
                pw = [jnp.broadcast_to(pw_ref[i:i + 1, lanes], shape) for i in range(8)]
                y = s_ref[pl.ds(base, sub), lanes]
                for k in range(3):
                    y = y + cmul(pw[2 * k], pw[2 * k + 1], shifted(y, 1 << k))
                x = st_ref[:, lanes]
                o_ref[pl.ds(base, sub), lanes] = shifted(y, 1) + cmul(p1_ref[:, lanes], p2_ref[:, lanes], x)
                end = y[0:1, :] if reverse else y[sub - 1:sub, :]
                st_ref[:, lanes] = jnp.broadcast_to(end, shape) + cmul(pw[6], pw[7], x)
            return carry

        lax.fori_loop(0, tc // sub, step, 0)

    imap = (lambda i: (nblk - 1 - i, 0)) if reverse else (lambda i: (i, 0))
    cst = pl.BlockSpec((1, w), lambda i: (0, 0))
    return _pcall(body, name=name, out_shape=[SDS((nc, w), F32)], grid=(nblk,),
                  in_specs=[pl.BlockSpec((tc, w), imap), cst, cst], out_specs=[pl.BlockSpec((tc, w), imap)],
                  scratch=[pltpu.VMEM((sub, w), F32)] * 4,
                  vmem=_vmem_limit([((tc, w), F32, 4)], extra=8 << 20), comm=comm)(s, a1, a2)


def _ssm_rows(atr, ati, conj):
    ng = atr.shape[0]
    ai = -ati if conj else ati
    a1 = jnp.concatenate([atr, atr], axis=2).reshape(1, ng * SSM_W)
    a2 = jnp.concatenate([-ai, ai], axis=2).reshape(1, ng * SSM_W)
    return a1, a2


def _peers():
    x, y, c = (lax.axis_index(a) for a in AXES)
    me = 4 * x + 2 * y + c
    peers = []
    for dx, dy, dc in [(0, 0, 1), (0, 1, 0), (0, 1, 1), (1, 0, 0), (1, 0, 1), (1, 1, 0), (1, 1, 1)]:
        px, py, pc = (1 - x) if dx else x, (1 - y) if dy else y, (1 - c) if dc else c
        peers.append(((px, py, pc), 4 * px + 2 * py + pc))
    return me, peers


class _Exchange:
    def __init__(self, arrs, scatter, layers=None):
        self.arrs = list(arrs)
        self.scatter = scatter
        self.layers = list(layers) if layers is not None else [None] * len(self.arrs)

    def out_shape(self):
        shapes = []
        for arr, layer in zip(self.arrs, self.layers):
            block = arr.shape[1:] if (self.scatter or layer is not None) else arr.shape
            shapes.append(SDS((NDEV,) + tuple(block), arr.dtype))
        return shapes

    def semaphores(self):
        n = len(self.arrs)
        return [pltpu.SemaphoreType.DMA((n * (NDEV - 1),)), pltpu.SemaphoreType.DMA((n * (NDEV - 1),)),
                pltpu.SemaphoreType.DMA((n,))]

    def _src(self, ref, a, block):
        if self.scatter:
            return ref.at[block]
        return ref if self.layers[a] is None else ref.at[self.layers[a]]

    def _remote(self, xin, xout, sems, a, k, peer, landing):
        pid, pidx = peer
        slot = a * (NDEV - 1) + k
        return pltpu.make_async_remote_copy(
            src_ref=self._src(xin[a], a, pidx), dst_ref=xout[a].at[landing],
            send_sem=sems[0].at[slot], recv_sem=sems[1].at[slot], device_id=pid, device_id_type=MESH)

    def _local(self, xin, xout, sems, a, me):
        return pltpu.make_async_copy(self._src(xin[a], a, me), xout[a].at[me], sems[2].at[a])

    def start(self, xin, xout, sems):
        me, peers = _peers()
        for a in range(len(self.arrs)):
            self._local(xin, xout, sems, a, me).start()
        for k, peer in enumerate(peers):
            for a in range(len(self.arrs)):
                self._remote(xin, xout, sems, a, k, peer, me).start()

    def wait(self, xin, xout, sems):
        me, peers = _peers()
        for a in range(len(self.arrs)):
            self._local(xin, xout, sems, a, me).wait()
        for k, peer in enumerate(peers):
            for a in range(len(self.arrs)):
                cp = self._remote(xin, xout, sems, a, k, peer, peer[1])
                cp.wait_send()
                cp.wait_recv()


def _exchange(arrs, scatter, name, layers=None):
    comm = _Exchange(arrs, scatter, layers)
    n = len(comm.arrs)

    def body(*refs):
        xin, xout, sems = refs[:n], refs[n:2 * n], refs[2 * n:]
        comm.start(xin, xout, sems)
        comm.wait(xin, xout, sems)

    hbm = pl.BlockSpec(memory_space=pl.ANY)
    return pl.pallas_call(
        body, out_shape=comm.out_shape(), in_specs=[hbm] * n, out_specs=[hbm] * n,
        scratch_shapes=comm.semaphores(), name=name, interpret=False)(*comm.arrs)


def _adamw(parts, w, m, v, name):
    rows, cols = w.shape
    tr = _tile(rows, max(8, (1 << 17) // cols))
    c1 = 1.0 - ADAM_B1 ** ADAM_STEP
    c2 = 1.0 - ADAM_B2 ** ADAM_STEP

    def body(p_ref, w_ref, m_ref, v_ref, g_ref, d_ref, nm_ref, nv_ref):
        g = p_ref[0]
        for j in range(1, NDEV):
            g = g + p_ref[j]
        mm = ADAM_B1 * m_ref[...] + (1.0 - ADAM_B1) * g
        vv = ADAM_B2 * v_ref[...] + (1.0 - ADAM_B2) * (g * g)
        g_ref[...] = g
        nm_ref[...] = mm
        nv_ref[...] = vv
        d_ref[...] = -ADAM_LR * ((mm / c1) / (jnp.sqrt(vv / c2) + ADAM_EPS) + ADAM_WD * w_ref[...])

    spec = pl.BlockSpec((tr, cols), lambda i: (i, 0))
    return _pcall(
        body, name=name, out_shape=[SDS((rows, cols), F32)] * 4, grid=(rows // tr,),
        in_specs=[pl.BlockSpec((NDEV, tr, cols), lambda i: (0, i, 0)), spec, spec, spec], out_specs=[spec] * 4,
        vmem=_vmem_limit([((NDEV + 7, tr, cols), F32, 2)]))(parts, w, m, v)


def kernel(x, p, norm_mix, ssm_lambda_re, ssm_lambda_im, ssm_log_dt, ssm_b_re, ssm_b_im, ssm_c_re, ssm_c_im, ssm_d, ssm_w_glu, kv_norm, w_k, w_v, w_q, attn_sinks, w_o, norm_mlp, w_up, w_down, norm_ple, w_ple_gate, w_ple_proj, norm_final, loss_target, m_norm_mix, m_ssm_lambda_re, m_ssm_lambda_im, m_ssm_log_dt, m_ssm_b_re, m_ssm_b_im, m_ssm_c_re, m_ssm_c_im, m_ssm_d, m_ssm_w_glu, m_kv_norm, m_w_k, m_w_v, m_w_q, m_attn_sinks, m_w_o, m_norm_mlp, m_w_up, m_w_down, m_norm_ple, m_w_ple_gate, m_w_ple_proj, m_norm_final, v_norm_mix, v_ssm_lambda_re, v_ssm_lambda_im, v_ssm_log_dt, v_ssm_b_re, v_ssm_b_im, v_ssm_c_re, v_ssm_c_im, v_ssm_d, v_ssm_w_glu, v_kv_norm, v_w_k, v_w_v, v_w_q, v_attn_sinks, v_w_o, v_norm_mlp, v_w_up, v_w_down, v_norm_ple, v_w_ple_gate, v_w_ple_proj, v_norm_final):
    names = ['norm_mix', 'ssm_lambda_re', 'ssm_lambda_im', 'ssm_log_dt', 'ssm_b_re', 'ssm_b_im', 'ssm_c_re',
             'ssm_c_im', 'ssm_d', 'ssm_w_glu', 'kv_norm', 'w_k', 'w_v', 'w_q', 'attn_sinks', 'w_o', 'norm_mlp',
             'w_up', 'w_down', 'norm_ple', 'w_ple_gate', 'w_ple_proj', 'norm_final']
    weights = dict(zip(names, (norm_mix, ssm_lambda_re, ssm_lambda_im, ssm_log_dt, ssm_b_re, ssm_b_im, ssm_c_re,
                               ssm_c_im, ssm_d, ssm_w_glu, kv_norm, w_k, w_v, w_q, attn_sinks, w_o, norm_mlp,
                               w_up, w_down, norm_ple, w_ple_gate, w_ple_proj, norm_final)))
    mom1 = dict(zip(names, (m_norm_mix, m_ssm_lambda_re, m_ssm_lambda_im, m_ssm_log_dt, m_ssm_b_re, m_ssm_b_im,
                            m_ssm_c_re, m_ssm_c_im, m_ssm_d, m_ssm_w_glu, m_kv_norm, m_w_k, m_w_v, m_w_q,
                            m_attn_sinks, m_w_o, m_norm_mlp, m_w_up, m_w_down, m_norm_ple, m_w_ple_gate,
                            m_w_ple_proj, m_norm_final)))
    mom2 = dict(zip(names, (v_norm_mix, v_ssm_lambda_re, v_ssm_lambda_im, v_ssm_log_dt, v_ssm_b_re, v_ssm_b_im,
                            v_ssm_c_re, v_ssm_c_im, v_ssm_d, v_ssm_w_glu, v_kv_norm, v_w_k, v_w_v, v_w_q,
                            v_attn_sinks, v_w_o, v_norm_mlp, v_w_up, v_w_down, v_norm_ple, v_w_ple_gate,
                            v_w_ple_proj, v_norm_final)))

    seqlen, d = x.shape[1], x.shape[2]
    depth = w_up.shape[0]
    n_ssm = ssm_w_glu.shape[0]
    n_att = w_q.shape[0]
    ng = d // SSM_GROUP
    nh = d // HEAD_DIM
    h0 = x[0]
    tgt = loss_target[0]
    tabs = _rope_tables(seqlen)

    sharded = ['w_up', 'w_down', 'w_ple_gate', 'w_ple_proj', 'ssm_w_glu', 'w_q', 'w_o', 'w_k', 'w_v']
    shards = {k: weights[k].astype(BF16) for k in sharded}
    shards['ssm_d'] = ssm_d
    dkv = w_k.shape[1]

    def layer_set(i):
        keys = [('w_up', i), ('w_down', i), ('w_ple_gate', i), ('w_ple_proj', i)]
        keys += [('ssm_w_glu', i), ('ssm_d', i)] if i < n_ssm else [('w_q', i - n_ssm), ('w_o', i - n_ssm)]
        if i == n_ssm:
            keys += [('w_k', None), ('w_v', None)]
        return keys

    def gather_of(i, only=None):
        keys = [kl for kl in layer_set(i) if only is None or kl[0] in only]
        return keys, _Exchange([shards[k] for k, _ in keys], False, [l for _, l in keys])

    def as_operands(keys, blocks):
        w = {}
        for (k, _), g in zip(keys, blocks):
            if k == 'ssm_d':
                w[k] = g.reshape(d)
            elif k in ('w_ple_proj', 'ssm_w_glu', 'w_up'):
                w[k] = g.transpose(1, 0, 2).reshape(g.shape[1], NDEV * g.shape[2])
                w[k + '_t'] = g.transpose(0, 2, 1).reshape(NDEV * g.shape[2], g.shape[1])
            else:
                w[k] = g.reshape(NDEV * g.shape[1], g.shape[2])
                w[k + '_t'] = g.transpose(2, 0, 1).reshape(g.shape[2], NDEV * g.shape[1])
        return w

    lw = {}

    def ssm_params(i):
        n = SSM_STATE
        return (ssm_lambda_re[i].reshape(ng, 1, n), ssm_lambda_im[i].reshape(ng, 1, n),
                ssm_log_dt[i].reshape(ng, 1, 1), jnp.swapaxes(ssm_b_re[i], 1, 2), jnp.swapaxes(ssm_b_im[i], 1, 2),
                ssm_c_re[i], ssm_c_im[i])

    h = h0
    h_in, h_a, h_b, acts = [], [], [], []
    ssm_saved, att_saved = {}, {}
    k_sh = v_sh = None
    for i in range(depth):
        h_in.append(h)
        if i < n_ssm:
            first = gather_of(0, ('ssm_w_glu', 'ssm_d')) if i == 0 else None
            hn, *got = _norm_fwd(h, norm_mix[i], f"norm_mix_fwd{i}", comm=first and first[1])
            if first:
                lw[0] = as_operands(first[0], got)
            mats = _ssm_prep(ssm_params(i), f"ssm_prep{i}")
            mmat, atr, ati = mats[0], mats[5], mats[6]
            pmat = jnp.concatenate([mats[1], mats[2]], axis=2)
            qt = jnp.concatenate([mats[3], mats[4]], axis=2)
            s_in = _ssm_state_in(hn, pmat, f"ssm_state_in{i}")
            xp = _ssm_carry(s_in, *_ssm_rows(atr, ati, False), False, f"ssm_carry_fwd{i}")[0]
            rest = gather_of(0, ('w_up', 'w_down', 'w_ple_gate', 'w_ple_proj')) if i == 0 else None
            y, *got = _ssm_out(hn, xp, mmat, qt, f"ssm_out{i}", comm=rest and rest[1])
            if rest:
                lw[0].update(as_operands(rest[0], got))
            ha = _glu_fwd(y, hn, h, lw[i]['ssm_d'], lw[i]['ssm_w_glu'], f"glu_fwd{i}")
            ssm_saved[i] = (hn, mmat, pmat, qt, atr, ati, xp, y)
        else:
            j = i - n_ssm
            q = _q_fwd(h, norm_mix[i], lw[i]['w_q'], tabs, f"q_fwd{j}")
            o = _attn_fwd(q, k_sh, v_sh, attn_sinks[j], f"attn_fwd{j}")
            ha = _lin_res(h, o, lw[i]['w_o'], f"attn_out{j}")
            att_saved[j] = (q, o)
        h_a.append(ha)
        nxt = gather_of(i + 1) if i + 1 < depth else None
        res = _mlp_fwd(ha, norm_mlp[i], lw[i]['w_up'], lw[i]['w_down'], f"mlp_fwd{i}", comm=nxt and nxt[1])
        hb = res[0]
        acts.append(res[1])
        if nxt:
            lw[i + 1] = as_operands(nxt[0], res[2:])
        h_b.append(hb)
        h = _ple_fwd(hb, p[i, 0], norm_ple[i], lw[i]['w_ple_gate'], lw[i]['w_ple_proj'], f"ple_fwd{i}")
        if i == n_ssm - 1:
            k_sh, v_sh = _kv_fwd(h, kv_norm, lw[n_ssm]['w_k'], lw[n_ssm]['w_v'], tabs, "kv_fwd")
    h_kv = h_in[n_ssm] if n_ssm < depth else h
    dh, loss_row, g_norm_final = _loss_bwd(h, norm_final, tgt, "loss_bwd")
    loss = lax.psum(loss_row[0, 0], AXES)

    g_norm_mix, g_norm_mlp, g_norm_ple = [None] * depth, [None] * depth, [None] * depth
    g_ssm, g_sinks = [None] * n_ssm, [None] * n_att
    g_kv_norm = None
    dks, dvs = [], []
    recv = {}

    def riding(stacks):
        if not stacks:
            return None
        ride = (list(stacks), _Exchange(list(stacks.values()), True))
        stacks.clear()
        return ride

    def landed(ride, blocks):
        if ride:
            recv.update(zip(ride[0], blocks))

    gl = {}
    for i in range(depth - 1, -1, -1):
        if i == n_ssm - 1:
            dh, dkp, dvb, hkb, g_kv_norm = _kv_bwd(dks, dvs, h_kv, kv_norm, dh, lw[n_ssm]['w_k_t'],
                                                   lw[n_ssm]['w_v_t'], tabs, "kv_bwd")
            gl['w_k', None] = _atb(hkb, dkp, False, "grad_w_k")
            gl['w_v', None] = _atb(hkb, dvb, False, "grad_w_v")
        dhb, dz, nb16, dpp, g_norm_ple[i] = _ple_bwd(h_b[i], p[i, 0], dh, norm_ple[i], lw[i]['w_ple_gate'],
                                                     lw[i]['w_ple_gate_t'], lw[i]['w_ple_proj'], f"ple_bwd{i}")
        gl['w_ple_gate', i] = _atb(nb16, dz, False, f"grad_w_ple_gate{i}")
        gl['w_ple_proj', i] = _atb(p[i, 0], dpp, True, f"grad_w_ple_proj{i}")
        ride = riding(gl)
        res = _mlp_bwd(h_a[i], acts[i], dhb, norm_mlp[i], lw[i]['w_up_t'], lw[i]['w_down_t'], f"mlp_bwd{i}",
                       comm=ride and ride[1])
        dha, hmb, da, g_norm_mlp[i] = res[:4]
        landed(ride, res[4:])
        g_up = {('w_up', i): _atb(hmb, da, True, f"grad_w_up{i}")}
        g_down = {('w_down', i): _atb(acts[i], dhb, False, f"grad_w_down{i}")}
        if i >= n_ssm:
            j = i - n_ssm
            q, o = att_saved[j]
            do = _lin_bf16(dha, lw[i]['w_o_t'], f"attn_out_bwd{j}")
            ride = riding({**g_up, **g_down, ('w_o', j): _atb(o, dha, False, f"grad_w_o{j}")})
            dq, dk_j, dv_j, dsink, *got = _attn_bwd(q, k_sh, v_sh, do, attn_sinks[j], f"attn_bwd{j}",
                                                    comm=ride[1])
            landed(ride, got)
            dks.append(dk_j)
            dvs.append(dv_j)
            g_sinks[j] = dsink[:, 0]
            dh, dqp, hnb, g_norm_mix[i] = _q_bwd(dq, h_in[i], norm_mix[i], dha, lw[i]['w_q_t'], tabs, f"q_bwd{j}")
            gl['w_q', j] = _atb(hnb, dqp, False, f"grad_w_q{j}")
        else:
            hn, mmat, pmat, qt, atr, ati, xp, y = ssm_saved[i]
            dyy, dhn_d, geb, dab, g_dskip = _glu_bwd(y, hn, dha, lw[i]['ssm_d'], lw[i]['ssm_w_glu'],
                                                     lw[i]['ssm_w_glu_t'], f"glu_bwd{i}")
            g_down['ssm_d', i] = g_dskip.reshape(NDEV, d // NDEV)
            g_down['ssm_w_glu', i] = _atb(geb, dab, True, f"grad_ssm_w_glu{i}")
            dxp = _ssm_dstate(dyy, qt, f"ssm_dstate{i}")
            ride = riding(g_up)
            gs, *got = _ssm_carry(dxp, *_ssm_rows(atr, ati, True), True, f"ssm_carry_bwd{i}", comm=ride[1])
            landed(ride, got)
            ride = riding(g_down)
            du, dm, dp, dqt, da_raw, *got = _ssm_bwd(hn, dyy, xp, gs, mmat, pmat, f"ssm_bwd{i}", comm=ride[1])
            landed(ride, got)
            n = SSM_STATE
            cots = (dm, dp[:, :, :n], dp[:, :, n:], dqt[:, :, :n], dqt[:, :, n:],
                    (da_raw[:, 0:1, :n] + da_raw[:, 0:1, n:]), (da_raw[:, 1:2, :n] - da_raw[:, 1:2, n:]))
            g_ssm[i] = _ssm_prep_vjp(ssm_params(i), cots, f"ssm_prep_vjp{i}")
            dh, g_norm_mix[i] = _norm_bwd(h_in[i], norm_mix[i], dhn_d, du, dha, f"norm_mix_bwd{i}")
    grad_x = dh[None]
    if gl:
        ride = riding(gl)
        landed(ride, _exchange(ride[1].arrs, True, "scatter_grads_rest"))

    out_g, out_d, out_m, out_v = {}, {}, {}, {}
    updated = {}
    for (k, l), parts in recv.items():
        pick = (lambda t: t) if l is None else (lambda t: t[l])
        shp = pick(weights[k]).shape
        r2 = (math.prod(shp[:-1]), shp[-1])
        res = _adamw(parts.reshape((NDEV,) + r2), pick(weights[k]).reshape(r2), pick(mom1[k]).reshape(r2),
                     pick(mom2[k]).reshape(r2), f"adamw_{k}{'' if l is None else l}")
        updated.setdefault(k, {})[l] = [t.reshape(shp) for t in res]
    for k, by_layer in updated.items():
        for n, dst in enumerate((out_g, out_d, out_m, out_v)):
            dst[k] = by_layer[None][n] if None in by_layer else jnp.stack([by_layer[l][n] for l in sorted(by_layer)])

    def ssm_grad(idx, unswap=False):
        g = jnp.stack([g_ssm[i][idx] for i in range(n_ssm)])
        return jnp.swapaxes(g, 2, 3) if unswap else g

    small = {'norm_mix': jnp.concatenate(g_norm_mix, axis=0),
             'ssm_lambda_re': ssm_grad(0), 'ssm_lambda_im': ssm_grad(1), 'ssm_log_dt': ssm_grad(2),
             'ssm_b_re': ssm_grad(3, True), 'ssm_b_im': ssm_grad(4, True),
             'ssm_c_re': ssm_grad(5), 'ssm_c_im': ssm_grad(6),
             'kv_norm': g_kv_norm, 'attn_sinks': jnp.stack(g_sinks),
             'norm_mlp': jnp.concatenate(g_norm_mlp, axis=0), 'norm_ple': jnp.concatenate(g_norm_ple, axis=0),
             'norm_final': g_norm_final}
    snames = list(small)
    sizes = [weights[k].size for k in snames]
    total = sum(sizes)
    lanes = 128
    padded = -(-total // (512 * lanes)) * (512 * lanes)

    def flat(parts):
        v = jnp.concatenate([t.reshape(-1) for t in parts] + [jnp.zeros((padded - total,), F32)])
        return v.reshape(padded // lanes, lanes)

    parts = _exchange([flat([small[k] for k in snames])], False, "gather_small_grads")[0]
    res = _adamw(parts, flat([weights[k] for k in snames]), flat([mom1[k] for k in snames]),
                 flat([mom2[k] for k in snames]), "adamw_small")
    off = 0
    for k, sz in zip(snames, sizes):
        for dst, t in zip((out_g, out_d, out_m, out_v), res):
            dst[k] = t.reshape(-1)[off:off + sz].reshape(weights[k].shape)
        off += sz

    return (loss, grad_x, *[out_g[k] for k in names], *[out_d[k] for k in names],
            *[out_m[k] for k in names], *[out_v[k] for k in names])
```

```python
import functools
import math

import jax
import jax.numpy as jnp
from jax import lax
from jax.experimental import pallas as pl
from jax.experimental.pallas import tpu as pltpu

F32 = jnp.float32
BF16 = jnp.bfloat16
SDS = jax.ShapeDtypeStruct
MESH = pl.DeviceIdType.MESH
AXES = ("x", "y", "c")
NDEV = 8

RMS_EPS = 1e-6
SSM_GROUP = 16
SSM_STATE = 64
SSM_T = 8
SSM_W = SSM_T * SSM_GROUP
HEAD_DIM = 64
GQA_GROUP = 4
ATTN_BLOCK = 128
ROT_DIM = 16
ROPE_THETA = 500000.0
NEG_INF = -1e30
ADAM_LR, ADAM_B1, ADAM_B2, ADAM_EPS, ADAM_WD, ADAM_STEP = 0.001, 0.9, 0.999, 1e-08, 0.01, 10

VMEM_CAP = 56 * 1024 * 1024
HI = lax.Precision.HIGHEST

NN = ((1,), (0,))
NT = ((1,), (1,))
TN = ((0,), (0,))


def _dot(a, b, dims=NN, precision=None):
    return lax.dot_general(a, b, (dims, ((), ())), preferred_element_type=F32, precision=precision)


def _split(a):
    if isinstance(a, tuple):
        return a
    hi = a.astype(BF16)
    return hi, (a - hi.astype(F32)).astype(BF16)


def _dot3(a, b, dims=NN):
    (ah, al), (bh, bl) = _split(a), _split(b)
    return _dot(ah, bh, dims) + (_dot(ah, bl, dims) + _dot(al, bh, dims))


@jax.custom_vjp
def _dot3_nt(a, b):
    return _dot3(a, b, NT)


def _dot3_nt_fwd(a, b):
    return _dot3(a, b, NT), (a, b)


def _dot3_nt_bwd(res, g):
    a, b = res
    return _dot3(g, b, NN), _dot3(g, a, TN)


_dot3_nt.defvjp(_dot3_nt_fwd, _dot3_nt_bwd)


def _tile(n, pref):
    t = min(n, pref)
    while n % t:
        t //= 2
    return t


def _nbytes(shape, dtype):
    return math.prod(s for s in shape if s is not None) * jnp.dtype(dtype).itemsize


def _vmem_limit(blocks, extra=0):
    need = sum(_nbytes(s, d) * n for s, d, n in blocks) + extra + (4 << 20)
    return int(min(VMEM_CAP, max(need, 16 << 20)))


def _pcall(body, *, name, out_shape, grid, in_specs, out_specs, scratch=(), vmem=None, comm=None):
    single = not isinstance(out_shape, (list, tuple))
    out_shape = [out_shape] if single else list(out_shape)
    out_specs = [out_specs] if single else list(out_specs)
    in_specs, scratch = list(in_specs), list(scratch)
    if comm is not None:
        n_in, n_out, n_scr, nx = len(in_specs), len(out_specs), len(scratch), len(comm.arrs)
        hbm = pl.BlockSpec(memory_space=pl.ANY)
        in_specs = in_specs + [hbm] * nx
        out_specs = out_specs + [hbm] * nx
        out_shape = out_shape + comm.out_shape()
        scratch = scratch + comm.semaphores()
        inner = body

        def body(*refs):
            ins, xin, rest = refs[:n_in], refs[n_in:n_in + nx], refs[n_in + nx:]
            outs, xout, rest = rest[:n_out], rest[n_out:n_out + nx], rest[n_out + nx:]
            scr, sems = rest[:n_scr], rest[n_scr:]
            first = functools.reduce(jnp.logical_and, [pl.program_id(a) == 0 for a in range(len(grid))])
            last = functools.reduce(jnp.logical_and, [pl.program_id(a) == g - 1 for a, g in enumerate(grid)])

            @pl.when(first)
            def _():
                comm.start(xin, xout, sems)
            inner(*ins, *outs, *scr)

            @pl.when(last)
            def _():
                comm.wait(xin, xout, sems)

    call = pl.pallas_call(
        body, out_shape=out_shape[0] if single and comm is None else out_shape, grid=grid, in_specs=in_specs,
        out_specs=out_specs[0] if single and comm is None else out_specs, scratch_shapes=scratch, name=name,
        compiler_params=pltpu.CompilerParams(
            dimension_semantics=("arbitrary",) * len(grid), vmem_limit_bytes=vmem),
        interpret=False)
    if comm is None:
        return call
    return lambda *args: call(*args, *comm.arrs)


def _rms(x, g):
    r = lax.rsqrt(jnp.mean(x * x, axis=-1, keepdims=True) + RMS_EPS)
    return x * r * g, r


def _rms_bwd(x, g, r, dy):
    xh = x * r
    dyg = dy * g
    dx = r * (dyg - xh * jnp.mean(dyg * xh, axis=-1, keepdims=True))
    return dx, jnp.sum(dy * xh, axis=0, keepdims=True)


_GELU_C = math.sqrt(2.0 / math.pi)


def _gelu_parts(x):
    t = jnp.tanh(_GELU_C * (x + 0.044715 * x * x * x))
    return 0.5 * x * (1.0 + t), t


def _gelu_grad(x, t):
    return 0.5 * (1.0 + t) + 0.5 * x * (1.0 - t * t) * _GELU_C * (1.0 + 3 * 0.044715 * x * x)


def _rope_tables(seqlen):
    half = ROT_DIM // 2
    inv = ROPE_THETA ** (-jnp.arange(0, ROT_DIM, 2, dtype=F32) / ROT_DIM)
    ang = jnp.arange(seqlen, dtype=jnp.int32).astype(F32)[:, None] * inv[None, :]
    cos, sin = jnp.cos(ang), jnp.sin(ang)
    zeros = jnp.zeros((seqlen, HEAD_DIM - ROT_DIM), F32)
    zh = jnp.zeros((seqlen, half), F32)
    c = jnp.concatenate([cos, cos, zeros + 1.0], axis=1)
    sa = jnp.concatenate([zh, sin, zeros], axis=1)
    sb = jnp.concatenate([-sin, zh, zeros], axis=1)
    return tuple(jnp.tile(t, (1, 128 // HEAD_DIM)) for t in (c, sa, sb))


def _rope(x, c, sa, sb):
    w = x.shape[1]
    reps = w // 128
    half = ROT_DIM // 2
    return (x * jnp.tile(c, (1, reps)) + pltpu.roll(x, half, 1) * jnp.tile(sa, (1, reps))
            + pltpu.roll(x, w - half, 1) * jnp.tile(sb, (1, reps)))


def _rope_bwd(dy, c, sa, sb):
    w = dy.shape[1]
    reps = w // 128
    half = ROT_DIM // 2
    return (dy * jnp.tile(c, (1, reps)) + pltpu.roll(dy * jnp.tile(sa, (1, reps)), w - half, 1)
            + pltpu.roll(dy * jnp.tile(sb, (1, reps)), half, 1))


def _rspec(tm, c):
    return pl.BlockSpec((tm, c), lambda i: (i, 0))


def _cspec(shape, idx=None):
    idx = tuple(idx) if idx is not None else (0,) * len(shape)
    return pl.BlockSpec(tuple(shape), lambda i: idx, pipeline_mode=pl.Buffered(1))


def _rowcall(body, name, seqlen, tm, rows_in, consts_in, rows_out, acc_out=(), extra_vmem=0, comm=None):
    in_specs = [_rspec(tm, a.shape[1]) for a in rows_in] + [_cspec(bs, ix) for _, bs, ix in consts_in]
    out_shape = [SDS((seqlen, c), d) for c, d in rows_out] + [SDS(s, F32) for s in acc_out]
    out_specs = [_rspec(tm, c) for c, _ in rows_out] + [pl.BlockSpec(s, lambda i: (0, 0)) for s in acc_out]
    blocks = ([((tm, a.shape[1]), a.dtype, 2) for a in rows_in] + [(bs, a.dtype, 1) for a, bs, _ in consts_in]
              + [((tm, c), d, 2) for c, d in rows_out])
    temporaries = 12 * tm * rows_in[0].shape[1] * 4
    return _pcall(body, name=name, out_shape=out_shape, grid=(seqlen // tm,), in_specs=in_specs,
                  out_specs=out_specs, vmem=_vmem_limit(blocks, extra_vmem + temporaries), comm=comm)(
                      *rows_in, *[a for a, _, _ in consts_in])


def _whole(a):
    return (a, a.shape, None)


def _norm_fwd(h, g, name, comm=None):
    seqlen, d = h.shape
    tm = _tile(seqlen, 1024)

    def body(h_ref, g_ref, o_ref):
        o_ref[...] = _rms(h_ref[...], g_ref[...])[0]

    return _rowcall(body, name, seqlen, tm, [h], [_whole(g.reshape(1, d))], [(d, F32)], comm=comm)


def _norm_bwd(h, g, dy1, dy2, dres, name, comm=None):
    seqlen, d = h.shape
    tm = _tile(seqlen, 512)

    def body(h_ref, dy1_ref, dy2_ref, dres_ref, g_ref, dh_ref, dg_ref):
        @pl.when(pl.program_id(0) == 0)
        def _():
            dg_ref[...] = jnp.zeros_like(dg_ref)
        x = h_ref[...]
        gv = g_ref[...]
        _, r = _rms(x, gv)
        dx, dg = _rms_bwd(x, gv, r, dy1_ref[...] + dy2_ref[...])
        dh_ref[...] = dres_ref[...] + dx
        dg_ref[...] += dg

    return _rowcall(body, name, seqlen, tm, [h, dy1, dy2, dres], [_whole(g.reshape(1, d))], [(d, F32)], [(1, d)],
                    comm=comm)


def _glu_fwd(y, hn, h, dskip, wglu, name):
    seqlen, d = h.shape
    tm = _tile(seqlen, 512)

    def body(y_ref, hn_ref, h_ref, d_ref, w_ref, o_ref):
        yy = y_ref[...] + d_ref[...] * hn_ref[...]
        ge, _ = _gelu_parts(yy)
        ab = _dot(ge.astype(BF16), w_ref[...])
        o_ref[...] = h_ref[...] + ab[:, :d] * jax.nn.sigmoid(ab[:, d:])

    return _rowcall(body, name, seqlen, tm, [y, hn, h], [_whole(dskip.reshape(1, d)), _whole(wglu)], [(d, F32)],
                    extra_vmem=tm * d * 4 * 6)[0]


def _glu_bwd(y, hn, dmix, dskip, wglu, wglu_t, name, comm=None):
    seqlen, d = hn.shape
    tm = _tile(seqlen, 512)

    def body(y_ref, hn_ref, dm_ref, d_ref, w_ref, wt_ref, dyy_ref, dhn_ref, ge_ref, dab_ref, dd_ref):
        @pl.when(pl.program_id(0) == 0)
        def _():
            dd_ref[...] = jnp.zeros_like(dd_ref)
        hn_v = hn_ref[...]
        dsk = d_ref[...]
        yy = y_ref[...] + dsk * hn_v
        ge, t = _gelu_parts(yy)
        geb = ge.astype(BF16)
        ab = _dot(geb, w_ref[...])
        a = ab[:, :d]
        sg = jax.nn.sigmoid(ab[:, d:])
        dm = dm_ref[...]
        dab_ref[:, :d] = (dm * sg).astype(BF16)
        dab_ref[:, d:] = (dm * a * sg * (1.0 - sg)).astype(BF16)
        dge = _dot(dab_ref[...], wt_ref[...])
        dyy = dge * _gelu_grad(yy, t)
        dyy_ref[...] = dyy
        dhn_ref[...] = dyy * dsk
        ge_ref[...] = geb
        dd_ref[...] += jnp.sum(dyy * hn_v, axis=0, keepdims=True)

    return _rowcall(body, name, seqlen, tm, [y, hn, dmix],
                    [_whole(dskip.reshape(1, d)), _whole(wglu), _whole(wglu_t)],
                    [(d, F32), (d, F32), (d, BF16), (2 * d, BF16)], [(1, d)], extra_vmem=tm * d * 4 * 8, comm=comm)


def _q_fwd(h, g, wq, tabs, name):
    seqlen, d = h.shape
    tm = _tile(seqlen, 512)

    def body(h_ref, c_ref, sa_ref, sb_ref, g_ref, w_ref, q_ref):
        hn, _ = _rms(h_ref[...], g_ref[...])
        qp = _dot(hn.astype(BF16), w_ref[...])
        q_ref[...] = _rope(qp, c_ref[...], sa_ref[...], sb_ref[...]).astype(BF16)

    return _rowcall(body, name, seqlen, tm, [h, *tabs], [_whole(g.reshape(1, d)), _whole(wq)], [(d, BF16)],
                    extra_vmem=tm * d * 4 * 6)[0]


def _q_bwd(dq, h, g, dres, wq, tabs, name):
    seqlen, d = h.shape
    tm = _tile(seqlen, 512)

    def body(dq_ref, h_ref, dres_ref, c_ref, sa_ref, sb_ref, g_ref, w_ref, dh_ref, dqp_ref, hn_ref, dg_ref):
        @pl.when(pl.program_id(0) == 0)
        def _():
            dg_ref[...] = jnp.zeros_like(dg_ref)
        dqp = _rope_bwd(dq_ref[...], c_ref[...], sa_ref[...], sb_ref[...]).astype(BF16)
        x = h_ref[...]
        gv = g_ref[...]
        hn, r = _rms(x, gv)
        dhn = _dot(dqp, w_ref[...])
        dx, dg = _rms_bwd(x, gv, r, dhn)
        dh_ref[...] = dres_ref[...] + dx
        dqp_ref[...] = dqp
        hn_ref[...] = hn.astype(BF16)
        dg_ref[...] += dg

    return _rowcall(body, name, seqlen, tm, [dq, h, dres, *tabs], [_whole(g.reshape(1, d)), _whole(wq)],
                    [(d, F32), (d, BF16), (d, BF16)], [(1, d)], extra_vmem=tm * d * 4 * 6)


def _kv_fwd(h, g, wk, wv, tabs, name):
    seqlen, d = h.shape
    dk = wk.shape[1]
    tm = _tile(seqlen, 512)

    def body(h_ref, c_ref, sa_ref, sb_ref, g_ref, wk_ref, wv_ref, k_ref, v_ref):
        hk = _rms(h_ref[...], g_ref[...])[0].astype(BF16)
        k_ref[...] = _rope(_dot(hk, wk_ref[...]), c_ref[...], sa_ref[...], sb_ref[...]).astype(BF16)
        v_ref[...] = _dot(hk, wv_ref[...]).astype(BF16)

    return _rowcall(body, name, seqlen, tm, [h, *tabs], [_whole(g.reshape(1, d)), _whole(wk), _whole(wv)],
                    [(dk, BF16), (dk, BF16)], extra_vmem=tm * d * 4 * 4)


def _kv_bwd(dks, dvs, h, g, dres, wk, wv, tabs, name):
    seqlen, d = h.shape
    dkw = wk.shape[0]
    tm = _tile(seqlen, 512)

    def body(dk0_ref, dk1_ref, dv0_ref, dv1_ref, h_ref, dres_ref, c_ref, sa_ref, sb_ref, g_ref, wk_ref, wv_ref,
             dh_ref, dkp_ref, dvb_ref, hk_ref, dg_ref):
        @pl.when(pl.program_id(0) == 0)
        def _():
            dg_ref[...] = jnp.zeros_like(dg_ref)
        dkp = _rope_bwd(dk0_ref[...] + dk1_ref[...], c_ref[...], sa_ref[...], sb_ref[...]).astype(BF16)
        dvb = (dv0_ref[...] + dv1_ref[...]).astype(BF16)
        x = h_ref[...]
        gv = g_ref[...]
        hk, r = _rms(x, gv)
        dhk = _dot(dkp, wk_ref[...]) + _dot(dvb, wv_ref[...])
        dx, dg = _rms_bwd(x, gv, r, dhk)
        dh_ref[...] = dres_ref[...] + dx
        dkp_ref[...] = dkp
        dvb_ref[...] = dvb
        hk_ref[...] = hk.astype(BF16)
        dg_ref[...] += dg

    return _rowcall(body, name, seqlen, tm, [dks[0], dks[1], dvs[0], dvs[1], h, dres, *tabs],
                    [_whole(g.reshape(1, d)), _whole(wk), _whole(wv)],
                    [(d, F32), (dkw, BF16), (dkw, BF16), (d, BF16)], [(1, d)], extra_vmem=tm * d * 4 * 6)


def _lin_res(h, xb, w, name):
    seqlen, d = h.shape
    tm = _tile(seqlen, 512)

    def body(h_ref, x_ref, w_ref, o_ref):
        o_ref[...] = h_ref[...] + _dot(x_ref[...], w_ref[...])

    return _rowcall(body, name, seqlen, tm, [h, xb], [_whole(w)], [(d, F32)], extra_vmem=tm * d * 4 * 2)[0]


def _lin_bf16(dy, w, name):
    seqlen, d = dy.shape
    tm = _tile(seqlen, 512)

    def body(dy_ref, w_ref, o_ref):
        o_ref[...] = _dot(dy_ref[...].astype(BF16), w_ref[...]).astype(BF16)

    return _rowcall(body, name, seqlen, tm, [dy], [_whole(w)], [(w.shape[1], BF16)], extra_vmem=tm * d * 4 * 2)[0]


def _mlp_fwd(h, g, wup, wdn, name, comm=None):
    seqlen, d = h.shape
    f = wup.shape[1]
    tm = _tile(seqlen, 512)

    def body(h_ref, g_ref, wup_ref, wdn_ref, o_ref, act_ref):
        x = h_ref[...]
        hm = _rms(x, g_ref[...])[0].astype(BF16)
        r = jnp.maximum(_dot(hm, wup_ref[...]), 0.0)
        act = (r * r).astype(BF16)
        act_ref[...] = act
        o_ref[...] = x + _dot(act, wdn_ref[...])

    consts = [_whole(g.reshape(1, d)), _whole(wup), _whole(wdn)]
    return _rowcall(body, name, seqlen, tm, [h], consts, [(d, F32), (f, BF16)], extra_vmem=tm * f * 4 * 3, comm=comm)


def _mlp_bwd(h, act, dh, g, wup_t, wdn_t, name, comm=None):
    seqlen, d = h.shape
    f = wup_t.shape[0]
    tm = _tile(seqlen, 512)

    def body(h_ref, act_ref, dh_ref, g_ref, wup_ref, wdn_ref, dhin_ref, hm_ref, da_ref, dg_ref):
        @pl.when(pl.program_id(0) == 0)
        def _():
            dg_ref[...] = jnp.zeros_like(dg_ref)
        x = h_ref[...]
        gv = g_ref[...]
        dy = dh_ref[...]
        hm, r = _rms(x, gv)
        rl2 = 2.0 * jnp.sqrt(act_ref[...].astype(F32))
        da = (_dot(dy.astype(BF16), wdn_ref[...]) * rl2).astype(BF16)
        da_ref[...] = da
        dx, dg = _rms_bwd(x, gv, r, _dot(da, wup_ref[...]))
        dhin_ref[...] = dy + dx
        hm_ref[...] = hm.astype(BF16)
        dg_ref[...] += dg

    consts = [_whole(g.reshape(1, d)), _whole(wup_t), _whole(wdn_t)]
    return _rowcall(body, name, seqlen, tm, [h, act, dh], consts, [(d, F32), (d, BF16), (f, BF16)], [(1, d)],
                    extra_vmem=tm * f * 4 * 3, comm=comm)


def _ple_fwd(h, p, g, wg, wpp, name):
    seqlen, d = h.shape
    tm = _tile(seqlen, 512)

    def body(h_ref, p_ref, g_ref, wg_ref, wpp_ref, o_ref):
        x = h_ref[...]
        n = _rms(x, g_ref[...])[0].astype(BF16)
        gate = jax.nn.sigmoid(_dot(n, wg_ref[...]))
        o_ref[...] = x + gate * _dot(p_ref[...].astype(BF16), wpp_ref[...])

    return _rowcall(body, name, seqlen, tm, [h, p], [_whole(g.reshape(1, d)), _whole(wg), _whole(wpp)], [(d, F32)],
                    extra_vmem=tm * d * 4 * 5)[0]


def _ple_bwd(h, p, dh, g, wg, wg_t, wpp, name):
    seqlen, d = h.shape
    tm = _tile(seqlen, 512)

    def body(h_ref, p_ref, dh_ref, g_ref, wg_ref, wgt_ref, wpp_ref, dhin_ref, dz_ref, n_ref, dpp_ref, dg_ref):
        @pl.when(pl.program_id(0) == 0)
        def _():
            dg_ref[...] = jnp.zeros_like(dg_ref)
        x = h_ref[...]
        gv = g_ref[...]
        dy = dh_ref[...]
        n, r = _rms(x, gv)
        nb16 = n.astype(BF16)
        gate = jax.nn.sigmoid(_dot(nb16, wg_ref[...]))
        pp = _dot(p_ref[...].astype(BF16), wpp_ref[...])
        dz = (dy * pp * gate * (1.0 - gate)).astype(BF16)
        dn = _dot(dz, wgt_ref[...])
        dx, dg = _rms_bwd(x, gv, r, dn)
        dhin_ref[...] = dy + dx
        dz_ref[...] = dz
        n_ref[...] = nb16
        dpp_ref[...] = (dy * gate).astype(BF16)
        dg_ref[...] += dg

    return _rowcall(body, name, seqlen, tm, [h, p, dh],
                    [_whole(g.reshape(1, d)), _whole(wg), _whole(wg_t), _whole(wpp)],
                    [(d, F32), (d, BF16), (d, BF16), (d, BF16)], [(1, d)], extra_vmem=tm * d * 4 * 8)


def _loss_bwd(h, g, tgt, name):
    seqlen, d = h.shape
    tm = _tile(seqlen, 512)

    def body(h_ref, t_ref, g_ref, dh_ref, loss_ref, dg_ref):
        @pl.when(pl.program_id(0) == 0)
        def _():
            dg_ref[...] = jnp.zeros_like(dg_ref)
            loss_ref[...] = jnp.zeros_like(loss_ref)
        x = h_ref[...]
        gv = g_ref[...]
        y, r = _rms(x, gv)
        diff = y - t_ref[...]
        loss_ref[...] += (0.5 / d) * jnp.sum(jnp.sum(diff * diff, axis=1, keepdims=True), axis=0, keepdims=True)
        dx, dg = _rms_bwd(x, gv, r, diff * (1.0 / d))
        dh_ref[...] = dx
        dg_ref[...] += dg

    return _rowcall(body, name, seqlen, tm, [h, tgt], [_whole(g.reshape(1, d))], [(d, F32)], [(1, 128), (1, d)])


def _atb(a, b, col_blocked, name):
    seqlen, k1 = a.shape
    k2 = b.shape[1]
    if col_blocked:
        cs = k2 // NDEV
        t1 = _tile(k1, 512)
        nblk = _tile(NDEV, max(1, 2048 // cs))
        t2 = nblk * cs
        oshape = (NDEV, k1, cs)
        oblock = (nblk, t1, cs)
        omap = lambda i, j, l: (j, i, 0)
    else:
        rs = k1 // NDEV
        t2 = _tile(k2, 2048)
        nblk = _tile(NDEV, max(1, 1024 // rs))
        t1 = nblk * rs
        oshape = (NDEV, rs, k2)
        oblock = (nblk, rs, t2)
        omap = lambda i, j, l: (i, 0, j)
    tl = _tile(seqlen, 2048 if b.dtype == BF16 else 1024)

    def body(a_ref, b_ref, o_ref):
        @pl.when(pl.program_id(2) == 0)
        def _():
            o_ref[...] = jnp.zeros_like(o_ref)
        res = _dot(a_ref[...].astype(BF16), b_ref[...].astype(BF16), TN)
        for n in range(nblk):
            if col_blocked:
                o_ref[n] += res[:, n * cs:(n + 1) * cs]
            else:
                o_ref[n] += res[n * rs:(n + 1) * rs, :]

    blocks = [((tl, t1), a.dtype, 2), ((tl, t2), b.dtype, 2), ((t1, t2), F32, 2)]
    return _pcall(
        body, name=name, out_shape=SDS(oshape, F32), grid=(k1 // t1, k2 // t2, seqlen // tl),
        in_specs=[pl.BlockSpec((tl, t1), lambda i, j, l: (l, i)), pl.BlockSpec((tl, t2), lambda i, j, l: (l, j))],
        out_specs=pl.BlockSpec(oblock, omap),
        vmem=_vmem_limit(blocks, extra=t1 * t2 * 4 + tl * (t1 + t2) * 2))(a, b)


_ATTN_SCALE = HEAD_DIM ** -0.5


def _attn_bias():
    qi = lax.broadcasted_iota(jnp.int32, (ATTN_BLOCK, 2 * ATTN_BLOCK), 0) + ATTN_BLOCK
    kj = lax.broadcasted_iota(jnp.int32, (ATTN_BLOCK, 2 * ATTN_BLOCK), 1)
    band = (kj <= qi) & (qi - kj < ATTN_BLOCK)
    return jnp.where(jnp.stack([band & (kj >= ATTN_BLOCK), band]), 0.0, NEG_INF).astype(F32)


def _bias_spec():
    return pl.BlockSpec((None, ATTN_BLOCK, 2 * ATTN_BLOCK), lambda n: (jnp.minimum(n, 1), 0, 0))


def _attn_probs(q4s, kks, sink_col, bias):
    s = jnp.concatenate([_dot(q4, kk, NT) for q4, kk in zip(q4s, kks)], axis=0)
    rows = s.shape[0]
    s = (s.reshape(rows // ATTN_BLOCK, ATTN_BLOCK, 2 * ATTN_BLOCK) + bias).reshape(rows, 2 * ATTN_BLOCK)
    m = jnp.maximum(jnp.max(s, axis=1, keepdims=True), sink_col)
    pr = jnp.exp(s - m)
    es = jnp.exp(sink_col - m)
    inv = 1.0 / (jnp.sum(pr, axis=1, keepdims=True) + es)
    return pr * inv, es * inv


def _sink_col(sink_ref, nheads):
    return jnp.concatenate([jnp.full((ATTN_BLOCK, 1), sink_ref[hq], F32) for hq in range(nheads)], axis=0)


def _kv_pair(p_ref, c_ref, kh):
    sl = slice(kh * HEAD_DIM, (kh + 1) * HEAD_DIM)
    return jnp.concatenate([p_ref[:, sl], c_ref[:, sl]], axis=0)


def _stack_heads(ref, kh, scale=None):
    x = jnp.concatenate(
        [ref[:, (kh * GQA_GROUP + g) * HEAD_DIM:(kh * GQA_GROUP + g + 1) * HEAD_DIM] for g in range(GQA_GROUP)], axis=0)
    return x if scale is None else x * scale


def _attn_fwd(q, k, v, sinks, name):
    seqlen, d = q.shape
    dkv = k.shape[1]
    nkv = dkv // HEAD_DIM
    nb = seqlen // ATTN_BLOCK
    blk = ATTN_BLOCK

    def body(sink_ref, bias_ref, q_ref, kc_ref, kp_ref, vc_ref, vp_ref, o_ref):
        q4s = [_stack_heads(q_ref, kh, _ATTN_SCALE) for kh in range(nkv)]
        kks = [_kv_pair(kp_ref, kc_ref, kh) for kh in range(nkv)]
        w, _ = _attn_probs(q4s, kks, _sink_col(sink_ref, nkv * GQA_GROUP), bias_ref[...])
        wb = w.astype(BF16)
        for kh in range(nkv):
            o4 = _dot(wb[kh * GQA_GROUP * blk:(kh + 1) * GQA_GROUP * blk, :], _kv_pair(vp_ref, vc_ref, kh))
            for g in range(GQA_GROUP):
                hq = kh * GQA_GROUP + g
                o_ref[:, hq * HEAD_DIM:(hq + 1) * HEAD_DIM] = o4[g * blk:(g + 1) * blk, :].astype(BF16)

    cur = lambda n: (n, 0)
    prev = lambda n: (jnp.maximum(n - 1, 0), 0)
    return _pcall(
        body, name=name, out_shape=SDS((seqlen, d), BF16), grid=(nb,),
        in_specs=[pl.BlockSpec(memory_space=pltpu.SMEM), _bias_spec(), pl.BlockSpec((blk, d), cur),
                  pl.BlockSpec((blk, dkv), cur), pl.BlockSpec((blk, dkv), prev),
                  pl.BlockSpec((blk, dkv), cur), pl.BlockSpec((blk, dkv), prev)],
        out_specs=pl.BlockSpec((blk, d), cur), vmem=32 << 20)(sinks, _attn_bias(), q, k, k, v, v)


def _attn_bwd(q, k, v, do, sinks, name, comm=None):
    seqlen, d = q.shape
    dkv = k.shape[1]
    nkv = dkv // HEAD_DIM
    nh = d // HEAD_DIM
    nb = seqlen // ATTN_BLOCK
    blk = ATTN_BLOCK

    def body(sink_ref, bias_ref, q_ref, kc_ref, kp_ref, vc_ref, vp_ref, do_ref, dq_ref, dk_ref, dv_ref, ds_ref,
             ck_ref, cv_ref):
        n = pl.program_id(0)

        @pl.when(n == 0)
        def _():
            ck_ref[...] = jnp.zeros_like(ck_ref)
            cv_ref[...] = jnp.zeros_like(cv_ref)
            ds_ref[...] = jnp.zeros_like(ds_ref)

        @pl.when(n == nb)
        def _():
            dk_ref[...] = ck_ref[...]
            dv_ref[...] = cv_ref[...]

        @pl.when(n < nb)
        def _():
            q4s = [_stack_heads(q_ref, kh, _ATTN_SCALE) for kh in range(nkv)]
            do4s = [_stack_heads(do_ref, kh) for kh in range(nkv)]
            kks = [_kv_pair(kp_ref, kc_ref, kh) for kh in range(nkv)]
            w, wsink = _attn_probs(q4s, kks, _sink_col(sink_ref, nh), bias_ref[...])
            dw = jnp.concatenate([_dot(do4s[kh], _kv_pair(vp_ref, vc_ref, kh), NT) for kh in range(nkv)], axis=0)
            dsum = jnp.sum(w * dw, axis=1, keepdims=True)
            ds_all = (w * (dw - dsum)).astype(BF16)
            wb = w.astype(BF16)
            dsk = -wsink * dsum
            for kh in range(nkv):
                sl = slice(kh * HEAD_DIM, (kh + 1) * HEAD_DIM)
                rows = slice(kh * GQA_GROUP * blk, (kh + 1) * GQA_GROUP * blk)
                ds = ds_all[rows, :]
                dq4 = _dot(ds, kks[kh]) * _ATTN_SCALE
                dkk = _dot(ds, q4s[kh], TN)
                dvv = _dot(wb[rows, :], do4s[kh], TN)
                for g in range(GQA_GROUP):
                    hq = kh * GQA_GROUP + g
                    dq_ref[:, hq * HEAD_DIM:(hq + 1) * HEAD_DIM] = dq4[g * blk:(g + 1) * blk, :]
                    ds_ref[hq:hq + 1, :] += jnp.sum(dsk[hq * blk:(hq + 1) * blk, :], axis=0, keepdims=True)
                dk_ref[:, sl] = ck_ref[:, sl] + dkk[:blk, :]
                dv_ref[:, sl] = cv_ref[:, sl] + dvv[:blk, :]
                ck_ref[:, sl] = dkk[blk:, :]
                cv_ref[:, sl] = dvv[blk:, :]

    cur = lambda n: (jnp.minimum(n, nb - 1), 0)
    prev = lambda n: (jnp.clip(n - 1, 0, nb - 1), 0)
    lag = lambda n: (jnp.maximum(n - 1, 0), 0)
    return _pcall(
        body, name=name,
        out_shape=[SDS((seqlen, d), F32), SDS((seqlen, dkv), F32), SDS((seqlen, dkv), F32), SDS((nh, 128), F32)],
        grid=(nb + 1,),
        in_specs=[pl.BlockSpec(memory_space=pltpu.SMEM), _bias_spec(), pl.BlockSpec((blk, d), cur),
                  pl.BlockSpec((blk, dkv), cur), pl.BlockSpec((blk, dkv), prev),
                  pl.BlockSpec((blk, dkv), cur), pl.BlockSpec((blk, dkv), prev), pl.BlockSpec((blk, d), cur)],
        out_specs=[pl.BlockSpec((blk, d), cur), pl.BlockSpec((blk, dkv), lag), pl.BlockSpec((blk, dkv), lag),
                   pl.BlockSpec((nh, 128), lambda n: (0, 0))],
        scratch=[pltpu.VMEM((blk, dkv), F32)] * 2, vmem=32 << 20, comm=comm)(sinks, _attn_bias(), q, k, k, v, v, do)


def _ssm_mats(lre, lim, ldt, btr, bti, cr, ci):
    dt = jnp.exp(ldt)
    mag = jnp.exp(lre * dt)
    ar = mag * jnp.cos(lim * dt)
    ai = mag * jnp.sin(lim * dt)
    den = lre * lre + lim * lim
    nr = ar - 1.0
    cfr = (nr * lre + ai * lim) / den
    cfi = (ai * lre - nr * lim) / den
    bbr = cfr * btr - cfi * bti
    bbi = cfr * bti + cfi * btr
    pr = [jnp.ones_like(ar)]
    pi = [jnp.zeros_like(ai)]
    for _ in range(SSM_T):
        pr.append(pr[-1] * ar - pi[-1] * ai)
        pi.append(pr[-2] * ai + pi[-1] * ar)
    last = SSM_T - 1
    p_re = jnp.concatenate([pr[last - s] * bbr - pi[last - s] * bbi for s in range(SSM_T)], axis=0)
    p_im = jnp.concatenate([pr[last - s] * bbi + pi[last - s] * bbr for s in range(SSM_T)], axis=0)
    qt_re = jnp.concatenate([pr[t + 1] * cr - pi[t + 1] * ci for t in range(SSM_T)], axis=0)
    qt_im = jnp.concatenate([-(pr[t + 1] * ci + pi[t + 1] * cr) for t in range(SSM_T)], axis=0)
    ctr = jnp.concatenate([cr] * SSM_T, axis=0)
    cti = jnp.concatenate([ci] * SSM_T, axis=0)
    lag = (lax.broadcasted_iota(jnp.int32, (SSM_W, SSM_W), 1) // SSM_GROUP
           - lax.broadcasted_iota(jnp.int32, (SSM_W, SSM_W), 0) // SSM_GROUP)
    m = jnp.zeros((SSM_W, SSM_W), F32)
    for l in range(SSM_T):
        zr = jnp.concatenate([pr[l] * bbr - pi[l] * bbi] * SSM_T, axis=0)
        zi = jnp.concatenate([pr[l] * bbi + pi[l] * bbr] * SSM_T, axis=0)
        kl = _dot3_nt(zr, ctr) - _dot3_nt(zi, cti)
        m = m + jnp.where(lag == l, kl, 0.0)
    return m, p_re, p_im, qt_re, qt_im, pr[SSM_T], pi[SSM_T]


_SSM_GB = 8


def _ssm_param_specs(ng):
    n, hh = SSM_STATE, SSM_GROUP
    gb = _tile(ng, _SSM_GB)
    row = pl.BlockSpec((gb, 1, n), lambda i: (i, 0, 0))
    one = pl.BlockSpec((gb, 1, 1), lambda i: (i, 0, 0))
    mat = pl.BlockSpec((gb, hh, n), lambda i: (i, 0, 0))
    big = pl.BlockSpec((gb, SSM_W, SSM_W), lambda i: (i, 0, 0))
    half = pl.BlockSpec((gb, SSM_W, n), lambda i: (i, 0, 0))
    return gb, row, one, mat, big, half


def _ssm_prep(params, name):
    ng = params[0].shape[0]
    n = SSM_STATE
    gb, row, one, mat, big, half = _ssm_param_specs(ng)

    def body(lre, lim, ldt, btr, bti, cr, ci, m_ref, pre_ref, pim_ref, qre_ref, qim_ref, atr_ref, ati_ref):
        for gi in range(gb):
            outs = _ssm_mats(lre[gi], lim[gi], ldt[gi], btr[gi], bti[gi], cr[gi], ci[gi])
            for ref, val in zip((m_ref, pre_ref, pim_ref, qre_ref, qim_ref, atr_ref, ati_ref), outs):
                ref[gi] = val

    return _pcall(
        body, name=name,
        out_shape=[SDS((ng, SSM_W, SSM_W), F32)] + [SDS((ng, SSM_W, n), F32)] * 4 + [SDS((ng, 1, n), F32)] * 2,
        grid=(ng // gb,), in_specs=[row, row, one, mat, mat, mat, mat],
        out_specs=[big, half, half, half, half, row, row], vmem=40 << 20)(*params)


def _ssm_prep_vjp(params, cots, name):
    ng = params[0].shape[0]
    n, hh = SSM_STATE, SSM_GROUP
    gb, row, one, mat, big, half = _ssm_param_specs(ng)

    def body(lre, lim, ldt, btr, bti, cr, ci, dm, dpre, dpim, dqre, dqim, datr, dati,
             o_lre, o_lim, o_ldt, o_btr, o_bti, o_cr, o_ci):
        for gi in range(gb):
            prm = (lre[gi], lim[gi], ldt[gi], btr[gi], bti[gi], cr[gi], ci[gi])
            _, pull = jax.vjp(_ssm_mats, *prm)
            grads = pull((dm[gi], dpre[gi], dpim[gi], dqre[gi], dqim[gi], datr[gi], dati[gi]))
            for ref, val in zip((o_lre, o_lim, o_ldt, o_btr, o_bti, o_cr, o_ci), grads):
                ref[gi] = val

    return _pcall(
        body, name=name,
        out_shape=[SDS((ng, 1, n), F32)] * 2 + [SDS((ng, 1, 1), F32)] + [SDS((ng, hh, n), F32)] * 4,
        grid=(ng // gb,), in_specs=[row, row, one, mat, mat, mat, mat, big, half, half, half, half, row, row],
        out_specs=[row, row, one, mat, mat, mat, mat], vmem=48 << 20)(*params, *cots)


_SSM_GT = SSM_W // SSM_GROUP


def _blk_transpose(xs):
    assert len(xs) == SSM_T == _SSM_GT
    blk = lax.broadcasted_iota(jnp.int32, xs[0].shape, 1) // SSM_GROUP
    xs = list(xs)
    k = SSM_T // 2
    while k:
        high = (blk // k) % 2 == 1
        nxt = []
        for i in range(SSM_T):
            if i & k:
                nxt.append(jnp.where(high, xs[i], pltpu.roll(xs[i ^ k], SSM_W - SSM_GROUP * k, 1)))
            else:
                nxt.append(jnp.where(high, pltpu.roll(xs[i ^ k], SSM_GROUP * k, 1), xs[i]))
        xs = nxt
        k //= 2
    return xs


def _tile_groups(x_ref, ncb):
    return _blk_transpose([x_ref[pl.ds(t, ncb, stride=SSM_T), :] for t in range(SSM_T)])


def _groups_tile(ys, o_ref, ncb):
    for t, y in enumerate(_blk_transpose(ys)):
        o_ref[pl.ds(t, ncb, stride=SSM_T), :] = y


def _ssm_specs(seqlen, d):
    ncb = _tile(seqlen // SSM_T, 512)
    grid = (d // SSM_W, seqlen // (SSM_T * ncb))
    act = pl.BlockSpec((SSM_T * ncb, SSM_W), lambda j, r: (r, j))
    state = pl.BlockSpec((ncb, _SSM_GT * SSM_W), lambda j, r: (r, j))
    mats = pl.BlockSpec((_SSM_GT, SSM_W, SSM_W), lambda j, r: (j, 0, 0))
    return ncb, grid, act, state, mats


def _gsl(gl):
    return slice(gl * SSM_W, (gl + 1) * SSM_W)


def _ssm_state_in(hn, pmat, name):
    seqlen, d = hn.shape
    ncb, grid, act, state, mats = _ssm_specs(seqlen, d)

    def body(x_ref, p_ref, s_ref):
        us = _tile_groups(x_ref, ncb)
        for gl in range(_SSM_GT):
            s_ref[:, _gsl(gl)] = _dot3(us[gl], p_ref[gl], NN)

    return _pcall(body, name=name, out_shape=SDS((seqlen // SSM_T, d * SSM_T), F32), grid=grid,
                  in_specs=[act, mats], out_specs=state, vmem=40 << 20)(hn, pmat)


def _ssm_out(hn, xp, mmat, qt, name, comm=None):
    seqlen, d = hn.shape
    ncb, grid, act, state, mats = _ssm_specs(seqlen, d)

    def body(x_ref, xp_ref, m_ref, q_ref, y_ref):
        us = _tile_groups(x_ref, ncb)
        ys = [_dot3(us[gl], m_ref[gl], NN) + _dot3(xp_ref[:, _gsl(gl)], q_ref[gl], NT)
              for gl in range(_SSM_GT)]
        _groups_tile(ys, y_ref, ncb)

    return _pcall(body, name=name, out_shape=[SDS((seqlen, d), F32)], grid=grid,
                  in_specs=[act, state, mats, mats], out_specs=[act], vmem=40 << 20, comm=comm)(hn, xp, mmat, qt)


def _ssm_dstate(dy, qt, name):
    seqlen, d = dy.shape
    ncb, grid, act, state, mats = _ssm_specs(seqlen, d)

    def body(dy_ref, q_ref, o_ref):
        dys = _tile_groups(dy_ref, ncb)
        for gl in range(_SSM_GT):
            o_ref[:, _gsl(gl)] = _dot3(dys[gl], q_ref[gl], NN)

    return _pcall(body, name=name, out_shape=SDS((seqlen // SSM_T, d * SSM_T), F32), grid=grid,
                  in_specs=[act, mats], out_specs=state, vmem=40 << 20)(dy, qt)


def _ssm_bwd(hn, dy, xp, gs, mmat, pmat, name, comm=None):
    seqlen, d = hn.shape
    ng = d // SSM_GROUP
    ncb, grid, act, state, mats = _ssm_specs(seqlen, d)

    def body(x_ref, dy_ref, xp_ref, g_ref, m_ref, p_ref, du_ref, dm_ref, dp_ref, dq_ref, da_ref):
        @pl.when(pl.program_id(1) == 0)
        def _():
            for ref in (dm_ref, dp_ref, dq_ref, da_ref):
                ref[...] = jnp.zeros_like(ref)
        us = _tile_groups(x_ref, ncb)
        dys = _tile_groups(dy_ref, ncb)
        dus = []
        for gl in range(_SSM_GT):
            xv, gv = xp_ref[:, _gsl(gl)], g_ref[:, _gsl(gl)]
            u2, dy2, x2, g2 = _split(us[gl]), _split(dys[gl]), _split(xv), _split(gv)
            dus.append(_dot3(dy2, m_ref[gl], NT) + _dot3(g2, p_ref[gl], NT))
            dm_ref[gl] += _dot3(u2, dy2, TN)
            dp_ref[gl] += _dot3(u2, g2, TN)
            dq_ref[gl] += _dot3(dy2, x2, TN)
            da_ref[gl, 0:1, :] += jnp.sum(xv * gv, axis=0, keepdims=True)
            da_ref[gl, 1:2, :] += jnp.sum(xv * pltpu.roll(gv, SSM_STATE, 1), axis=0, keepdims=True)
        _groups_tile(dus, du_ref, ncb)

    return _pcall(
        body, name=name,
        out_shape=[SDS((seqlen, d), F32)] + [SDS((ng, SSM_W, SSM_W), F32)] * 3 + [SDS((ng, 2, SSM_W), F32)],
        grid=grid, in_specs=[act, act, state, state, mats, mats],
        out_specs=[act, mats, mats, mats, pl.BlockSpec((_SSM_GT, 2, SSM_W), lambda j, r: (j, 0, 0))],
        vmem=48 << 20, comm=comm)(hn, dy, xp, gs, mmat, pmat)


def _ssm_carry(s, a1, a2, reverse, name, comm=None):
    nc, w = s.shape
    tc = _tile(nc, 256)
    nblk = nc // tc
    sub = 8
    shape = (sub, SSM_W)

    def body(s_ref, a1_ref, a2_ref, o_ref, st_ref, pw_ref, p1_ref, p2_ref):
        rows = lax.broadcasted_iota(jnp.int32, (sub, w), 0)
        sign = jnp.where(lax.broadcasted_iota(jnp.int32, (1, w), 1) % SSM_W < SSM_STATE, -1.0, 1.0)

        @pl.when(pl.program_id(0) == 0)
        def _():
            st_ref[...] = jnp.zeros_like(st_ref)
            cr, ci = a1_ref[...], a2_ref[...] * sign
            qr, qi = jnp.ones_like(cr), jnp.zeros_like(ci)
            p1, p2 = jnp.zeros((sub, w), F32), jnp.zeros((sub, w), F32)
            for r in range(sub):
                at = (sub - 1 - r) if reverse else r
                p1 = jnp.where(rows == at, qr, p1)
                p2 = jnp.where(rows == at, qi * sign, p2)
                qr, qi = qr * cr - qi * ci, qr * ci + qi * cr
            p1_ref[...] = p1
            p2_ref[...] = p2
            for k in range(4):
                pw_ref[2 * k:2 * k + 1, :] = cr
                pw_ref[2 * k + 1:2 * k + 2, :] = ci * sign
                cr, ci = cr * cr - ci * ci, 2.0 * cr * ci

        row = lax.broadcasted_iota(jnp.int32, shape, 0)

        def cmul(c1, c2, v):
            return c1 * v + c2 * pltpu.roll(v, SSM_STATE, 1)

        def shifted(v, k):
            if reverse:
                return jnp.where(row < sub - k, pltpu.roll(v, sub - k, 0), 0.0)
            return jnp.where(row >= k, pltpu.roll(v, k, 0), 0.0)

        def step(t, carry):
            tt = (tc // sub - 1 - t) if reverse else t
            base = pl.multiple_of(tt * sub, sub)
            for g in range(w // SSM_W):
                lanes = slice(g * SSM_W, (g + 1) * SSM_W)
                pw = [jnp.broadcast_to(pw_ref[i:i + 1, lanes], shape) for i in range(8)]
                y = s_ref[pl.ds(base, sub), lanes]
                for k in range(3):
                    y = y + cmul(pw[2 * k], pw[2 * k + 1], shifted(y, 1 << k))
                x = st_ref[:, lanes]
                o_ref[pl.ds(base, sub), lanes] = shifted(y, 1) + cmul(p1_ref[:, lanes], p2_ref[:, lanes], x)
                end = y[0:1, :] if reverse else y[sub - 1:sub, :]
                st_ref[:, lanes] = jnp.broadcast_to(end, shape) + cmul(pw[6], pw[7], x)
            return carry

        lax.fori_loop(0, tc // sub, step, 0)

    imap = (lambda i: (nblk - 1 - i, 0)) if reverse else (lambda i: (i, 0))
    cst = pl.BlockSpec((1, w), lambda i: (0, 0))
    return _pcall(body, name=name, out_shape=[SDS((nc, w), F32)], grid=(nblk,),
                  in_specs=[pl.BlockSpec((tc, w), imap), cst, cst], out_specs=[pl.BlockSpec((tc, w), imap)],
                  scratch=[pltpu.VMEM((sub, w), F32)] * 4,
                  vmem=_vmem_limit([((tc, w), F32, 4)], extra=8 << 20), comm=comm)(s, a1, a2)


def _ssm_rows(atr, ati, conj):
    ng = atr.shape[0]
    ai = -ati if conj else ati
    a1 = jnp.concatenate([atr, atr], axis=2).reshape(1, ng * SSM_W)
    a2 = jnp.concatenate([-ai, ai], axis=2).reshape(1, ng * SSM_W)
    return a1, a2


def _peers():
    x, y, c = (lax.axis_index(a) for a in AXES)
    me = 4 * x + 2 * y + c
    peers = []
    for dx, dy, dc in [(0, 0, 1), (0, 1, 0), (0, 1, 1), (1, 0, 0), (1, 0, 1), (1, 1, 0), (1, 1, 1)]:
        px, py, pc = (1 - x) if dx else x, (1 - y) if dy else y, (1 - c) if dc else c
        peers.append(((px, py, pc), 4 * px + 2 * py + pc))
    return me, peers


class _Exchange:
    def __init__(self, arrs, scatter, layers=None):
        self.arrs = list(arrs)
        self.scatter = scatter
        self.layers = list(layers) if layers is not None else [None] * len(self.arrs)

    def out_shape(self):
        shapes = []
        for arr, layer in zip(self.arrs, self.layers):
            block = arr.shape[1:] if (self.scatter or layer is not None) else arr.shape
            shapes.append(SDS((NDEV,) + tuple(block), arr.dtype))
        return shapes

    def semaphores(self):
        n = len(self.arrs)
        return [pltpu.SemaphoreType.DMA((n * (NDEV - 1),)), pltpu.SemaphoreType.DMA((n * (NDEV - 1),)),
                pltpu.SemaphoreType.DMA((n,))]

    def _src(self, ref, a, block):
        if self.scatter:
            return ref.at[block]
        return ref if self.layers[a] is None else ref.at[self.layers[a]]

    def _remote(self, xin, xout, sems, a, k, peer, landing):
        pid, pidx = peer
        slot = a * (NDEV - 1) + k
        return pltpu.make_async_remote_copy(
            src_ref=self._src(xin[a], a, pidx), dst_ref=xout[a].at[landing],
            send_sem=sems[0].at[slot], recv_sem=sems[1].at[slot], device_id=pid, device_id_type=MESH)

    def _local(self, xin, xout, sems, a, me):
        return pltpu.make_async_copy(self._src(xin[a], a, me), xout[a].at[me], sems[2].at[a])

    def start(self, xin, xout, sems):
        me, peers = _peers()
        for a in range(len(self.arrs)):
            self._local(xin, xout, sems, a, me).start()
        for k, peer in enumerate(peers):
            for a in range(len(self.arrs)):
                self._remote(xin, xout, sems, a, k, peer, me).start()

    def wait(self, xin, xout, sems):
        me, peers = _peers()
        for a in range(len(self.arrs)):
            self._local(xin, xout, sems, a, me).wait()
        for k, peer in enumerate(peers):
            for a in range(len(self.arrs)):
                cp = self._remote(xin, xout, sems, a, k, peer, peer[1])
                cp.wait_send()
                cp.wait_recv()


def _exchange(arrs, scatter, name, layers=None):
    comm = _Exchange(arrs, scatter, layers)
    n = len(comm.arrs)

    def body(*refs):
        xin, xout, sems = refs[:n], refs[n:2 * n], refs[2 * n:]
        comm.start(xin, xout, sems)
        comm.wait(xin, xout, sems)

    hbm = pl.BlockSpec(memory_space=pl.ANY)
    return pl.pallas_call(
        body, out_shape=comm.out_shape(), in_specs=[hbm] * n, out_specs=[hbm] * n,
        scratch_shapes=comm.semaphores(), name=name, interpret=False)(*comm.arrs)


def _adamw(parts, w, m, v, name):
    rows, cols = w.shape
    tr = _tile(rows, max(8, (1 << 17) // cols))
    c1 = 1.0 - ADAM_B1 ** ADAM_STEP
    c2 = 1.0 - ADAM_B2 ** ADAM_STEP

    def body(p_ref, w_ref, m_ref, v_ref, g_ref, d_ref, nm_ref, nv_ref):
        g = p_ref[0]
        for j in range(1, NDEV):
            g = g + p_ref[j]
        mm = ADAM_B1 * m_ref[...] + (1.0 - ADAM_B1) * g
        vv = ADAM_B2 * v_ref[...] + (1.0 - ADAM_B2) * (g * g)
        g_ref[...] = g
        nm_ref[...] = mm
        nv_ref[...] = vv
        d_ref[...] = -ADAM_LR * ((mm / c1) / (jnp.sqrt(vv / c2) + ADAM_EPS) + ADAM_WD * w_ref[...])

    spec = pl.BlockSpec((tr, cols), lambda i: (i, 0))
    return _pcall(
        body, name=name, out_shape=[SDS((rows, cols), F32)] * 4, grid=(rows // tr,),
        in_specs=[pl.BlockSpec((NDEV, tr, cols), lambda i: (0, i, 0)), spec, spec, spec], out_specs=[spec] * 4,
        vmem=_vmem_limit([((NDEV + 7, tr, cols), F32, 2)]))(parts, w, m, v)


def kernel(x, p, norm_mix, ssm_lambda_re, ssm_lambda_im, ssm_log_dt, ssm_b_re, ssm_b_im, ssm_c_re, ssm_c_im, ssm_d, ssm_w_glu, kv_norm, w_k, w_v, w_q, attn_sinks, w_o, norm_mlp, w_up, w_down, norm_ple, w_ple_gate, w_ple_proj, norm_final, loss_target, m_norm_mix, m_ssm_lambda_re, m_ssm_lambda_im, m_ssm_log_dt, m_ssm_b_re, m_ssm_b_im, m_ssm_c_re, m_ssm_c_im, m_ssm_d, m_ssm_w_glu, m_kv_norm, m_w_k, m_w_v, m_w_q, m_attn_sinks, m_w_o, m_norm_mlp, m_w_up, m_w_down, m_norm_ple, m_w_ple_gate, m_w_ple_proj, m_norm_final, v_norm_mix, v_ssm_lambda_re, v_ssm_lambda_im, v_ssm_log_dt, v_ssm_b_re, v_ssm_b_im, v_ssm_c_re, v_ssm_c_im, v_ssm_d, v_ssm_w_glu, v_kv_norm, v_w_k, v_w_v, v_w_q, v_attn_sinks, v_w_o, v_norm_mlp, v_w_up, v_w_down, v_norm_ple, v_w_ple_gate, v_w_ple_proj, v_norm_final):
    names = ['norm_mix', 'ssm_lambda_re', 'ssm_lambda_im', 'ssm_log_dt', 'ssm_b_re', 'ssm_b_im', 'ssm_c_re',
             'ssm_c_im', 'ssm_d', 'ssm_w_glu', 'kv_norm', 'w_k', 'w_v', 'w_q', 'attn_sinks', 'w_o', 'norm_mlp',
             'w_up', 'w_down', 'norm_ple', 'w_ple_gate', 'w_ple_proj', 'norm_final']
    weights = dict(zip(names, (norm_mix, ssm_lambda_re, ssm_lambda_im, ssm_log_dt, ssm_b_re, ssm_b_im, ssm_c_re,
                               ssm_c_im, ssm_d, ssm_w_glu, kv_norm, w_k, w_v, w_q, attn_sinks, w_o, norm_mlp,
                               w_up, w_down, norm_ple, w_ple_gate, w_ple_proj, norm_final)))
    mom1 = dict(zip(names, (m_norm_mix, m_ssm_lambda_re, m_ssm_lambda_im, m_ssm_log_dt, m_ssm_b_re, m_ssm_b_im,
                            m_ssm_c_re, m_ssm_c_im, m_ssm_d, m_ssm_w_glu, m_kv_norm, m_w_k, m_w_v, m_w_q,
                            m_attn_sinks, m_w_o, m_norm_mlp, m_w_up, m_w_down, m_norm_ple, m_w_ple_gate,
                            m_w_ple_proj, m_norm_final)))
    mom2 = dict(zip(names, (v_norm_mix, v_ssm_lambda_re, v_ssm_lambda_im, v_ssm_log_dt, v_ssm_b_re, v_ssm_b_im,
                            v_ssm_c_re, v_ssm_c_im, v_ssm_d, v_ssm_w_glu, v_kv_norm, v_w_k, v_w_v, v_w_q,
                            v_attn_sinks, v_w_o, v_norm_mlp, v_w_up, v_w_down, v_norm_ple, v_w_ple_gate,
                            v_w_ple_proj, v_norm_final)))

    seqlen, d = x.shape[1], x.shape[2]
    depth = w_up.shape[0]
    n_ssm = ssm_w_glu.shape[0]
    n_att = w_q.shape[0]
    ng = d // SSM_GROUP
    nh = d // HEAD_DIM
    h0 = x[0]
    tgt = loss_target[0]
    tabs = _rope_tables(seqlen)

    sharded = ['w_up', 'w_down', 'w_ple_gate', 'w_ple_proj', 'ssm_w_glu', 'w_q', 'w_o', 'w_k', 'w_v']
    shards = {k: weights[k].astype(BF16) for k in sharded}
    shards['ssm_d'] = ssm_d
    dkv = w_k.shape[1]

    def layer_set(i):
        keys = [('w_up', i), ('w_down', i), ('w_ple_gate', i), ('w_ple_proj', i)]
        keys += [('ssm_w_glu', i), ('ssm_d', i)] if i < n_ssm else [('w_q', i - n_ssm), ('w_o', i - n_ssm)]
        if i == n_ssm:
            keys += [('w_k', None), ('w_v', None)]
        return keys

    def gather_of(i, only=None):
        keys = [kl for kl in layer_set(i) if only is None or kl[0] in only]
        return keys, _Exchange([shards[k] for k, _ in keys], False, [l for _, l in keys])

    def as_operands(keys, blocks):
        w = {}
        for (k, _), g in zip(keys, blocks):
            if k == 'ssm_d':
                w[k] = g.reshape(d)
            elif k in ('w_ple_proj', 'ssm_w_glu', 'w_up'):
                w[k] = g.transpose(1, 0, 2).reshape(g.shape[1], NDEV * g.shape[2])
                w[k + '_t'] = g.transpose(0, 2, 1).reshape(NDEV * g.shape[2], g.shape[1])
            else:
                w[k] = g.reshape(NDEV * g.shape[1], g.shape[2])
                w[k + '_t'] = g.transpose(2, 0, 1).reshape(g.shape[2], NDEV * g.shape[1])
        return w

    lw = {}

    def ssm_params(i):
        n = SSM_STATE
        return (ssm_lambda_re[i].reshape(ng, 1, n), ssm_lambda_im[i].reshape(ng, 1, n),
                ssm_log_dt[i].reshape(ng, 1, 1), jnp.swapaxes(ssm_b_re[i], 1, 2), jnp.swapaxes(ssm_b_im[i], 1, 2),
                ssm_c_re[i], ssm_c_im[i])

    h = h0
    h_in, h_a, h_b, acts = [], [], [], []
    ssm_saved, att_saved = {}, {}
    k_sh = v_sh = None
    for i in range(depth):
        h_in.append(h)
        if i < n_ssm:
            first = gather_of(0, ('ssm_w_glu', 'ssm_d')) if i == 0 else None
            hn, *got = _norm_fwd(h, norm_mix[i], f"norm_mix_fwd{i}", comm=first and first[1])
            if first:
                lw[0] = as_operands(first[0], got)
            mats = _ssm_prep(ssm_params(i), f"ssm_prep{i}")
            mmat, atr, ati = mats[0], mats[5], mats[6]
            pmat = jnp.concatenate([mats[1], mats[2]], axis=2)
            qt = jnp.concatenate([mats[3], mats[4]], axis=2)
            s_in = _ssm_state_in(hn, pmat, f"ssm_state_in{i}")
            xp = _ssm_carry(s_in, *_ssm_rows(atr, ati, False), False, f"ssm_carry_fwd{i}")[0]
            rest = gather_of(0, ('w_up', 'w_down', 'w_ple_gate', 'w_ple_proj')) if i == 0 else None
            y, *got = _ssm_out(hn, xp, mmat, qt, f"ssm_out{i}", comm=rest and rest[1])
            if rest:
                lw[0].update(as_operands(rest[0], got))
            ha = _glu_fwd(y, hn, h, lw[i]['ssm_d'], lw[i]['ssm_w_glu'], f"glu_fwd{i}")
            ssm_saved[i] = (hn, mmat, pmat, qt, atr, ati, xp, y)
        else:
            j = i - n_ssm
            q = _q_fwd(h, norm_mix[i], lw[i]['w_q'], tabs, f"q_fwd{j}")
            o = _attn_fwd(q, k_sh, v_sh, attn_sinks[j], f"attn_fwd{j}")
            ha = _lin_res(h, o, lw[i]['w_o'], f"attn_out{j}")
            att_saved[j] = (q, o)
        h_a.append(ha)
        nxt = gather_of(i + 1) if i + 1 < depth else None
        res = _mlp_fwd(ha, norm_mlp[i], lw[i]['w_up'], lw[i]['w_down'], f"mlp_fwd{i}", comm=nxt and nxt[1])
        hb = res[0]
        acts.append(res[1])
        if nxt:
            lw[i + 1] = as_operands(nxt[0], res[2:])
        h_b.append(hb)
        h = _ple_fwd(hb, p[i, 0], norm_ple[i], lw[i]['w_ple_gate'], lw[i]['w_ple_proj'], f"ple_fwd{i}")
        if i == n_ssm - 1:
            k_sh, v_sh = _kv_fwd(h, kv_norm, lw[n_ssm]['w_k'], lw[n_ssm]['w_v'], tabs, "kv_fwd")
    h_kv = h_in[n_ssm] if n_ssm < depth else h
    dh, loss_row, g_norm_final = _loss_bwd(h, norm_final, tgt, "loss_bwd")
    loss = lax.psum(loss_row[0, 0], AXES)

    g_norm_mix, g_norm_mlp, g_norm_ple = [None] * depth, [None] * depth, [None] * depth
    g_ssm, g_sinks = [None] * n_ssm, [None] * n_att
    g_kv_norm = None
    dks, dvs = [], []
    recv = {}

    def riding(stacks):
        if not stacks:
            return None
        ride = (list(stacks), _Exchange(list(stacks.values()), True))
        stacks.clear()
        return ride

    def landed(ride, blocks):
        if ride:
            recv.update(zip(ride[0], blocks))

    split_names = ('norm_mix', 'ssm_lambda_re', 'ssm_lambda_im', 'ssm_log_dt', 'ssm_b_re', 'ssm_b_im',
                   'ssm_c_re', 'ssm_c_im')
    small_names = split_names + ('kv_norm', 'attn_sinks', 'norm_mlp', 'norm_ple', 'norm_final')

    def layers_of(t, k, first):
        if k not in split_names:
            return t
        return t[:1] if first else t[1:]

    def flat_rows(arrs):
        n = sum(a.size for a in arrs)
        padded = -(-n // (512 * 128)) * (512 * 128)
        v = jnp.concatenate([a.reshape(-1) for a in arrs] + [jnp.zeros((padded - n,), F32)])
        return v.reshape(padded // 128, 128)

    def small_grads(first):
        which = [0] if first else list(range(1, n_ssm))

        def ssm(idx, unswap=False):
            g = jnp.stack([g_ssm[i][idx] for i in which])
            return jnp.swapaxes(g, 2, 3) if unswap else g

        g = {'norm_mix': jnp.concatenate(g_norm_mix[:1] if first else g_norm_mix[1:], axis=0),
             'ssm_lambda_re': ssm(0), 'ssm_lambda_im': ssm(1), 'ssm_log_dt': ssm(2),
             'ssm_b_re': ssm(3, True), 'ssm_b_im': ssm(4, True), 'ssm_c_re': ssm(5), 'ssm_c_im': ssm(6)}
        if not first:
            g.update(kv_norm=g_kv_norm, attn_sinks=jnp.stack(g_sinks), norm_final=g_norm_final,
                     norm_mlp=jnp.concatenate(g_norm_mlp, axis=0), norm_ple=jnp.concatenate(g_norm_ple, axis=0))
        return g

    parts_early = None
    gl = {}
    for i in range(depth - 1, -1, -1):
        if i == n_ssm - 1:
            dh, dkp, dvb, hkb, g_kv_norm = _kv_bwd(dks, dvs, h_kv, kv_norm, dh, lw[n_ssm]['w_k_t'],
                                                   lw[n_ssm]['w_v_t'], tabs, "kv_bwd")
            gl['w_k', None] = _atb(hkb, dkp, False, "grad_w_k")
            gl['w_v', None] = _atb(hkb, dvb, False, "grad_w_v")
        dhb, dz, nb16, dpp, g_norm_ple[i] = _ple_bwd(h_b[i], p[i, 0], dh, norm_ple[i], lw[i]['w_ple_gate'],
                                                     lw[i]['w_ple_gate_t'], lw[i]['w_ple_proj'], f"ple_bwd{i}")
        gl['w_ple_gate', i] = _atb(nb16, dz, False, f"grad_w_ple_gate{i}")
        gl['w_ple_proj', i] = _atb(p[i, 0], dpp, True, f"grad_w_ple_proj{i}")
        ride = riding(gl)
        res = _mlp_bwd(h_a[i], acts[i], dhb, norm_mlp[i], lw[i]['w_up_t'], lw[i]['w_down_t'], f"mlp_bwd{i}",
                       comm=ride and ride[1])
        dha, hmb, da, g_norm_mlp[i] = res[:4]
        landed(ride, res[4:])
        g_up = {('w_up', i): _atb(hmb, da, True, f"grad_w_up{i}")}
        g_down = {('w_down', i): _atb(acts[i], dhb, False, f"grad_w_down{i}")}
        if i >= n_ssm:
            j = i - n_ssm
            q, o = att_saved[j]
            do = _lin_bf16(dha, lw[i]['w_o_t'], f"attn_out_bwd{j}")
            ride = riding({**g_up, **g_down, ('w_o', j): _atb(o, dha, False, f"grad_w_o{j}")})
            dq, dk_j, dv_j, dsink, *got = _attn_bwd(q, k_sh, v_sh, do, attn_sinks[j], f"attn_bwd{j}",
                                                    comm=ride[1])
            landed(ride, got)
            dks.append(dk_j)
            dvs.append(dv_j)
            g_sinks[j] = dsink[:, 0]
            dh, dqp, hnb, g_norm_mix[i] = _q_bwd(dq, h_in[i], norm_mix[i], dha, lw[i]['w_q_t'], tabs, f"q_bwd{j}")
            gl['w_q', j] = _atb(hnb, dqp, False, f"grad_w_q{j}")
        else:
            hn, mmat, pmat, qt, atr, ati, xp, y = ssm_saved[i]
            early = None
            if i == 0:
                grads = small_grads(False)
                early = _Exchange([flat_rows([grads[k] for k in small_names])], False)
            dyy, dhn_d, geb, dab, g_dskip, *got = _glu_bwd(y, hn, dha, lw[i]['ssm_d'], lw[i]['ssm_w_glu'],
                                                           lw[i]['ssm_w_glu_t'], f"glu_bwd{i}", comm=early)
            if early:
                parts_early = got[0]
            g_down['ssm_d', i] = g_dskip.reshape(NDEV, d // NDEV)
            g_down['ssm_w_glu', i] = _atb(geb, dab, True, f"grad_ssm_w_glu{i}")
            dxp = _ssm_dstate(dyy, qt, f"ssm_dstate{i}")
            ride = riding(g_up)
            gs, *got = _ssm_carry(dxp, *_ssm_rows(atr, ati, True), True, f"ssm_carry_bwd{i}", comm=ride[1])
            landed(ride, got)
            ride = riding(g_down)
            du, dm, dp, dqt, da_raw, *got = _ssm_bwd(hn, dyy, xp, gs, mmat, pmat, f"ssm_bwd{i}", comm=ride[1])
            landed(ride, got)
            n = SSM_STATE
            cots = (dm, dp[:, :, :n], dp[:, :, n:], dqt[:, :, :n], dqt[:, :, n:],
                    (da_raw[:, 0:1, :n] + da_raw[:, 0:1, n:]), (da_raw[:, 1:2, :n] - da_raw[:, 1:2, n:]))
            g_ssm[i] = _ssm_prep_vjp(ssm_params(i), cots, f"ssm_prep_vjp{i}")
            dh, g_norm_mix[i] = _norm_bwd(h_in[i], norm_mix[i], dhn_d, du, dha, f"norm_mix_bwd{i}")
    grad_x = dh[None]
    if gl:
        ride = riding(gl)
        landed(ride, _exchange(ride[1].arrs, True, "scatter_grads_rest"))

    out_g, out_d, out_m, out_v = {}, {}, {}, {}
    updated = {}
    for (k, l), parts in recv.items():
        pick = (lambda t: t) if l is None else (lambda t: t[l])
        shp = pick(weights[k]).shape
        r2 = (math.prod(shp[:-1]), shp[-1])
        res = _adamw(parts.reshape((NDEV,) + r2), pick(weights[k]).reshape(r2), pick(mom1[k]).reshape(r2),
                     pick(mom2[k]).reshape(r2), f"adamw_{k}{'' if l is None else l}")
        updated.setdefault(k, {})[l] = [t.reshape(shp) for t in res]
    for k, by_layer in updated.items():
        for n, dst in enumerate((out_g, out_d, out_m, out_v)):
            dst[k] = by_layer[None][n] if None in by_layer else jnp.stack([by_layer[l][n] for l in sorted(by_layer)])

    late_flat = flat_rows([small_grads(True)[k] for k in split_names])
    parts_late = _exchange([late_flat], False, "gather_small_grads")[0]
    parts = jnp.concatenate([parts_early, parts_late], axis=1)

    def both(src):
        return jnp.concatenate([flat_rows([layers_of(src[k], k, False) for k in small_names]),
                                flat_rows([layers_of(src[k], k, True) for k in split_names])], axis=0)

    res = _adamw(parts, both(weights), both(mom1), both(mom2), "adamw_small")
    for dst, t in zip((out_g, out_d, out_m, out_v), res):
        t = t.reshape(-1)
        off, rest = 0, {}
        for k in small_names:
            shp = layers_of(weights[k], k, False).shape
            rest[k] = t[off:off + math.prod(shp)].reshape(shp)
            off += math.prod(shp)
        off = parts_early.shape[1] * 128
        for k in small_names:
            if k in split_names:
                shp = layers_of(weights[k], k, True).shape
                dst[k] = jnp.concatenate([t[off:off + math.prod(shp)].reshape(shp), rest[k]], axis=0)
                off += math.prod(shp)
            else:
                dst[k] = rest[k]

    return (loss, grad_x, *[out_g[k] for k in names], *[out_d[k] for k in names],
            *[out_m[k] for k in names], *[out_v[k] for k in names])
```

```python
import functools
import math

import jax
import jax.numpy as jnp
from jax import lax
from jax.experimental import pallas as pl
from jax.experimental.pallas import tpu as pltpu

F32 = jnp.float32
BF16 = jnp.bfloat16
SDS = jax.ShapeDtypeStruct
MESH = pl.DeviceIdType.MESH
AXES = ("x", "y", "c")
NDEV = 8

RMS_EPS = 1e-6
SSM_GROUP = 16
SSM_STATE = 64
SSM_T = 8
SSM_W = SSM_T * SSM_GROUP
HEAD_DIM = 64
GQA_GROUP = 4
ATTN_BLOCK = 128
ROT_DIM = 16
ROPE_THETA = 500000.0
NEG_INF = -1e30
ADAM_LR, ADAM_B1, ADAM_B2, ADAM_EPS, ADAM_WD, ADAM_STEP = 0.001, 0.9, 0.999, 1e-08, 0.01, 10

VMEM_CAP = 56 * 1024 * 1024
HI = lax.Precision.HIGHEST

NN = ((1,), (0,))
NT = ((1,), (1,))
TN = ((0,), (0,))


def _dot(a, b, dims=NN, precision=None):
    return lax.dot_general(a, b, (dims, ((), ())), preferred_element_type=F32, precision=precision)


def _split(a):
    if isinstance(a, tuple):
        return a
    hi = a.astype(BF16)
    return hi, (a - hi.astype(F32)).astype(BF16)


def _dot3(a, b, dims=NN):
    (ah, al), (bh, bl) = _split(a), _split(b)
    return _dot(ah, bh, dims) + (_dot(ah, bl, dims) + _dot(al, bh, dims))


@jax.custom_vjp
def _dot3_nt(a, b):
    return _dot3(a, b, NT)


def _dot3_nt_fwd(a, b):
    return _dot3(a, b, NT), (a, b)


def _dot3_nt_bwd(res, g):
    a, b = res
    return _dot3(g, b, NN), _dot3(g, a, TN)


_dot3_nt.defvjp(_dot3_nt_fwd, _dot3_nt_bwd)


def _tile(n, pref):
    t = min(n, pref)
    while n % t:
        t //= 2
    return t


def _nbytes(shape, dtype):
    return math.prod(s for s in shape if s is not None) * jnp.dtype(dtype).itemsize


def _vmem_limit(blocks, extra=0):
    need = sum(_nbytes(s, d) * n for s, d, n in blocks) + extra + (4 << 20)
    return int(min(VMEM_CAP, max(need, 16 << 20)))


def _pcall(body, *, name, out_shape, grid, in_specs, out_specs, scratch=(), vmem=None, comm=None):
    single = not isinstance(out_shape, (list, tuple))
    out_shape = [out_shape] if single else list(out_shape)
    out_specs = [out_specs] if single else list(out_specs)
    in_specs, scratch = list(in_specs), list(scratch)
    if comm is not None:
        n_in, n_out, n_scr, nx = len(in_specs), len(out_specs), len(scratch), len(comm.arrs)
        hbm = pl.BlockSpec(memory_space=pl.ANY)
        in_specs = in_specs + [hbm] * nx
        out_specs = out_specs + [hbm] * nx
        out_shape = out_shape + comm.out_shape()
        scratch = scratch + comm.semaphores()
        inner = body

        def body(*refs):
            ins, xin, rest = refs[:n_in], refs[n_in:n_in + nx], refs[n_in + nx:]
            outs, xout, rest = rest[:n_out], rest[n_out:n_out + nx], rest[n_out + nx:]
            scr, sems = rest[:n_scr], rest[n_scr:]
            first = functools.reduce(jnp.logical_and, [pl.program_id(a) == 0 for a in range(len(grid))])
            last = functools.reduce(jnp.logical_and, [pl.program_id(a) == g - 1 for a, g in enumerate(grid)])

            @pl.when(first)
            def _():
                comm.start(xin, xout, sems)
            inner(*ins, *outs, *scr)

            @pl.when(last)
            def _():
                comm.wait(xin, xout, sems)

    call = pl.pallas_call(
        body, out_shape=out_shape[0] if single and comm is None else out_shape, grid=grid, in_specs=in_specs,
        out_specs=out_specs[0] if single and comm is None else out_specs, scratch_shapes=scratch, name=name,
        compiler_params=pltpu.CompilerParams(
            dimension_semantics=("arbitrary",) * len(grid), vmem_limit_bytes=vmem),
        interpret=False)
    if comm is None:
        return call
    return lambda *args: call(*args, *comm.arrs)


def _rms(x, g):
    r = lax.rsqrt(jnp.mean(x * x, axis=-1, keepdims=True) + RMS_EPS)
    return x * r * g, r


def _rms_bwd(x, g, r, dy):
    xh = x * r
    dyg = dy * g
    dx = r * (dyg - xh * jnp.mean(dyg * xh, axis=-1, keepdims=True))
    return dx, jnp.sum(dy * xh, axis=0, keepdims=True)


_GELU_C = math.sqrt(2.0 / math.pi)


def _gelu_parts(x):
    t = jnp.tanh(_GELU_C * (x + 0.044715 * x * x * x))
    return 0.5 * x * (1.0 + t), t


def _gelu_grad(x, t):
    return 0.5 * (1.0 + t) + 0.5 * x * (1.0 - t * t) * _GELU_C * (1.0 + 3 * 0.044715 * x * x)


def _rope_tables(seqlen):
    half = ROT_DIM // 2
    inv = ROPE_THETA ** (-jnp.arange(0, ROT_DIM, 2, dtype=F32) / ROT_DIM)
    ang = jnp.arange(seqlen, dtype=jnp.int32).astype(F32)[:, None] * inv[None, :]
    cos, sin = jnp.cos(ang), jnp.sin(ang)
    zeros = jnp.zeros((seqlen, HEAD_DIM - ROT_DIM), F32)
    zh = jnp.zeros((seqlen, half), F32)
    c = jnp.concatenate([cos, cos, zeros + 1.0], axis=1)
    sa = jnp.concatenate([zh, sin, zeros], axis=1)
    sb = jnp.concatenate([-sin, zh, zeros], axis=1)
    return tuple(jnp.tile(t, (1, 128 // HEAD_DIM)) for t in (c, sa, sb))


def _rope(x, c, sa, sb):
    w = x.shape[1]
    reps = w // 128
    half = ROT_DIM // 2
    return (x * jnp.tile(c, (1, reps)) + pltpu.roll(x, half, 1) * jnp.tile(sa, (1, reps))
            + pltpu.roll(x, w - half, 1) * jnp.tile(sb, (1, reps)))


def _rope_bwd(dy, c, sa, sb):
    w = dy.shape[1]
    reps = w // 128
    half = ROT_DIM // 2
    return (dy * jnp.tile(c, (1, reps)) + pltpu.roll(dy * jnp.tile(sa, (1, reps)), w - half, 1)
            + pltpu.roll(dy * jnp.tile(sb, (1, reps)), half, 1))


def _rspec(tm, c):
    return pl.BlockSpec((tm, c), lambda i: (i, 0))


def _cspec(shape, idx=None):
    idx = tuple(idx) if idx is not None else (0,) * len(shape)
    return pl.BlockSpec(tuple(shape), lambda i: idx, pipeline_mode=pl.Buffered(1))


def _rowcall(body, name, seqlen, tm, rows_in, consts_in, rows_out, acc_out=(), extra_vmem=0, comm=None):
    in_specs = [_rspec(tm, a.shape[1]) for a in rows_in] + [_cspec(bs, ix) for _, bs, ix in consts_in]
    out_shape = [SDS((seqlen, c), d) for c, d in rows_out] + [SDS(s, F32) for s in acc_out]
    out_specs = [_rspec(tm, c) for c, _ in rows_out] + [pl.BlockSpec(s, lambda i: (0, 0)) for s in acc_out]
    blocks = ([((tm, a.shape[1]), a.dtype, 2) for a in rows_in] + [(bs, a.dtype, 1) for a, bs, _ in consts_in]
              + [((tm, c), d, 2) for c, d in rows_out])
    temporaries = 12 * tm * rows_in[0].shape[1] * 4
    return _pcall(body, name=name, out_shape=out_shape, grid=(seqlen // tm,), in_specs=in_specs,
                  out_specs=out_specs, vmem=_vmem_limit(blocks, extra_vmem + temporaries), comm=comm)(
                      *rows_in, *[a for a, _, _ in consts_in])


def _whole(a):
    return (a, a.shape, None)


def _norm_fwd(h, g, name, comm=None):
    seqlen, d = h.shape
    tm = _tile(seqlen, 1024)

    def body(h_ref, g_ref, o_ref):
        o_ref[...] = _rms(h_ref[...], g_ref[...])[0]

    return _rowcall(body, name, seqlen, tm, [h], [_whole(g.reshape(1, d))], [(d, F32)], comm=comm)


def _norm_bwd(h, g, dy1, dy2, dres, name, comm=None):
    seqlen, d = h.shape
    tm = _tile(seqlen, 512)

    def body(h_ref, dy1_ref, dy2_ref, dres_ref, g_ref, dh_ref, dg_ref):
        @pl.when(pl.program_id(0) == 0)
        def _():
            dg_ref[...] = jnp.zeros_like(dg_ref)
        x = h_ref[...]
        gv = g_ref[...]
        _, r = _rms(x, gv)
        dx, dg = _rms_bwd(x, gv, r, dy1_ref[...] + dy2_ref[...])
        dh_ref[...] = dres_ref[...] + dx
        dg_ref[...] += dg

    return _rowcall(body, name, seqlen, tm, [h, dy1, dy2, dres], [_whole(g.reshape(1, d))], [(d, F32)], [(1, d)],
                    comm=comm)


def _glu_fwd(y, hn, h, dskip, wglu, name):
    seqlen, d = h.shape
    tm = _tile(seqlen, 512)

    def body(y_ref, hn_ref, h_ref, d_ref, w_ref, o_ref):
        yy = y_ref[...] + d_ref[...] * hn_ref[...]
        ge, _ = _gelu_parts(yy)
        ab = _dot(ge.astype(BF16), w_ref[...])
        o_ref[...] = h_ref[...] + ab[:, :d] * jax.nn.sigmoid(ab[:, d:])

    return _rowcall(body, name, seqlen, tm, [y, hn, h], [_whole(dskip.reshape(1, d)), _whole(wglu)], [(d, F32)],
                    extra_vmem=tm * d * 4 * 6)[0]


def _glu_bwd(y, hn, dmix, dskip, wglu, wglu_t, name):
    seqlen, d = hn.shape
    tm = _tile(seqlen, 512)

    def body(y_ref, hn_ref, dm_ref, d_ref, w_ref, wt_ref, dyy_ref, dhn_ref, ge_ref, dab_ref, dd_ref):
        @pl.when(pl.program_id(0) == 0)
        def _():
            dd_ref[...] = jnp.zeros_like(dd_ref)
        hn_v = hn_ref[...]
        dsk = d_ref[...]
        yy = y_ref[...] + dsk * hn_v
        ge, t = _gelu_parts(yy)
        geb = ge.astype(BF16)
        ab = _dot(geb, w_ref[...])
        a = ab[:, :d]
        sg = jax.nn.sigmoid(ab[:, d:])
        dm = dm_ref[...]
        dab_ref[:, :d] = (dm * sg).astype(BF16)
        dab_ref[:, d:] = (dm * a * sg * (1.0 - sg)).astype(BF16)
        dge = _dot(dab_ref[...], wt_ref[...])
        dyy = dge * _gelu_grad(yy, t)
        dyy_ref[...] = dyy
        dhn_ref[...] = dyy * dsk
        ge_ref[...] = geb
        dd_ref[...] += jnp.sum(dyy * hn_v, axis=0, keepdims=True)

    return _rowcall(body, name, seqlen, tm, [y, hn, dmix],
                    [_whole(dskip.reshape(1, d)), _whole(wglu), _whole(wglu_t)],
                    [(d, F32), (d, F32), (d, BF16), (2 * d, BF16)], [(1, d)], extra_vmem=tm * d * 4 * 8)


def _q_fwd(h, g, wq, tabs, name):
    seqlen, d = h.shape
    tm = _tile(seqlen, 512)

    def body(h_ref, c_ref, sa_ref, sb_ref, g_ref, w_ref, q_ref):
        hn, _ = _rms(h_ref[...], g_ref[...])
        qp = _dot(hn.astype(BF16), w_ref[...])
        q_ref[...] = _rope(qp, c_ref[...], sa_ref[...], sb_ref[...]).astype(BF16)

    return _rowcall(body, name, seqlen, tm, [h, *tabs], [_whole(g.reshape(1, d)), _whole(wq)], [(d, BF16)],
                    extra_vmem=tm * d * 4 * 6)[0]


def _q_bwd(dq, h, g, dres, wq, tabs, name):
    seqlen, d = h.shape
    tm = _tile(seqlen, 512)

    def body(dq_ref, h_ref, dres_ref, c_ref, sa_ref, sb_ref, g_ref, w_ref, dh_ref, dqp_ref, hn_ref, dg_ref):
        @pl.when(pl.program_id(0) == 0)
        def _():
            dg_ref[...] = jnp.zeros_like(dg_ref)
        dqp = _rope_bwd(dq_ref[...], c_ref[...], sa_ref[...], sb_ref[...]).astype(BF16)
        x = h_ref[...]
        gv = g_ref[...]
        hn, r = _rms(x, gv)
        dhn = _dot(dqp, w_ref[...])
        dx, dg = _rms_bwd(x, gv, r, dhn)
        dh_ref[...] = dres_ref[...] + dx
        dqp_ref[...] = dqp
        hn_ref[...] = hn.astype(BF16)
        dg_ref[...] += dg

    return _rowcall(body, name, seqlen, tm, [dq, h, dres, *tabs], [_whole(g.reshape(1, d)), _whole(wq)],
                    [(d, F32), (d, BF16), (d, BF16)], [(1, d)], extra_vmem=tm * d * 4 * 6)


def _kv_fwd(h, g, wk, wv, tabs, name):
    seqlen, d = h.shape
    dk = wk.shape[1]
    tm = _tile(seqlen, 512)

    def body(h_ref, c_ref, sa_ref, sb_ref, g_ref, wk_ref, wv_ref, k_ref, v_ref):
        hk = _rms(h_ref[...], g_ref[...])[0].astype(BF16)
        k_ref[...] = _rope(_dot(hk, wk_ref[...]), c_ref[...], sa_ref[...], sb_ref[...]).astype(BF16)
        v_ref[...] = _dot(hk, wv_ref[...]).astype(BF16)

    return _rowcall(body, name, seqlen, tm, [h, *tabs], [_whole(g.reshape(1, d)), _whole(wk), _whole(wv)],
                    [(dk, BF16), (dk, BF16)], extra_vmem=tm * d * 4 * 4)


def _kv_bwd(dks, dvs, h, g, dres, wk, wv, tabs, name):
    seqlen, d = h.shape
    dkw = wk.shape[0]
    tm = _tile(seqlen, 512)

    def body(dk0_ref, dk1_ref, dv0_ref, dv1_ref, h_ref, dres_ref, c_ref, sa_ref, sb_ref, g_ref, wk_ref, wv_ref,
             dh_ref, dkp_ref, dvb_ref, hk_ref, dg_ref):
        @pl.when(pl.program_id(0) == 0)
        def _():
            dg_ref[...] = jnp.zeros_like(dg_ref)
        dkp = _rope_bwd(dk0_ref[...] + dk1_ref[...], c_ref[...], sa_ref[...], sb_ref[...]).astype(BF16)
        dvb = (dv0_ref[...] + dv1_ref[...]).astype(BF16)
        x = h_ref[...]
        gv = g_ref[...]
        hk, r = _rms(x, gv)
        dhk = _dot(dkp, wk_ref[...]) + _dot(dvb, wv_ref[...])
        dx, dg = _rms_bwd(x, gv, r, dhk)
        dh_ref[...] = dres_ref[...] + dx
        dkp_ref[...] = dkp
        dvb_ref[...] = dvb
        hk_ref[...] = hk.astype(BF16)
        dg_ref[...] += dg

    return _rowcall(body, name, seqlen, tm, [dks[0], dks[1], dvs[0], dvs[1], h, dres, *tabs],
                    [_whole(g.reshape(1, d)), _whole(wk), _whole(wv)],
                    [(d, F32), (dkw, BF16), (dkw, BF16), (d, BF16)], [(1, d)], extra_vmem=tm * d * 4 * 6)


def _lin_res(h, xb, w, name):
    seqlen, d = h.shape
    tm = _tile(seqlen, 512)

    def body(h_ref, x_ref, w_ref, o_ref):
        o_ref[...] = h_ref[...] + _dot(x_ref[...], w_ref[...])

    return _rowcall(body, name, seqlen, tm, [h, xb], [_whole(w)], [(d, F32)], extra_vmem=tm * d * 4 * 2)[0]


def _lin_bf16(dy, w, name):
    seqlen, d = dy.shape
    tm = _tile(seqlen, 512)

    def body(dy_ref, w_ref, o_ref):
        o_ref[...] = _dot(dy_ref[...].astype(BF16), w_ref[...]).astype(BF16)

    return _rowcall(body, name, seqlen, tm, [dy], [_whole(w)], [(w.shape[1], BF16)], extra_vmem=tm * d * 4 * 2)[0]


def _mlp_ple_fwd(h, p, g, wup, wdn, g_ple, wg, wpp, name, comm=None):
    seqlen, d = h.shape
    f = wup.shape[1]
    tm = _tile(seqlen, 512)

    def body(h_ref, p_ref, g_ref, wup_ref, wdn_ref, gp_ref, wg_ref, wpp_ref, hb_ref, act_ref, o_ref):
        x = h_ref[...]
        hm = _rms(x, g_ref[...])[0].astype(BF16)
        r = jnp.maximum(_dot(hm, wup_ref[...]), 0.0)
        act = (r * r).astype(BF16)
        act_ref[...] = act
        hb = x + _dot(act, wdn_ref[...])
        hb_ref[...] = hb
        n = _rms(hb, gp_ref[...])[0].astype(BF16)
        gate = jax.nn.sigmoid(_dot(n, wg_ref[...]))
        o_ref[...] = hb + gate * _dot(p_ref[...].astype(BF16), wpp_ref[...])

    consts = [_whole(g.reshape(1, d)), _whole(wup), _whole(wdn), _whole(g_ple.reshape(1, d)), _whole(wg), _whole(wpp)]
    return _rowcall(body, name, seqlen, tm, [h, p], consts, [(d, F32), (f, BF16), (d, F32)],
                    extra_vmem=tm * f * 4 * 3, comm=comm)


def _mlp_bwd(h, act, dh, g, wup_t, wdn_t, name, comm=None):
    seqlen, d = h.shape
    f = wup_t.shape[0]
    tm = _tile(seqlen, 512)

    def body(h_ref, act_ref, dh_ref, g_ref, wup_ref, wdn_ref, dhin_ref, hm_ref, da_ref, dg_ref):
        @pl.when(pl.program_id(0) == 0)
        def _():
            dg_ref[...] = jnp.zeros_like(dg_ref)
        x = h_ref[...]
        gv = g_ref[...]
        dy = dh_ref[...]
        hm, r = _rms(x, gv)
        rl2 = 2.0 * jnp.sqrt(act_ref[...].astype(F32))
        da = (_dot(dy.astype(BF16), wdn_ref[...]) * rl2).astype(BF16)
        da_ref[...] = da
        dx, dg = _rms_bwd(x, gv, r, _dot(da, wup_ref[...]))
        dhin_ref[...] = dy + dx
        hm_ref[...] = hm.astype(BF16)
        dg_ref[...] += dg

    consts = [_whole(g.reshape(1, d)), _whole(wup_t), _whole(wdn_t)]
    return _rowcall(body, name, seqlen, tm, [h, act, dh], consts, [(d, F32), (d, BF16), (f, BF16)], [(1, d)],
                    extra_vmem=tm * f * 4 * 3, comm=comm)


def _ple_bwd(h, p, dh, g, wg, wg_t, wpp, name):
    seqlen, d = h.shape
    tm = _tile(seqlen, 512)

    def body(h_ref, p_ref, dh_ref, g_ref, wg_ref, wgt_ref, wpp_ref, dhin_ref, dz_ref, n_ref, dpp_ref, dg_ref):
        @pl.when(pl.program_id(0) == 0)
        def _():
            dg_ref[...] = jnp.zeros_like(dg_ref)
        x = h_ref[...]
        gv = g_ref[...]
        dy = dh_ref[...]
        n, r = _rms(x, gv)
        nb16 = n.astype(BF16)
        gate = jax.nn.sigmoid(_dot(nb16, wg_ref[...]))
        pp = _dot(p_ref[...].astype(BF16), wpp_ref[...])
        dz = (dy * pp * gate * (1.0 - gate)).astype(BF16)
        dn = _dot(dz, wgt_ref[...])
        dx, dg = _rms_bwd(x, gv, r, dn)
        dhin_ref[...] = dy + dx
        dz_ref[...] = dz
        n_ref[...] = nb16
        dpp_ref[...] = (dy * gate).astype(BF16)
        dg_ref[...] += dg

    return _rowcall(body, name, seqlen, tm, [h, p, dh],
                    [_whole(g.reshape(1, d)), _whole(wg), _whole(wg_t), _whole(wpp)],
                    [(d, F32), (d, BF16), (d, BF16), (d, BF16)], [(1, d)], extra_vmem=tm * d * 4 * 8)


def _loss_bwd(h, g, tgt, name):
    seqlen, d = h.shape
    tm = _tile(seqlen, 512)

    def body(h_ref, t_ref, g_ref, dh_ref, loss_ref, dg_ref):
        @pl.when(pl.program_id(0) == 0)
        def _():
            dg_ref[...] = jnp.zeros_like(dg_ref)
            loss_ref[...] = jnp.zeros_like(loss_ref)
        x = h_ref[...]
        gv = g_ref[...]
        y, r = _rms(x, gv)
        diff = y - t_ref[...]
        loss_ref[...] += (0.5 / d) * jnp.sum(jnp.sum(diff * diff, axis=1, keepdims=True), axis=0, keepdims=True)
        dx, dg = _rms_bwd(x, gv, r, diff * (1.0 / d))
        dh_ref[...] = dx
        dg_ref[...] += dg

    return _rowcall(body, name, seqlen, tm, [h, tgt], [_whole(g.reshape(1, d))], [(d, F32)], [(1, 128), (1, d)])


def _atb(a, b, col_blocked, name):
    seqlen, k1 = a.shape
    k2 = b.shape[1]
    if col_blocked:
        cs = k2 // NDEV
        t1 = _tile(k1, 512)
        nblk = _tile(NDEV, max(1, 2048 // cs))
        t2 = nblk * cs
        oshape = (NDEV, k1, cs)
        oblock = (nblk, t1, cs)
        omap = lambda i, j, l: (j, i, 0)
    else:
        rs = k1 // NDEV
        t2 = _tile(k2, 2048)
        nblk = _tile(NDEV, max(1, 1024 // rs))
        t1 = nblk * rs
        oshape = (NDEV, rs, k2)
        oblock = (nblk, rs, t2)
        omap = lambda i, j, l: (i, 0, j)
    tl = _tile(seqlen, 2048 if b.dtype == BF16 else 1024)

    def body(a_ref, b_ref, o_ref):
        @pl.when(pl.program_id(2) == 0)
        def _():
            o_ref[...] = jnp.zeros_like(o_ref)
        res = _dot(a_ref[...].astype(BF16), b_ref[...].astype(BF16), TN)
        for n in range(nblk):
            if col_blocked:
                o_ref[n] += res[:, n * cs:(n + 1) * cs]
            else:
                o_ref[n] += res[n * rs:(n + 1) * rs, :]

    blocks = [((tl, t1), a.dtype, 2), ((tl, t2), b.dtype, 2), ((t1, t2), F32, 2)]
    return _pcall(
        body, name=name, out_shape=SDS(oshape, F32), grid=(k1 // t1, k2 // t2, seqlen // tl),
        in_specs=[pl.BlockSpec((tl, t1), lambda i, j, l: (l, i)), pl.BlockSpec((tl, t2), lambda i, j, l: (l, j))],
        out_specs=pl.BlockSpec(oblock, omap),
        vmem=_vmem_limit(blocks, extra=t1 * t2 * 4 + tl * (t1 + t2) * 2))(a, b)


_ATTN_SCALE = HEAD_DIM ** -0.5


def _attn_bias():
    qi = lax.broadcasted_iota(jnp.int32, (ATTN_BLOCK, 2 * ATTN_BLOCK), 0) + ATTN_BLOCK
    kj = lax.broadcasted_iota(jnp.int32, (ATTN_BLOCK, 2 * ATTN_BLOCK), 1)
    band = (kj <= qi) & (qi - kj < ATTN_BLOCK)
    return jnp.where(jnp.stack([band & (kj >= ATTN_BLOCK), band]), 0.0, NEG_INF).astype(F32)


def _bias_spec():
    return pl.BlockSpec((None, ATTN_BLOCK, 2 * ATTN_BLOCK), lambda n: (jnp.minimum(n, 1), 0, 0))


def _attn_probs(q4s, kks, sink_col, bias):
    s = jnp.concatenate([_dot(q4, kk, NT) for q4, kk in zip(q4s, kks)], axis=0)
    rows = s.shape[0]
    s = (s.reshape(rows // ATTN_BLOCK, ATTN_BLOCK, 2 * ATTN_BLOCK) + bias).reshape(rows, 2 * ATTN_BLOCK)
    m = jnp.maximum(jnp.max(s, axis=1, keepdims=True), sink_col)
    pr = jnp.exp(s - m)
    es = jnp.exp(sink_col - m)
    inv = 1.0 / (jnp.sum(pr, axis=1, keepdims=True) + es)
    return pr * inv, es * inv


def _sink_col(sink_ref, nheads):
    return jnp.concatenate([jnp.full((ATTN_BLOCK, 1), sink_ref[hq], F32) for hq in range(nheads)], axis=0)


def _kv_pair(p_ref, c_ref, kh):
    sl = slice(kh * HEAD_DIM, (kh + 1) * HEAD_DIM)
    return jnp.concatenate([p_ref[:, sl], c_ref[:, sl]], axis=0)


def _stack_heads(ref, kh, scale=None):
    x = jnp.concatenate(
        [ref[:, (kh * GQA_GROUP + g) * HEAD_DIM:(kh * GQA_GROUP + g + 1) * HEAD_DIM] for g in range(GQA_GROUP)], axis=0)
    return x if scale is None else x * scale


def _attn_fwd(q, k, v, sinks, name):
    seqlen, d = q.shape
    dkv = k.shape[1]
    nkv = dkv // HEAD_DIM
    nb = seqlen // ATTN_BLOCK
    blk = ATTN_BLOCK

    def body(sink_ref, bias_ref, q_ref, kc_ref, kp_ref, vc_ref, vp_ref, o_ref):
        q4s = [_stack_heads(q_ref, kh, _ATTN_SCALE) for kh in range(nkv)]
        kks = [_kv_pair(kp_ref, kc_ref, kh) for kh in range(nkv)]
        w, _ = _attn_probs(q4s, kks, _sink_col(sink_ref, nkv * GQA_GROUP), bias_ref[...])
        wb = w.astype(BF16)
        for kh in range(nkv):
            o4 = _dot(wb[kh * GQA_GROUP * blk:(kh + 1) * GQA_GROUP * blk, :], _kv_pair(vp_ref, vc_ref, kh))
            for g in range(GQA_GROUP):
                hq = kh * GQA_GROUP + g
                o_ref[:, hq * HEAD_DIM:(hq + 1) * HEAD_DIM] = o4[g * blk:(g + 1) * blk, :].astype(BF16)

    cur = lambda n: (n, 0)
    prev = lambda n: (jnp.maximum(n - 1, 0), 0)
    return _pcall(
        body, name=name, out_shape=SDS((seqlen, d), BF16), grid=(nb,),
        in_specs=[pl.BlockSpec(memory_space=pltpu.SMEM), _bias_spec(), pl.BlockSpec((blk, d), cur),
                  pl.BlockSpec((blk, dkv), cur), pl.BlockSpec((blk, dkv), prev),
                  pl.BlockSpec((blk, dkv), cur), pl.BlockSpec((blk, dkv), prev)],
        out_specs=pl.BlockSpec((blk, d), cur), vmem=32 << 20)(sinks, _attn_bias(), q, k, k, v, v)


def _attn_bwd(q, k, v, do, sinks, name, comm=None):
    seqlen, d = q.shape
    dkv = k.shape[1]
    nkv = dkv // HEAD_DIM
    nh = d // HEAD_DIM
    nb = seqlen // ATTN_BLOCK
    blk = ATTN_BLOCK

    def body(sink_ref, bias_ref, q_ref, kc_ref, kp_ref, vc_ref, vp_ref, do_ref, dq_ref, dk_ref, dv_ref, ds_ref,
             ck_ref, cv_ref):
        n = pl.program_id(0)

        @pl.when(n == 0)
        def _():
            ck_ref[...] = jnp.zeros_like(ck_ref)
            cv_ref[...] = jnp.zeros_like(cv_ref)
            ds_ref[...] = jnp.zeros_like(ds_ref)

        @pl.when(n == nb)
        def _():
            dk_ref[...] = ck_ref[...]
            dv_ref[...] = cv_ref[...]

        @pl.when(n < nb)
        def _():
            q4s = [_stack_heads(q_ref, kh, _ATTN_SCALE) for kh in range(nkv)]
            do4s = [_stack_heads(do_ref, kh) for kh in range(nkv)]
            kks = [_kv_pair(kp_ref, kc_ref, kh) for kh in range(nkv)]
            w, wsink = _attn_probs(q4s, kks, _sink_col(sink_ref, nh), bias_ref[...])
            dw = jnp.concatenate([_dot(do4s[kh], _kv_pair(vp_ref, vc_ref, kh), NT) for kh in range(nkv)], axis=0)
            dsum = jnp.sum(w * dw, axis=1, keepdims=True)
            ds_all = (w * (dw - dsum)).astype(BF16)
            wb = w.astype(BF16)
            dsk = -wsink * dsum
            for kh in range(nkv):
                sl = slice(kh * HEAD_DIM, (kh + 1) * HEAD_DIM)
                rows = slice(kh * GQA_GROUP * blk, (kh + 1) * GQA_GROUP * blk)
                ds = ds_all[rows, :]
                dq4 = _dot(ds, kks[kh]) * _ATTN_SCALE
                dkk = _dot(ds, q4s[kh], TN)
                dvv = _dot(wb[rows, :], do4s[kh], TN)
                for g in range(GQA_GROUP):
                    hq = kh * GQA_GROUP + g
                    dq_ref[:, hq * HEAD_DIM:(hq + 1) * HEAD_DIM] = dq4[g * blk:(g + 1) * blk, :]
                    ds_ref[hq:hq + 1, :] += jnp.sum(dsk[hq * blk:(hq + 1) * blk, :], axis=0, keepdims=True)
                dk_ref[:, sl] = ck_ref[:, sl] + dkk[:blk, :]
                dv_ref[:, sl] = cv_ref[:, sl] + dvv[:blk, :]
                ck_ref[:, sl] = dkk[blk:, :]
                cv_ref[:, sl] = dvv[blk:, :]

    cur = lambda n: (jnp.minimum(n, nb - 1), 0)
    prev = lambda n: (jnp.clip(n - 1, 0, nb - 1), 0)
    lag = lambda n: (jnp.maximum(n - 1, 0), 0)
    return _pcall(
        body, name=name,
        out_shape=[SDS((seqlen, d), F32), SDS((seqlen, dkv), F32), SDS((seqlen, dkv), F32), SDS((nh, 128), F32)],
        grid=(nb + 1,),
        in_specs=[pl.BlockSpec(memory_space=pltpu.SMEM), _bias_spec(), pl.BlockSpec((blk, d), cur),
                  pl.BlockSpec((blk, dkv), cur), pl.BlockSpec((blk, dkv), prev),
                  pl.BlockSpec((blk, dkv), cur), pl.BlockSpec((blk, dkv), prev), pl.BlockSpec((blk, d), cur)],
        out_specs=[pl.BlockSpec((blk, d), cur), pl.BlockSpec((blk, dkv), lag), pl.BlockSpec((blk, dkv), lag),
                   pl.BlockSpec((nh, 128), lambda n: (0, 0))],
        scratch=[pltpu.VMEM((blk, dkv), F32)] * 2, vmem=32 << 20, comm=comm)(sinks, _attn_bias(), q, k, k, v, v, do)


def _ssm_mats(lre, lim, ldt, btr, bti, cr, ci):
    dt = jnp.exp(ldt)
    mag = jnp.exp(lre * dt)
    ar = mag * jnp.cos(lim * dt)
    ai = mag * jnp.sin(lim * dt)
    den = lre * lre + lim * lim
    nr = ar - 1.0
    cfr = (nr * lre + ai * lim) / den
    cfi = (ai * lre - nr * lim) / den
    bbr = cfr * btr - cfi * bti
    bbi = cfr * bti + cfi * btr
    pr = [jnp.ones_like(ar)]
    pi = [jnp.zeros_like(ai)]
    for _ in range(SSM_T):
        pr.append(pr[-1] * ar - pi[-1] * ai)
        pi.append(pr[-2] * ai + pi[-1] * ar)
    last = SSM_T - 1
    p_re = jnp.concatenate([pr[last - s] * bbr - pi[last - s] * bbi for s in range(SSM_T)], axis=0)
    p_im = jnp.concatenate([pr[last - s] * bbi + pi[last - s] * bbr for s in range(SSM_T)], axis=0)
    qt_re = jnp.concatenate([pr[t + 1] * cr - pi[t + 1] * ci for t in range(SSM_T)], axis=0)
    qt_im = jnp.concatenate([-(pr[t + 1] * ci + pi[t + 1] * cr) for t in range(SSM_T)], axis=0)
    ctr = jnp.concatenate([cr] * SSM_T, axis=0)
    cti = jnp.concatenate([ci] * SSM_T, axis=0)
    lag = (lax.broadcasted_iota(jnp.int32, (SSM_W, SSM_W), 1) // SSM_GROUP
           - lax.broadcasted_iota(jnp.int32, (SSM_W, SSM_W), 0) // SSM_GROUP)
    m = jnp.zeros((SSM_W, SSM_W), F32)
    for l in range(SSM_T):
        zr = jnp.concatenate([pr[l] * bbr - pi[l] * bbi] * SSM_T, axis=0)
        zi = jnp.concatenate([pr[l] * bbi + pi[l] * bbr] * SSM_T, axis=0)
        kl = _dot3_nt(zr, ctr) - _dot3_nt(zi, cti)
        m = m + jnp.where(lag == l, kl, 0.0)
    return m, p_re, p_im, qt_re, qt_im, pr[SSM_T], pi[SSM_T]


_SSM_GB = 8


def _ssm_param_specs(ng):
    n, hh = SSM_STATE, SSM_GROUP
    gb = _tile(ng, _SSM_GB)
    row = pl.BlockSpec((gb, 1, n), lambda i: (i, 0, 0))
    one = pl.BlockSpec((gb, 1, 1), lambda i: (i, 0, 0))
    mat = pl.BlockSpec((gb, hh, n), lambda i: (i, 0, 0))
    big = pl.BlockSpec((gb, SSM_W, SSM_W), lambda i: (i, 0, 0))
    half = pl.BlockSpec((gb, SSM_W, n), lambda i: (i, 0, 0))
    return gb, row, one, mat, big, half


def _ssm_prep(params, name):
    ng = params[0].shape[0]
    n = SSM_STATE
    gb, row, one, mat, big, half = _ssm_param_specs(ng)

    def body(lre, lim, ldt, btr, bti, cr, ci, m_ref, pre_ref, pim_ref, qre_ref, qim_ref, atr_ref, ati_ref):
        for gi in range(gb):
            outs = _ssm_mats(lre[gi], lim[gi], ldt[gi], btr[gi], bti[gi], cr[gi], ci[gi])
            for ref, val in zip((m_ref, pre_ref, pim_ref, qre_ref, qim_ref, atr_ref, ati_ref), outs):
                ref[gi] = val

    return _pcall(
        body, name=name,
        out_shape=[SDS((ng, SSM_W, SSM_W), F32)] + [SDS((ng, SSM_W, n), F32)] * 4 + [SDS((ng, 1, n), F32)] * 2,
        grid=(ng // gb,), in_specs=[row, row, one, mat, mat, mat, mat],
        out_specs=[big, half, half, half, half, row, row], vmem=40 << 20)(*params)


def _ssm_prep_vjp(params, cots, name):
    ng = params[0].shape[0]
    n, hh = SSM_STATE, SSM_GROUP
    gb, row, one, mat, big, half = _ssm_param_specs(ng)

    def body(lre, lim, ldt, btr, bti, cr, ci, dm, dpre, dpim, dqre, dqim, datr, dati,
             o_lre, o_lim, o_ldt, o_btr, o_bti, o_cr, o_ci):
        for gi in range(gb):
            prm = (lre[gi], lim[gi], ldt[gi], btr[gi], bti[gi], cr[gi], ci[gi])
            _, pull = jax.vjp(_ssm_mats, *prm)
            grads = pull((dm[gi], dpre[gi], dpim[gi], dqre[gi], dqim[gi], datr[gi], dati[gi]))
            for ref, val in zip((o_lre, o_lim, o_ldt, o_btr, o_bti, o_cr, o_ci), grads):
                ref[gi] = val

    return _pcall(
        body, name=name,
        out_shape=[SDS((ng, 1, n), F32)] * 2 + [SDS((ng, 1, 1), F32)] + [SDS((ng, hh, n), F32)] * 4,
        grid=(ng // gb,), in_specs=[row, row, one, mat, mat, mat, mat, big, half, half, half, half, row, row],
        out_specs=[row, row, one, mat, mat, mat, mat], vmem=48 << 20)(*params, *cots)


_SSM_GT = SSM_W // SSM_GROUP


def _blk_transpose(xs):
    assert len(xs) == SSM_T == _SSM_GT
    blk = lax.broadcasted_iota(jnp.int32, xs[0].shape, 1) // SSM_GROUP
    xs = list(xs)
    k = SSM_T // 2
    while k:
        high = (blk // k) % 2 == 1
        nxt = []
        for i in range(SSM_T):
            if i & k:
                nxt.append(jnp.where(high, xs[i], pltpu.roll(xs[i ^ k], SSM_W - SSM_GROUP * k, 1)))
            else:
                nxt.append(jnp.where(high, pltpu.roll(xs[i ^ k], SSM_GROUP * k, 1), xs[i]))
        xs = nxt
        k //= 2
    return xs


def _tile_groups(x_ref, ncb):
    return _blk_transpose([x_ref[pl.ds(t, ncb, stride=SSM_T), :] for t in range(SSM_T)])


def _groups_tile(ys, o_ref, ncb):
    for t, y in enumerate(_blk_transpose(ys)):
        o_ref[pl.ds(t, ncb, stride=SSM_T), :] = y


def _ssm_specs(seqlen, d):
    ncb = _tile(seqlen // SSM_T, 512)
    grid = (d // SSM_W, seqlen // (SSM_T * ncb))
    act = pl.BlockSpec((SSM_T * ncb, SSM_W), lambda j, r: (r, j))
    state = pl.BlockSpec((ncb, _SSM_GT * SSM_W), lambda j, r: (r, j))
    mats = pl.BlockSpec((_SSM_GT, SSM_W, SSM_W), lambda j, r: (j, 0, 0))
    return ncb, grid, act, state, mats


def _gsl(gl):
    return slice(gl * SSM_W, (gl + 1) * SSM_W)


def _ssm_state_in(hn, pmat, name):
    seqlen, d = hn.shape
    ncb, grid, act, state, mats = _ssm_specs(seqlen, d)

    def body(x_ref, p_ref, s_ref):
        us = _tile_groups(x_ref, ncb)
        for gl in range(_SSM_GT):
            s_ref[:, _gsl(gl)] = _dot3(us[gl], p_ref[gl], NN)

    return _pcall(body, name=name, out_shape=SDS((seqlen // SSM_T, d * SSM_T), F32), grid=grid,
                  in_specs=[act, mats], out_specs=state, vmem=40 << 20)(hn, pmat)


def _ssm_out(hn, xp, mmat, qt, name, comm=None):
    seqlen, d = hn.shape
    ncb, grid, act, state, mats = _ssm_specs(seqlen, d)

    def body(x_ref, xp_ref, m_ref, q_ref, y_ref):
        us = _tile_groups(x_ref, ncb)
        ys = [_dot3(us[gl], m_ref[gl], NN) + _dot3(xp_ref[:, _gsl(gl)], q_ref[gl], NT)
              for gl in range(_SSM_GT)]
        _groups_tile(ys, y_ref, ncb)

    return _pcall(body, name=name, out_shape=[SDS((seqlen, d), F32)], grid=grid,
                  in_specs=[act, state, mats, mats], out_specs=[act], vmem=40 << 20, comm=comm)(hn, xp, mmat, qt)


def _ssm_dstate(dy, qt, name):
    seqlen, d = dy.shape
    ncb, grid, act, state, mats = _ssm_specs(seqlen, d)

    def body(dy_ref, q_ref, o_ref):
        dys = _tile_groups(dy_ref, ncb)
        for gl in range(_SSM_GT):
            o_ref[:, _gsl(gl)] = _dot3(dys[gl], q_ref[gl], NN)

    return _pcall(body, name=name, out_shape=SDS((seqlen // SSM_T, d * SSM_T), F32), grid=grid,
                  in_specs=[act, mats], out_specs=state, vmem=40 << 20)(dy, qt)


def _ssm_bwd(hn, dy, xp, gs, mmat, pmat, name, comm=None):
    seqlen, d = hn.shape
    ng = d // SSM_GROUP
    ncb, grid, act, state, mats = _ssm_specs(seqlen, d)

    def body(x_ref, dy_ref, xp_ref, g_ref, m_ref, p_ref, du_ref, dm_ref, dp_ref, dq_ref, da_ref):
        @pl.when(pl.program_id(1) == 0)
        def _():
            for ref in (dm_ref, dp_ref, dq_ref, da_ref):
                ref[...] = jnp.zeros_like(ref)
        us = _tile_groups(x_ref, ncb)
        dys = _tile_groups(dy_ref, ncb)
        dus = []
        for gl in range(_SSM_GT):
            xv, gv = xp_ref[:, _gsl(gl)], g_ref[:, _gsl(gl)]
            u2, dy2, x2, g2 = _split(us[gl]), _split(dys[gl]), _split(xv), _split(gv)
            dus.append(_dot3(dy2, m_ref[gl], NT) + _dot3(g2, p_ref[gl], NT))
            dm_ref[gl] += _dot3(u2, dy2, TN)
            dp_ref[gl] += _dot3(u2, g2, TN)
            dq_ref[gl] += _dot3(dy2, x2, TN)
            da_ref[gl, 0:1, :] += jnp.sum(xv * gv, axis=0, keepdims=True)
            da_ref[gl, 1:2, :] += jnp.sum(xv * pltpu.roll(gv, SSM_STATE, 1), axis=0, keepdims=True)
        _groups_tile(dus, du_ref, ncb)

    return _pcall(
        body, name=name,
        out_shape=[SDS((seqlen, d), F32)] + [SDS((ng, SSM_W, SSM_W), F32)] * 3 + [SDS((ng, 2, SSM_W), F32)],
        grid=grid, in_specs=[act, act, state, state, mats, mats],
        out_specs=[act, mats, mats, mats, pl.BlockSpec((_SSM_GT, 2, SSM_W), lambda j, r: (j, 0, 0))],
        vmem=48 << 20, comm=comm)(hn, dy, xp, gs, mmat, pmat)


def _ssm_carry(s, a1, a2, reverse, name, comm=None):
    nc, w = s.shape
    tc = _tile(nc, 256)
    nblk = nc // tc
    sub = 8
    shape = (sub, SSM_W)

    def body(s_ref, a1_ref, a2_ref, o_ref, st_ref, pw_ref, p1_ref, p2_ref):
        rows = lax.broadcasted_iota(jnp.int32, (sub, w), 0)
        sign = jnp.where(lax.broadcasted_iota(jnp.int32, (1, w), 1) % SSM_W < SSM_STATE, -1.0, 1.0)

        @pl.when(pl.program_id(0) == 0)
        def _():
            st_ref[...] = jnp.zeros_like(st_ref)
            cr, ci = a1_ref[...], a2_ref[...] * sign
            qr, qi = jnp.ones_like(cr), jnp.zeros_like(ci)
            p1, p2 = jnp.zeros((sub, w), F32), jnp.zeros((sub, w), F32)
            for r in range(sub):
                at = (sub - 1 - r) if reverse else r
                p1 = jnp.where(rows == at, qr, p1)
                p2 = jnp.where(rows == at, qi * sign, p2)
                qr, qi = qr * cr - qi * ci, qr * ci + qi * cr
            p1_ref[...] = p1
            p2_ref[...] = p2
            for k in range(4):
                pw_ref[2 * k:2 * k + 1, :] = cr
                pw_ref[2 * k + 1:2 * k + 2, :] = ci * sign
                cr, ci = cr * cr - ci * ci, 2.0 * cr * ci

        row = lax.broadcasted_iota(jnp.int32, shape, 0)

        def cmul(c1, c2, v):
            return c1 * v + c2 * pltpu.roll(v, SSM_STATE, 1)

        def shifted(v, k):
            if reverse:
                return jnp.where(row < sub - k, pltpu.roll(v, sub - k, 0), 0.0)
            return jnp.where(row >= k, pltpu.roll(v, k, 0), 0.0)

        def step(t, carry):
            tt = (tc // sub - 1 - t) if reverse else t
            base = pl.multiple_of(tt * sub, sub)
            for g in range(w // SSM_W):
                lanes = slice(g * SSM_W, (g + 1) * SSM_W)
                pw = [jnp.broadcast_to(pw_ref[i:i + 1, lanes], shape) for i in range(8)]
                y = s_ref[pl.ds(base, sub), lanes]
                for k in range(3):
                    y = y + cmul(pw[2 * k], pw[2 * k + 1], shifted(y, 1 << k))
                x = st_ref[:, lanes]
                o_ref[pl.ds(base, sub), lanes] = shifted(y, 1) + cmul(p1_ref[:, lanes], p2_ref[:, lanes], x)
                end = y[0:1, :] if reverse else y[sub - 1:sub, :]
                st_ref[:, lanes] = jnp.broadcast_to(end, shape) + cmul(pw[6], pw[7], x)
            return carry

        lax.fori_loop(0, tc // sub, step, 0)

    imap = (lambda i: (nblk - 1 - i, 0)) if reverse else (lambda i: (i, 0))
    cst = pl.BlockSpec((1, w), lambda i: (0, 0))
    return _pcall(body, name=name, out_shape=[SDS((nc, w), F32)], grid=(nblk,),
                  in_specs=[pl.BlockSpec((tc, w), imap), cst, cst], out_specs=[pl.BlockSpec((tc, w), imap)],
                  scratch=[pltpu.VMEM((sub, w), F32)] * 4,
                  vmem=_vmem_limit([((tc, w), F32, 4)], extra=8 << 20), comm=comm)(s, a1, a2)


def _ssm_rows(atr, ati, conj):
    ng = atr.shape[0]
    ai = -ati if conj else ati
    a1 = jnp.concatenate([atr, atr], axis=2).reshape(1, ng * SSM_W)
    a2 = jnp.concatenate([-ai, ai], axis=2).reshape(1, ng * SSM_W)
    return a1, a2


def _peers():
    x, y, c = (lax.axis_index(a) for a in AXES)
    me = 4 * x + 2 * y + c
    peers = []
    for dx, dy, dc in [(0, 0, 1), (0, 1, 0), (0, 1, 1), (1, 0, 0), (1, 0, 1), (1, 1, 0), (1, 1, 1)]:
        px, py, pc = (1 - x) if dx else x, (1 - y) if dy else y, (1 - c) if dc else c
        peers.append(((px, py, pc), 4 * px + 2 * py + pc))
    return me, peers


class _Exchange:
    def __init__(self, arrs, scatter, layers=None):
        self.arrs = list(arrs)
        self.scatter = scatter
        self.layers = list(layers) if layers is not None else [None] * len(self.arrs)

    def out_shape(self):
        shapes = []
        for arr, layer in zip(self.arrs, self.layers):
            block = arr.shape[1:] if (self.scatter or layer is not None) else arr.shape
            shapes.append(SDS((NDEV,) + tuple(block), arr.dtype))
        return shapes

    def semaphores(self):
        n = len(self.arrs)
        return [pltpu.SemaphoreType.DMA((n * (NDEV - 1),)), pltpu.SemaphoreType.DMA((n * (NDEV - 1),)),
                pltpu.SemaphoreType.DMA((n,))]

    def _src(self, ref, a, block):
        if self.scatter:
            return ref.at[block]
        return ref if self.layers[a] is None else ref.at[self.layers[a]]

    def _remote(self, xin, xout, sems, a, k, peer, landing):
        pid, pidx = peer
        slot = a * (NDEV - 1) + k
        return pltpu.make_async_remote_copy(
            src_ref=self._src(xin[a], a, pidx), dst_ref=xout[a].at[landing],
            send_sem=sems[0].at[slot], recv_sem=sems[1].at[slot], device_id=pid, device_id_type=MESH)

    def _local(self, xin, xout, sems, a, me):
        return pltpu.make_async_copy(self._src(xin[a], a, me), xout[a].at[me], sems[2].at[a])

    def start(self, xin, xout, sems):
        me, peers = _peers()
        for a in range(len(self.arrs)):
            self._local(xin, xout, sems, a, me).start()
        for k, peer in enumerate(peers):
            for a in range(len(self.arrs)):
                self._remote(xin, xout, sems, a, k, peer, me).start()

    def wait(self, xin, xout, sems):
        me, peers = _peers()
        for a in range(len(self.arrs)):
            self._local(xin, xout, sems, a, me).wait()
        for k, peer in enumerate(peers):
            for a in range(len(self.arrs)):
                cp = self._remote(xin, xout, sems, a, k, peer, peer[1])
                cp.wait_send()
                cp.wait_recv()


def _exchange(arrs, scatter, name, layers=None):
    comm = _Exchange(arrs, scatter, layers)
    n = len(comm.arrs)

    def body(*refs):
        xin, xout, sems = refs[:n], refs[n:2 * n], refs[2 * n:]
        comm.start(xin, xout, sems)
        comm.wait(xin, xout, sems)

    hbm = pl.BlockSpec(memory_space=pl.ANY)
    return pl.pallas_call(
        body, out_shape=comm.out_shape(), in_specs=[hbm] * n, out_specs=[hbm] * n,
        scratch_shapes=comm.semaphores(), name=name, interpret=False)(*comm.arrs)


def _adamw(parts, w, m, v, name):
    rows, cols = w.shape
    tr = _tile(rows, max(8, (1 << 17) // cols))
    c1 = 1.0 - ADAM_B1 ** ADAM_STEP
    c2 = 1.0 - ADAM_B2 ** ADAM_STEP

    def body(p_ref, w_ref, m_ref, v_ref, g_ref, d_ref, nm_ref, nv_ref):
        g = p_ref[0]
        for j in range(1, NDEV):
            g = g + p_ref[j]
        mm = ADAM_B1 * m_ref[...] + (1.0 - ADAM_B1) * g
        vv = ADAM_B2 * v_ref[...] + (1.0 - ADAM_B2) * (g * g)
        g_ref[...] = g
        nm_ref[...] = mm
        nv_ref[...] = vv
        d_ref[...] = -ADAM_LR * ((mm / c1) / (jnp.sqrt(vv / c2) + ADAM_EPS) + ADAM_WD * w_ref[...])

    spec = pl.BlockSpec((tr, cols), lambda i: (i, 0))
    return _pcall(
        body, name=name, out_shape=[SDS((rows, cols), F32)] * 4, grid=(rows // tr,),
        in_specs=[pl.BlockSpec((NDEV, tr, cols), lambda i: (0, i, 0)), spec, spec, spec], out_specs=[spec] * 4,
        vmem=_vmem_limit([((NDEV + 7, tr, cols), F32, 2)]))(parts, w, m, v)


def kernel(x, p, norm_mix, ssm_lambda_re, ssm_lambda_im, ssm_log_dt, ssm_b_re, ssm_b_im, ssm_c_re, ssm_c_im, ssm_d, ssm_w_glu, kv_norm, w_k, w_v, w_q, attn_sinks, w_o, norm_mlp, w_up, w_down, norm_ple, w_ple_gate, w_ple_proj, norm_final, loss_target, m_norm_mix, m_ssm_lambda_re, m_ssm_lambda_im, m_ssm_log_dt, m_ssm_b_re, m_ssm_b_im, m_ssm_c_re, m_ssm_c_im, m_ssm_d, m_ssm_w_glu, m_kv_norm, m_w_k, m_w_v, m_w_q, m_attn_sinks, m_w_o, m_norm_mlp, m_w_up, m_w_down, m_norm_ple, m_w_ple_gate, m_w_ple_proj, m_norm_final, v_norm_mix, v_ssm_lambda_re, v_ssm_lambda_im, v_ssm_log_dt, v_ssm_b_re, v_ssm_b_im, v_ssm_c_re, v_ssm_c_im, v_ssm_d, v_ssm_w_glu, v_kv_norm, v_w_k, v_w_v, v_w_q, v_attn_sinks, v_w_o, v_norm_mlp, v_w_up, v_w_down, v_norm_ple, v_w_ple_gate, v_w_ple_proj, v_norm_final):
    names = ['norm_mix', 'ssm_lambda_re', 'ssm_lambda_im', 'ssm_log_dt', 'ssm_b_re', 'ssm_b_im', 'ssm_c_re',
             'ssm_c_im', 'ssm_d', 'ssm_w_glu', 'kv_norm', 'w_k', 'w_v', 'w_q', 'attn_sinks', 'w_o', 'norm_mlp',
             'w_up', 'w_down', 'norm_ple', 'w_ple_gate', 'w_ple_proj', 'norm_final']
    weights = dict(zip(names, (norm_mix, ssm_lambda_re, ssm_lambda_im, ssm_log_dt, ssm_b_re, ssm_b_im, ssm_c_re,
                               ssm_c_im, ssm_d, ssm_w_glu, kv_norm, w_k, w_v, w_q, attn_sinks, w_o, norm_mlp,
                               w_up, w_down, norm_ple, w_ple_gate, w_ple_proj, norm_final)))
    mom1 = dict(zip(names, (m_norm_mix, m_ssm_lambda_re, m_ssm_lambda_im, m_ssm_log_dt, m_ssm_b_re, m_ssm_b_im,
                            m_ssm_c_re, m_ssm_c_im, m_ssm_d, m_ssm_w_glu, m_kv_norm, m_w_k, m_w_v, m_w_q,
                            m_attn_sinks, m_w_o, m_norm_mlp, m_w_up, m_w_down, m_norm_ple, m_w_ple_gate,
                            m_w_ple_proj, m_norm_final)))
    mom2 = dict(zip(names, (v_norm_mix, v_ssm_lambda_re, v_ssm_lambda_im, v_ssm_log_dt, v_ssm_b_re, v_ssm_b_im,
                            v_ssm_c_re, v_ssm_c_im, v_ssm_d, v_ssm_w_glu, v_kv_norm, v_w_k, v_w_v, v_w_q,
                            v_attn_sinks, v_w_o, v_norm_mlp, v_w_up, v_w_down, v_norm_ple, v_w_ple_gate,
                            v_w_ple_proj, v_norm_final)))

    seqlen, d = x.shape[1], x.shape[2]
    depth = w_up.shape[0]
    n_ssm = ssm_w_glu.shape[0]
    n_att = w_q.shape[0]
    ng = d // SSM_GROUP
    nh = d // HEAD_DIM
    h0 = x[0]
    tgt = loss_target[0]
    tabs = _rope_tables(seqlen)

    sharded = ['w_up', 'w_down', 'w_ple_gate', 'w_ple_proj', 'ssm_w_glu', 'w_q', 'w_o', 'w_k', 'w_v']
    shards = {k: weights[k].astype(BF16) for k in sharded}
    shards['ssm_d'] = ssm_d
    dkv = w_k.shape[1]

    def layer_set(i):
        keys = [('w_up', i), ('w_down', i), ('w_ple_gate', i), ('w_ple_proj', i)]
        keys += [('ssm_w_glu', i), ('ssm_d', i)] if i < n_ssm else [('w_q', i - n_ssm), ('w_o', i - n_ssm)]
        if i == n_ssm:
            keys += [('w_k', None), ('w_v', None)]
        return keys

    def gather_of(i, only=None):
        keys = [kl for kl in layer_set(i) if only is None or kl[0] in only]
        return keys, _Exchange([shards[k] for k, _ in keys], False, [l for _, l in keys])

    def as_operands(keys, blocks):
        w = {}
        for (k, _), g in zip(keys, blocks):
            if k == 'ssm_d':
                w[k] = g.reshape(d)
            elif k in ('w_ple_proj', 'ssm_w_glu', 'w_up'):
                w[k] = g.transpose(1, 0, 2).reshape(g.shape[1], NDEV * g.shape[2])
                w[k + '_t'] = g.transpose(0, 2, 1).reshape(NDEV * g.shape[2], g.shape[1])
            else:
                w[k] = g.reshape(NDEV * g.shape[1], g.shape[2])
                w[k + '_t'] = g.transpose(2, 0, 1).reshape(g.shape[2], NDEV * g.shape[1])
        return w

    lw = {}

    def ssm_params(i):
        n = SSM_STATE
        return (ssm_lambda_re[i].reshape(ng, 1, n), ssm_lambda_im[i].reshape(ng, 1, n),
                ssm_log_dt[i].reshape(ng, 1, 1), jnp.swapaxes(ssm_b_re[i], 1, 2), jnp.swapaxes(ssm_b_im[i], 1, 2),
                ssm_c_re[i], ssm_c_im[i])

    h = h0
    h_in, h_a, h_b, acts = [], [], [], []
    ssm_saved, att_saved = {}, {}
    k_sh = v_sh = None
    for i in range(depth):
        h_in.append(h)
        if i < n_ssm:
            first = gather_of(0, ('ssm_w_glu', 'ssm_d')) if i == 0 else None
            hn, *got = _norm_fwd(h, norm_mix[i], f"norm_mix_fwd{i}", comm=first and first[1])
            if first:
                lw[0] = as_operands(first[0], got)
            mats = _ssm_prep(ssm_params(i), f"ssm_prep{i}")
            mmat, atr, ati = mats[0], mats[5], mats[6]
            pmat = jnp.concatenate([mats[1], mats[2]], axis=2)
            qt = jnp.concatenate([mats[3], mats[4]], axis=2)
            s_in = _ssm_state_in(hn, pmat, f"ssm_state_in{i}")
            xp = _ssm_carry(s_in, *_ssm_rows(atr, ati, False), False, f"ssm_carry_fwd{i}")[0]
            rest = gather_of(0, ('w_up', 'w_down', 'w_ple_gate', 'w_ple_proj')) if i == 0 else None
            y, *got = _ssm_out(hn, xp, mmat, qt, f"ssm_out{i}", comm=rest and rest[1])
            if rest:
                lw[0].update(as_operands(rest[0], got))
            ha = _glu_fwd(y, hn, h, lw[i]['ssm_d'], lw[i]['ssm_w_glu'], f"glu_fwd{i}")
            ssm_saved[i] = (hn, mmat, pmat, qt, atr, ati, xp, y)
        else:
            j = i - n_ssm
            q = _q_fwd(h, norm_mix[i], lw[i]['w_q'], tabs, f"q_fwd{j}")
            o = _attn_fwd(q, k_sh, v_sh, attn_sinks[j], f"attn_fwd{j}")
            ha = _lin_res(h, o, lw[i]['w_o'], f"attn_out{j}")
            att_saved[j] = (q, o)
        h_a.append(ha)
        nxt = gather_of(i + 1) if i + 1 < depth else None
        res = _mlp_ple_fwd(ha, p[i, 0], norm_mlp[i], lw[i]['w_up'], lw[i]['w_down'], norm_ple[i],
                           lw[i]['w_ple_gate'], lw[i]['w_ple_proj'], f"mlp_fwd{i}", comm=nxt and nxt[1])
        hb, act, h = res[:3]
        acts.append(act)
        if nxt:
            lw[i + 1] = as_operands(nxt[0], res[3:])
        h_b.append(hb)
        if i == n_ssm - 1:
            k_sh, v_sh = _kv_fwd(h, kv_norm, lw[n_ssm]['w_k'], lw[n_ssm]['w_v'], tabs, "kv_fwd")
    h_kv = h_in[n_ssm] if n_ssm < depth else h
    dh, loss_row, g_norm_final = _loss_bwd(h, norm_final, tgt, "loss_bwd")
    loss = lax.psum(loss_row[0, 0], AXES)

    g_norm_mix, g_norm_mlp, g_norm_ple = [None] * depth, [None] * depth, [None] * depth
    g_ssm, g_sinks = [None] * n_ssm, [None] * n_att
    g_kv_norm = None
    dks, dvs = [], []
    recv = {}

    def riding(stacks):
        if not stacks:
            return None
        ride = (list(stacks), _Exchange(list(stacks.values()), True))
        stacks.clear()
        return ride

    def landed(ride, blocks):
        if ride:
            recv.update(zip(ride[0], blocks))

    gl = {}
    for i in range(depth - 1, -1, -1):
        if i == n_ssm - 1:
            dh, dkp, dvb, hkb, g_kv_norm = _kv_bwd(dks, dvs, h_kv, kv_norm, dh, lw[n_ssm]['w_k_t'],
                                                   lw[n_ssm]['w_v_t'], tabs, "kv_bwd")
            gl['w_k', None] = _atb(hkb, dkp, False, "grad_w_k")
            gl['w_v', None] = _atb(hkb, dvb, False, "grad_w_v")
        dhb, dz, nb16, dpp, g_norm_ple[i] = _ple_bwd(h_b[i], p[i, 0], dh, norm_ple[i], lw[i]['w_ple_gate'],
                                                     lw[i]['w_ple_gate_t'], lw[i]['w_ple_proj'], f"ple_bwd{i}")
        gl['w_ple_gate', i] = _atb(nb16, dz, False, f"grad_w_ple_gate{i}")
        gl['w_ple_proj', i] = _atb(p[i, 0], dpp, True, f"grad_w_ple_proj{i}")
        ride = riding(gl)
        res = _mlp_bwd(h_a[i], acts[i], dhb, norm_mlp[i], lw[i]['w_up_t'], lw[i]['w_down_t'], f"mlp_bwd{i}",
                       comm=ride and ride[1])
        dha, hmb, da, g_norm_mlp[i] = res[:4]
        landed(ride, res[4:])
        g_up = {('w_up', i): _atb(hmb, da, True, f"grad_w_up{i}")}
        g_down = {('w_down', i): _atb(acts[i], dhb, False, f"grad_w_down{i}")}
        if i >= n_ssm:
            j = i - n_ssm
            q, o = att_saved[j]
            do = _lin_bf16(dha, lw[i]['w_o_t'], f"attn_out_bwd{j}")
            ride = riding({**g_up, **g_down, ('w_o', j): _atb(o, dha, False, f"grad_w_o{j}")})
            dq, dk_j, dv_j, dsink, *got = _attn_bwd(q, k_sh, v_sh, do, attn_sinks[j], f"attn_bwd{j}",
                                                    comm=ride[1])
            landed(ride, got)
            dks.append(dk_j)
            dvs.append(dv_j)
            g_sinks[j] = dsink[:, 0]
            dh, dqp, hnb, g_norm_mix[i] = _q_bwd(dq, h_in[i], norm_mix[i], dha, lw[i]['w_q_t'], tabs, f"q_bwd{j}")
            gl['w_q', j] = _atb(hnb, dqp, False, f"grad_w_q{j}")
        else:
            hn, mmat, pmat, qt, atr, ati, xp, y = ssm_saved[i]
            dyy, dhn_d, geb, dab, g_dskip = _glu_bwd(y, hn, dha, lw[i]['ssm_d'], lw[i]['ssm_w_glu'],
                                                     lw[i]['ssm_w_glu_t'], f"glu_bwd{i}")
            g_down['ssm_d', i] = g_dskip.reshape(NDEV, d // NDEV)
            g_down['ssm_w_glu', i] = _atb(geb, dab, True, f"grad_ssm_w_glu{i}")
            dxp = _ssm_dstate(dyy, qt, f"ssm_dstate{i}")
            ride = riding(g_up)
            gs, *got = _ssm_carry(dxp, *_ssm_rows(atr, ati, True), True, f"ssm_carry_bwd{i}", comm=ride[1])
            landed(ride, got)
            ride = riding(g_down)
            du, dm, dp, dqt, da_raw, *got = _ssm_bwd(hn, dyy, xp, gs, mmat, pmat, f"ssm_bwd{i}", comm=ride[1])
            landed(ride, got)
            n = SSM_STATE
            cots = (dm, dp[:, :, :n], dp[:, :, n:], dqt[:, :, :n], dqt[:, :, n:],
                    (da_raw[:, 0:1, :n] + da_raw[:, 0:1, n:]), (da_raw[:, 1:2, :n] - da_raw[:, 1:2, n:]))
            g_ssm[i] = _ssm_prep_vjp(ssm_params(i), cots, f"ssm_prep_vjp{i}")
            dh, g_norm_mix[i] = _norm_bwd(h_in[i], norm_mix[i], dhn_d, du, dha, f"norm_mix_bwd{i}")
    grad_x = dh[None]
    if gl:
        ride = riding(gl)
        landed(ride, _exchange(ride[1].arrs, True, "scatter_grads_rest"))

    out_g, out_d, out_m, out_v = {}, {}, {}, {}
    updated = {}
    for (k, l), parts in recv.items():
        pick = (lambda t: t) if l is None else (lambda t: t[l])
        shp = pick(weights[k]).shape
        r2 = (math.prod(shp[:-1]), shp[-1])
        res = _adamw(parts.reshape((NDEV,) + r2), pick(weights[k]).reshape(r2), pick(mom1[k]).reshape(r2),
                     pick(mom2[k]).reshape(r2), f"adamw_{k}{'' if l is None else l}")
        updated.setdefault(k, {})[l] = [t.reshape(shp) for t in res]
    for k, by_layer in updated.items():
        for n, dst in enumerate((out_g, out_d, out_m, out_v)):
            dst[k] = by_layer[None][n] if None in by_layer else jnp.stack([by_layer[l][n] for l in sorted(by_layer)])

    def ssm_grad(idx, unswap=False):
        g = jnp.stack([g_ssm[i][idx] for i in range(n_ssm)])
        return jnp.swapaxes(g, 2, 3) if unswap else g

    small = {'norm_mix': jnp.concatenate(g_norm_mix, axis=0),
             'ssm_lambda_re': ssm_grad(0), 'ssm_lambda_im': ssm_grad(1), 'ssm_log_dt': ssm_grad(2),
             'ssm_b_re': ssm_grad(3, True), 'ssm_b_im': ssm_grad(4, True),
             'ssm_c_re': ssm_grad(5), 'ssm_c_im': ssm_grad(6),
             'kv_norm': g_kv_norm, 'attn_sinks': jnp.stack(g_sinks),
             'norm_mlp': jnp.concatenate(g_norm_mlp, axis=0), 'norm_ple': jnp.concatenate(g_norm_ple, axis=0),
             'norm_final': g_norm_final}
    snames = list(small)
    sizes = [weights[k].size for k in snames]
    total = sum(sizes)
    lanes = 128
    padded = -(-total // (512 * lanes)) * (512 * lanes)

    def flat(parts):
        v = jnp.concatenate([t.reshape(-1) for t in parts] + [jnp.zeros((padded - total,), F32)])
        return v.reshape(padded // lanes, lanes)

    parts = _exchange([flat([small[k] for k in snames])], False, "gather_small_grads")[0]
    res = _adamw(parts, flat([weights[k] for k in snames]), flat([mom1[k] for k in snames]),
                 flat([mom2[k] for k in snames]), "adamw_small")
    off = 0
    for k, sz in zip(snames, sizes):
        for dst, t in zip((out_g, out_d, out_m, out_v), res):
            dst[k] = t.reshape(-1)[off:off + sz].reshape(weights[k].shape)
        off += sz

    return (loss, grad_x, *[out_g[k] for k in names], *[out_d[k] for k in names],
            *[out_m[k] for k in names], *[out_v[k] for k in names])
```

```python
import functools
import math

import jax
import jax.numpy as jnp
from jax import lax
from jax.experimental import pallas as pl
from jax.experimental.pallas import tpu as pltpu

F32 = jnp.float32
BF16 = jnp.bfloat16
SDS = jax.ShapeDtypeStruct
MESH = pl.DeviceIdType.MESH
AXES = ("x", "y", "c")
NDEV = 8

RMS_EPS = 1e-6
SSM_GROUP = 16
SSM_STATE = 64
SSM_T = 8
SSM_W = SSM_T * SSM_GROUP
HEAD_DIM = 64
GQA_GROUP = 4
ATTN_BLOCK = 128
ROT_DIM = 16
ROPE_THETA = 500000.0
NEG_INF = -1e30
ADAM_LR, ADAM_B1, ADAM_B2, ADAM_EPS, ADAM_WD, ADAM_STEP = 0.001, 0.9, 0.999, 1e-08, 0.01, 10

VMEM_CAP = 56 * 1024 * 1024
HI = lax.Precision.HIGHEST

NN = ((1,), (0,))
NT = ((1,), (1,))
TN = ((0,), (0,))


def _dot(a, b, dims=NN, precision=None):
    return lax.dot_general(a, b, (dims, ((), ())), preferred_element_type=F32, precision=precision)


def _split(a):
    if isinstance(a, tuple):
        return a
    hi = a.astype(BF16)
    return hi, (a - hi.astype(F32)).astype(BF16)


def _dot3(a, b, dims=NN):
    (ah, al), (bh, bl) = _split(a), _split(b)
    return _dot(ah, bh, dims) + (_dot(ah, bl, dims) + _dot(al, bh, dims))


@jax.custom_vjp
def _dot3_nt(a, b):
    return _dot3(a, b, NT)


def _dot3_nt_fwd(a, b):
    return _dot3(a, b, NT), (a, b)


def _dot3_nt_bwd(res, g):
    a, b = res
    return _dot3(g, b, NN), _dot3(g, a, TN)


_dot3_nt.defvjp(_dot3_nt_fwd, _dot3_nt_bwd)


def _tile(n, pref):
    t = min(n, pref)
    while n % t:
        t //= 2
    return t


def _nbytes(shape, dtype):
    return math.prod(s for s in shape if s is not None) * jnp.dtype(dtype).itemsize


def _vmem_limit(blocks, extra=0):
    need = sum(_nbytes(s, d) * n for s, d, n in blocks) + extra + (4 << 20)
    return int(min(VMEM_CAP, max(need, 16 << 20)))


def _pcall(body, *, name, out_shape, grid, in_specs, out_specs, scratch=(), vmem=None, comm=None):
    single = not isinstance(out_shape, (list, tuple))
    out_shape = [out_shape] if single else list(out_shape)
    out_specs = [out_specs] if single else list(out_specs)
    in_specs, scratch = list(in_specs), list(scratch)
    if comm is not None:
        n_in, n_out, n_scr, nx = len(in_specs), len(out_specs), len(scratch), len(comm.arrs)
        hbm = pl.BlockSpec(memory_space=pl.ANY)
        in_specs = in_specs + [hbm] * nx
        out_specs = out_specs + [hbm] * nx
        out_shape = out_shape + comm.out_shape()
        scratch = scratch + comm.semaphores()
        inner = body

        def body(*refs):
            ins, xin, rest = refs[:n_in], refs[n_in:n_in + nx], refs[n_in + nx:]
            outs, xout, rest = rest[:n_out], rest[n_out:n_out + nx], rest[n_out + nx:]
            scr, sems = rest[:n_scr], rest[n_scr:]
            first = functools.reduce(jnp.logical_and, [pl.program_id(a) == 0 for a in range(len(grid))])
            last = functools.reduce(jnp.logical_and, [pl.program_id(a) == g - 1 for a, g in enumerate(grid)])

            @pl.when(first)
            def _():
                comm.start(xin, xout, sems)
            inner(*ins, *outs, *scr)

            @pl.when(last)
            def _():
                comm.wait(xin, xout, sems)

    call = pl.pallas_call(
        body, out_shape=out_shape[0] if single and comm is None else out_shape, grid=grid, in_specs=in_specs,
        out_specs=out_specs[0] if single and comm is None else out_specs, scratch_shapes=scratch, name=name,
        compiler_params=pltpu.CompilerParams(
            dimension_semantics=("arbitrary",) * len(grid), vmem_limit_bytes=vmem),
        interpret=False)
    if comm is None:
        return call
    return lambda *args: call(*args, *comm.arrs)


def _rms(x, g):
    r = lax.rsqrt(jnp.mean(x * x, axis=-1, keepdims=True) + RMS_EPS)
    return x * r * g, r


def _rms_bwd(x, g, r, dy):
    xh = x * r
    dyg = dy * g
    dx = r * (dyg - xh * jnp.mean(dyg * xh, axis=-1, keepdims=True))
    return dx, jnp.sum(dy * xh, axis=0, keepdims=True)


_GELU_C = math.sqrt(2.0 / math.pi)


def _gelu_parts(x):
    t = jnp.tanh(_GELU_C * (x + 0.044715 * x * x * x))
    return 0.5 * x * (1.0 + t), t


def _gelu_grad(x, t):
    return 0.5 * (1.0 + t) + 0.5 * x * (1.0 - t * t) * _GELU_C * (1.0 + 3 * 0.044715 * x * x)


def _rope_tables(seqlen):
    half = ROT_DIM // 2
    inv = ROPE_THETA ** (-jnp.arange(0, ROT_DIM, 2, dtype=F32) / ROT_DIM)
    ang = jnp.arange(seqlen, dtype=jnp.int32).astype(F32)[:, None] * inv[None, :]
    cos, sin = jnp.cos(ang), jnp.sin(ang)
    zeros = jnp.zeros((seqlen, HEAD_DIM - ROT_DIM), F32)
    zh = jnp.zeros((seqlen, half), F32)
    c = jnp.concatenate([cos, cos, zeros + 1.0], axis=1)
    sa = jnp.concatenate([zh, sin, zeros], axis=1)
    sb = jnp.concatenate([-sin, zh, zeros], axis=1)
    return tuple(jnp.tile(t, (1, 128 // HEAD_DIM)) for t in (c, sa, sb))


def _rope(x, c, sa, sb):
    w = x.shape[1]
    reps = w // 128
    half = ROT_DIM // 2
    return (x * jnp.tile(c, (1, reps)) + pltpu.roll(x, half, 1) * jnp.tile(sa, (1, reps))
            + pltpu.roll(x, w - half, 1) * jnp.tile(sb, (1, reps)))


def _rope_bwd(dy, c, sa, sb):
    w = dy.shape[1]
    reps = w // 128
    half = ROT_DIM // 2
    return (dy * jnp.tile(c, (1, reps)) + pltpu.roll(dy * jnp.tile(sa, (1, reps)), w - half, 1)
            + pltpu.roll(dy * jnp.tile(sb, (1, reps)), half, 1))


def _rspec(tm, c):
    return pl.BlockSpec((tm, c), lambda i: (i, 0))


def _cspec(shape, idx=None):
    idx = tuple(idx) if idx is not None else (0,) * len(shape)
    return pl.BlockSpec(tuple(shape), lambda i: idx, pipeline_mode=pl.Buffered(1))


def _rowcall(body, name, seqlen, tm, rows_in, consts_in, rows_out, acc_out=(), extra_vmem=0, comm=None):
    in_specs = [_rspec(tm, a.shape[1]) for a in rows_in] + [_cspec(bs, ix) for _, bs, ix in consts_in]
    out_shape = [SDS((seqlen, c), d) for c, d in rows_out] + [SDS(s, F32) for s in acc_out]
    out_specs = [_rspec(tm, c) for c, _ in rows_out] + [pl.BlockSpec(s, lambda i: (0, 0)) for s in acc_out]
    blocks = ([((tm, a.shape[1]), a.dtype, 2) for a in rows_in] + [(bs, a.dtype, 1) for a, bs, _ in consts_in]
              + [((tm, c), d, 2) for c, d in rows_out])
    temporaries = 12 * tm * rows_in[0].shape[1] * 4
    return _pcall(body, name=name, out_shape=out_shape, grid=(seqlen // tm,), in_specs=in_specs,
                  out_specs=out_specs, vmem=_vmem_limit(blocks, extra_vmem + temporaries), comm=comm)(
                      *rows_in, *[a for a, _, _ in consts_in])


def _whole(a):
    return (a, a.shape, None)


def _norm_fwd(h, g, name, comm=None):
    seqlen, d = h.shape
    tm = _tile(seqlen, 1024)

    def body(h_ref, g_ref, o_ref):
        o_ref[...] = _rms(h_ref[...], g_ref[...])[0]

    return _rowcall(body, name, seqlen, tm, [h], [_whole(g.reshape(1, d))], [(d, F32)], comm=comm)


def _norm_bwd(h, g, dy1, scale1, dy2, dres, name, comm=None):
    seqlen, d = h.shape
    tm = _tile(seqlen, 512)

    def body(h_ref, dy1_ref, dy2_ref, dres_ref, g_ref, s_ref, dh_ref, dg_ref):
        @pl.when(pl.program_id(0) == 0)
        def _():
            dg_ref[...] = jnp.zeros_like(dg_ref)
        x = h_ref[...]
        gv = g_ref[...]
        _, r = _rms(x, gv)
        dx, dg = _rms_bwd(x, gv, r, dy1_ref[...] * s_ref[...] + dy2_ref[...])
        dh_ref[...] = dres_ref[...] + dx
        dg_ref[...] += dg

    return _rowcall(body, name, seqlen, tm, [h, dy1, dy2, dres],
                    [_whole(g.reshape(1, d)), _whole(scale1.reshape(1, d))], [(d, F32)], [(1, d)], comm=comm)


def _glu_fwd(y, hn, h, dskip, wglu, name):
    seqlen, d = h.shape
    tm = _tile(seqlen, 512)

    def body(y_ref, hn_ref, h_ref, d_ref, w_ref, o_ref):
        yy = y_ref[...] + d_ref[...] * hn_ref[...]
        ge, _ = _gelu_parts(yy)
        ab = _dot(ge.astype(BF16), w_ref[...])
        o_ref[...] = h_ref[...] + ab[:, :d] * jax.nn.sigmoid(ab[:, d:])

    return _rowcall(body, name, seqlen, tm, [y, hn, h], [_whole(dskip.reshape(1, d)), _whole(wglu)], [(d, F32)],
                    extra_vmem=tm * d * 4 * 6)[0]


def _glu_bwd(y, hn, dmix, dskip, wglu, wglu_t, name):
    seqlen, d = hn.shape
    tm = _tile(seqlen, 512)

    def body(y_ref, hn_ref, dm_ref, d_ref, w_ref, wt_ref, dyy_ref, ge_ref, dab_ref, dd_ref):
        @pl.when(pl.program_id(0) == 0)
        def _():
            dd_ref[...] = jnp.zeros_like(dd_ref)
        hn_v = hn_ref[...]
        dsk = d_ref[...]
        yy = y_ref[...] + dsk * hn_v
        ge, t = _gelu_parts(yy)
        geb = ge.astype(BF16)
        ab = _dot(geb, w_ref[...])
        a = ab[:, :d]
        sg = jax.nn.sigmoid(ab[:, d:])
        dm = dm_ref[...]
        dab_ref[:, :d] = (dm * sg).astype(BF16)
        dab_ref[:, d:] = (dm * a * sg * (1.0 - sg)).astype(BF16)
        dge = _dot(dab_ref[...], wt_ref[...])
        dyy = dge * _gelu_grad(yy, t)
        dyy_ref[...] = dyy
        ge_ref[...] = geb
        dd_ref[...] += jnp.sum(dyy * hn_v, axis=0, keepdims=True)

    return _rowcall(body, name, seqlen, tm, [y, hn, dmix],
                    [_whole(dskip.reshape(1, d)), _whole(wglu), _whole(wglu_t)],
                    [(d, F32), (d, BF16), (2 * d, BF16)], [(1, d)], extra_vmem=tm * d * 4 * 8)


def _q_fwd(h, g, wq, tabs, name):
    seqlen, d = h.shape
    tm = _tile(seqlen, 512)

    def body(h_ref, c_ref, sa_ref, sb_ref, g_ref, w_ref, q_ref):
        hn, _ = _rms(h_ref[...], g_ref[...])
        qp = _dot(hn.astype(BF16), w_ref[...])
        q_ref[...] = _rope(qp, c_ref[...], sa_ref[...], sb_ref[...]).astype(BF16)

    return _rowcall(body, name, seqlen, tm, [h, *tabs], [_whole(g.reshape(1, d)), _whole(wq)], [(d, BF16)],
                    extra_vmem=tm * d * 4 * 6)[0]


def _q_bwd(dq, h, g, dres, wq, tabs, name):
    seqlen, d = h.shape
    tm = _tile(seqlen, 512)

    def body(dq_ref, h_ref, dres_ref, c_ref, sa_ref, sb_ref, g_ref, w_ref, dh_ref, dqp_ref, hn_ref, dg_ref):
        @pl.when(pl.program_id(0) == 0)
        def _():
            dg_ref[...] = jnp.zeros_like(dg_ref)
        dqp = _rope_bwd(dq_ref[...], c_ref[...], sa_ref[...], sb_ref[...]).astype(BF16)
        x = h_ref[...]
        gv = g_ref[...]
        hn, r = _rms(x, gv)
        dhn = _dot(dqp, w_ref[...])
        dx, dg = _rms_bwd(x, gv, r, dhn)
        dh_ref[...] = dres_ref[...] + dx
        dqp_ref[...] = dqp
        hn_ref[...] = hn.astype(BF16)
        dg_ref[...] += dg

    return _rowcall(body, name, seqlen, tm, [dq, h, dres, *tabs], [_whole(g.reshape(1, d)), _whole(wq)],
                    [(d, F32), (d, BF16), (d, BF16)], [(1, d)], extra_vmem=tm * d * 4 * 6)


def _kv_fwd(h, g, wk, wv, tabs, name):
    seqlen, d = h.shape
    dk = wk.shape[1]
    tm = _tile(seqlen, 512)

    def body(h_ref, c_ref, sa_ref, sb_ref, g_ref, wk_ref, wv_ref, k_ref, v_ref):
        hk = _rms(h_ref[...], g_ref[...])[0].astype(BF16)
        k_ref[...] = _rope(_dot(hk, wk_ref[...]), c_ref[...], sa_ref[...], sb_ref[...]).astype(BF16)
        v_ref[...] = _dot(hk, wv_ref[...]).astype(BF16)

    return _rowcall(body, name, seqlen, tm, [h, *tabs], [_whole(g.reshape(1, d)), _whole(wk), _whole(wv)],
                    [(dk, BF16), (dk, BF16)], extra_vmem=tm * d * 4 * 4)


def _kv_bwd(dks, dvs, h, g, dres, wk, wv, tabs, name):
    seqlen, d = h.shape
    dkw = wk.shape[0]
    tm = _tile(seqlen, 512)

    def body(dk0_ref, dk1_ref, dv0_ref, dv1_ref, h_ref, dres_ref, c_ref, sa_ref, sb_ref, g_ref, wk_ref, wv_ref,
             dh_ref, dkp_ref, dvb_ref, hk_ref, dg_ref):
        @pl.when(pl.program_id(0) == 0)
        def _():
            dg_ref[...] = jnp.zeros_like(dg_ref)
        dkp = _rope_bwd(dk0_ref[...] + dk1_ref[...], c_ref[...], sa_ref[...], sb_ref[...]).astype(BF16)
        dvb = (dv0_ref[...] + dv1_ref[...]).astype(BF16)
        x = h_ref[...]
        gv = g_ref[...]
        hk, r = _rms(x, gv)
        dhk = _dot(dkp, wk_ref[...]) + _dot(dvb, wv_ref[...])
        dx, dg = _rms_bwd(x, gv, r, dhk)
        dh_ref[...] = dres_ref[...] + dx
        dkp_ref[...] = dkp
        dvb_ref[...] = dvb
        hk_ref[...] = hk.astype(BF16)
        dg_ref[...] += dg

    return _rowcall(body, name, seqlen, tm, [dks[0], dks[1], dvs[0], dvs[1], h, dres, *tabs],
                    [_whole(g.reshape(1, d)), _whole(wk), _whole(wv)],
                    [(d, F32), (dkw, BF16), (dkw, BF16), (d, BF16)], [(1, d)], extra_vmem=tm * d * 4 * 6)


def _lin_res(h, xb, w, name):
    seqlen, d = h.shape
    tm = _tile(seqlen, 512)

    def body(h_ref, x_ref, w_ref, o_ref):
        o_ref[...] = h_ref[...] + _dot(x_ref[...], w_ref[...])

    return _rowcall(body, name, seqlen, tm, [h, xb], [_whole(w)], [(d, F32)], extra_vmem=tm * d * 4 * 2)[0]


def _lin_bf16(dy, w, name):
    seqlen, d = dy.shape
    tm = _tile(seqlen, 512)

    def body(dy_ref, w_ref, o_ref):
        o_ref[...] = _dot(dy_ref[...].astype(BF16), w_ref[...]).astype(BF16)

    return _rowcall(body, name, seqlen, tm, [dy], [_whole(w)], [(w.shape[1], BF16)], extra_vmem=tm * d * 4 * 2)[0]


def _mlp_ple_fwd(h, p, g, wup, wdn, g_ple, wg, wpp, name, comm=None):
    seqlen, d = h.shape
    f = wup.shape[1]
    tm = _tile(seqlen, 512)

    def body(h_ref, p_ref, g_ref, wup_ref, wdn_ref, gp_ref, wg_ref, wpp_ref, hb_ref, act_ref, o_ref):
        x = h_ref[...]
        hm = _rms(x, g_ref[...])[0].astype(BF16)
        r = jnp.maximum(_dot(hm, wup_ref[...]), 0.0)
        act = (r * r).astype(BF16)
        act_ref[...] = act
        hb = x + _dot(act, wdn_ref[...])
        hb_ref[...] = hb
        n = _rms(hb, gp_ref[...])[0].astype(BF16)
        gate = jax.nn.sigmoid(_dot(n, wg_ref[...]))
        o_ref[...] = hb + gate * _dot(p_ref[...].astype(BF16), wpp_ref[...])

    consts = [_whole(g.reshape(1, d)), _whole(wup), _whole(wdn), _whole(g_ple.reshape(1, d)), _whole(wg), _whole(wpp)]
    return _rowcall(body, name, seqlen, tm, [h, p], consts, [(d, F32), (f, BF16), (d, F32)],
                    extra_vmem=tm * f * 4 * 3, comm=comm)


def _mlp_bwd(h, act, dh, g, wup_t, wdn_t, name, comm=None):
    seqlen, d = h.shape
    f = wup_t.shape[0]
    tm = _tile(seqlen, 512)

    def body(h_ref, act_ref, dh_ref, g_ref, wup_ref, wdn_ref, dhin_ref, hm_ref, da_ref, dg_ref):
        @pl.when(pl.program_id(0) == 0)
        def _():
            dg_ref[...] = jnp.zeros_like(dg_ref)
        x = h_ref[...]
        gv = g_ref[...]
        dy = dh_ref[...]
        hm, r = _rms(x, gv)
        rl2 = 2.0 * jnp.sqrt(act_ref[...].astype(F32))
        da = (_dot(dy.astype(BF16), wdn_ref[...]) * rl2).astype(BF16)
        da_ref[...] = da
        dx, dg = _rms_bwd(x, gv, r, _dot(da, wup_ref[...]))
        dhin_ref[...] = dy + dx
        hm_ref[...] = hm.astype(BF16)
        dg_ref[...] += dg

    consts = [_whole(g.reshape(1, d)), _whole(wup_t), _whole(wdn_t)]
    return _rowcall(body, name, seqlen, tm, [h, act, dh], consts, [(d, F32), (d, BF16), (f, BF16)], [(1, d)],
                    extra_vmem=tm * f * 4 * 3, comm=comm)


def _ple_bwd(h, p, dh, g, wg, wg_t, wpp, name):
    seqlen, d = h.shape
    tm = _tile(seqlen, 512)

    def body(h_ref, p_ref, dh_ref, g_ref, wg_ref, wgt_ref, wpp_ref, dhin_ref, dz_ref, n_ref, dpp_ref, dg_ref):
        @pl.when(pl.program_id(0) == 0)
        def _():
            dg_ref[...] = jnp.zeros_like(dg_ref)
        x = h_ref[...]
        gv = g_ref[...]
        dy = dh_ref[...]
        n, r = _rms(x, gv)
        nb16 = n.astype(BF16)
        gate = jax.nn.sigmoid(_dot(nb16, wg_ref[...]))
        pp = _dot(p_ref[...].astype(BF16), wpp_ref[...])
        dz = (dy * pp * gate * (1.0 - gate)).astype(BF16)
        dn = _dot(dz, wgt_ref[...])
        dx, dg = _rms_bwd(x, gv, r, dn)
        dhin_ref[...] = dy + dx
        dz_ref[...] = dz
        n_ref[...] = nb16
        dpp_ref[...] = (dy * gate).astype(BF16)
        dg_ref[...] += dg

    return _rowcall(body, name, seqlen, tm, [h, p, dh],
                    [_whole(g.reshape(1, d)), _whole(wg), _whole(wg_t), _whole(wpp)],
                    [(d, F32), (d, BF16), (d, BF16), (d, BF16)], [(1, d)], extra_vmem=tm * d * 4 * 8)


def _loss_bwd(h, g, tgt, name):
    seqlen, d = h.shape
    tm = _tile(seqlen, 512)

    def body(h_ref, t_ref, g_ref, dh_ref, loss_ref, dg_ref):
        @pl.when(pl.program_id(0) == 0)
        def _():
            dg_ref[...] = jnp.zeros_like(dg_ref)
            loss_ref[...] = jnp.zeros_like(loss_ref)
        x = h_ref[...]
        gv = g_ref[...]
        y, r = _rms(x, gv)
        diff = y - t_ref[...]
        loss_ref[...] += (0.5 / d) * jnp.sum(jnp.sum(diff * diff, axis=1, keepdims=True), axis=0, keepdims=True)
        dx, dg = _rms_bwd(x, gv, r, diff * (1.0 / d))
        dh_ref[...] = dx
        dg_ref[...] += dg

    return _rowcall(body, name, seqlen, tm, [h, tgt], [_whole(g.reshape(1, d))], [(d, F32)], [(1, 128), (1, d)])


def _atb(a, b, col_blocked, name):
    seqlen, k1 = a.shape
    k2 = b.shape[1]
    if col_blocked:
        cs = k2 // NDEV
        t1 = _tile(k1, 512)
        nblk = _tile(NDEV, max(1, 2048 // cs))
        t2 = nblk * cs
        oshape = (NDEV, k1, cs)
        oblock = (nblk, t1, cs)
        omap = lambda i, j, l: (j, i, 0)
    else:
        rs = k1 // NDEV
        t2 = _tile(k2, 2048)
        nblk = _tile(NDEV, max(1, 1024 // rs))
        t1 = nblk * rs
        oshape = (NDEV, rs, k2)
        oblock = (nblk, rs, t2)
        omap = lambda i, j, l: (i, 0, j)
    tl = _tile(seqlen, 2048 if b.dtype == BF16 else 1024)

    def body(a_ref, b_ref, o_ref):
        @pl.when(pl.program_id(2) == 0)
        def _():
            o_ref[...] = jnp.zeros_like(o_ref)
        res = _dot(a_ref[...].astype(BF16), b_ref[...].astype(BF16), TN)
        for n in range(nblk):
            if col_blocked:
                o_ref[n] += res[:, n * cs:(n + 1) * cs]
            else:
                o_ref[n] += res[n * rs:(n + 1) * rs, :]

    blocks = [((tl, t1), a.dtype, 2), ((tl, t2), b.dtype, 2), ((t1, t2), F32, 2)]
    return _pcall(
        body, name=name, out_shape=SDS(oshape, F32), grid=(k1 // t1, k2 // t2, seqlen // tl),
        in_specs=[pl.BlockSpec((tl, t1), lambda i, j, l: (l, i)), pl.BlockSpec((tl, t2), lambda i, j, l: (l, j))],
        out_specs=pl.BlockSpec(oblock, omap),
        vmem=_vmem_limit(blocks, extra=t1 * t2 * 4 + tl * (t1 + t2) * 2))(a, b)


_ATTN_SCALE = HEAD_DIM ** -0.5


def _attn_bias():
    qi = lax.broadcasted_iota(jnp.int32, (ATTN_BLOCK, 2 * ATTN_BLOCK), 0) + ATTN_BLOCK
    kj = lax.broadcasted_iota(jnp.int32, (ATTN_BLOCK, 2 * ATTN_BLOCK), 1)
    band = (kj <= qi) & (qi - kj < ATTN_BLOCK)
    return jnp.where(jnp.stack([band & (kj >= ATTN_BLOCK), band]), 0.0, NEG_INF).astype(F32)


def _bias_spec():
    return pl.BlockSpec((None, ATTN_BLOCK, 2 * ATTN_BLOCK), lambda n: (jnp.minimum(n, 1), 0, 0))


def _attn_probs(q4s, kks, sink_col, bias):
    s = jnp.concatenate([_dot(q4, kk, NT) for q4, kk in zip(q4s, kks)], axis=0)
    rows = s.shape[0]
    s = (s.reshape(rows // ATTN_BLOCK, ATTN_BLOCK, 2 * ATTN_BLOCK) + bias).reshape(rows, 2 * ATTN_BLOCK)
    m = jnp.maximum(jnp.max(s, axis=1, keepdims=True), sink_col)
    pr = jnp.exp(s - m)
    es = jnp.exp(sink_col - m)
    inv = 1.0 / (jnp.sum(pr, axis=1, keepdims=True) + es)
    return pr * inv, es * inv


def _sink_col(sink_ref, nheads):
    return jnp.concatenate([jnp.full((ATTN_BLOCK, 1), sink_ref[hq], F32) for hq in range(nheads)], axis=0)


def _kv_pair(p_ref, c_ref, kh):
    sl = slice(kh * HEAD_DIM, (kh + 1) * HEAD_DIM)
    return jnp.concatenate([p_ref[:, sl], c_ref[:, sl]], axis=0)


def _stack_heads(ref, kh, scale=None):
    x = jnp.concatenate(
        [ref[:, (kh * GQA_GROUP + g) * HEAD_DIM:(kh * GQA_GROUP + g + 1) * HEAD_DIM] for g in range(GQA_GROUP)], axis=0)
    return x if scale is None else x * scale


def _attn_fwd(q, k, v, sinks, name):
    seqlen, d = q.shape
    dkv = k.shape[1]
    nkv = dkv // HEAD_DIM
    nb = seqlen // ATTN_BLOCK
    blk = ATTN_BLOCK

    def body(sink_ref, bias_ref, q_ref, kc_ref, kp_ref, vc_ref, vp_ref, o_ref):
        q4s = [_stack_heads(q_ref, kh, _ATTN_SCALE) for kh in range(nkv)]
        kks = [_kv_pair(kp_ref, kc_ref, kh) for kh in range(nkv)]
        w, _ = _attn_probs(q4s, kks, _sink_col(sink_ref, nkv * GQA_GROUP), bias_ref[...])
        wb = w.astype(BF16)
        for kh in range(nkv):
            o4 = _dot(wb[kh * GQA_GROUP * blk:(kh + 1) * GQA_GROUP * blk, :], _kv_pair(vp_ref, vc_ref, kh))
            for g in range(GQA_GROUP):
                hq = kh * GQA_GROUP + g
                o_ref[:, hq * HEAD_DIM:(hq + 1) * HEAD_DIM] = o4[g * blk:(g + 1) * blk, :].astype(BF16)

    cur = lambda n: (n, 0)
    prev = lambda n: (jnp.maximum(n - 1, 0), 0)
    return _pcall(
        body, name=name, out_shape=SDS((seqlen, d), BF16), grid=(nb,),
        in_specs=[pl.BlockSpec(memory_space=pltpu.SMEM), _bias_spec(), pl.BlockSpec((blk, d), cur),
                  pl.BlockSpec((blk, dkv), cur), pl.BlockSpec((blk, dkv), prev),
                  pl.BlockSpec((blk, dkv), cur), pl.BlockSpec((blk, dkv), prev)],
        out_specs=pl.BlockSpec((blk, d), cur), vmem=32 << 20)(sinks, _attn_bias(), q, k, k, v, v)


def _attn_bwd(q, k, v, do, sinks, name, comm=None):
    seqlen, d = q.shape
    dkv = k.shape[1]
    nkv = dkv // HEAD_DIM
    nh = d // HEAD_DIM
    nb = seqlen // ATTN_BLOCK
    blk = ATTN_BLOCK

    def body(sink_ref, bias_ref, q_ref, kc_ref, kp_ref, vc_ref, vp_ref, do_ref, dq_ref, dk_ref, dv_ref, ds_ref,
             ck_ref, cv_ref):
        n = pl.program_id(0)

        @pl.when(n == 0)
        def _():
            ck_ref[...] = jnp.zeros_like(ck_ref)
            cv_ref[...] = jnp.zeros_like(cv_ref)
            ds_ref[...] = jnp.zeros_like(ds_ref)

        @pl.when(n == nb)
        def _():
            dk_ref[...] = ck_ref[...]
            dv_ref[...] = cv_ref[...]

        @pl.when(n < nb)
        def _():
            q4s = [_stack_heads(q_ref, kh, _ATTN_SCALE) for kh in range(nkv)]
            do4s = [_stack_heads(do_ref, kh) for kh in range(nkv)]
            kks = [_kv_pair(kp_ref, kc_ref, kh) for kh in range(nkv)]
            w, wsink = _attn_probs(q4s, kks, _sink_col(sink_ref, nh), bias_ref[...])
            dw = jnp.concatenate([_dot(do4s[kh], _kv_pair(vp_ref, vc_ref, kh), NT) for kh in range(nkv)], axis=0)
            dsum = jnp.sum(w * dw, axis=1, keepdims=True)
            ds_all = (w * (dw - dsum)).astype(BF16)
            wb = w.astype(BF16)
            dsk = -wsink * dsum
            for kh in range(nkv):
                sl = slice(kh * HEAD_DIM, (kh + 1) * HEAD_DIM)
                rows = slice(kh * GQA_GROUP * blk, (kh + 1) * GQA_GROUP * blk)
                ds = ds_all[rows, :]
                dq4 = _dot(ds, kks[kh]) * _ATTN_SCALE
                dkk = _dot(ds, q4s[kh], TN)
                dvv = _dot(wb[rows, :], do4s[kh], TN)
                for g in range(GQA_GROUP):
                    hq = kh * GQA_GROUP + g
                    dq_ref[:, hq * HEAD_DIM:(hq + 1) * HEAD_DIM] = dq4[g * blk:(g + 1) * blk, :]
                    ds_ref[hq:hq + 1, :] += jnp.sum(dsk[hq * blk:(hq + 1) * blk, :], axis=0, keepdims=True)
                dk_ref[:, sl] = ck_ref[:, sl] + dkk[:blk, :]
                dv_ref[:, sl] = cv_ref[:, sl] + dvv[:blk, :]
                ck_ref[:, sl] = dkk[blk:, :]
                cv_ref[:, sl] = dvv[blk:, :]

    cur = lambda n: (jnp.minimum(n, nb - 1), 0)
    prev = lambda n: (jnp.clip(n - 1, 0, nb - 1), 0)
    lag = lambda n: (jnp.maximum(n - 1, 0), 0)
    return _pcall(
        body, name=name,
        out_shape=[SDS((seqlen, d), F32), SDS((seqlen, dkv), F32), SDS((seqlen, dkv), F32), SDS((nh, 128), F32)],
        grid=(nb + 1,),
        in_specs=[pl.BlockSpec(memory_space=pltpu.SMEM), _bias_spec(), pl.BlockSpec((blk, d), cur),
                  pl.BlockSpec((blk, dkv), cur), pl.BlockSpec((blk, dkv), prev),
                  pl.BlockSpec((blk, dkv), cur), pl.BlockSpec((blk, dkv), prev), pl.BlockSpec((blk, d), cur)],
        out_specs=[pl.BlockSpec((blk, d), cur), pl.BlockSpec((blk, dkv), lag), pl.BlockSpec((blk, dkv), lag),
                   pl.BlockSpec((nh, 128), lambda n: (0, 0))],
        scratch=[pltpu.VMEM((blk, dkv), F32)] * 2, vmem=32 << 20, comm=comm)(sinks, _attn_bias(), q, k, k, v, v, do)


def _ssm_mats(lre, lim, ldt, btr, bti, cr, ci):
    dt = jnp.exp(ldt)
    mag = jnp.exp(lre * dt)
    ar = mag * jnp.cos(lim * dt)
    ai = mag * jnp.sin(lim * dt)
    den = lre * lre + lim * lim
    nr = ar - 1.0
    cfr = (nr * lre + ai * lim) / den
    cfi = (ai * lre - nr * lim) / den
    bbr = cfr * btr - cfi * bti
    bbi = cfr * bti + cfi * btr
    pr = [jnp.ones_like(ar)]
    pi = [jnp.zeros_like(ai)]
    for _ in range(SSM_T):
        pr.append(pr[-1] * ar - pi[-1] * ai)
        pi.append(pr[-2] * ai + pi[-1] * ar)
    last = SSM_T - 1
    p_re = jnp.concatenate([pr[last - s] * bbr - pi[last - s] * bbi for s in range(SSM_T)], axis=0)
    p_im = jnp.concatenate([pr[last - s] * bbi + pi[last - s] * bbr for s in range(SSM_T)], axis=0)
    qt_re = jnp.concatenate([pr[t + 1] * cr - pi[t + 1] * ci for t in range(SSM_T)], axis=0)
    qt_im = jnp.concatenate([-(pr[t + 1] * ci + pi[t + 1] * cr) for t in range(SSM_T)], axis=0)
    ctr = jnp.concatenate([cr] * SSM_T, axis=0)
    cti = jnp.concatenate([ci] * SSM_T, axis=0)
    lag = (lax.broadcasted_iota(jnp.int32, (SSM_W, SSM_W), 1) // SSM_GROUP
           - lax.broadcasted_iota(jnp.int32, (SSM_W, SSM_W), 0) // SSM_GROUP)
    m = jnp.zeros((SSM_W, SSM_W), F32)
    for l in range(SSM_T):
        zr = jnp.concatenate([pr[l] * bbr - pi[l] * bbi] * SSM_T, axis=0)
        zi = jnp.concatenate([pr[l] * bbi + pi[l] * bbr] * SSM_T, axis=0)
        kl = _dot3_nt(zr, ctr) - _dot3_nt(zi, cti)
        m = m + jnp.where(lag == l, kl, 0.0)
    return m, p_re, p_im, qt_re, qt_im, pr[SSM_T], pi[SSM_T]


_SSM_GB = 8


def _ssm_param_specs(ng):
    n, hh = SSM_STATE, SSM_GROUP
    gb = _tile(ng, _SSM_GB)
    row = pl.BlockSpec((gb, 1, n), lambda i: (i, 0, 0))
    one = pl.BlockSpec((gb, 1, 1), lambda i: (i, 0, 0))
    mat = pl.BlockSpec((gb, hh, n), lambda i: (i, 0, 0))
    big = pl.BlockSpec((gb, SSM_W, SSM_W), lambda i: (i, 0, 0))
    half = pl.BlockSpec((gb, SSM_W, n), lambda i: (i, 0, 0))
    return gb, row, one, mat, big, half


def _ssm_prep(params, name):
    ng = params[0].shape[0]
    n = SSM_STATE
    gb, row, one, mat, big, half = _ssm_param_specs(ng)

    def body(lre, lim, ldt, btr, bti, cr, ci, m_ref, pre_ref, pim_ref, qre_ref, qim_ref, atr_ref, ati_ref):
        for gi in range(gb):
            outs = _ssm_mats(lre[gi], lim[gi], ldt[gi], btr[gi], bti[gi], cr[gi], ci[gi])
            for ref, val in zip((m_ref, pre_ref, pim_ref, qre_ref, qim_ref, atr_ref, ati_ref), outs):
                ref[gi] = val

    return _pcall(
        body, name=name,
        out_shape=[SDS((ng, SSM_W, SSM_W), F32)] + [SDS((ng, SSM_W, n), F32)] * 4 + [SDS((ng, 1, n), F32)] * 2,
        grid=(ng // gb,), in_specs=[row, row, one, mat, mat, mat, mat],
        out_specs=[big, half, half, half, half, row, row], vmem=40 << 20)(*params)


def _ssm_prep_vjp(params, cots, name):
    ng = params[0].shape[0]
    n, hh = SSM_STATE, SSM_GROUP
    gb, row, one, mat, big, half = _ssm_param_specs(ng)

    def body(lre, lim, ldt, btr, bti, cr, ci, dm, dpre, dpim, dqre, dqim, datr, dati,
             o_lre, o_lim, o_ldt, o_btr, o_bti, o_cr, o_ci):
        for gi in range(gb):
            prm = (lre[gi], lim[gi], ldt[gi], btr[gi], bti[gi], cr[gi], ci[gi])
            _, pull = jax.vjp(_ssm_mats, *prm)
            grads = pull((dm[gi], dpre[gi], dpim[gi], dqre[gi], dqim[gi], datr[gi], dati[gi]))
            for ref, val in zip((o_lre, o_lim, o_ldt, o_btr, o_bti, o_cr, o_ci), grads):
                ref[gi] = val

    return _pcall(
        body, name=name,
        out_shape=[SDS((ng, 1, n), F32)] * 2 + [SDS((ng, 1, 1), F32)] + [SDS((ng, hh, n), F32)] * 4,
        grid=(ng // gb,), in_specs=[row, row, one, mat, mat, mat, mat, big, half, half, half, half, row, row],
        out_specs=[row, row, one, mat, mat, mat, mat], vmem=48 << 20)(*params, *cots)


_SSM_GT = SSM_W // SSM_GROUP


def _blk_transpose(xs):
    assert len(xs) == SSM_T == _SSM_GT
    blk = lax.broadcasted_iota(jnp.int32, xs[0].shape, 1) // SSM_GROUP
    xs = list(xs)
    k = SSM_T // 2
    while k:
        high = (blk // k) % 2 == 1
        nxt = []
        for i in range(SSM_T):
            if i & k:
                nxt.append(jnp.where(high, xs[i], pltpu.roll(xs[i ^ k], SSM_W - SSM_GROUP * k, 1)))
            else:
                nxt.append(jnp.where(high, pltpu.roll(xs[i ^ k], SSM_GROUP * k, 1), xs[i]))
        xs = nxt
        k //= 2
    return xs


def _tile_groups(x_ref, ncb):
    return _blk_transpose([x_ref[pl.ds(t, ncb, stride=SSM_T), :] for t in range(SSM_T)])


def _groups_tile(ys, o_ref, ncb):
    for t, y in enumerate(_blk_transpose(ys)):
        o_ref[pl.ds(t, ncb, stride=SSM_T), :] = y


def _ssm_specs(seqlen, d):
    ncb = _tile(seqlen // SSM_T, 512)
    grid = (d // SSM_W, seqlen // (SSM_T * ncb))
    act = pl.BlockSpec((SSM_T * ncb, SSM_W), lambda j, r: (r, j))
    state = pl.BlockSpec((ncb, _SSM_GT * SSM_W), lambda j, r: (r, j))
    mats = pl.BlockSpec((_SSM_GT, SSM_W, SSM_W), lambda j, r: (j, 0, 0))
    return ncb, grid, act, state, mats


def _gsl(gl):
    return slice(gl * SSM_W, (gl + 1) * SSM_W)


def _ssm_state_in(hn, pmat, name):
    seqlen, d = hn.shape
    ncb, grid, act, state, mats = _ssm_specs(seqlen, d)

    def body(x_ref, p_ref, s_ref):
        us = _tile_groups(x_ref, ncb)
        for gl in range(_SSM_GT):
            s_ref[:, _gsl(gl)] = _dot3(us[gl], p_ref[gl], NN)

    return _pcall(body, name=name, out_shape=SDS((seqlen // SSM_T, d * SSM_T), F32), grid=grid,
                  in_specs=[act, mats], out_specs=state, vmem=40 << 20)(hn, pmat)


def _ssm_out(hn, xp, mmat, qt, name, comm=None):
    seqlen, d = hn.shape
    ncb, grid, act, state, mats = _ssm_specs(seqlen, d)

    def body(x_ref, xp_ref, m_ref, q_ref, y_ref):
        us = _tile_groups(x_ref, ncb)
        ys = [_dot3(us[gl], m_ref[gl], NN) + _dot3(xp_ref[:, _gsl(gl)], q_ref[gl], NT)
              for gl in range(_SSM_GT)]
        _groups_tile(ys, y_ref, ncb)

    return _pcall(body, name=name, out_shape=[SDS((seqlen, d), F32)], grid=grid,
                  in_specs=[act, state, mats, mats], out_specs=[act], vmem=40 << 20, comm=comm)(hn, xp, mmat, qt)


def _ssm_dstate(dy, qt, name):
    seqlen, d = dy.shape
    ncb, grid, act, state, mats = _ssm_specs(seqlen, d)

    def body(dy_ref, q_ref, o_ref):
        dys = _tile_groups(dy_ref, ncb)
        for gl in range(_SSM_GT):
            o_ref[:, _gsl(gl)] = _dot3(dys[gl], q_ref[gl], NN)

    return _pcall(body, name=name, out_shape=SDS((seqlen // SSM_T, d * SSM_T), F32), grid=grid,
                  in_specs=[act, mats], out_specs=state, vmem=40 << 20)(dy, qt)


def _ssm_bwd(hn, dy, xp, gs, mmat, pmat, name, comm=None):
    seqlen, d = hn.shape
    ng = d // SSM_GROUP
    ncb, grid, act, state, mats = _ssm_specs(seqlen, d)

    def body(x_ref, dy_ref, xp_ref, g_ref, m_ref, p_ref, du_ref, dm_ref, dp_ref, dq_ref, da_ref):
        @pl.when(pl.program_id(1) == 0)
        def _():
            for ref in (dm_ref, dp_ref, dq_ref, da_ref):
                ref[...] = jnp.zeros_like(ref)
        us = _tile_groups(x_ref, ncb)
        dys = _tile_groups(dy_ref, ncb)
        dus = []
        for gl in range(_SSM_GT):
            xv, gv = xp_ref[:, _gsl(gl)], g_ref[:, _gsl(gl)]
            u2, dy2, x2, g2 = _split(us[gl]), _split(dys[gl]), _split(xv), _split(gv)
            dus.append(_dot3(dy2, m_ref[gl], NT) + _dot3(g2, p_ref[gl], NT))
            dm_ref[gl] += _dot3(u2, dy2, TN)
            dp_ref[gl] += _dot3(u2, g2, TN)
            dq_ref[gl] += _dot3(dy2, x2, TN)
            da_ref[gl, 0:1, :] += jnp.sum(xv * gv, axis=0, keepdims=True)
            da_ref[gl, 1:2, :] += jnp.sum(xv * pltpu.roll(gv, SSM_STATE, 1), axis=0, keepdims=True)
        _groups_tile(dus, du_ref, ncb)

    return _pcall(
        body, name=name,
        out_shape=[SDS((seqlen, d), F32)] + [SDS((ng, SSM_W, SSM_W), F32)] * 3 + [SDS((ng, 2, SSM_W), F32)],
        grid=grid, in_specs=[act, act, state, state, mats, mats],
        out_specs=[act, mats, mats, mats, pl.BlockSpec((_SSM_GT, 2, SSM_W), lambda j, r: (j, 0, 0))],
        vmem=48 << 20, comm=comm)(hn, dy, xp, gs, mmat, pmat)


def _ssm_carry(s, a1, a2, reverse, name, comm=None):
    nc, w = s.shape
    tc = _tile(nc, 256)
    nblk = nc // tc
    sub = 8
    shape = (sub, SSM_W)

    def body(s_ref, a1_ref, a2_ref, o_ref, st_ref, pw_ref, p1_ref, p2_ref):
        rows = lax.broadcasted_iota(jnp.int32, (sub, w), 0)
        sign = jnp.where(lax.broadcasted_iota(jnp.int32, (1, w), 1) % SSM_W < SSM_STATE, -1.0, 1.0)

        @pl.when(pl.program_id(0) == 0)
        def _():
            st_ref[...] = jnp.zeros_like(st_ref)
            cr, ci = a1_ref[...], a2_ref[...] * sign
            qr, qi = jnp.ones_like(cr), jnp.zeros_like(ci)
            p1, p2 = jnp.zeros((sub, w), F32), jnp.zeros((sub, w), F32)
            for r in range(sub):
                at = (sub - 1 - r) if reverse else r
                p1 = jnp.where(rows == at, qr, p1)
                p2 = jnp.where(rows == at, qi * sign, p2)
                qr, qi = qr * cr - qi * ci, qr * ci + qi * cr
            p1_ref[...] = p1
            p2_ref[...] = p2
            for k in range(4):
                pw_ref[2 * k:2 * k + 1, :] = cr
                pw_ref[2 * k + 1:2 * k + 2, :] = ci * sign
                cr, ci = cr * cr - ci * ci, 2.0 * cr * ci

        row = lax.broadcasted_iota(jnp.int32, shape, 0)

        def cmul(c1, c2, v):
            return c1 * v + c2 * pltpu.roll(v, SSM_STATE, 1)

        def shifted(v, k):
            if reverse:
                return jnp.where(row < sub - k, pltpu.roll(v, sub - k, 0), 0.0)
            return jnp.where(row >= k, pltpu.roll(v, k, 0), 0.0)

        def step(t, carry):
            tt = (tc // sub - 1 - t) if reverse else t
            base = pl.multiple_of(tt * sub, sub)
            for g in range(w // SSM_W):
                lanes = slice(g * SSM_W, (g + 1) * SSM_W)
                pw = [jnp.broadcast_to(pw_ref[i:i + 1, lanes], shape) for i in range(8)]
                y = s_ref[pl.ds(base, sub), lanes]
                for k in range(3):
                    y = y + cmul(pw[2 * k], pw[2 * k + 1], shifted(y, 1 << k))
                x = st_ref[:, lanes]
                o_ref[pl.ds(base, sub), lanes] = shifted(y, 1) + cmul(p1_ref[:, lanes], p2_ref[:, lanes], x)
                end = y[0:1, :] if reverse else y[sub - 1:sub, :]
                st_ref[:, lanes] = jnp.broadcast_to(end, shape) + cmul(pw[6], pw[7], x)
            return carry

        lax.fori_loop(0, tc // sub, step, 0)

    imap = (lambda i: (nblk - 1 - i, 0)) if reverse else (lambda i: (i, 0))
    cst = pl.BlockSpec((1, w), lambda i: (0, 0))
    return _pcall(body, name=name, out_shape=[SDS((nc, w), F32)], grid=(nblk,),
                  in_specs=[pl.BlockSpec((tc, w), imap), cst, cst], out_specs=[pl.BlockSpec((tc, w), imap)],
                  scratch=[pltpu.VMEM((sub, w), F32)] * 4,
                  vmem=_vmem_limit([((tc, w), F32, 4)], extra=8 << 20), comm=comm)(s, a1, a2)


def _ssm_rows(atr, ati, conj):
    ng = atr.shape[0]
    ai = -ati if conj else ati
    a1 = jnp.concatenate([atr, atr], axis=2).reshape(1, ng * SSM_W)
    a2 = jnp.concatenate([-ai, ai], axis=2).reshape(1, ng * SSM_W)
    return a1, a2


def _peers():
    x, y, c = (lax.axis_index(a) for a in AXES)
    me = 4 * x + 2 * y + c
    peers = []
    for dx, dy, dc in [(0, 0, 1), (0, 1, 0), (0, 1, 1), (1, 0, 0), (1, 0, 1), (1, 1, 0), (1, 1, 1)]:
        px, py, pc = (1 - x) if dx else x, (1 - y) if dy else y, (1 - c) if dc else c
        peers.append(((px, py, pc), 4 * px + 2 * py + pc))
    return me, peers


class _Exchange:
    def __init__(self, arrs, scatter, layers=None):
        self.arrs = list(arrs)
        self.scatter = scatter
        self.layers = list(layers) if layers is not None else [None] * len(self.arrs)

    def out_shape(self):
        shapes = []
        for arr, layer in zip(self.arrs, self.layers):
            block = arr.shape[1:] if (self.scatter or layer is not None) else arr.shape
            shapes.append(SDS((NDEV,) + tuple(block), arr.dtype))
        return shapes

    def semaphores(self):
        n = len(self.arrs)
        return [pltpu.SemaphoreType.DMA((n * (NDEV - 1),)), pltpu.SemaphoreType.DMA((n * (NDEV - 1),)),
                pltpu.SemaphoreType.DMA((n,))]

    def _src(self, ref, a, block):
        if self.scatter:
            return ref.at[block]
        return ref if self.layers[a] is None else ref.at[self.layers[a]]

    def _remote(self, xin, xout, sems, a, k, peer, landing):
        pid, pidx = peer
        slot = a * (NDEV - 1) + k
        return pltpu.make_async_remote_copy(
            src_ref=self._src(xin[a], a, pidx), dst_ref=xout[a].at[landing],
            send_sem=sems[0].at[slot], recv_sem=sems[1].at[slot], device_id=pid, device_id_type=MESH)

    def _local(self, xin, xout, sems, a, me):
        return pltpu.make_async_copy(self._src(xin[a], a, me), xout[a].at[me], sems[2].at[a])

    def start(self, xin, xout, sems):
        me, peers = _peers()
        for a in range(len(self.arrs)):
            self._local(xin, xout, sems, a, me).start()
        for k, peer in enumerate(peers):
            for a in range(len(self.arrs)):
                self._remote(xin, xout, sems, a, k, peer, me).start()

    def wait(self, xin, xout, sems):
        me, peers = _peers()
        for a in range(len(self.arrs)):
            self._local(xin, xout, sems, a, me).wait()
        for k, peer in enumerate(peers):
            for a in range(len(self.arrs)):
                cp = self._remote(xin, xout, sems, a, k, peer, peer[1])
                cp.wait_send()
                cp.wait_recv()


def _exchange(arrs, scatter, name, layers=None):
    comm = _Exchange(arrs, scatter, layers)
    n = len(comm.arrs)

    def body(*refs):
        xin, xout, sems = refs[:n], refs[n:2 * n], refs[2 * n:]
        comm.start(xin, xout, sems)
        comm.wait(xin, xout, sems)

    hbm = pl.BlockSpec(memory_space=pl.ANY)
    return pl.pallas_call(
        body, out_shape=comm.out_shape(), in_specs=[hbm] * n, out_specs=[hbm] * n,
        scratch_shapes=comm.semaphores(), name=name, interpret=False)(*comm.arrs)


def _adamw(parts, w, m, v, name):
    rows, cols = w.shape
    tr = _tile(rows, max(8, (1 << 17) // cols))
    c1 = 1.0 - ADAM_B1 ** ADAM_STEP
    c2 = 1.0 - ADAM_B2 ** ADAM_STEP

    def body(p_ref, w_ref, m_ref, v_ref, g_ref, d_ref, nm_ref, nv_ref):
        g = p_ref[0]
        for j in range(1, NDEV):
            g = g + p_ref[j]
        mm = ADAM_B1 * m_ref[...] + (1.0 - ADAM_B1) * g
        vv = ADAM_B2 * v_ref[...] + (1.0 - ADAM_B2) * (g * g)
        g_ref[...] = g
        nm_ref[...] = mm
        nv_ref[...] = vv
        d_ref[...] = -ADAM_LR * ((mm / c1) / (jnp.sqrt(vv / c2) + ADAM_EPS) + ADAM_WD * w_ref[...])

    spec = pl.BlockSpec((tr, cols), lambda i: (i, 0))
    return _pcall(
        body, name=name, out_shape=[SDS((rows, cols), F32)] * 4, grid=(rows // tr,),
        in_specs=[pl.BlockSpec((NDEV, tr, cols), lambda i: (0, i, 0)), spec, spec, spec], out_specs=[spec] * 4,
        vmem=_vmem_limit([((NDEV + 7, tr, cols), F32, 2)]))(parts, w, m, v)


def kernel(x, p, norm_mix, ssm_lambda_re, ssm_lambda_im, ssm_log_dt, ssm_b_re, ssm_b_im, ssm_c_re, ssm_c_im, ssm_d, ssm_w_glu, kv_norm, w_k, w_v, w_q, attn_sinks, w_o, norm_mlp, w_up, w_down, norm_ple, w_ple_gate, w_ple_proj, norm_final, loss_target, m_norm_mix, m_ssm_lambda_re, m_ssm_lambda_im, m_ssm_log_dt, m_ssm_b_re, m_ssm_b_im, m_ssm_c_re, m_ssm_c_im, m_ssm_d, m_ssm_w_glu, m_kv_norm, m_w_k, m_w_v, m_w_q, m_attn_sinks, m_w_o, m_norm_mlp, m_w_up, m_w_down, m_norm_ple, m_w_ple_gate, m_w_ple_proj, m_norm_final, v_norm_mix, v_ssm_lambda_re, v_ssm_lambda_im, v_ssm_log_dt, v_ssm_b_re, v_ssm_b_im, v_ssm_c_re, v_ssm_c_im, v_ssm_d, v_ssm_w_glu, v_kv_norm, v_w_k, v_w_v, v_w_q, v_attn_sinks, v_w_o, v_norm_mlp, v_w_up, v_w_down, v_norm_ple, v_w_ple_gate, v_w_ple_proj, v_norm_final):
    names = ['norm_mix', 'ssm_lambda_re', 'ssm_lambda_im', 'ssm_log_dt', 'ssm_b_re', 'ssm_b_im', 'ssm_c_re',
             'ssm_c_im', 'ssm_d', 'ssm_w_glu', 'kv_norm', 'w_k', 'w_v', 'w_q', 'attn_sinks', 'w_o', 'norm_mlp',
             'w_up', 'w_down', 'norm_ple', 'w_ple_gate', 'w_ple_proj', 'norm_final']
    weights = dict(zip(names, (norm_mix, ssm_lambda_re, ssm_lambda_im, ssm_log_dt, ssm_b_re, ssm_b_im, ssm_c_re,
                               ssm_c_im, ssm_d, ssm_w_glu, kv_norm, w_k, w_v, w_q, attn_sinks, w_o, norm_mlp,
                               w_up, w_down, norm_ple, w_ple_gate, w_ple_proj, norm_final)))
    mom1 = dict(zip(names, (m_norm_mix, m_ssm_lambda_re, m_ssm_lambda_im, m_ssm_log_dt, m_ssm_b_re, m_ssm_b_im,
                            m_ssm_c_re, m_ssm_c_im, m_ssm_d, m_ssm_w_glu, m_kv_norm, m_w_k, m_w_v, m_w_q,
                            m_attn_sinks, m_w_o, m_norm_mlp, m_w_up, m_w_down, m_norm_ple, m_w_ple_gate,
                            m_w_ple_proj, m_norm_final)))
    mom2 = dict(zip(names, (v_norm_mix, v_ssm_lambda_re, v_ssm_lambda_im, v_ssm_log_dt, v_ssm_b_re, v_ssm_b_im,
                            v_ssm_c_re, v_ssm_c_im, v_ssm_d, v_ssm_w_glu, v_kv_norm, v_w_k, v_w_v, v_w_q,
                            v_attn_sinks, v_w_o, v_norm_mlp, v_w_up, v_w_down, v_norm_ple, v_w_ple_gate,
                            v_w_ple_proj, v_norm_final)))

    seqlen, d = x.shape[1], x.shape[2]
    depth = w_up.shape[0]
    n_ssm = ssm_w_glu.shape[0]
    n_att = w_q.shape[0]
    ng = d // SSM_GROUP
    nh = d // HEAD_DIM
    h0 = x[0]
    tgt = loss_target[0]
    tabs = _rope_tables(seqlen)

    sharded = ['w_up', 'w_down', 'w_ple_gate', 'w_ple_proj', 'ssm_w_glu', 'w_q', 'w_o', 'w_k', 'w_v']
    shards = {k: weights[k].astype(BF16) for k in sharded}
    shards['ssm_d'] = ssm_d
    dkv = w_k.shape[1]

    def layer_set(i):
        keys = [('w_up', i), ('w_down', i), ('w_ple_gate', i), ('w_ple_proj', i)]
        keys += [('ssm_w_glu', i), ('ssm_d', i)] if i < n_ssm else [('w_q', i - n_ssm), ('w_o', i - n_ssm)]
        if i == n_ssm:
            keys += [('w_k', None), ('w_v', None)]
        return keys

    def gather_of(i, only=None):
        keys = [kl for kl in layer_set(i) if only is None or kl[0] in only]
        return keys, _Exchange([shards[k] for k, _ in keys], False, [l for _, l in keys])

    def as_operands(keys, blocks):
        w = {}
        for (k, _), g in zip(keys, blocks):
            if k == 'ssm_d':
                w[k] = g.reshape(d)
            elif k in ('w_ple_proj', 'ssm_w_glu', 'w_up'):
                w[k] = g.transpose(1, 0, 2).reshape(g.shape[1], NDEV * g.shape[2])
                w[k + '_t'] = g.transpose(0, 2, 1).reshape(NDEV * g.shape[2], g.shape[1])
            else:
                w[k] = g.reshape(NDEV * g.shape[1], g.shape[2])
                w[k + '_t'] = g.transpose(2, 0, 1).reshape(g.shape[2], NDEV * g.shape[1])
        return w

    lw = {}

    def ssm_params(i):
        n = SSM_STATE
        return (ssm_lambda_re[i].reshape(ng, 1, n), ssm_lambda_im[i].reshape(ng, 1, n),
                ssm_log_dt[i].reshape(ng, 1, 1), jnp.swapaxes(ssm_b_re[i], 1, 2), jnp.swapaxes(ssm_b_im[i], 1, 2),
                ssm_c_re[i], ssm_c_im[i])

    h = h0
    h_in, h_a, h_b, acts = [], [], [], []
    ssm_saved, att_saved = {}, {}
    k_sh = v_sh = None
    for i in range(depth):
        h_in.append(h)
        if i < n_ssm:
            first = gather_of(0, ('ssm_w_glu', 'ssm_d')) if i == 0 else None
            hn, *got = _norm_fwd(h, norm_mix[i], f"norm_mix_fwd{i}", comm=first and first[1])
            if first:
                lw[0] = as_operands(first[0], got)
            mats = _ssm_prep(ssm_params(i), f"ssm_prep{i}")
            mmat, atr, ati = mats[0], mats[5], mats[6]
            pmat = jnp.concatenate([mats[1], mats[2]], axis=2)
            qt = jnp.concatenate([mats[3], mats[4]], axis=2)
            s_in = _ssm_state_in(hn, pmat, f"ssm_state_in{i}")
            xp = _ssm_carry(s_in, *_ssm_rows(atr, ati, False), False, f"ssm_carry_fwd{i}")[0]
            rest = gather_of(0, ('w_up', 'w_down', 'w_ple_gate', 'w_ple_proj')) if i == 0 else None
            y, *got = _ssm_out(hn, xp, mmat, qt, f"ssm_out{i}", comm=rest and rest[1])
            if rest:
                lw[0].update(as_operands(rest[0], got))
            ha = _glu_fwd(y, hn, h, lw[i]['ssm_d'], lw[i]['ssm_w_glu'], f"glu_fwd{i}")
            ssm_saved[i] = (hn, mmat, pmat, qt, atr, ati, xp, y)
        else:
            j = i - n_ssm
            q = _q_fwd(h, norm_mix[i], lw[i]['w_q'], tabs, f"q_fwd{j}")
            o = _attn_fwd(q, k_sh, v_sh, attn_sinks[j], f"attn_fwd{j}")
            ha = _lin_res(h, o, lw[i]['w_o'], f"attn_out{j}")
            att_saved[j] = (q, o)
        h_a.append(ha)
        nxt = gather_of(i + 1) if i + 1 < depth else None
        res = _mlp_ple_fwd(ha, p[i, 0], norm_mlp[i], lw[i]['w_up'], lw[i]['w_down'], norm_ple[i],
                           lw[i]['w_ple_gate'], lw[i]['w_ple_proj'], f"mlp_fwd{i}", comm=nxt and nxt[1])
        hb, act, h = res[:3]
        acts.append(act)
        if nxt:
            lw[i + 1] = as_operands(nxt[0], res[3:])
        h_b.append(hb)
        if i == n_ssm - 1:
            k_sh, v_sh = _kv_fwd(h, kv_norm, lw[n_ssm]['w_k'], lw[n_ssm]['w_v'], tabs, "kv_fwd")
    h_kv = h_in[n_ssm] if n_ssm < depth else h
    dh, loss_row, g_norm_final = _loss_bwd(h, norm_final, tgt, "loss_bwd")
    loss = lax.psum(loss_row[0, 0], AXES)

    g_norm_mix, g_norm_mlp, g_norm_ple = [None] * depth, [None] * depth, [None] * depth
    g_ssm, g_sinks = [None] * n_ssm, [None] * n_att
    g_kv_norm = None
    dks, dvs = [], []
    recv = {}

    def riding(stacks):
        if not stacks:
            return None
        ride = (list(stacks), _Exchange(list(stacks.values()), True))
        stacks.clear()
        return ride

    def landed(ride, blocks):
        if ride:
            recv.update(zip(ride[0], blocks))

    gl = {}
    for i in range(depth - 1, -1, -1):
        if i == n_ssm - 1:
            dh, dkp, dvb, hkb, g_kv_norm = _kv_bwd(dks, dvs, h_kv, kv_norm, dh, lw[n_ssm]['w_k_t'],
                                                   lw[n_ssm]['w_v_t'], tabs, "kv_bwd")
            gl['w_k', None] = _atb(hkb, dkp, False, "grad_w_k")
            gl['w_v', None] = _atb(hkb, dvb, False, "grad_w_v")
        dhb, dz, nb16, dpp, g_norm_ple[i] = _ple_bwd(h_b[i], p[i, 0], dh, norm_ple[i], lw[i]['w_ple_gate'],
                                                     lw[i]['w_ple_gate_t'], lw[i]['w_ple_proj'], f"ple_bwd{i}")
        gl['w_ple_gate', i] = _atb(nb16, dz, False, f"grad_w_ple_gate{i}")
        gl['w_ple_proj', i] = _atb(p[i, 0], dpp, True, f"grad_w_ple_proj{i}")
        ride = riding(gl)
        res = _mlp_bwd(h_a[i], acts[i], dhb, norm_mlp[i], lw[i]['w_up_t'], lw[i]['w_down_t'], f"mlp_bwd{i}",
                       comm=ride and ride[1])
        dha, hmb, da, g_norm_mlp[i] = res[:4]
        landed(ride, res[4:])
        g_up = {('w_up', i): _atb(hmb, da, True, f"grad_w_up{i}")}
        g_down = {('w_down', i): _atb(acts[i], dhb, False, f"grad_w_down{i}")}
        if i >= n_ssm:
            j = i - n_ssm
            q, o = att_saved[j]
            do = _lin_bf16(dha, lw[i]['w_o_t'], f"attn_out_bwd{j}")
            ride = riding({**g_up, **g_down, ('w_o', j): _atb(o, dha, False, f"grad_w_o{j}")})
            dq, dk_j, dv_j, dsink, *got = _attn_bwd(q, k_sh, v_sh, do, attn_sinks[j], f"attn_bwd{j}",
                                                    comm=ride[1])
            landed(ride, got)
            dks.append(dk_j)
            dvs.append(dv_j)
            g_sinks[j] = dsink[:, 0]
            dh, dqp, hnb, g_norm_mix[i] = _q_bwd(dq, h_in[i], norm_mix[i], dha, lw[i]['w_q_t'], tabs, f"q_bwd{j}")
            gl['w_q', j] = _atb(hnb, dqp, False, f"grad_w_q{j}")
        else:
            hn, mmat, pmat, qt, atr, ati, xp, y = ssm_saved[i]
            dyy, geb, dab, g_dskip = _glu_bwd(y, hn, dha, lw[i]['ssm_d'], lw[i]['ssm_w_glu'],
                                              lw[i]['ssm_w_glu_t'], f"glu_bwd{i}")
            g_down['ssm_d', i] = g_dskip.reshape(NDEV, d // NDEV)
            g_down['ssm_w_glu', i] = _atb(geb, dab, True, f"grad_ssm_w_glu{i}")
            dxp = _ssm_dstate(dyy, qt, f"ssm_dstate{i}")
            ride = riding(g_up)
            gs, *got = _ssm_carry(dxp, *_ssm_rows(atr, ati, True), True, f"ssm_carry_bwd{i}", comm=ride[1])
            landed(ride, got)
            ride = riding(g_down)
            du, dm, dp, dqt, da_raw, *got = _ssm_bwd(hn, dyy, xp, gs, mmat, pmat, f"ssm_bwd{i}", comm=ride[1])
            landed(ride, got)
            n = SSM_STATE
            cots = (dm, dp[:, :, :n], dp[:, :, n:], dqt[:, :, :n], dqt[:, :, n:],
                    (da_raw[:, 0:1, :n] + da_raw[:, 0:1, n:]), (da_raw[:, 1:2, :n] - da_raw[:, 1:2, n:]))
            g_ssm[i] = _ssm_prep_vjp(ssm_params(i), cots, f"ssm_prep_vjp{i}")
            dh, g_norm_mix[i] = _norm_bwd(h_in[i], norm_mix[i], dyy, lw[i]['ssm_d'], du, dha, f"norm_mix_bwd{i}")
    grad_x = dh[None]
    if gl:
        ride = riding(gl)
        landed(ride, _exchange(ride[1].arrs, True, "scatter_grads_rest"))

    out_g, out_d, out_m, out_v = {}, {}, {}, {}
    updated = {}
    for (k, l), parts in recv.items():
        pick = (lambda t: t) if l is None else (lambda t: t[l])
        shp = pick(weights[k]).shape
        r2 = (math.prod(shp[:-1]), shp[-1])
        res = _adamw(parts.reshape((NDEV,) + r2), pick(weights[k]).reshape(r2), pick(mom1[k]).reshape(r2),
                     pick(mom2[k]).reshape(r2), f"adamw_{k}{'' if l is None else l}")
        updated.setdefault(k, {})[l] = [t.reshape(shp) for t in res]
    for k, by_layer in updated.items():
        for n, dst in enumerate((out_g, out_d, out_m, out_v)):
            dst[k] = by_layer[None][n] if None in by_layer else jnp.stack([by_layer[l][n] for l in sorted(by_layer)])

    def ssm_grad(idx, unswap=False):
        g = jnp.stack([g_ssm[i][idx] for i in range(n_ssm)])
        return jnp.swapaxes(g, 2, 3) if unswap else g

    small = {'norm_mix': jnp.concatenate(g_norm_mix, axis=0),
             'ssm_lambda_re': ssm_grad(0), 'ssm_lambda_im': ssm_grad(1), 'ssm_log_dt': ssm_grad(2),
             'ssm_b_re': ssm_grad(3, True), 'ssm_b_im': ssm_grad(4, True),
             'ssm_c_re': ssm_grad(5), 'ssm_c_im': ssm_grad(6),
             'kv_norm': g_kv_norm, 'attn_sinks': jnp.stack(g_sinks),
             'norm_mlp': jnp.concatenate(g_norm_mlp, axis=0), 'norm_ple': jnp.concatenate(g_norm_ple, axis=0),
             'norm_final': g_norm_final}
    snames = list(small)
    sizes = [weights[k].size for k in snames]
    total = sum(sizes)
    lanes = 128
    padded = -(-total // (512 * lanes)) * (512 * lanes)

    def flat(parts):
        v = jnp.concatenate([t.reshape(-1) for t in parts] + [jnp.zeros((padded - total,), F32)])
        return v.reshape(padded // lanes, lanes)

    parts = _exchange([flat([small[k] for k in snames])], False, "gather_small_grads")[0]
    res = _adamw(parts, flat([weights[k] for k in snames]), flat([mom1[k] for k in snames]),
                 flat([mom2[k] for k in snames]), "adamw_small")
    off = 0
    for k, sz in zip(snames, sizes):
        for dst, t in zip((out_g, out_d, out_m, out_v), res):
            dst[k] = t.reshape(-1)[off:off + sz].reshape(weights[k].shape)
        off += sz

    return (loss, grad_x, *[out_g[k] for k in names], *[out_d[k] for k in names],
            *[out_m[k] for k in names], *[out_v[k] for k in names])
```
